```python
import math
import jax, jax.numpy as jnp
from jax import lax
import numpy as np

D_MODEL = 1024
BATCH = 8
SEQ = 2048
DEPTH = 1
DEC_BATCH = 128
DEC_SEQ = 8
PAST_LEN = 16384
PAGE_SIZE = 128

MIX_WIDTH = D_MODEL
ML_WIDTH = MIX_WIDTH // 2
ML_HEADS = 4
ML_HD = ML_WIDTH // ML_HEADS
ML_CHUNK = 64
CONV_W = 4
RW_WIDTH = MIX_WIDTH - ML_WIDTH
RW_HD = 64
RW_HEADS = RW_WIDTH // RW_HD
DECAY_LORA = 64
AAA_LORA = 64
GATE_LORA = 128
PEER_HEADS = 8
N_KEYS = 128
N_EXPERTS = N_KEYS * N_KEYS
PEER_QDIM = 256
PEER_TOPK = 16
PEER_BLOCK = 128
NORM_EPS = 1e-6
ML_NORM_EPS = 1e-6
GN_EPS = RW_HD * 1e-5
M_INIT = -1e30
ML_COLS = 4 * ML_WIDTH + 2 * ML_HEADS
RW_COLS = 3 * RW_WIDTH + DECAY_LORA + AAA_LORA + GATE_LORA
IN_COLS = ML_COLS + RW_COLS

kernel_name = 'hymba_mlstm_rwkv7_peer_step'


def rmsnorm(x, g):
    xf = x.astype(jnp.float32)
    y = xf * lax.rsqrt(jnp.mean(xf * xf, axis=-1, keepdims=True) + NORM_EPS)
    return (y * g.astype(jnp.float32)).astype(x.dtype)


def head_norm(y, g, eps):
    mu = jnp.mean(y, axis=-1, keepdims=True)
    var = jnp.mean(jnp.square(y - mu), axis=-1, keepdims=True)
    y = (y - mu) * lax.rsqrt(var + eps)
    return y.reshape(y.shape[:2] + (-1,)) * g


def mlstm_chunkwise(q, k, v, ig, lf, C0, n0, m0):
    B, H, T, DH = q.shape
    L = math.gcd(T, ML_CHUNK)
    nc = T // L
    to_chunks = lambda a: jnp.moveaxis(a.reshape((B, H, nc, L) + a.shape[3:]), 2, 0)
    causal = jnp.tril(jnp.ones((L, L), dtype=bool))

    def chunk_step(carry, inp):
        C, n, m = carry
        qc, kc, vc, ic, fc = inp
        b = jnp.cumsum(fc, axis=-1)
        log_d = jnp.where(causal, b[..., :, None] - b[..., None, :] + ic[..., None, :], -jnp.inf)
        log_inter = b + m[..., None]
        m_t = jnp.maximum(log_inter, jnp.max(log_d, axis=-1))
        d = jnp.exp(log_d - m_t[..., None])
        s_inter = jnp.exp(log_inter - m_t)
        qk = jnp.einsum('bhtd,bhsd->bhts', qc, kc) * d
        num = s_inter[..., None] * jnp.einsum('bhtd,bhde->bhte', qc, C) + jnp.einsum('bhts,bhse->bhte', qk, vc)
        den = s_inter * jnp.einsum('bhtd,bhd->bht', qc, n) + jnp.sum(qk, axis=-1)
        h = num / jnp.maximum(jnp.abs(den), jnp.exp(-m_t))[..., None]
        m_new = m_t[..., -1]
        s_state = jnp.exp(b[..., -1] + m - m_new)
        w_chunk = jnp.exp(b[..., -1:] - b + ic - m_new[..., None])
        C_new = s_state[..., None, None] * C + jnp.einsum('bhs,bhsd,bhse->bhde', w_chunk, kc, vc)
        n_new = s_state[..., None] * n + jnp.einsum('bhs,bhsd->bhd', w_chunk, kc)
        return (C_new, n_new, m_new), h

    (C1, n1, m1), hs = lax.scan(chunk_step, (C0, n0, m0), tuple(to_chunks(a) for a in (q, k, v, ig, lf)))
    h = jnp.moveaxis(hs, 0, 2).reshape(B, H, T, DH)
    return h, C1, n1, m1


def mlstm_mixer(cols, conv0, C0, n0, m0, conv_w, conv_b, b_i, b_f, norm_g):
    B, T, _ = cols.shape
    qk_raw = cols[..., :2 * ML_WIDTH]
    v = cols[..., 2 * ML_WIDTH:3 * ML_WIDTH]
    o = cols[..., 3 * ML_WIDTH:4 * ML_WIDTH]
    ig = cols[..., 4 * ML_WIDTH:4 * ML_WIDTH + ML_HEADS] + b_i
    fg = cols[..., 4 * ML_WIDTH + ML_HEADS:] + b_f
    ext = jnp.concatenate([conv0, qk_raw], axis=1)
    qk = conv_b + sum(ext[:, j:j + T] * conv_w[j] for j in range(CONV_W))
    qk = jax.nn.silu(qk)
    heads = lambda a: a.reshape(B, T, ML_HEADS, ML_HD).transpose(0, 2, 1, 3)
    q = heads(qk[..., :ML_WIDTH])
    k = heads(qk[..., ML_WIDTH:]) * (ML_HD ** -0.5)
    h, C1, n1, m1 = mlstm_chunkwise(q, k, heads(v), ig.transpose(0, 2, 1),
                                    jax.nn.log_sigmoid(fg).transpose(0, 2, 1), C0, n0, m0)
    h = head_norm(h.transpose(0, 2, 1, 3), norm_g, ML_NORM_EPS) * jax.nn.sigmoid(o)
    return h, C1, n1, m1, ext[:, T:]


def rwkv7_mixer(cols, shift0, S0, mu, w0, w2, a0, a2, g2, k_k, k_a, r_k, gn_g, gn_b):
    B, T, _ = cols.shape
    W = RW_WIDTH
    prev = jnp.concatenate([shift0[:, None, :], cols[:, :-1]], axis=1)
    xs = cols + (prev - cols) * mu
    r, k, v = xs[..., :W], xs[..., W:2 * W], xs[..., 2 * W:3 * W]
    wd = xs[..., 3 * W:3 * W + DECAY_LORA]
    ad = xs[..., 3 * W + DECAY_LORA:3 * W + DECAY_LORA + AAA_LORA]
    gd = xs[..., 3 * W + DECAY_LORA + AAA_LORA:]
    w_log = -jax.nn.softplus(-(w0 + jnp.tanh(wd) @ w2)) - 0.5
    decay = jnp.exp(-jnp.exp(w_log))
    a = jax.nn.sigmoid(a0 + ad @ a2)
    g = jax.nn.sigmoid(gd) @ g2
    kk = k * k_k
    k = k * (1.0 + (a - 1.0) * k_a)
    heads = lambda t: t.reshape(B, T, RW_HEADS, RW_HD)
    r, k, v, kk, a, decay = map(heads, (r, k, v, kk, a, decay))
    kk = kk / jnp.maximum(jnp.linalg.norm(kk, axis=-1, keepdims=True), 1e-12)

    def step(S, inp):
        rt, wt, kt, vt, kkt, at = inp
        sa = jnp.einsum('bhvk,bhk->bhv', S, -kkt)
        S = S * wt[:, :, None, :] + sa[..., None] * (kkt * at)[:, :, None, :] + vt[..., None] * kt[:, :, None, :]
        return S, jnp.einsum('bhvk,bhk->bhv', S, rt)

    S1, ys = lax.scan(step, S0, tuple(jnp.swapaxes(t, 0, 1) for t in (r, decay, k, v, kk, a)))
    y = head_norm(jnp.swapaxes(ys, 0, 1), gn_g, GN_EPS) + gn_b
    y = y + (jnp.sum(r * k * r_k, axis=-1, keepdims=True) * v).reshape(B, T, W)
    return y * g, S1, cols[:, -1]


def peer(h, w_q, sub_keys, u_emb, v_emb):
    B, T, D = h.shape
    n = B * T
    n_blocks = -(-n // PEER_BLOCK)
    flat = jnp.pad(h.reshape(n, D), ((0, n_blocks * PEER_BLOCK - n), (0, 0))).reshape(n_blocks, PEER_BLOCK, D)

    def block(xb):
        q = (xb @ w_q).astype(jnp.float32).reshape(PEER_BLOCK, PEER_HEADS, 2, PEER_QDIM // 2)
        s = jnp.einsum('bhpd,hpkd->bhpk', q, sub_keys.astype(jnp.float32))
        sv, si = lax.top_k(s, PEER_TOPK)
        cand = (sv[:, :, 0, :, None] + sv[:, :, 1, None, :]).reshape(PEER_BLOCK, PEER_HEADS, PEER_TOPK * PEER_TOPK)
        cv, ci = lax.top_k(cand, PEER_TOPK)
        i1 = jnp.take_along_axis(si[:, :, 0], ci // PEER_TOPK, axis=-1)
        i2 = jnp.take_along_axis(si[:, :, 1], ci % PEER_TOPK, axis=-1)
        idx = i1 * N_KEYS + i2
        gate = jax.nn.softmax(cv, axis=-1)
        act = jax.nn.gelu(jnp.einsum('bhkd,bd->bhk', u_emb[idx], xb).astype(jnp.float32), approximate=False)
        return jnp.einsum('bhk,bhkd->bd', (gate * act).astype(v_emb.dtype), v_emb[idx])

    out = lax.map(block, flat)
    return out.reshape(-1, D)[:n].reshape(B, T, D).astype(h.dtype)


def hybrid_layer(x, C0, n0, m0, conv0, S0, shift0,
                 norm_mix_g, w_in, ml_conv_w, ml_conv_b, ml_b_i, ml_b_f, ml_norm_g,
                 rw_mu, rw_w0, rw_w2, rw_a0, rw_a2, rw_g2, rw_k_k, rw_k_a, rw_r_k, rw_gn_g, rw_gn_b,
                 w_out, norm_ffn_g, peer_w_q, peer_sub_keys, peer_u, peer_v):
    f32 = jnp.float32
    cols = (rmsnorm(x, norm_mix_g) @ w_in).astype(f32)
    ml_out, C1, n1, m1, conv1 = mlstm_mixer(cols[..., :ML_COLS], conv0.astype(f32), C0.astype(f32),
                                            n0.astype(f32), m0.astype(f32), ml_conv_w, ml_conv_b,
                                            ml_b_i, ml_b_f, ml_norm_g)
    rw_out, S1, shift1 = rwkv7_mixer(cols[..., ML_COLS:], shift0.astype(f32), S0.astype(f32), rw_mu,
                                     rw_w0, rw_w2, rw_a0, rw_a2, rw_g2, rw_k_k, rw_k_a, rw_r_k,
                                     rw_gn_g, rw_gn_b)
    mix = jnp.concatenate([ml_out, rw_out], axis=-1).astype(x.dtype)
    x = x + mix @ w_out
    x = x + peer(rmsnorm(x, norm_ffn_g), peer_w_q, peer_sub_keys, peer_u, peer_v)
    return x, (C1, n1, m1, conv1, S1, shift1)


def trunk(x, states, layer_weights, norm_final_g):
    new = [[] for _ in states]
    for l in range(DEPTH):
        lw = [w[l] for w in layer_weights]
        x, st = hybrid_layer(x, *[s[l] for s in states], *lw)
        for lst, s_new, s_old in zip(new, st, states):
            lst.append(s_new.astype(s_old.dtype))
    return rmsnorm(x, norm_final_g), [jnp.stack(lst) for lst in new]


def setup_inputs(seed: int = 0) -> dict:
    key = jax.random.key(seed)
    ks = iter(jax.random.split(key, 40))
    nrm = lambda shape, scale: scale * jax.random.normal(next(ks), shape, jnp.float32)
    unif = lambda shape, lo, hi: jax.random.uniform(next(ks), shape, jnp.float32, lo, hi)
    L = DEPTH
    return {
        'x_prompt': nrm((BATCH, SEQ, D_MODEL), 1.0),
        'x_sample': nrm((DEC_BATCH, DEC_SEQ, D_MODEL), 1.0),
        'state_mlstm_C': nrm((L, DEC_BATCH, ML_HEADS, ML_HD, ML_HD), 0.05),
        'state_mlstm_n': nrm((L, DEC_BATCH, ML_HEADS, ML_HD), 0.5),
        'state_mlstm_m': nrm((L, DEC_BATCH, ML_HEADS), 1.0),
        'state_mlstm_conv': nrm((L, DEC_BATCH, CONV_W - 1, 2 * ML_WIDTH), 1.0),
        'state_rwkv_S': nrm((L, DEC_BATCH, RW_HEADS, RW_HD, RW_HD), 0.3),
        'state_rwkv_shift': nrm((L, DEC_BATCH, RW_COLS), 1.0),
        'norm_mix_g': 1.0 + nrm((L, D_MODEL), 0.02),
        'w_in': nrm((L, D_MODEL, IN_COLS), D_MODEL ** -0.5),
        'ml_conv_w': nrm((L, CONV_W, 2 * ML_WIDTH), 0.5),
        'ml_conv_b': nrm((L, 2 * ML_WIDTH), 0.02),
        'ml_b_i': nrm((L, ML_HEADS), 0.1),
        'ml_b_f': unif((L, ML_HEADS), 3.0, 6.0),
        'ml_norm_g': 1.0 + nrm((L, ML_WIDTH), 0.02),
        'rw_mu': unif((L, RW_COLS), 0.0, 1.0),
        'rw_w0': unif((L, RW_WIDTH), -6.0, 1.0),
        'rw_w2': nrm((L, DECAY_LORA, RW_WIDTH), 0.1),
        'rw_a0': nrm((L, RW_WIDTH), 0.1),
        'rw_a2': nrm((L, AAA_LORA, RW_WIDTH), 0.1),
        'rw_g2': nrm((L, GATE_LORA, RW_WIDTH), GATE_LORA ** -0.5),
        'rw_k_k': 0.85 + nrm((L, RW_WIDTH), 0.02),
        'rw_k_a': 1.0 + nrm((L, RW_WIDTH), 0.02),
        'rw_r_k': nrm((L, RW_HEADS, RW_HD), 0.1),
        'rw_gn_g': 1.0 + nrm((L, RW_WIDTH), 0.02),
        'rw_gn_b': nrm((L, RW_WIDTH), 0.02),
        'w_out': nrm((L, MIX_WIDTH, D_MODEL), MIX_WIDTH ** -0.5),
        'norm_ffn_g': 1.0 + nrm((L, D_MODEL), 0.02),
        'peer_w_q': nrm((L, D_MODEL, PEER_HEADS * PEER_QDIM), D_MODEL ** -0.5),
        'peer_sub_keys': nrm((L, PEER_HEADS, 2, N_KEYS, PEER_QDIM // 2), (PEER_QDIM // 2) ** -0.5),
        'peer_u': nrm((L, N_EXPERTS, D_MODEL), D_MODEL ** -0.5),
        'peer_v': nrm((L, N_EXPERTS, D_MODEL), PEER_HEADS ** -0.5),
        'norm_final_g': 1.0 + nrm((D_MODEL,), 0.02),
    }


def reference(x_prompt, x_sample, state_mlstm_C, state_mlstm_n, state_mlstm_m, state_mlstm_conv,
              state_rwkv_S, state_rwkv_shift,
              norm_mix_g, w_in, ml_conv_w, ml_conv_b, ml_b_i, ml_b_f, ml_norm_g,
              rw_mu, rw_w0, rw_w2, rw_a0, rw_a2, rw_g2, rw_k_k, rw_k_a, rw_r_k, rw_gn_g, rw_gn_b,
              w_out, norm_ffn_g, peer_w_q, peer_sub_keys, peer_u, peer_v, norm_final_g):
    layer_weights = (norm_mix_g, w_in, ml_conv_w, ml_conv_b, ml_b_i, ml_b_f, ml_norm_g,
                     rw_mu, rw_w0, rw_w2, rw_a0, rw_a2, rw_g2, rw_k_k, rw_k_a, rw_r_k, rw_gn_g, rw_gn_b,
                     w_out, norm_ffn_g, peer_w_q, peer_sub_keys, peer_u, peer_v)
    f32 = jnp.float32
    z = lambda *s: jnp.zeros((DEPTH, BATCH) + s, f32)
    prompt_states = (z(ML_HEADS, ML_HD, ML_HD), z(ML_HEADS, ML_HD),
                     jnp.full((DEPTH, BATCH, ML_HEADS), M_INIT, f32), z(CONV_W - 1, 2 * ML_WIDTH),
                     z(RW_HEADS, RW_HD, RW_HD), z(RW_COLS))
    sample_states = (state_mlstm_C, state_mlstm_n, state_mlstm_m, state_mlstm_conv,
                     state_rwkv_S, state_rwkv_shift)
    y_prompt, (p_C, p_n, p_m, p_conv, p_S, p_shift) = trunk(x_prompt, prompt_states, layer_weights, norm_final_g)
    y_sample, (s_C, s_n, s_m, s_conv, s_S, s_shift) = trunk(x_sample, sample_states, layer_weights, norm_final_g)
    return (y_prompt, y_sample, p_C, p_n, p_m, p_conv, p_S, p_shift, s_C, s_n, s_m, s_conv, s_S, s_shift)
```

```python
import functools

import jax
import jax.numpy as jnp
from jax import lax
from jax.experimental import pallas as pl
from jax.experimental.pallas import tpu as pltpu

F32 = jnp.float32
BF16 = jnp.bfloat16

D_MODEL = 1024
ML_HEADS = 4
ML_HD = 128
ML_WIDTH = ML_HEADS * ML_HD
CONV_W = 4
RW_HEADS = 8
RW_HD = 64
RW_WIDTH = RW_HEADS * RW_HD
DECAY_LORA = 64
AAA_LORA = 64
GATE_LORA = 128
RW_COLS = 3 * RW_WIDTH + DECAY_LORA + AAA_LORA + GATE_LORA
ML_QKVO = 4 * ML_WIDTH
PEER_HEADS = 8
N_KEYS = 128
PEER_TOPK = 16
PEER_HALF = 128
NORM_EPS = 1e-6
ML_NORM_EPS = 1e-6
GN_EPS = RW_HD * 1e-5
M_INIT = -1e30
NEG_BIG = -1e30

LANE = 128
ML_CHUNK = 128
RW_CHUNK = 64
RW_GROUP = 4
PROJ_TILE = 256
PEER_TILE = 512
PEER_EXPERT_CHUNK = 1024
VMEM_LIMIT = 56 * 1024 * 1024


def _dot(a, b):
    return jnp.dot(a, b, preferred_element_type=F32)


def _dot_nt(a, b):
    return lax.dot_general(a, b, (((1,), (1,)), ((), ())), preferred_element_type=F32)


def _split_bf16(x, n):
    parts = []
    r = x
    for _ in range(n):
        p = r.astype(BF16)
        parts.append(p)
        r = r - p.astype(F32)
    return parts


def _dot_exact_lhs(mask_bf16, x, n):
    return sum(_dot(mask_bf16, p) for p in _split_bf16(x, n))


def _dot_exact_rhs(x, mask_bf16, n):
    return sum(_dot(p, mask_bf16) for p in _split_bf16(x, n))


def _sigmoid(x):
    return 1.0 / (1.0 + jnp.exp(-x))


def _softplus(x):
    return jnp.maximum(x, 0.0) + jnp.log1p(jnp.exp(-jnp.abs(x)))


def _iota(shape, dim):
    return lax.broadcasted_iota(jnp.int32, shape, dim)


def _pad_rows(x, rows, value=0.0):
    if x.shape[0] == rows:
        return x
    return jnp.concatenate([x, jnp.full((rows - x.shape[0], x.shape[1]), value, x.dtype)], axis=0)


def _proj_in_kernel(x_ref, g_ref, wml_ref, wg_ref, wrw_ref, ml_ref, gate_ref, rw_ref):
    x = x_ref[...]
    xn = x * lax.rsqrt(jnp.mean(x * x, axis=-1, keepdims=True) + NORM_EPS) * g_ref[...]
    xb = xn.astype(BF16)
    ml_ref[...] = _dot(xb, wml_ref[...])
    rw_ref[...] = _dot(xb, wrw_ref[...])
    xlo = (xn - xb.astype(F32)).astype(BF16)
    gate_ref[...] = _dot(xb, wg_ref[0]) + _dot(xlo, wg_ref[0]) + _dot(xb, wg_ref[1])


def _proj_in(x2d, g, wml, wg, wrw, tm):
    n = x2d.shape[0]
    const2 = lambda i: (0, 0)
    return pl.pallas_call(
        _proj_in_kernel,
        grid=(n // tm,),
        in_specs=[
            pl.BlockSpec((tm, D_MODEL), lambda i: (i, 0)),
            pl.BlockSpec((1, D_MODEL), const2),
            pl.BlockSpec((D_MODEL, ML_QKVO), const2),
            pl.BlockSpec((2, D_MODEL, LANE), lambda i: (0, 0, 0)),
            pl.BlockSpec((D_MODEL, RW_COLS), const2),
        ],
        out_specs=[
            pl.BlockSpec((tm, ML_QKVO), lambda i: (i, 0)),
            pl.BlockSpec((tm, LANE), lambda i: (i, 0)),
            pl.BlockSpec((tm, RW_COLS), lambda i: (i, 0)),
        ],
        out_shape=[
            jax.ShapeDtypeStruct((n, ML_QKVO), F32),
            jax.ShapeDtypeStruct((n, LANE), F32),
            jax.ShapeDtypeStruct((n, RW_COLS), F32),
        ],
        compiler_params=pltpu.CompilerParams(
            dimension_semantics=("arbitrary",), vmem_limit_bytes=VMEM_LIMIT),
        name="proj_in",
    )(x2d, g, wml, wg, wrw)


def _mlstm_kernel(ml_ref, gate_ref, c0_ref, n0_ref, m0_ref, conv0_ref, cw_ref, cb_ref, gb_ref, ng_ref,
                  out_ref, c1_ref, n1_ref, m1_ref, conv1_ref,
                  caug_ref, m_ref, ext_ref, *, tr):
    L = ML_CHUNK
    c = pl.program_id(1)
    nc = pl.num_programs(1)

    @pl.when(c == 0)
    def _init():
        for h in range(ML_HEADS):
            caug_ref[h, :, 0:ML_HD] = c0_ref[0, h]
            nrow = n0_ref[0, h:h + 1, :]
            caug_ref[h, :, ML_HD:2 * ML_HD] = jnp.broadcast_to(nrow, (ML_HD, ML_HD)).T
        m_ref[...] = jnp.zeros(m_ref.shape, F32)
        m_ref[:, 0:ML_HEADS] = m0_ref[0]
        ext_ref[5:8, :] = conv0_ref[0]

    ext_ref[8:8 + tr, :] = ml_ref[0, :, 0:2 * ML_WIDTH]
    acc = cb_ref[...] + ext_ref[5:5 + tr, :] * cw_ref[0:1, :]
    for j in range(1, CONV_W):
        acc = acc + ext_ref[5 + j:5 + j + tr, :] * cw_ref[j:j + 1, :]
    qk = acc * _sigmoid(acc)
    tail = ext_ref[tr + 5:tr + 8, :]
    ext_ref[5:8, :] = tail

    @pl.when(c == nc - 1)
    def _conv_out():
        conv1_ref[0] = tail

    q_all = _pad_rows(qk[:, 0:ML_WIDTH], L)
    k_all = _pad_rows(qk[:, ML_WIDTH:2 * ML_WIDTH] * (ML_HD ** -0.5), L)
    v_all = _pad_rows(ml_ref[0, :, 2 * ML_WIDTH:3 * ML_WIDTH], L)
    o_all = ml_ref[0, :, 3 * ML_WIDTH:4 * ML_WIDTH]

    g = gate_ref[0] + gb_ref[...]
    i_all = _pad_rows(g, L, NEG_BIG)
    lf_all = _pad_rows(-_softplus(-g), L, 0.0)
    row = _iota((L, L), 0)
    col = _iota((L, L), 1)
    causal = row >= col
    tri = jnp.where(causal, 1.0, 0.0).astype(BF16)
    b_col = _dot_exact_lhs(tri, lf_all, 3)
    b_t = b_col.T
    i_t = i_all.T

    ones = jnp.ones((L, ML_HD), BF16)
    for h in range(ML_HEADS):
        bc = b_col[:, ML_HEADS + h:ML_HEADS + h + 1]
        br = b_t[ML_HEADS + h:ML_HEADS + h + 1, :]
        ir = i_t[h:h + 1, :]
        m_prev = m_ref[:, h:h + 1]
        logd = jnp.where(causal, bc - br + ir, -jnp.inf)
        linter = bc + m_prev
        m_t = jnp.maximum(linter, jnp.max(logd, axis=1, keepdims=True))
        d = jnp.exp(logd - m_t)
        s_inter = jnp.exp(linter - m_t)
        sl = slice(h * ML_HD, (h + 1) * ML_HD)
        qh = q_all[:, sl].astype(BF16)
        k_t = k_all[:, sl].T
        vaug = jnp.concatenate([v_all[:, sl].astype(BF16), ones], axis=1)
        caug = caug_ref[h]
        s = _dot(qh, k_t.astype(BF16)) * d
        num = s_inter * _dot(qh, caug.astype(BF16)) + _dot(s.astype(BF16), vaug)
        den = num[:, ML_HD:2 * ML_HD]
        hh = num[:, 0:ML_HD] / jnp.maximum(jnp.abs(den), jnp.exp(-m_t))
        mu = jnp.mean(hh, axis=-1, keepdims=True)
        dv = hh - mu
        var = jnp.mean(dv * dv, axis=-1, keepdims=True)
        y = dv * lax.rsqrt(var + ML_NORM_EPS) * ng_ref[:, sl]
        out_ref[0, :, sl] = y[0:tr] * _sigmoid(o_all[:, sl])
        m_new = m_t[L - 1:L, :]
        b_last = bc[L - 1:L, :]
        s_state = jnp.exp(b_last + m_prev - m_new)
        w_row = jnp.exp(b_last - br + ir - m_new)
        caug_ref[h] = s_state * caug + _dot((k_t * w_row).astype(BF16), vaug)
        m_ref[:, h:h + 1] = m_new

    @pl.when(c == nc - 1)
    def _state_out():
        for h in range(ML_HEADS):
            caug = caug_ref[h]
            c1_ref[0, h] = caug[:, 0:ML_HD]
            n1_ref[0, h:h + 1, :] = caug[:, ML_HD:2 * ML_HD].T[0:1, :]
        m1_ref[0] = m_ref[:, 0:ML_HEADS]


def _mlstm(ml, gates, c0, n0, m0, conv0, cw, cb, gb, ng, tr):
    b, t, _ = ml.shape
    nc = t // tr
    bmap = lambda i, j: (i, 0, 0)
    const2 = lambda i, j: (0, 0)
    return pl.pallas_call(
        functools.partial(_mlstm_kernel, tr=tr),
        grid=(b, nc),
        in_specs=[
            pl.BlockSpec((1, tr, ML_QKVO), lambda i, j: (i, j, 0)),
            pl.BlockSpec((1, tr, LANE), lambda i, j: (i, j, 0)),
            pl.BlockSpec((1, ML_HEADS, ML_HD, ML_HD), lambda i, j: (i, 0, 0, 0)),
            pl.BlockSpec((1, ML_HEADS, ML_HD), bmap),
            pl.BlockSpec((1, 1, ML_HEADS), bmap),
            pl.BlockSpec((1, CONV_W - 1, 2 * ML_WIDTH), bmap),
            pl.BlockSpec((CONV_W, 2 * ML_WIDTH), const2),
            pl.BlockSpec((1, 2 * ML_WIDTH), const2),
            pl.BlockSpec((1, LANE), const2),
            pl.BlockSpec((1, ML_WIDTH), const2),
        ],
        out_specs=[
            pl.BlockSpec((1, tr, ML_WIDTH), lambda i, j: (i, j, 0)),
            pl.BlockSpec((1, ML_HEADS, ML_HD, ML_HD), lambda i, j: (i, 0, 0, 0)),
            pl.BlockSpec((1, ML_HEADS, ML_HD), bmap),
            pl.BlockSpec((1, 1, ML_HEADS), bmap),
            pl.BlockSpec((1, CONV_W - 1, 2 * ML_WIDTH), bmap),
        ],
        out_shape=[
            jax.ShapeDtypeStruct((b, t, ML_WIDTH), F32),
            jax.ShapeDtypeStruct((b, ML_HEADS, ML_HD, ML_HD), F32),
            jax.ShapeDtypeStruct((b, ML_HEADS, ML_HD), F32),
            jax.ShapeDtypeStruct((b, 1, ML_HEADS), F32),
            jax.ShapeDtypeStruct((b, CONV_W - 1, 2 * ML_WIDTH), F32),
        ],
        scratch_shapes=[
            pltpu.VMEM((ML_HEADS, ML_HD, 2 * ML_HD), F32),
            pltpu.VMEM((1, LANE), F32),
            pltpu.VMEM((tr + 8, 2 * ML_WIDTH), F32),
        ],
        compiler_params=pltpu.CompilerParams(
            dimension_semantics=("arbitrary", "arbitrary"), vmem_limit_bytes=VMEM_LIMIT),
        name="mlstm",
    )(ml, gates, c0, n0, m0, conv0, cw, cb, gb, ng)


def _rwkv_kernel(c_ref, p_ref, s0_ref, mu_ref, w0_ref, wwa_ref, a0_ref, g2_ref, kk_ref, ka_ref, rk_ref,
                 gng_ref, gnb_ref, out_ref, s1_ref, s_ref, *, tr):
    L = RW_CHUNK
    W = RW_WIDTH
    GW = RW_GROUP * RW_HD
    GL = RW_GROUP * L
    NGRP = RW_HEADS // RW_GROUP
    ci = pl.program_id(1)
    nc = pl.num_programs(1)

    r2 = _iota((GL, GW), 0)
    c2 = _iota((GL, GW), 1)
    bd = (r2 // L) == (c2 // RW_HD)
    s_lower = r2 > c2
    i_lower = r2 >= c2

    @pl.when(ci == 0)
    def _init():
        for gi in range(NGRP):
            x = s0_ref[0, gi * GW:(gi + 1) * GW, :]
            s_ref[gi] = jnp.where(bd, jnp.concatenate([x] * RW_GROUP, axis=1), 0.0)

    c = c_ref[0]
    xs = c + (p_ref[0] - c) * mu_ref[...]
    r = xs[:, 0:W]
    k = xs[:, W:2 * W]
    v = xs[:, 2 * W:3 * W]
    slab = xs[:, 3 * W:3 * W + DECAY_LORA + AAA_LORA]
    gd = xs[:, 3 * W + DECAY_LORA + AAA_LORA:]
    lane = _iota(slab.shape, 1)
    t_in = jnp.where(lane < DECAY_LORA, jnp.tanh(slab), slab)
    la = _dot(t_in.astype(BF16), wwa_ref[...])
    w_log = -_softplus(-(w0_ref[...] + la[:, 0:W])) - 0.5
    lw = -jnp.exp(w_log)
    a = _sigmoid(a0_ref[...] + la[:, W:2 * W])
    g = _dot(_sigmoid(gd).astype(BF16), g2_ref[...])

    rs = _iota((W, W), 0)
    cs = _iota((W, W), 1)
    seg = jnp.where((rs // RW_HD) == (cs // RW_HD), 1.0, 0.0).astype(BF16)

    kk = k * kk_ref[...]
    kn = k * (1.0 + (a - 1.0) * ka_ref[...])
    ss = _dot_exact_rhs(kk * kk, seg, 2)
    kap = kk / jnp.maximum(jnp.sqrt(ss), 1e-12)
    bonus = _dot_exact_rhs(r * kn * rk_ref[...], seg, 2) * v

    lw = _pad_rows(lw, L)
    kap = _pad_rows(kap, L)
    kn_p = _pad_rows(kn, L)
    a_p = _pad_rows(a, L)
    v_p = _pad_rows(v, L)
    r_p = _pad_rows(r, L)

    rl = _iota((L, L), 0)
    cl_ = _iota((L, L), 1)
    tri = jnp.where(rl >= cl_, 1.0, 0.0).astype(BF16)
    cum = _dot_exact_lhs(tri, lw, 3)
    cum_last = cum[L - 1:L, :]
    e_neg = jnp.exp(-cum)
    abar = -kap * jnp.exp(cum - lw)
    btil = kap * a_p * e_neg
    ktil = kn_p * e_neg
    rbar = r_p * jnp.exp(cum)
    e_rem = jnp.exp(cum_last - cum)
    bhat = kap * a_p * e_rem
    khat = kn_p * e_rem
    g_last = jnp.exp(cum_last)

    def stack(x):
        return jnp.where(bd, jnp.concatenate([x] * RW_GROUP, axis=0), 0.0).astype(BF16)

    ys = []
    for gi in range(NGRP):
        ls = slice(gi * GW, (gi + 1) * GW)
        a_s, b_s, k_s, r_s, v_s = (stack(t[:, ls]) for t in (abar, btil, ktil, rbar, v_p))
        bh_s, kh_s = stack(bhat[:, ls]), stack(khat[:, ls])
        ar = jnp.concatenate([a_s, r_s], axis=0)
        bk = jnp.concatenate([b_s, k_s], axis=0)
        p = _dot_nt(ar, bk)
        a_ab = jnp.where(s_lower, p[0:GL, 0:GL], 0.0)
        a_ak = jnp.where(s_lower, p[0:GL, GL:2 * GL], 0.0)
        r_b = jnp.where(i_lower, p[GL:2 * GL, 0:GL], 0.0)
        r_k = jnp.where(i_lower, p[GL:2 * GL, GL:2 * GL], 0.0)
        s = s_ref[gi]
        q0 = _dot_nt(ar, s.astype(BF16))
        u = q0[0:GL] + _dot(a_ak.astype(BF16), v_s)
        npow = a_ab
        lvls = L.bit_length() - 1
        for lvl in range(lvls):
            u = u + _dot(npow.astype(BF16), u.astype(BF16))
            if lvl < lvls - 1:
                npow = _dot(npow.astype(BF16), npow.astype(BF16))
        u_b = u.astype(BF16)
        y = q0[GL:2 * GL] + _dot(r_b.astype(BF16), u_b) + _dot(r_k.astype(BF16), v_s)
        yg = y[0:L]
        for j in range(1, RW_GROUP):
            yg = yg + y[j * L:(j + 1) * L]
        ys.append(yg)
        uv_t = jnp.concatenate([u, v_s.astype(F32)], axis=0).T.astype(BF16)
        bkh = jnp.concatenate([bh_s, kh_s], axis=0)
        s_ref[gi] = s * g_last[:, ls] + _dot(uv_t, bkh)

    y_all = jnp.concatenate(ys, axis=1)[0:tr]
    inv = 1.0 / RW_HD
    mu_ = _dot_exact_rhs(y_all, seg, 2) * inv
    dy = y_all - mu_
    var = _dot_exact_rhs(dy * dy, seg, 2) * inv
    yn = dy * lax.rsqrt(var + GN_EPS) * gng_ref[...] + gnb_ref[...]
    out_ref[0] = (yn + bonus) * g

    @pl.when(ci == nc - 1)
    def _state_out():
        for gi in range(NGRP):
            s = s_ref[gi]
            f = s[:, 0:RW_HD]
            for j in range(1, RW_GROUP):
                f = f + s[:, j * RW_HD:(j + 1) * RW_HD]
            s1_ref[0, gi * GW:(gi + 1) * GW, :] = f


def _rwkv(cols, prev, s0, mu, w0, wwa, a0, g2, kk, ka, rk, gng, gnb, tr):
    b, t, _ = cols.shape
    nc = t // tr
    const2 = lambda i, j: (0, 0)
    vec = pl.BlockSpec((1, RW_WIDTH), const2)
    return pl.pallas_call(
        functools.partial(_rwkv_kernel, tr=tr),
        grid=(b, nc),
        in_specs=[
            pl.BlockSpec((1, tr, RW_COLS), lambda i, j: (i, j, 0)),
            pl.BlockSpec((1, tr, RW_COLS), lambda i, j: (i, j, 0)),
            pl.BlockSpec((1, RW_WIDTH, RW_HD), lambda i, j: (i, 0, 0)),
            pl.BlockSpec((1, RW_COLS), const2),
            vec,
            pl.BlockSpec((DECAY_LORA + AAA_LORA, 2 * RW_WIDTH), const2),
            vec,
            pl.BlockSpec((GATE_LORA, RW_WIDTH), const2),
            vec, vec, vec, vec, vec,
        ],
        out_specs=[
            pl.BlockSpec((1, tr, RW_WIDTH), lambda i, j: (i, j, 0)),
            pl.BlockSpec((1, RW_WIDTH, RW_HD), lambda i, j: (i, 0, 0)),
        ],
        out_shape=[
            jax.ShapeDtypeStruct((b, t, RW_WIDTH), F32),
            jax.ShapeDtypeStruct((b, RW_WIDTH, RW_HD), F32),
        ],
        scratch_shapes=[pltpu.VMEM((RW_HEADS // RW_GROUP, RW_GROUP * RW_HD, RW_GROUP * RW_HD), F32)],
        compiler_params=pltpu.CompilerParams(
            dimension_semantics=("arbitrary", "arbitrary"), vmem_limit_bytes=VMEM_LIMIT),
        name="rwkv",
    )(cols, prev, s0, mu, w0, wwa, a0, g2, kk, ka, rk, gng, gnb)


def _extract_top(work, idx, n):
    rank = jnp.full(work.shape, 99.0, F32)
    vals = []
    for j in range(n):
        m = jnp.max(work, axis=0, keepdims=True)
        first = jnp.min(jnp.where(work == m, idx, 1e9), axis=0, keepdims=True)
        sel = idx == first
        rank = jnp.where(sel, float(j), rank)
        work = jnp.where(sel, -jnp.inf, work)
        vals.append(m)
    return vals, rank


def _peer_kernel(x_ref, ml_ref, rw_ref, woml_ref, worw_ref, gffn_ref, gfin_ref, wqt_ref, keys_ref,
                 u_ref, vt_ref, y_ref,
                 x1_ref, ht_ref, qt_ref, rank_ref, e_ref, cnt_ref, vals_ref, act_ref, p_ref, acc_ref,
                 *, tm, ec):
    e = pl.program_id(1)
    ne = pl.num_programs(1)
    ng = tm // LANE
    K = PEER_TOPK
    nslab = ec // N_KEYS

    @pl.when(e == 0)
    def _select():
        x1 = (x_ref[...] + _dot(ml_ref[...].astype(BF16), woml_ref[...])
              + _dot(rw_ref[...].astype(BF16), worw_ref[...]))
        x1_ref[...] = x1
        h = x1 * lax.rsqrt(jnp.mean(x1 * x1, axis=-1, keepdims=True) + NORM_EPS) * gffn_ref[...]
        ht = h.T.astype(BF16)
        ht_ref[...] = ht
        qt_ref[...] = _dot(wqt_ref[...], ht)
        acc_ref[...] = jnp.zeros(acc_ref.shape, F32)

        def score_body(hp, carry):
            q = qt_ref[pl.ds(pl.multiple_of(hp * PEER_HALF, PEER_HALF), PEER_HALF), :]
            e_ref[hp] = _dot(keys_ref[hp], q.astype(BF16))
            return carry

        lax.fori_loop(0, 2 * PEER_HEADS, score_body, 0)

        key_idx = _iota((N_KEYS, LANE), 0).astype(F32)

        def topk_body(i, carry):
            hp = i // ng
            lanes = pl.ds(pl.multiple_of((i % ng) * LANE, LANE), LANE)
            s = e_ref[hp, :, lanes]
            vals, rank = _extract_top(s, key_idx, K)
            rank_ref[hp, :, lanes] = rank
            e_ref[hp, :, lanes] = jnp.exp(s - vals[0])
            vals_ref[hp, :, lanes] = jnp.concatenate(vals, axis=0)
            return carry

        lax.fori_loop(0, 2 * PEER_HEADS * ng, topk_body, 0)

        sub8 = _iota((8, LANE), 0)

        def cand_body(i, carry):
            h = i // ng
            lanes = pl.ds(pl.multiple_of((i % ng) * LANE, LANE), LANE)
            v1 = vals_ref[2 * h, :, lanes]
            v2 = vals_ref[2 * h + 1, :, lanes]
            slabs, idxs, slab_a = [], [], []
            for a_i in range(K):
                nb = K // (a_i + 1)
                for b0 in range(0, nb, 8):
                    rows = v1[a_i:a_i + 1, :] + v2[b0:b0 + 8, :]
                    bidx = sub8 + b0
                    slabs.append(jnp.where(bidx < nb, rows, -jnp.inf))
                    idxs.append((bidx + a_i * K).astype(F32))
                    slab_a.append(a_i)
            work = jnp.concatenate(slabs, axis=0)
            idx = jnp.concatenate(idxs, axis=0)
            top = v1[0:1, :] + v2[0:1, :]
            vals, rank = _extract_top(work, idx, K)
            z = jnp.exp(vals[0] - top)
            for j in range(1, K):
                z = z + jnp.exp(vals[j] - top)
            picked = jnp.where(rank < float(K), 1.0, 0.0)
            r1 = rank_ref[2 * h, :, lanes]
            cnt = jnp.zeros((N_KEYS, LANE), F32)
            for a_i in range(K):
                ca = None
                for si, sa in enumerate(slab_a):
                    if sa == a_i:
                        part = jnp.sum(picked[si * 8:(si + 1) * 8], axis=0, keepdims=True)
                        ca = part if ca is None else ca + part
                cnt = cnt + jnp.where(r1 == float(a_i), ca, 0.0)
            cnt_ref[h, :, lanes] = cnt
            e1 = e_ref[2 * h, :, lanes]
            e_ref[2 * h, :, lanes] = jnp.where(r1 < float(K), e1 / z, 0.0)
            return carry

        lax.fori_loop(0, PEER_HEADS * ng, cand_body, 0)

    act_ref[...] = _dot(u_ref[...], ht_ref[...])

    key1 = pl.ds(pl.multiple_of(e * nslab, nslab), nslab)

    def group_body(gi, carry):
        lanes = pl.ds(pl.multiple_of(gi * LANE, LANE), LANE)
        cnt8 = [cnt_ref[h, key1, lanes] for h in range(PEER_HEADS)]
        gate8 = [e_ref[2 * h, key1, lanes] for h in range(PEER_HEADS)]
        for j in range(nslab):
            w = jnp.zeros((N_KEYS, LANE), F32)
            for h in range(PEER_HEADS):
                hit = rank_ref[2 * h + 1, :, lanes] < cnt8[h][j:j + 1, :]
                w = w + jnp.where(hit, e_ref[2 * h + 1, :, lanes], 0.0) * gate8[h][j:j + 1, :]
            act = act_ref[j * N_KEYS:(j + 1) * N_KEYS, lanes]
            gelu = 0.5 * act * (1.0 + lax.erf(act * 0.7071067811865476))
            p_ref[j * N_KEYS:(j + 1) * N_KEYS, lanes] = (w * gelu).astype(BF16)
        return carry

    lax.fori_loop(0, ng, group_body, 0)
    acc_ref[...] += _dot(vt_ref[...], p_ref[...])

    @pl.when(e == ne - 1)
    def _finish():
        x2 = x1_ref[...] + acc_ref[...].T
        y_ref[...] = x2 * lax.rsqrt(jnp.mean(x2 * x2, axis=-1, keepdims=True) + NORM_EPS) * gfin_ref[...]


def _peer(x2d, ml2d, rw2d, woml, worw, gffn, gfin, wqt, keys, u, vt, tm, ec):
    n = x2d.shape[0]
    n_exp = u.shape[0]
    assert ec == 8 * N_KEYS and tm % LANE == 0 and n % tm == 0 and n_exp % ec == 0
    tok = lambda i, e: (i, 0)
    const2 = lambda i, e: (0, 0)
    qrows = 2 * PEER_HEADS * PEER_HALF
    return pl.pallas_call(
        functools.partial(_peer_kernel, tm=tm, ec=ec),
        grid=(n // tm, n_exp // ec),
        in_specs=[
            pl.BlockSpec((tm, D_MODEL), tok),
            pl.BlockSpec((tm, ML_WIDTH), tok),
            pl.BlockSpec((tm, RW_WIDTH), tok),
            pl.BlockSpec((ML_WIDTH, D_MODEL), const2),
            pl.BlockSpec((RW_WIDTH, D_MODEL), const2),
            pl.BlockSpec((1, D_MODEL), const2),
            pl.BlockSpec((1, D_MODEL), const2),
            pl.BlockSpec((qrows, D_MODEL), const2),
            pl.BlockSpec((2 * PEER_HEADS, N_KEYS, PEER_HALF), lambda i, e: (0, 0, 0)),
            pl.BlockSpec((ec, D_MODEL), lambda i, e: (e, 0)),
            pl.BlockSpec((D_MODEL, ec), lambda i, e: (0, e)),
        ],
        out_specs=pl.BlockSpec((tm, D_MODEL), tok),
        out_shape=jax.ShapeDtypeStruct((n, D_MODEL), F32),
        scratch_shapes=[
            pltpu.VMEM((tm, D_MODEL), F32),
            pltpu.VMEM((D_MODEL, tm), BF16),
            pltpu.VMEM((qrows, tm), F32),
            pltpu.VMEM((2 * PEER_HEADS, N_KEYS, tm), F32),
            pltpu.VMEM((2 * PEER_HEADS, N_KEYS, tm), F32),
            pltpu.VMEM((PEER_HEADS, N_KEYS, tm), F32),
            pltpu.VMEM((2 * PEER_HEADS, PEER_TOPK, tm), F32),
            pltpu.VMEM((ec, tm), F32),
            pltpu.VMEM((ec, tm), BF16),
            pltpu.VMEM((D_MODEL, tm), F32),
        ],
        compiler_params=pltpu.CompilerParams(
            dimension_semantics=("arbitrary", "arbitrary"), vmem_limit_bytes=VMEM_LIMIT),
        name="peer",
    )(x2d, ml2d, rw2d, woml, worw, gffn, gfin, wqt, keys, u, vt)


def _prep_weights(norm_mix_g, w_in, ml_conv_w, ml_conv_b, ml_b_i, ml_b_f, ml_norm_g,
                  rw_mu, rw_w0, rw_w2, rw_a0, rw_a2, rw_g2, rw_k_k, rw_k_a, rw_r_k, rw_gn_g, rw_gn_b,
                  w_out, norm_ffn_g, peer_w_q, peer_sub_keys, peer_u, peer_v, norm_final_g):
    assert w_in.shape[0] == 1, "one layer"
    w = w_in[0]
    wg = jnp.pad(w[:, ML_QKVO:ML_QKVO + 2 * ML_HEADS], ((0, 0), (0, LANE - 2 * ML_HEADS)))
    wg_hi = wg.astype(BF16)
    wg_lo = (wg - wg_hi.astype(F32)).astype(BF16)
    zeros = jnp.zeros((DECAY_LORA, RW_WIDTH), F32)
    wwa = jnp.concatenate([jnp.concatenate([rw_w2[0], zeros], axis=1),
                           jnp.concatenate([zeros, rw_a2[0]], axis=1)], axis=0)
    gate_bias = jnp.pad(jnp.concatenate([ml_b_i[0], ml_b_f[0]]), (0, LANE - 2 * ML_HEADS))[None, :]
    row = lambda a: a.reshape(1, -1)
    return dict(
        g_mix=row(norm_mix_g[0]),
        wml=w[:, 0:ML_QKVO].astype(BF16),
        wg=jnp.stack([wg_hi, wg_lo]),
        wrw=w[:, ML_QKVO + 2 * ML_HEADS:].astype(BF16),
        conv_w=ml_conv_w[0], conv_b=row(ml_conv_b[0]), gate_bias=gate_bias, ml_norm_g=row(ml_norm_g[0]),
        mu=row(rw_mu[0]), w0=row(rw_w0[0]), wwa=wwa.astype(BF16), a0=row(rw_a0[0]),
        g2=rw_g2[0].astype(BF16), k_k=row(rw_k_k[0]), k_a=row(rw_k_a[0]), r_k=row(rw_r_k[0]),
        gn_g=row(rw_gn_g[0]), gn_b=row(rw_gn_b[0]),
        wo_ml=w_out[0, 0:ML_WIDTH].astype(BF16), wo_rw=w_out[0, ML_WIDTH:].astype(BF16),
        g_ffn=row(norm_ffn_g[0]), g_fin=row(norm_final_g),
        wqt=peer_w_q[0].T.astype(BF16),
        keys=peer_sub_keys[0].reshape(2 * PEER_HEADS, N_KEYS, PEER_HALF).astype(BF16),
        u=peer_u[0].astype(BF16),
        vt=peer_v[0].T.astype(BF16),
    )


def _trunk(x, states, wp, tr_ml, tr_rw, tm_in, tm_peer, ec):
    b, t, d = x.shape
    c0, n0, m0, conv0, s0, shift0 = (s[0] for s in states)
    x2d = x.reshape(b * t, d)
    ml, gates, rw = _proj_in(x2d, wp["g_mix"], wp["wml"], wp["wg"], wp["wrw"], tm_in)
    ml3 = ml.reshape(b, t, ML_QKVO)
    rw3 = rw.reshape(b, t, RW_COLS)
    ml_out, c1, n1, m1, conv1 = _mlstm(
        ml3, gates.reshape(b, t, LANE), c0, n0, m0.reshape(b, 1, ML_HEADS), conv0,
        wp["conv_w"], wp["conv_b"], wp["gate_bias"], wp["ml_norm_g"], tr_ml)
    prev = jnp.concatenate([shift0[:, None, :], rw3[:, :-1]], axis=1)
    rw_out, s1 = _rwkv(
        rw3, prev, s0.reshape(b, RW_WIDTH, RW_HD), wp["mu"], wp["w0"], wp["wwa"], wp["a0"], wp["g2"],
        wp["k_k"], wp["k_a"], wp["r_k"], wp["gn_g"], wp["gn_b"], tr_rw)
    y = _peer(x2d, ml_out.reshape(b * t, ML_WIDTH), rw_out.reshape(b * t, RW_WIDTH),
              wp["wo_ml"], wp["wo_rw"], wp["g_ffn"], wp["g_fin"], wp["wqt"], wp["keys"],
              wp["u"], wp["vt"], tm_peer, ec)
    new_states = (c1[None], n1[None], m1.reshape(b, ML_HEADS)[None], conv1[None],
                  s1.reshape(b, RW_HEADS, RW_HD, RW_HD)[None], rw3[:, -1][None])
    return y.reshape(b, t, d), new_states


def kernel(x_prompt, x_sample, state_mlstm_C, state_mlstm_n, state_mlstm_m, state_mlstm_conv, state_rwkv_S, state_rwkv_shift, norm_mix_g, w_in, ml_conv_w, ml_conv_b, ml_b_i, ml_b_f, ml_norm_g, rw_mu, rw_w0, rw_w2, rw_a0, rw_a2, rw_g2, rw_k_k, rw_k_a, rw_r_k, rw_gn_g, rw_gn_b, w_out, norm_ffn_g, peer_w_q, peer_sub_keys, peer_u, peer_v, norm_final_g):
    wp = _prep_weights(norm_mix_g, w_in, ml_conv_w, ml_conv_b, ml_b_i, ml_b_f, ml_norm_g,
                       rw_mu, rw_w0, rw_w2, rw_a0, rw_a2, rw_g2, rw_k_k, rw_k_a, rw_r_k, rw_gn_g, rw_gn_b,
                       w_out, norm_ffn_g, peer_w_q, peer_sub_keys, peer_u, peer_v, norm_final_g)
    bp = x_prompt.shape[0]
    z = lambda *s: jnp.zeros((1, bp) + s, F32)
    prompt_states = (z(ML_HEADS, ML_HD, ML_HD), z(ML_HEADS, ML_HD), jnp.full((1, bp, ML_HEADS), M_INIT, F32),
                     z(CONV_W - 1, 2 * ML_WIDTH), z(RW_HEADS, RW_HD, RW_HD), z(RW_COLS))
    sample_states = (state_mlstm_C, state_mlstm_n, state_mlstm_m, state_mlstm_conv,
                     state_rwkv_S, state_rwkv_shift)
    tp, ts = x_prompt.shape[1], x_sample.shape[1]
    np_, ns = bp * tp, x_sample.shape[0] * ts
    y_p, st_p = _trunk(x_prompt, prompt_states, wp, min(tp, ML_CHUNK), min(tp, RW_CHUNK),
                       min(np_, PROJ_TILE), min(np_, PEER_TILE), PEER_EXPERT_CHUNK)
    y_s, st_s = _trunk(x_sample, sample_states, wp, min(ts, ML_CHUNK), min(ts, RW_CHUNK),
                       min(ns, PROJ_TILE), min(ns, PEER_TILE), PEER_EXPERT_CHUNK)
    return (y_p, y_s) + tuple(st_p) + tuple(st_s)
```

```python
import functools

import jax
import jax.numpy as jnp
from jax import lax
from jax.experimental import pallas as pl
from jax.experimental.pallas import tpu as pltpu

F32 = jnp.float32
BF16 = jnp.bfloat16

D_MODEL = 1024
ML_HEADS = 4
ML_HD = 128
ML_WIDTH = ML_HEADS * ML_HD
CONV_W = 4
RW_HEADS = 8
RW_HD = 64
RW_WIDTH = RW_HEADS * RW_HD
DECAY_LORA = 64
AAA_LORA = 64
GATE_LORA = 128
RW_COLS = 3 * RW_WIDTH + DECAY_LORA + AAA_LORA + GATE_LORA
ML_QKVO = 4 * ML_WIDTH
PEER_HEADS = 8
N_KEYS = 128
PEER_TOPK = 16
PEER_HALF = 128
NORM_EPS = 1e-6
ML_NORM_EPS = 1e-6
GN_EPS = RW_HD * 1e-5
M_INIT = -1e30
NEG_BIG = -1e30

LANE = 128
ML_CHUNK = 128
RW_CHUNK = 64
RW_GROUP = 4
PROJ_TILE = 256
PEER_TILE = 512
PEER_EXPERT_CHUNK = 1024
VMEM_LIMIT = 56 * 1024 * 1024


def _dot(a, b):
    return jnp.dot(a, b, preferred_element_type=F32)


def _dot_nt(a, b):
    return lax.dot_general(a, b, (((1,), (1,)), ((), ())), preferred_element_type=F32)


def _split_bf16(x, n):
    parts = []
    r = x
    for _ in range(n):
        p = r.astype(BF16)
        parts.append(p)
        r = r - p.astype(F32)
    return parts


def _dot_exact_lhs(mask_bf16, x, n):
    return sum(_dot(mask_bf16, p) for p in _split_bf16(x, n))


def _dot_exact_rhs(x, mask_bf16, n):
    return sum(_dot(p, mask_bf16) for p in _split_bf16(x, n))


def _sigmoid(x):
    return 1.0 / (1.0 + jnp.exp(-x))


def _softplus(x):
    return jnp.maximum(x, 0.0) + jnp.log1p(jnp.exp(-jnp.abs(x)))


def _iota(shape, dim):
    return lax.broadcasted_iota(jnp.int32, shape, dim)


def _pad_rows(x, rows, value=0.0):
    if x.shape[0] == rows:
        return x
    return jnp.concatenate([x, jnp.full((rows - x.shape[0], x.shape[1]), value, x.dtype)], axis=0)


def _proj_in_kernel(x_ref, g_ref, wml_ref, wg_ref, wrw_ref, ml_ref, gate_ref, rw_ref):
    x = x_ref[...]
    xn = x * lax.rsqrt(jnp.mean(x * x, axis=-1, keepdims=True) + NORM_EPS) * g_ref[...]
    xb = xn.astype(BF16)
    ml_ref[...] = _dot(xb, wml_ref[...])
    rw_ref[...] = _dot(xb, wrw_ref[...])
    xlo = (xn - xb.astype(F32)).astype(BF16)
    gate_ref[...] = _dot(xb, wg_ref[0]) + _dot(xlo, wg_ref[0]) + _dot(xb, wg_ref[1])


def _proj_in(x2d, g, wml, wg, wrw, tm):
    n = x2d.shape[0]
    const2 = lambda i: (0, 0)
    return pl.pallas_call(
        _proj_in_kernel,
        grid=(n // tm,),
        in_specs=[
            pl.BlockSpec((tm, D_MODEL), lambda i: (i, 0)),
            pl.BlockSpec((1, D_MODEL), const2),
            pl.BlockSpec((D_MODEL, ML_QKVO), const2),
            pl.BlockSpec((2, D_MODEL, LANE), lambda i: (0, 0, 0)),
            pl.BlockSpec((D_MODEL, RW_COLS), const2),
        ],
        out_specs=[
            pl.BlockSpec((tm, ML_QKVO), lambda i: (i, 0)),
            pl.BlockSpec((tm, LANE), lambda i: (i, 0)),
            pl.BlockSpec((tm, RW_COLS), lambda i: (i, 0)),
        ],
        out_shape=[
            jax.ShapeDtypeStruct((n, ML_QKVO), F32),
            jax.ShapeDtypeStruct((n, LANE), F32),
            jax.ShapeDtypeStruct((n, RW_COLS), F32),
        ],
        compiler_params=pltpu.CompilerParams(
            dimension_semantics=("arbitrary",), vmem_limit_bytes=VMEM_LIMIT),
        name="proj_in",
    )(x2d, g, wml, wg, wrw)


def _mlstm_kernel(ml_ref, gate_ref, c0_ref, n0_ref, m0_ref, conv0_ref, cw_ref, cb_ref, gb_ref, ng_ref,
                  out_ref, c1_ref, n1_ref, m1_ref, conv1_ref,
                  caug_ref, m_ref, ext_ref, *, tr):
    L = ML_CHUNK
    c = pl.program_id(1)
    nc = pl.num_programs(1)

    @pl.when(c == 0)
    def _init():
        for h in range(ML_HEADS):
            caug_ref[h, :, 0:ML_HD] = c0_ref[0, h]
            nrow = n0_ref[0, h:h + 1, :]
            caug_ref[h, :, ML_HD:2 * ML_HD] = jnp.broadcast_to(nrow, (ML_HD, ML_HD)).T
        m_ref[...] = jnp.zeros(m_ref.shape, F32)
        m_ref[:, 0:ML_HEADS] = m0_ref[0]
        ext_ref[5:8, :] = conv0_ref[0]

    ext_ref[8:8 + tr, :] = ml_ref[0, :, 0:2 * ML_WIDTH]
    acc = cb_ref[...] + ext_ref[5:5 + tr, :] * cw_ref[0:1, :]
    for j in range(1, CONV_W):
        acc = acc + ext_ref[5 + j:5 + j + tr, :] * cw_ref[j:j + 1, :]
    qk = acc * _sigmoid(acc)
    tail = ext_ref[tr + 5:tr + 8, :]
    ext_ref[5:8, :] = tail

    @pl.when(c == nc - 1)
    def _conv_out():
        conv1_ref[0] = tail

    q_all = _pad_rows(qk[:, 0:ML_WIDTH], L)
    k_all = _pad_rows(qk[:, ML_WIDTH:2 * ML_WIDTH] * (ML_HD ** -0.5), L)
    v_all = _pad_rows(ml_ref[0, :, 2 * ML_WIDTH:3 * ML_WIDTH], L)
    o_all = ml_ref[0, :, 3 * ML_WIDTH:4 * ML_WIDTH]

    g = gate_ref[0] + gb_ref[...]
    i_all = _pad_rows(g, L, NEG_BIG)
    lf_all = _pad_rows(-_softplus(-g), L, 0.0)
    row = _iota((L, L), 0)
    col = _iota((L, L), 1)
    causal = row >= col
    tri = jnp.where(causal, 1.0, 0.0).astype(BF16)
    b_col = _dot_exact_lhs(tri, lf_all, 3)
    b_t = b_col.T
    i_t = i_all.T

    ones = jnp.ones((L, ML_HD), BF16)
    for h in range(ML_HEADS):
        bc = b_col[:, ML_HEADS + h:ML_HEADS + h + 1]
        br = b_t[ML_HEADS + h:ML_HEADS + h + 1, :]
        ir = i_t[h:h + 1, :]
        m_prev = m_ref[:, h:h + 1]
        logd = jnp.where(causal, bc - br + ir, -jnp.inf)
        linter = bc + m_prev
        m_t = jnp.maximum(linter, jnp.max(logd, axis=1, keepdims=True))
        d = jnp.exp(logd - m_t)
        s_inter = jnp.exp(linter - m_t)
        sl = slice(h * ML_HD, (h + 1) * ML_HD)
        qh = q_all[:, sl].astype(BF16)
        k_t = k_all[:, sl].T
        vaug = jnp.concatenate([v_all[:, sl].astype(BF16), ones], axis=1)
        caug = caug_ref[h]
        s = _dot(qh, k_t.astype(BF16)) * d
        num = s_inter * _dot(qh, caug.astype(BF16)) + _dot(s.astype(BF16), vaug)
        den = num[:, ML_HD:2 * ML_HD]
        hh = num[:, 0:ML_HD] / jnp.maximum(jnp.abs(den), jnp.exp(-m_t))
        mu = jnp.mean(hh, axis=-1, keepdims=True)
        dv = hh - mu
        var = jnp.mean(dv * dv, axis=-1, keepdims=True)
        y = dv * lax.rsqrt(var + ML_NORM_EPS) * ng_ref[:, sl]
        out_ref[0, :, sl] = y[0:tr] * _sigmoid(o_all[:, sl])
        m_new = m_t[L - 1:L, :]
        b_last = bc[L - 1:L, :]
        s_state = jnp.exp(b_last + m_prev - m_new)
        w_row = jnp.exp(b_last - br + ir - m_new)
        caug_ref[h] = s_state * caug + _dot((k_t * w_row).astype(BF16), vaug)
        m_ref[:, h:h + 1] = m_new

    @pl.when(c == nc - 1)
    def _state_out():
        for h in range(ML_HEADS):
            caug = caug_ref[h]
            c1_ref[0, h] = caug[:, 0:ML_HD]
            n1_ref[0, h:h + 1, :] = caug[:, ML_HD:2 * ML_HD].T[0:1, :]
        m1_ref[0] = m_ref[:, 0:ML_HEADS]


def _mlstm(ml, gates, c0, n0, m0, conv0, cw, cb, gb, ng, tr):
    b, t, _ = ml.shape
    nc = t // tr
    bmap = lambda i, j: (i, 0, 0)
    const2 = lambda i, j: (0, 0)
    return pl.pallas_call(
        functools.partial(_mlstm_kernel, tr=tr),
        grid=(b, nc),
        in_specs=[
            pl.BlockSpec((1, tr, ML_QKVO), lambda i, j: (i, j, 0)),
            pl.BlockSpec((1, tr, LANE), lambda i, j: (i, j, 0)),
            pl.BlockSpec((1, ML_HEADS, ML_HD, ML_HD), lambda i, j: (i, 0, 0, 0)),
            pl.BlockSpec((1, ML_HEADS, ML_HD), bmap),
            pl.BlockSpec((1, 1, ML_HEADS), bmap),
            pl.BlockSpec((1, CONV_W - 1, 2 * ML_WIDTH), bmap),
            pl.BlockSpec((CONV_W, 2 * ML_WIDTH), const2),
            pl.BlockSpec((1, 2 * ML_WIDTH), const2),
            pl.BlockSpec((1, LANE), const2),
            pl.BlockSpec((1, ML_WIDTH), const2),
        ],
        out_specs=[
            pl.BlockSpec((1, tr, ML_WIDTH), lambda i, j: (i, j, 0)),
            pl.BlockSpec((1, ML_HEADS, ML_HD, ML_HD), lambda i, j: (i, 0, 0, 0)),
            pl.BlockSpec((1, ML_HEADS, ML_HD), bmap),
            pl.BlockSpec((1, 1, ML_HEADS), bmap),
            pl.BlockSpec((1, CONV_W - 1, 2 * ML_WIDTH), bmap),
        ],
        out_shape=[
            jax.ShapeDtypeStruct((b, t, ML_WIDTH), F32),
            jax.ShapeDtypeStruct((b, ML_HEADS, ML_HD, ML_HD), F32),
            jax.ShapeDtypeStruct((b, ML_HEADS, ML_HD), F32),
            jax.ShapeDtypeStruct((b, 1, ML_HEADS), F32),
            jax.ShapeDtypeStruct((b, CONV_W - 1, 2 * ML_WIDTH), F32),
        ],
        scratch_shapes=[
            pltpu.VMEM((ML_HEADS, ML_HD, 2 * ML_HD), F32),
            pltpu.VMEM((1, LANE), F32),
            pltpu.VMEM((tr + 8, 2 * ML_WIDTH), F32),
        ],
        compiler_params=pltpu.CompilerParams(
            dimension_semantics=("arbitrary", "arbitrary"), vmem_limit_bytes=VMEM_LIMIT),
        name="mlstm",
    )(ml, gates, c0, n0, m0, conv0, cw, cb, gb, ng)


def _rwkv_kernel(c_ref, p_ref, s0_ref, mu_ref, w0_ref, wwa_ref, a0_ref, g2_ref, kk_ref, ka_ref, rk_ref,
                 gng_ref, gnb_ref, out_ref, s1_ref, s_ref, *, tr):
    L = RW_CHUNK
    W = RW_WIDTH
    GW = RW_GROUP * RW_HD
    GL = RW_GROUP * L
    NGRP = RW_HEADS // RW_GROUP
    ci = pl.program_id(1)
    nc = pl.num_programs(1)

    r2 = _iota((GL, GW), 0)
    c2 = _iota((GL, GW), 1)
    bd = (r2 // L) == (c2 // RW_HD)
    s_lower = r2 > c2
    i_lower = r2 >= c2

    @pl.when(ci == 0)
    def _init():
        for gi in range(NGRP):
            x = s0_ref[0, gi * GW:(gi + 1) * GW, :]
            s_ref[gi] = jnp.where(bd, jnp.concatenate([x] * RW_GROUP, axis=1), 0.0)

    c = c_ref[0]
    xs = c + (p_ref[0] - c) * mu_ref[...]
    r = xs[:, 0:W]
    k = xs[:, W:2 * W]
    v = xs[:, 2 * W:3 * W]
    slab = xs[:, 3 * W:3 * W + DECAY_LORA + AAA_LORA]
    gd = xs[:, 3 * W + DECAY_LORA + AAA_LORA:]
    lane = _iota(slab.shape, 1)
    t_in = jnp.where(lane < DECAY_LORA, jnp.tanh(slab), slab)
    la = _dot(t_in.astype(BF16), wwa_ref[...])
    w_log = -_softplus(-(w0_ref[...] + la[:, 0:W])) - 0.5
    lw = -jnp.exp(w_log)
    a = _sigmoid(a0_ref[...] + la[:, W:2 * W])
    g = _dot(_sigmoid(gd).astype(BF16), g2_ref[...])

    rs = _iota((W, W), 0)
    cs = _iota((W, W), 1)
    seg = jnp.where((rs // RW_HD) == (cs // RW_HD), 1.0, 0.0).astype(BF16)

    kk = k * kk_ref[...]
    kn = k * (1.0 + (a - 1.0) * ka_ref[...])
    ss = _dot_exact_rhs(kk * kk, seg, 2)
    kap = kk / jnp.maximum(jnp.sqrt(ss), 1e-12)
    bonus = _dot_exact_rhs(r * kn * rk_ref[...], seg, 2) * v

    lw = _pad_rows(lw, L)
    kap = _pad_rows(kap, L)
    kn_p = _pad_rows(kn, L)
    a_p = _pad_rows(a, L)
    v_p = _pad_rows(v, L)
    r_p = _pad_rows(r, L)

    rl = _iota((L, L), 0)
    cl_ = _iota((L, L), 1)
    tri = jnp.where(rl >= cl_, 1.0, 0.0).astype(BF16)
    cum = _dot_exact_lhs(tri, lw, 3)
    cum_last = cum[L - 1:L, :]
    e_neg = jnp.exp(-cum)
    abar = -kap * jnp.exp(cum - lw)
    btil = kap * a_p * e_neg
    ktil = kn_p * e_neg
    rbar = r_p * jnp.exp(cum)
    e_rem = jnp.exp(cum_last - cum)
    bhat = kap * a_p * e_rem
    khat = kn_p * e_rem
    g_last = jnp.exp(cum_last)

    def stack(x):
        return jnp.where(bd, jnp.concatenate([x] * RW_GROUP, axis=0), 0.0).astype(BF16)

    ys = []
    for gi in range(NGRP):
        ls = slice(gi * GW, (gi + 1) * GW)
        a_s, b_s, k_s, r_s, v_s = (stack(t[:, ls]) for t in (abar, btil, ktil, rbar, v_p))
        bh_s, kh_s = stack(bhat[:, ls]), stack(khat[:, ls])
        ar = jnp.concatenate([a_s, r_s], axis=0)
        bk = jnp.concatenate([b_s, k_s], axis=0)
        p = _dot_nt(ar, bk)
        a_ab = jnp.where(s_lower, p[0:GL, 0:GL], 0.0)
        a_ak = jnp.where(s_lower, p[0:GL, GL:2 * GL], 0.0)
        r_b = jnp.where(i_lower, p[GL:2 * GL, 0:GL], 0.0)
        r_k = jnp.where(i_lower, p[GL:2 * GL, GL:2 * GL], 0.0)
        s = s_ref[gi]
        q0 = _dot_nt(ar, s.astype(BF16))
        u = q0[0:GL] + _dot(a_ak.astype(BF16), v_s)
        npow = a_ab
        lvls = L.bit_length() - 1
        for lvl in range(lvls):
            u = u + _dot(npow.astype(BF16), u.astype(BF16))
            if lvl < lvls - 1:
                npow = _dot(npow.astype(BF16), npow.astype(BF16))
        u_b = u.astype(BF16)
        y = q0[GL:2 * GL] + _dot(r_b.astype(BF16), u_b) + _dot(r_k.astype(BF16), v_s)
        yg = y[0:L]
        for j in range(1, RW_GROUP):
            yg = yg + y[j * L:(j + 1) * L]
        ys.append(yg)
        uv_t = jnp.concatenate([u, v_s.astype(F32)], axis=0).T.astype(BF16)
        bkh = jnp.concatenate([bh_s, kh_s], axis=0)
        s_ref[gi] = s * g_last[:, ls] + _dot(uv_t, bkh)

    y_all = jnp.concatenate(ys, axis=1)[0:tr]
    inv = 1.0 / RW_HD
    mu_ = _dot_exact_rhs(y_all, seg, 2) * inv
    dy = y_all - mu_
    var = _dot_exact_rhs(dy * dy, seg, 2) * inv
    yn = dy * lax.rsqrt(var + GN_EPS) * gng_ref[...] + gnb_ref[...]
    out_ref[0] = (yn + bonus) * g

    @pl.when(ci == nc - 1)
    def _state_out():
        for gi in range(NGRP):
            s = s_ref[gi]
            f = s[:, 0:RW_HD]
            for j in range(1, RW_GROUP):
                f = f + s[:, j * RW_HD:(j + 1) * RW_HD]
            s1_ref[0, gi * GW:(gi + 1) * GW, :] = f


def _rwkv(cols, prev, s0, mu, w0, wwa, a0, g2, kk, ka, rk, gng, gnb, tr):
    b, t, _ = cols.shape
    nc = t // tr
    const2 = lambda i, j: (0, 0)
    vec = pl.BlockSpec((1, RW_WIDTH), const2)
    return pl.pallas_call(
        functools.partial(_rwkv_kernel, tr=tr),
        grid=(b, nc),
        in_specs=[
            pl.BlockSpec((1, tr, RW_COLS), lambda i, j: (i, j, 0)),
            pl.BlockSpec((1, tr, RW_COLS), lambda i, j: (i, j, 0)),
            pl.BlockSpec((1, RW_WIDTH, RW_HD), lambda i, j: (i, 0, 0)),
            pl.BlockSpec((1, RW_COLS), const2),
            vec,
            pl.BlockSpec((DECAY_LORA + AAA_LORA, 2 * RW_WIDTH), const2),
            vec,
            pl.BlockSpec((GATE_LORA, RW_WIDTH), const2),
            vec, vec, vec, vec, vec,
        ],
        out_specs=[
            pl.BlockSpec((1, tr, RW_WIDTH), lambda i, j: (i, j, 0)),
            pl.BlockSpec((1, RW_WIDTH, RW_HD), lambda i, j: (i, 0, 0)),
        ],
        out_shape=[
            jax.ShapeDtypeStruct((b, t, RW_WIDTH), F32),
            jax.ShapeDtypeStruct((b, RW_WIDTH, RW_HD), F32),
        ],
        scratch_shapes=[pltpu.VMEM((RW_HEADS // RW_GROUP, RW_GROUP * RW_HD, RW_GROUP * RW_HD), F32)],
        compiler_params=pltpu.CompilerParams(
            dimension_semantics=("arbitrary", "arbitrary"), vmem_limit_bytes=VMEM_LIMIT),
        name="rwkv",
    )(cols, prev, s0, mu, w0, wwa, a0, g2, kk, ka, rk, gng, gnb)


def _extract_top(work, idx, n):
    rank = jnp.full(work.shape, 99.0, F32)
    vals = []
    for j in range(n):
        m = jnp.max(work, axis=0, keepdims=True)
        first = jnp.min(jnp.where(work == m, idx, 1e9), axis=0, keepdims=True)
        sel = idx == first
        rank = jnp.where(sel, float(j), rank)
        work = jnp.where(sel, -jnp.inf, work)
        vals.append(m)
    return vals, rank


def _dup_bf16_words(x):
    bits = pltpu.bitcast(x.astype(BF16).astype(F32), jnp.uint32)
    return bits | (bits >> 16)


def _rows_from_words(word_row, rows):
    packed = pltpu.bitcast(jnp.broadcast_to(word_row, (8, word_row.shape[1])), BF16)
    return jnp.concatenate([packed] * (rows // 16), axis=0)


def _peer_kernel(x_ref, ml_ref, rw_ref, woml_ref, worw_ref, gffn_ref, gfin_ref, wqt_ref, keys_ref,
                 u_ref, vt_ref, y_ref,
                 ht_ref, qt_ref, rank_ref, e_ref, vals_ref, r2b_ref, e2b_ref, cntw_ref, gatew_ref,
                 p_ref, acc_ref, *, tm, ec):
    e = pl.program_id(1)
    ne = pl.num_programs(1)
    ng = tm // LANE
    K = PEER_TOPK
    nslab = ec // N_KEYS
    act_ref = qt_ref

    @pl.when(e == 0)
    def _select():
        x1 = (x_ref[...] + _dot(ml_ref[...].astype(BF16), woml_ref[...])
              + _dot(rw_ref[...].astype(BF16), worw_ref[...]))
        y_ref[...] = x1
        h = x1 * lax.rsqrt(jnp.mean(x1 * x1, axis=-1, keepdims=True) + NORM_EPS) * gffn_ref[...]
        ht = h.T.astype(BF16)
        ht_ref[...] = ht
        qt_ref[...] = _dot(wqt_ref[...], ht)
        acc_ref[...] = jnp.zeros(acc_ref.shape, F32)

        def score_body(hp, carry):
            q = qt_ref[pl.ds(pl.multiple_of(hp * PEER_HALF, PEER_HALF), PEER_HALF), :]
            e_ref[hp] = _dot(keys_ref[hp], q.astype(BF16))
            return carry

        lax.fori_loop(0, 2 * PEER_HEADS, score_body, 0)

        key_idx = _iota((N_KEYS, LANE), 0).astype(F32)

        def topk_body(i, carry):
            hp = i // ng
            lanes = pl.ds(pl.multiple_of((i % ng) * LANE, LANE), LANE)
            s = e_ref[hp, :, lanes]
            vals, rank = _extract_top(s, key_idx, K)
            rank_ref[hp, :, lanes] = rank
            e_ref[hp, :, lanes] = jnp.exp(s - vals[0])
            vals_ref[hp, :, lanes] = jnp.concatenate(vals, axis=0)
            return carry

        lax.fori_loop(0, 2 * PEER_HEADS * ng, topk_body, 0)

        sub8 = _iota((8, LANE), 0)

        def cand_body(i, carry):
            h = i // ng
            lanes = pl.ds(pl.multiple_of((i % ng) * LANE, LANE), LANE)
            v1 = vals_ref[2 * h, :, lanes]
            v2 = vals_ref[2 * h + 1, :, lanes]
            slabs, idxs, slab_a = [], [], []
            for a_i in range(K):
                nb = K // (a_i + 1)
                for b0 in range(0, nb, 8):
                    rows = v1[a_i:a_i + 1, :] + v2[b0:b0 + 8, :]
                    bidx = sub8 + b0
                    slabs.append(jnp.where(bidx < nb, rows, -jnp.inf))
                    idxs.append((bidx + a_i * K).astype(F32))
                    slab_a.append(a_i)
            work = jnp.concatenate(slabs, axis=0)
            idx = jnp.concatenate(idxs, axis=0)
            top = v1[0:1, :] + v2[0:1, :]
            vals, rank = _extract_top(work, idx, K)
            z = jnp.exp(vals[0] - top)
            for j in range(1, K):
                z = z + jnp.exp(vals[j] - top)
            picked = jnp.where(rank < float(K), 1.0, 0.0)
            r1 = rank_ref[2 * h, :, lanes]
            cnt = jnp.zeros((N_KEYS, LANE), F32)
            for a_i in range(K):
                ca = None
                for si, sa in enumerate(slab_a):
                    if sa == a_i:
                        part = jnp.sum(picked[si * 8:(si + 1) * 8], axis=0, keepdims=True)
                        ca = part if ca is None else ca + part
                cnt = cnt + jnp.where(r1 == float(a_i), ca, 0.0)
            cntw_ref[h, :, lanes] = _dup_bf16_words(cnt)
            e1 = e_ref[2 * h, :, lanes]
            gatew_ref[h, :, lanes] = _dup_bf16_words(jnp.where(r1 < float(K), e1 / z, 0.0))
            r2b_ref[h, :, lanes] = rank_ref[2 * h + 1, :, lanes].astype(BF16)
            e2b_ref[h, :, lanes] = e_ref[2 * h + 1, :, lanes].astype(BF16)
            return carry

        lax.fori_loop(0, PEER_HEADS * ng, cand_body, 0)

    key1 = pl.ds(pl.multiple_of(e * nslab, nslab), nslab)
    hs = nslab // 2
    hrows = hs * N_KEYS
    for half in range(2):
        act_ref[half * hrows:(half + 1) * hrows, :] = _dot(u_ref[half * hrows:(half + 1) * hrows, :], ht_ref[...])
    for half in range(2):
        for gi in range(ng):
            lanes = slice(gi * LANE, (gi + 1) * LANE)
            w = [jnp.zeros((N_KEYS, LANE), BF16) for _ in range(hs)]
            for h in range(PEER_HEADS):
                r2 = r2b_ref[h, :, lanes]
                e2 = e2b_ref[h, :, lanes]
                cw = cntw_ref[h, key1, lanes]
                gw = gatew_ref[h, key1, lanes]
                for jj in range(hs):
                    j = half * hs + jj
                    hit = r2 < _rows_from_words(cw[j:j + 1, :], N_KEYS)
                    w[jj] = w[jj] + jnp.where(hit, e2, jnp.zeros_like(e2)) * _rows_from_words(gw[j:j + 1, :], N_KEYS)
            for jj in range(hs):
                rows = slice((half * hs + jj) * N_KEYS, (half * hs + jj + 1) * N_KEYS)
                act = act_ref[rows, lanes]
                gelu = 0.5 * act * (1.0 + lax.erf(act * 0.7071067811865476))
                p_ref[rows, lanes] = w[jj] * gelu.astype(BF16)
        acc_ref[...] += _dot(vt_ref[:, half * hrows:(half + 1) * hrows],
                             p_ref[half * hrows:(half + 1) * hrows, :])

    @pl.when(e == ne - 1)
    def _finish():
        x2 = y_ref[...] + acc_ref[...].T
        y_ref[...] = x2 * lax.rsqrt(jnp.mean(x2 * x2, axis=-1, keepdims=True) + NORM_EPS) * gfin_ref[...]


def _peer(x2d, ml2d, rw2d, woml, worw, gffn, gfin, wqt, keys, u, vt, tm, ec):
    n = x2d.shape[0]
    n_exp = u.shape[0]
    assert ec == 8 * N_KEYS and tm % LANE == 0 and n % tm == 0 and n_exp % ec == 0
    tok = lambda i, e: (i, 0)
    const2 = lambda i, e: (0, 0)
    qrows = 2 * PEER_HEADS * PEER_HALF
    once = pl.Buffered(1)
    return pl.pallas_call(
        functools.partial(_peer_kernel, tm=tm, ec=ec),
        grid=(n // tm, n_exp // ec),
        in_specs=[
            pl.BlockSpec((tm, D_MODEL), tok),
            pl.BlockSpec((tm, ML_WIDTH), tok),
            pl.BlockSpec((tm, RW_WIDTH), tok),
            pl.BlockSpec((ML_WIDTH, D_MODEL), const2, pipeline_mode=once),
            pl.BlockSpec((RW_WIDTH, D_MODEL), const2, pipeline_mode=once),
            pl.BlockSpec((1, D_MODEL), const2),
            pl.BlockSpec((1, D_MODEL), const2),
            pl.BlockSpec((qrows, D_MODEL), const2, pipeline_mode=once),
            pl.BlockSpec((2 * PEER_HEADS, N_KEYS, PEER_HALF), lambda i, e: (0, 0, 0), pipeline_mode=once),
            pl.BlockSpec((ec, D_MODEL), lambda i, e: (e, 0)),
            pl.BlockSpec((D_MODEL, ec), lambda i, e: (0, e)),
        ],
        out_specs=pl.BlockSpec((tm, D_MODEL), tok),
        out_shape=jax.ShapeDtypeStruct((n, D_MODEL), F32),
        scratch_shapes=[
            pltpu.VMEM((D_MODEL, tm), BF16),
            pltpu.VMEM((qrows, tm), F32),
            pltpu.VMEM((2 * PEER_HEADS, N_KEYS, tm), F32),
            pltpu.VMEM((2 * PEER_HEADS, N_KEYS, tm), F32),
            pltpu.VMEM((2 * PEER_HEADS, PEER_TOPK, tm), F32),
            pltpu.VMEM((PEER_HEADS, N_KEYS, tm), BF16),
            pltpu.VMEM((PEER_HEADS, N_KEYS, tm), BF16),
            pltpu.VMEM((PEER_HEADS, N_KEYS, tm), jnp.uint32),
            pltpu.VMEM((PEER_HEADS, N_KEYS, tm), jnp.uint32),
            pltpu.VMEM((ec, tm), BF16),
            pltpu.VMEM((D_MODEL, tm), F32),
        ],
        compiler_params=pltpu.CompilerParams(
            dimension_semantics=("arbitrary", "arbitrary"), vmem_limit_bytes=VMEM_LIMIT),
        name="peer",
    )(x2d, ml2d, rw2d, woml, worw, gffn, gfin, wqt, keys, u, vt)


def _prep_weights(norm_mix_g, w_in, ml_conv_w, ml_conv_b, ml_b_i, ml_b_f, ml_norm_g,
                  rw_mu, rw_w0, rw_w2, rw_a0, rw_a2, rw_g2, rw_k_k, rw_k_a, rw_r_k, rw_gn_g, rw_gn_b,
                  w_out, norm_ffn_g, peer_w_q, peer_sub_keys, peer_u, peer_v, norm_final_g):
    assert w_in.shape[0] == 1, "one layer"
    w = w_in[0]
    wg = jnp.pad(w[:, ML_QKVO:ML_QKVO + 2 * ML_HEADS], ((0, 0), (0, LANE - 2 * ML_HEADS)))
    wg_hi = wg.astype(BF16)
    wg_lo = (wg - wg_hi.astype(F32)).astype(BF16)
    zeros = jnp.zeros((DECAY_LORA, RW_WIDTH), F32)
    wwa = jnp.concatenate([jnp.concatenate([rw_w2[0], zeros], axis=1),
                           jnp.concatenate([zeros, rw_a2[0]], axis=1)], axis=0)
    gate_bias = jnp.pad(jnp.concatenate([ml_b_i[0], ml_b_f[0]]), (0, LANE - 2 * ML_HEADS))[None, :]
    row = lambda a: a.reshape(1, -1)
    return dict(
        g_mix=row(norm_mix_g[0]),
        wml=w[:, 0:ML_QKVO].astype(BF16),
        wg=jnp.stack([wg_hi, wg_lo]),
        wrw=w[:, ML_QKVO + 2 * ML_HEADS:].astype(BF16),
        conv_w=ml_conv_w[0], conv_b=row(ml_conv_b[0]), gate_bias=gate_bias, ml_norm_g=row(ml_norm_g[0]),
        mu=row(rw_mu[0]), w0=row(rw_w0[0]), wwa=wwa.astype(BF16), a0=row(rw_a0[0]),
        g2=rw_g2[0].astype(BF16), k_k=row(rw_k_k[0]), k_a=row(rw_k_a[0]), r_k=row(rw_r_k[0]),
        gn_g=row(rw_gn_g[0]), gn_b=row(rw_gn_b[0]),
        wo_ml=w_out[0, 0:ML_WIDTH].astype(BF16), wo_rw=w_out[0, ML_WIDTH:].astype(BF16),
        g_ffn=row(norm_ffn_g[0]), g_fin=row(norm_final_g),
        wqt=peer_w_q[0].T.astype(BF16),
        keys=peer_sub_keys[0].reshape(2 * PEER_HEADS, N_KEYS, PEER_HALF).astype(BF16),
        u=peer_u[0].astype(BF16),
        vt=peer_v[0].T.astype(BF16),
    )


def _trunk(x, states, wp, tr_ml, tr_rw, tm_in, tm_peer, ec):
    b, t, d = x.shape
    c0, n0, m0, conv0, s0, shift0 = (s[0] for s in states)
    x2d = x.reshape(b * t, d)
    ml, gates, rw = _proj_in(x2d, wp["g_mix"], wp["wml"], wp["wg"], wp["wrw"], tm_in)
    ml3 = ml.reshape(b, t, ML_QKVO)
    rw3 = rw.reshape(b, t, RW_COLS)
    ml_out, c1, n1, m1, conv1 = _mlstm(
        ml3, gates.reshape(b, t, LANE), c0, n0, m0.reshape(b, 1, ML_HEADS), conv0,
        wp["conv_w"], wp["conv_b"], wp["gate_bias"], wp["ml_norm_g"], tr_ml)
    prev = jnp.concatenate([shift0[:, None, :], rw3[:, :-1]], axis=1)
    rw_out, s1 = _rwkv(
        rw3, prev, s0.reshape(b, RW_WIDTH, RW_HD), wp["mu"], wp["w0"], wp["wwa"], wp["a0"], wp["g2"],
        wp["k_k"], wp["k_a"], wp["r_k"], wp["gn_g"], wp["gn_b"], tr_rw)
    y = _peer(x2d, ml_out.reshape(b * t, ML_WIDTH), rw_out.reshape(b * t, RW_WIDTH),
              wp["wo_ml"], wp["wo_rw"], wp["g_ffn"], wp["g_fin"], wp["wqt"], wp["keys"],
              wp["u"], wp["vt"], tm_peer, ec)
    new_states = (c1[None], n1[None], m1.reshape(b, ML_HEADS)[None], conv1[None],
                  s1.reshape(b, RW_HEADS, RW_HD, RW_HD)[None], rw3[:, -1][None])
    return y.reshape(b, t, d), new_states


def kernel(x_prompt, x_sample, state_mlstm_C, state_mlstm_n, state_mlstm_m, state_mlstm_conv, state_rwkv_S, state_rwkv_shift, norm_mix_g, w_in, ml_conv_w, ml_conv_b, ml_b_i, ml_b_f, ml_norm_g, rw_mu, rw_w0, rw_w2, rw_a0, rw_a2, rw_g2, rw_k_k, rw_k_a, rw_r_k, rw_gn_g, rw_gn_b, w_out, norm_ffn_g, peer_w_q, peer_sub_keys, peer_u, peer_v, norm_final_g):
    wp = _prep_weights(norm_mix_g, w_in, ml_conv_w, ml_conv_b, ml_b_i, ml_b_f, ml_norm_g,
                       rw_mu, rw_w0, rw_w2, rw_a0, rw_a2, rw_g2, rw_k_k, rw_k_a, rw_r_k, rw_gn_g, rw_gn_b,
                       w_out, norm_ffn_g, peer_w_q, peer_sub_keys, peer_u, peer_v, norm_final_g)
    bp = x_prompt.shape[0]
    z = lambda *s: jnp.zeros((1, bp) + s, F32)
    prompt_states = (z(ML_HEADS, ML_HD, ML_HD), z(ML_HEADS, ML_HD), jnp.full((1, bp, ML_HEADS), M_INIT, F32),
                     z(CONV_W - 1, 2 * ML_WIDTH), z(RW_HEADS, RW_HD, RW_HD), z(RW_COLS))
    sample_states = (state_mlstm_C, state_mlstm_n, state_mlstm_m, state_mlstm_conv,
                     state_rwkv_S, state_rwkv_shift)
    tp, ts = x_prompt.shape[1], x_sample.shape[1]
    np_, ns = bp * tp, x_sample.shape[0] * ts
    y_p, st_p = _trunk(x_prompt, prompt_states, wp, min(tp, ML_CHUNK), min(tp, RW_CHUNK),
                       min(np_, PROJ_TILE), min(np_, PEER_TILE), PEER_EXPERT_CHUNK)
    y_s, st_s = _trunk(x_sample, sample_states, wp, min(ts, ML_CHUNK), min(ts, RW_CHUNK),
                       min(ns, PROJ_TILE), min(ns, PEER_TILE), PEER_EXPERT_CHUNK)
    return (y_p, y_s) + tuple(st_p) + tuple(st_s)
```

```python
import functools

import jax
import jax.numpy as jnp
from jax import lax
from jax.experimental import pallas as pl
from jax.experimental.pallas import tpu as pltpu

F32 = jnp.float32
BF16 = jnp.bfloat16

D_MODEL = 1024
ML_HEADS = 4
ML_HD = 128
ML_WIDTH = ML_HEADS * ML_HD
CONV_W = 4
RW_HEADS = 8
RW_HD = 64
RW_WIDTH = RW_HEADS * RW_HD
DECAY_LORA = 64
AAA_LORA = 64
GATE_LORA = 128
RW_COLS = 3 * RW_WIDTH + DECAY_LORA + AAA_LORA + GATE_LORA
ML_QKVO = 4 * ML_WIDTH
PEER_HEADS = 8
N_KEYS = 128
PEER_TOPK = 16
PEER_HALF = 128
NORM_EPS = 1e-6
ML_NORM_EPS = 1e-6
GN_EPS = RW_HD * 1e-5
M_INIT = -1e30
NEG_BIG = -1e30

LANE = 128
ML_CHUNK = 128
RW_CHUNK = 64
RW_GROUP = 4
PROJ_TILE = 256
PEER_TILE = 512
PEER_EXPERT_CHUNK = 1024
VMEM_LIMIT = 56 * 1024 * 1024


def _dot(a, b):
    return jnp.dot(a, b, preferred_element_type=F32)


def _dot_nt(a, b):
    return lax.dot_general(a, b, (((1,), (1,)), ((), ())), preferred_element_type=F32)


def _split_bf16(x, n):
    parts = []
    r = x
    for _ in range(n):
        p = r.astype(BF16)
        parts.append(p)
        r = r - p.astype(F32)
    return parts


def _dot_exact_lhs(mask_bf16, x, n):
    return sum(_dot(mask_bf16, p) for p in _split_bf16(x, n))


def _dot_exact_rhs(x, mask_bf16, n):
    return sum(_dot(p, mask_bf16) for p in _split_bf16(x, n))


def _sigmoid(x):
    return 1.0 / (1.0 + jnp.exp(-x))


def _softplus(x):
    return jnp.maximum(x, 0.0) + jnp.log1p(jnp.exp(-jnp.abs(x)))


def _iota(shape, dim):
    return lax.broadcasted_iota(jnp.int32, shape, dim)


def _pad_rows(x, rows, value=0.0):
    if x.shape[0] == rows:
        return x
    return jnp.concatenate([x, jnp.full((rows - x.shape[0], x.shape[1]), value, x.dtype)], axis=0)


def _proj_in_kernel(x_ref, g_ref, wml_ref, wg_ref, wrw_ref, ml_ref, gate_ref, rw_ref):
    x = x_ref[...]
    xn = x * lax.rsqrt(jnp.mean(x * x, axis=-1, keepdims=True) + NORM_EPS) * g_ref[...]
    xb = xn.astype(BF16)
    ml_ref[...] = _dot(xb, wml_ref[...])
    rw_ref[...] = _dot(xb, wrw_ref[...])
    xlo = (xn - xb.astype(F32)).astype(BF16)
    gate_ref[...] = _dot(xb, wg_ref[0]) + _dot(xlo, wg_ref[0]) + _dot(xb, wg_ref[1])


def _proj_in(x2d, g, wml, wg, wrw, tm):
    n = x2d.shape[0]
    const2 = lambda i: (0, 0)
    return pl.pallas_call(
        _proj_in_kernel,
        grid=(n // tm,),
        in_specs=[
            pl.BlockSpec((tm, D_MODEL), lambda i: (i, 0)),
            pl.BlockSpec((1, D_MODEL), const2),
            pl.BlockSpec((D_MODEL, ML_QKVO), const2),
            pl.BlockSpec((2, D_MODEL, LANE), lambda i: (0, 0, 0)),
            pl.BlockSpec((D_MODEL, RW_COLS), const2),
        ],
        out_specs=[
            pl.BlockSpec((tm, ML_QKVO), lambda i: (i, 0)),
            pl.BlockSpec((tm, LANE), lambda i: (i, 0)),
            pl.BlockSpec((tm, RW_COLS), lambda i: (i, 0)),
        ],
        out_shape=[
            jax.ShapeDtypeStruct((n, ML_QKVO), F32),
            jax.ShapeDtypeStruct((n, LANE), F32),
            jax.ShapeDtypeStruct((n, RW_COLS), F32),
        ],
        compiler_params=pltpu.CompilerParams(
            dimension_semantics=("arbitrary",), vmem_limit_bytes=VMEM_LIMIT),
        name="proj_in",
    )(x2d, g, wml, wg, wrw)


def _mlstm_kernel(ml_ref, gate_ref, c0_ref, n0_ref, m0_ref, conv0_ref, cw_ref, cb_ref, gb_ref, ng_ref,
                  out_ref, c1_ref, n1_ref, m1_ref, conv1_ref,
                  caug_ref, m_ref, ext_ref, *, tr):
    L = ML_CHUNK
    c = pl.program_id(1)
    nc = pl.num_programs(1)

    @pl.when(c == 0)
    def _init():
        for h in range(ML_HEADS):
            caug_ref[h, :, 0:ML_HD] = c0_ref[0, h]
            nrow = n0_ref[0, h:h + 1, :]
            caug_ref[h, :, ML_HD:2 * ML_HD] = jnp.broadcast_to(nrow, (ML_HD, ML_HD)).T
        m_ref[...] = jnp.zeros(m_ref.shape, F32)
        m_ref[:, 0:ML_HEADS] = m0_ref[0]
        ext_ref[5:8, :] = conv0_ref[0]

    ext_ref[8:8 + tr, :] = ml_ref[0, :, 0:2 * ML_WIDTH]
    acc = cb_ref[...] + ext_ref[5:5 + tr, :] * cw_ref[0:1, :]
    for j in range(1, CONV_W):
        acc = acc + ext_ref[5 + j:5 + j + tr, :] * cw_ref[j:j + 1, :]
    qk = acc * _sigmoid(acc)
    tail = ext_ref[tr + 5:tr + 8, :]
    ext_ref[5:8, :] = tail

    @pl.when(c == nc - 1)
    def _conv_out():
        conv1_ref[0] = tail

    q_all = _pad_rows(qk[:, 0:ML_WIDTH], L)
    k_all = _pad_rows(qk[:, ML_WIDTH:2 * ML_WIDTH] * (ML_HD ** -0.5), L)
    v_all = _pad_rows(ml_ref[0, :, 2 * ML_WIDTH:3 * ML_WIDTH], L)
    o_all = ml_ref[0, :, 3 * ML_WIDTH:4 * ML_WIDTH]

    g = gate_ref[0] + gb_ref[...]
    i_all = _pad_rows(g, L, NEG_BIG)
    lf_all = _pad_rows(-_softplus(-g), L, 0.0)
    row = _iota((L, L), 0)
    col = _iota((L, L), 1)
    causal = row >= col
    tri = jnp.where(causal, 1.0, 0.0).astype(BF16)
    b_col = _dot_exact_lhs(tri, lf_all, 3)
    b_t = b_col.T
    i_t = i_all.T

    ones = jnp.ones((L, ML_HD), BF16)
    for h in range(ML_HEADS):
        bc = b_col[:, ML_HEADS + h:ML_HEADS + h + 1]
        br = b_t[ML_HEADS + h:ML_HEADS + h + 1, :]
        ir = i_t[h:h + 1, :]
        m_prev = m_ref[:, h:h + 1]
        logd = jnp.where(causal, bc - br + ir, -jnp.inf)
        linter = bc + m_prev
        m_t = jnp.maximum(linter, jnp.max(logd, axis=1, keepdims=True))
        d = jnp.exp(logd - m_t)
        s_inter = jnp.exp(linter - m_t)
        sl = slice(h * ML_HD, (h + 1) * ML_HD)
        qh = q_all[:, sl].astype(BF16)
        k_t = k_all[:, sl].T
        vaug = jnp.concatenate([v_all[:, sl].astype(BF16), ones], axis=1)
        caug = caug_ref[h]
        s = _dot(qh, k_t.astype(BF16)) * d
        num = s_inter * _dot(qh, caug.astype(BF16)) + _dot(s.astype(BF16), vaug)
        den = num[:, ML_HD:2 * ML_HD]
        hh = num[:, 0:ML_HD] / jnp.maximum(jnp.abs(den), jnp.exp(-m_t))
        mu = jnp.mean(hh, axis=-1, keepdims=True)
        dv = hh - mu
        var = jnp.mean(dv * dv, axis=-1, keepdims=True)
        y = dv * lax.rsqrt(var + ML_NORM_EPS) * ng_ref[:, sl]
        out_ref[0, :, sl] = y[0:tr] * _sigmoid(o_all[:, sl])
        m_new = m_t[L - 1:L, :]
        b_last = bc[L - 1:L, :]
        s_state = jnp.exp(b_last + m_prev - m_new)
        w_row = jnp.exp(b_last - br + ir - m_new)
        caug_ref[h] = s_state * caug + _dot((k_t * w_row).astype(BF16), vaug)
        m_ref[:, h:h + 1] = m_new

    @pl.when(c == nc - 1)
    def _state_out():
        for h in range(ML_HEADS):
            caug = caug_ref[h]
            c1_ref[0, h] = caug[:, 0:ML_HD]
            n1_ref[0, h:h + 1, :] = caug[:, ML_HD:2 * ML_HD].T[0:1, :]
        m1_ref[0] = m_ref[:, 0:ML_HEADS]


def _mlstm(ml, gates, c0, n0, m0, conv0, cw, cb, gb, ng, tr):
    b, t, _ = ml.shape
    nc = t // tr
    bmap = lambda i, j: (i, 0, 0)
    const2 = lambda i, j: (0, 0)
    return pl.pallas_call(
        functools.partial(_mlstm_kernel, tr=tr),
        grid=(b, nc),
        in_specs=[
            pl.BlockSpec((1, tr, ML_QKVO), lambda i, j: (i, j, 0)),
            pl.BlockSpec((1, tr, LANE), lambda i, j: (i, j, 0)),
            pl.BlockSpec((1, ML_HEADS, ML_HD, ML_HD), lambda i, j: (i, 0, 0, 0)),
            pl.BlockSpec((1, ML_HEADS, ML_HD), bmap),
            pl.BlockSpec((1, 1, ML_HEADS), bmap),
            pl.BlockSpec((1, CONV_W - 1, 2 * ML_WIDTH), bmap),
            pl.BlockSpec((CONV_W, 2 * ML_WIDTH), const2),
            pl.BlockSpec((1, 2 * ML_WIDTH), const2),
            pl.BlockSpec((1, LANE), const2),
            pl.BlockSpec((1, ML_WIDTH), const2),
        ],
        out_specs=[
            pl.BlockSpec((1, tr, ML_WIDTH), lambda i, j: (i, j, 0)),
            pl.BlockSpec((1, ML_HEADS, ML_HD, ML_HD), lambda i, j: (i, 0, 0, 0)),
            pl.BlockSpec((1, ML_HEADS, ML_HD), bmap),
            pl.BlockSpec((1, 1, ML_HEADS), bmap),
            pl.BlockSpec((1, CONV_W - 1, 2 * ML_WIDTH), bmap),
        ],
        out_shape=[
            jax.ShapeDtypeStruct((b, t, ML_WIDTH), F32),
            jax.ShapeDtypeStruct((b, ML_HEADS, ML_HD, ML_HD), F32),
            jax.ShapeDtypeStruct((b, ML_HEADS, ML_HD), F32),
            jax.ShapeDtypeStruct((b, 1, ML_HEADS), F32),
            jax.ShapeDtypeStruct((b, CONV_W - 1, 2 * ML_WIDTH), F32),
        ],
        scratch_shapes=[
            pltpu.VMEM((ML_HEADS, ML_HD, 2 * ML_HD), F32),
            pltpu.VMEM((1, LANE), F32),
            pltpu.VMEM((tr + 8, 2 * ML_WIDTH), F32),
        ],
        compiler_params=pltpu.CompilerParams(
            dimension_semantics=("arbitrary", "arbitrary"), vmem_limit_bytes=VMEM_LIMIT),
        name="mlstm",
    )(ml, gates, c0, n0, m0, conv0, cw, cb, gb, ng)


def _rwkv_kernel(c_ref, p_ref, s0_ref, mu_ref, w0_ref, wwa_ref, a0_ref, g2_ref, kk_ref, ka_ref, rk_ref,
                 gng_ref, gnb_ref, out_ref, s1_ref, s_ref, *, tr):
    L = RW_CHUNK
    W = RW_WIDTH
    GW = RW_GROUP * RW_HD
    GL = RW_GROUP * L
    NGRP = RW_HEADS // RW_GROUP
    ci = pl.program_id(1)
    nc = pl.num_programs(1)

    r2 = _iota((GL, GW), 0)
    c2 = _iota((GL, GW), 1)
    bd = (r2 // L) == (c2 // RW_HD)
    s_lower = r2 > c2
    i_lower = r2 >= c2

    @pl.when(ci == 0)
    def _init():
        for gi in range(NGRP):
            x = s0_ref[0, gi * GW:(gi + 1) * GW, :]
            s_ref[gi] = jnp.where(bd, jnp.concatenate([x] * RW_GROUP, axis=1), 0.0)

    c = c_ref[0]
    xs = c + (p_ref[0] - c) * mu_ref[...]
    r = xs[:, 0:W]
    k = xs[:, W:2 * W]
    v = xs[:, 2 * W:3 * W]
    slab = xs[:, 3 * W:3 * W + DECAY_LORA + AAA_LORA]
    gd = xs[:, 3 * W + DECAY_LORA + AAA_LORA:]
    lane = _iota(slab.shape, 1)
    t_in = jnp.where(lane < DECAY_LORA, jnp.tanh(slab), slab)
    la = _dot(t_in.astype(BF16), wwa_ref[...])
    w_log = -_softplus(-(w0_ref[...] + la[:, 0:W])) - 0.5
    lw = -jnp.exp(w_log)
    a = _sigmoid(a0_ref[...] + la[:, W:2 * W])
    g = _dot(_sigmoid(gd).astype(BF16), g2_ref[...])

    rs = _iota((W, W), 0)
    cs = _iota((W, W), 1)
    seg = jnp.where((rs // RW_HD) == (cs // RW_HD), 1.0, 0.0).astype(BF16)

    kk = k * kk_ref[...]
    kn = k * (1.0 + (a - 1.0) * ka_ref[...])
    ss = _dot_exact_rhs(kk * kk, seg, 2)
    kap = kk / jnp.maximum(jnp.sqrt(ss), 1e-12)
    bonus = _dot_exact_rhs(r * kn * rk_ref[...], seg, 2) * v

    lw = _pad_rows(lw, L)
    kap = _pad_rows(kap, L)
    kn_p = _pad_rows(kn, L)
    a_p = _pad_rows(a, L)
    v_p = _pad_rows(v, L)
    r_p = _pad_rows(r, L)

    rl = _iota((L, L), 0)
    cl_ = _iota((L, L), 1)
    tri = jnp.where(rl >= cl_, 1.0, 0.0).astype(BF16)
    cum = _dot_exact_lhs(tri, lw, 3)
    cum_last = cum[L - 1:L, :]
    e_neg = jnp.exp(-cum)
    abar = -kap * jnp.exp(cum - lw)
    btil = kap * a_p * e_neg
    ktil = kn_p * e_neg
    rbar = r_p * jnp.exp(cum)
    e_rem = jnp.exp(cum_last - cum)
    bhat = kap * a_p * e_rem
    khat = kn_p * e_rem
    g_last = jnp.exp(cum_last)

    def stack(x):
        return jnp.where(bd, jnp.concatenate([x] * RW_GROUP, axis=0), 0.0).astype(BF16)

    ys = []
    for gi in range(NGRP):
        ls = slice(gi * GW, (gi + 1) * GW)
        a_s, b_s, k_s, r_s, v_s = (stack(t[:, ls]) for t in (abar, btil, ktil, rbar, v_p))
        bh_s, kh_s = stack(bhat[:, ls]), stack(khat[:, ls])
        ar = jnp.concatenate([a_s, r_s], axis=0)
        bk = jnp.concatenate([b_s, k_s], axis=0)
        p = _dot_nt(ar, bk)
        a_ab = jnp.where(s_lower, p[0:GL, 0:GL], 0.0)
        a_ak = jnp.where(s_lower, p[0:GL, GL:2 * GL], 0.0)
        r_b = jnp.where(i_lower, p[GL:2 * GL, 0:GL], 0.0)
        r_k = jnp.where(i_lower, p[GL:2 * GL, GL:2 * GL], 0.0)
        s = s_ref[gi]
        q0 = _dot_nt(ar, s.astype(BF16))
        u = q0[0:GL] + _dot(a_ak.astype(BF16), v_s)
        npow = a_ab
        lvls = L.bit_length() - 1
        for lvl in range(lvls):
            u = u + _dot(npow.astype(BF16), u.astype(BF16))
            if lvl < lvls - 1:
                npow = _dot(npow.astype(BF16), npow.astype(BF16))
        u_b = u.astype(BF16)
        y = q0[GL:2 * GL] + _dot(r_b.astype(BF16), u_b) + _dot(r_k.astype(BF16), v_s)
        yg = y[0:L]
        for j in range(1, RW_GROUP):
            yg = yg + y[j * L:(j + 1) * L]
        ys.append(yg)
        uv_t = jnp.concatenate([u, v_s.astype(F32)], axis=0).T.astype(BF16)
        bkh = jnp.concatenate([bh_s, kh_s], axis=0)
        s_ref[gi] = s * g_last[:, ls] + _dot(uv_t, bkh)

    y_all = jnp.concatenate(ys, axis=1)[0:tr]
    inv = 1.0 / RW_HD
    mu_ = _dot_exact_rhs(y_all, seg, 2) * inv
    dy = y_all - mu_
    var = _dot_exact_rhs(dy * dy, seg, 2) * inv
    yn = dy * lax.rsqrt(var + GN_EPS) * gng_ref[...] + gnb_ref[...]
    out_ref[0] = (yn + bonus) * g

    @pl.when(ci == nc - 1)
    def _state_out():
        for gi in range(NGRP):
            s = s_ref[gi]
            f = s[:, 0:RW_HD]
            for j in range(1, RW_GROUP):
                f = f + s[:, j * RW_HD:(j + 1) * RW_HD]
            s1_ref[0, gi * GW:(gi + 1) * GW, :] = f


def _rwkv(cols, prev, s0, mu, w0, wwa, a0, g2, kk, ka, rk, gng, gnb, tr):
    b, t, _ = cols.shape
    nc = t // tr
    const2 = lambda i, j: (0, 0)
    vec = pl.BlockSpec((1, RW_WIDTH), const2)
    return pl.pallas_call(
        functools.partial(_rwkv_kernel, tr=tr),
        grid=(b, nc),
        in_specs=[
            pl.BlockSpec((1, tr, RW_COLS), lambda i, j: (i, j, 0)),
            pl.BlockSpec((1, tr, RW_COLS), lambda i, j: (i, j, 0)),
            pl.BlockSpec((1, RW_WIDTH, RW_HD), lambda i, j: (i, 0, 0)),
            pl.BlockSpec((1, RW_COLS), const2),
            vec,
            pl.BlockSpec((DECAY_LORA + AAA_LORA, 2 * RW_WIDTH), const2),
            vec,
            pl.BlockSpec((GATE_LORA, RW_WIDTH), const2),
            vec, vec, vec, vec, vec,
        ],
        out_specs=[
            pl.BlockSpec((1, tr, RW_WIDTH), lambda i, j: (i, j, 0)),
            pl.BlockSpec((1, RW_WIDTH, RW_HD), lambda i, j: (i, 0, 0)),
        ],
        out_shape=[
            jax.ShapeDtypeStruct((b, t, RW_WIDTH), F32),
            jax.ShapeDtypeStruct((b, RW_WIDTH, RW_HD), F32),
        ],
        scratch_shapes=[pltpu.VMEM((RW_HEADS // RW_GROUP, RW_GROUP * RW_HD, RW_GROUP * RW_HD), F32)],
        compiler_params=pltpu.CompilerParams(
            dimension_semantics=("arbitrary", "arbitrary"), vmem_limit_bytes=VMEM_LIMIT),
        name="rwkv",
    )(cols, prev, s0, mu, w0, wwa, a0, g2, kk, ka, rk, gng, gnb)


def _extract_top(work, idx, n):
    rank = jnp.full(work.shape, 99.0, F32)
    vals = []
    for j in range(n):
        m = jnp.max(work, axis=0, keepdims=True)
        first = jnp.min(jnp.where(work == m, idx, 1e9), axis=0, keepdims=True)
        sel = idx == first
        rank = jnp.where(sel, float(j), rank)
        work = jnp.where(sel, -jnp.inf, work)
        vals.append(m)
    return vals, rank


def _dup_bf16_words(x):
    bits = pltpu.bitcast(x.astype(BF16).astype(F32), jnp.uint32)
    return bits | (bits >> 16)


def _rows_from_words(word_row, rows):
    packed = pltpu.bitcast(jnp.broadcast_to(word_row, (8, word_row.shape[1])), BF16)
    return jnp.concatenate([packed] * (rows // 16), axis=0)


def _peer_kernel(x_ref, ml_ref, rw_ref, woml_ref, worw_ref, gffn_ref, gfin_ref, wqt_ref, keys_ref,
                 ua_ref, ub_ref, vta_ref, vtb_ref, y_ref,
                 ht_ref, qt_ref, rank_ref, e_ref, vals_ref, r2b_ref, e2b_ref, cntw_ref, gatew_ref,
                 p_ref, acc_ref, *, tm, ec):
    e = pl.program_id(1)
    ne = pl.num_programs(1)
    ng = tm // LANE
    K = PEER_TOPK
    nslab = ec // N_KEYS
    act_ref = qt_ref

    @pl.when(e == 0)
    def _select():
        x1 = (x_ref[...] + _dot(ml_ref[...].astype(BF16), woml_ref[...])
              + _dot(rw_ref[...].astype(BF16), worw_ref[...]))
        y_ref[...] = x1
        h = x1 * lax.rsqrt(jnp.mean(x1 * x1, axis=-1, keepdims=True) + NORM_EPS) * gffn_ref[...]
        ht = h.T.astype(BF16)
        ht_ref[...] = ht
        qt_ref[...] = _dot(wqt_ref[...], ht)
        acc_ref[...] = jnp.zeros(acc_ref.shape, F32)

        def score_body(hp, carry):
            q = qt_ref[pl.ds(pl.multiple_of(hp * PEER_HALF, PEER_HALF), PEER_HALF), :]
            e_ref[hp] = _dot(keys_ref[hp], q.astype(BF16))
            return carry

        lax.fori_loop(0, 2 * PEER_HEADS, score_body, 0)

        key_idx = _iota((N_KEYS, LANE), 0).astype(F32)

        def topk_body(i, carry):
            hp = i // ng
            lanes = pl.ds(pl.multiple_of((i % ng) * LANE, LANE), LANE)
            s = e_ref[hp, :, lanes]
            vals, rank = _extract_top(s, key_idx, K)
            rank_ref[hp, :, lanes] = rank
            e_ref[hp, :, lanes] = jnp.exp(s - vals[0])
            vals_ref[hp, :, lanes] = jnp.concatenate(vals, axis=0)
            return carry

        lax.fori_loop(0, 2 * PEER_HEADS * ng, topk_body, 0)

        sub8 = _iota((8, LANE), 0)

        def cand_body(i, carry):
            h = i // ng
            lanes = pl.ds(pl.multiple_of((i % ng) * LANE, LANE), LANE)
            v1 = vals_ref[2 * h, :, lanes]
            v2 = vals_ref[2 * h + 1, :, lanes]
            slabs, idxs, slab_a = [], [], []
            for a_i in range(K):
                nb = K // (a_i + 1)
                for b0 in range(0, nb, 8):
                    rows = v1[a_i:a_i + 1, :] + v2[b0:b0 + 8, :]
                    bidx = sub8 + b0
                    slabs.append(jnp.where(bidx < nb, rows, -jnp.inf))
                    idxs.append((bidx + a_i * K).astype(F32))
                    slab_a.append(a_i)
            work = jnp.concatenate(slabs, axis=0)
            idx = jnp.concatenate(idxs, axis=0)
            top = v1[0:1, :] + v2[0:1, :]
            vals, rank = _extract_top(work, idx, K)
            z = jnp.exp(vals[0] - top)
            for j in range(1, K):
                z = z + jnp.exp(vals[j] - top)
            picked = jnp.where(rank < float(K), 1.0, 0.0)
            r1 = rank_ref[2 * h, :, lanes]
            cnt = jnp.zeros((N_KEYS, LANE), F32)
            for a_i in range(K):
                ca = None
                for si, sa in enumerate(slab_a):
                    if sa == a_i:
                        part = jnp.sum(picked[si * 8:(si + 1) * 8], axis=0, keepdims=True)
                        ca = part if ca is None else ca + part
                cnt = cnt + jnp.where(r1 == float(a_i), ca, 0.0)
            cntw_ref[h, :, lanes] = _dup_bf16_words(cnt)
            e1 = e_ref[2 * h, :, lanes]
            gatew_ref[h, :, lanes] = _dup_bf16_words(jnp.where(r1 < float(K), e1 / z, 0.0))
            r2b_ref[h, :, lanes] = rank_ref[2 * h + 1, :, lanes].astype(BF16)
            e2b_ref[h, :, lanes] = e_ref[2 * h + 1, :, lanes].astype(BF16)
            return carry

        lax.fori_loop(0, PEER_HEADS * ng, cand_body, 0)

    key1 = pl.ds(pl.multiple_of(e * nslab, nslab), nslab)
    hs = nslab // 2
    hrows = hs * N_KEYS
    for half in range(2):
        act_ref[half * hrows:(half + 1) * hrows, :] = _dot((ua_ref, ub_ref)[half][...], ht_ref[...])
    for half in range(2):
        for gi in range(ng):
            lanes = slice(gi * LANE, (gi + 1) * LANE)
            w = [jnp.zeros((N_KEYS, LANE), BF16) for _ in range(hs)]
            for h in range(PEER_HEADS):
                r2 = r2b_ref[h, :, lanes]
                e2 = e2b_ref[h, :, lanes]
                cw = cntw_ref[h, key1, lanes]
                gw = gatew_ref[h, key1, lanes]
                for jj in range(hs):
                    j = half * hs + jj
                    hit = r2 < _rows_from_words(cw[j:j + 1, :], N_KEYS)
                    w[jj] = w[jj] + jnp.where(hit, e2, jnp.zeros_like(e2)) * _rows_from_words(gw[j:j + 1, :], N_KEYS)
            for jj in range(hs):
                rows = slice((half * hs + jj) * N_KEYS, (half * hs + jj + 1) * N_KEYS)
                act = act_ref[rows, lanes]
                gelu = 0.5 * act * (1.0 + lax.erf(act * 0.7071067811865476))
                p_ref[rows, lanes] = w[jj] * gelu.astype(BF16)
        acc_ref[...] += _dot((vta_ref, vtb_ref)[half][0], p_ref[half * hrows:(half + 1) * hrows, :])

    @pl.when(e == ne - 1)
    def _finish():
        x2 = y_ref[...] + acc_ref[...].T
        y_ref[...] = x2 * lax.rsqrt(jnp.mean(x2 * x2, axis=-1, keepdims=True) + NORM_EPS) * gfin_ref[...]


def _peer(x2d, ml2d, rw2d, woml, worw, gffn, gfin, wqt, keys, u, vt, tm, ec):
    n = x2d.shape[0]
    n_exp = u.shape[0]
    assert ec == 8 * N_KEYS and tm % LANE == 0 and n % tm == 0 and n_exp % ec == 0
    tok = lambda i, e: (i, 0)
    const2 = lambda i, e: (0, 0)
    qrows = 2 * PEER_HEADS * PEER_HALF
    eh = ec // 2
    once = pl.Buffered(1)
    return pl.pallas_call(
        functools.partial(_peer_kernel, tm=tm, ec=ec),
        grid=(n // tm, n_exp // ec),
        in_specs=[
            pl.BlockSpec((tm, D_MODEL), tok),
            pl.BlockSpec((tm, ML_WIDTH), tok),
            pl.BlockSpec((tm, RW_WIDTH), tok),
            pl.BlockSpec((ML_WIDTH, D_MODEL), const2, pipeline_mode=once),
            pl.BlockSpec((RW_WIDTH, D_MODEL), const2, pipeline_mode=once),
            pl.BlockSpec((1, D_MODEL), const2),
            pl.BlockSpec((1, D_MODEL), const2),
            pl.BlockSpec((qrows, D_MODEL), const2, pipeline_mode=once),
            pl.BlockSpec((2 * PEER_HEADS, N_KEYS, PEER_HALF), lambda i, e: (0, 0, 0), pipeline_mode=once),
            pl.BlockSpec((eh, D_MODEL), lambda i, e: (2 * e, 0)),
            pl.BlockSpec((eh, D_MODEL), lambda i, e: (2 * e + 1, 0)),
            pl.BlockSpec((1, D_MODEL, eh), lambda i, e: (2 * e, 0, 0)),
            pl.BlockSpec((1, D_MODEL, eh), lambda i, e: (2 * e + 1, 0, 0)),
        ],
        out_specs=pl.BlockSpec((tm, D_MODEL), tok),
        out_shape=jax.ShapeDtypeStruct((n, D_MODEL), F32),
        scratch_shapes=[
            pltpu.VMEM((D_MODEL, tm), BF16),
            pltpu.VMEM((qrows, tm), F32),
            pltpu.VMEM((2 * PEER_HEADS, N_KEYS, tm), F32),
            pltpu.VMEM((2 * PEER_HEADS, N_KEYS, tm), F32),
            pltpu.VMEM((2 * PEER_HEADS, PEER_TOPK, tm), F32),
            pltpu.VMEM((PEER_HEADS, N_KEYS, tm), BF16),
            pltpu.VMEM((PEER_HEADS, N_KEYS, tm), BF16),
            pltpu.VMEM((PEER_HEADS, N_KEYS, tm), jnp.uint32),
            pltpu.VMEM((PEER_HEADS, N_KEYS, tm), jnp.uint32),
            pltpu.VMEM((ec, tm), BF16),
            pltpu.VMEM((D_MODEL, tm), F32),
        ],
        compiler_params=pltpu.CompilerParams(
            dimension_semantics=("arbitrary", "arbitrary"), vmem_limit_bytes=VMEM_LIMIT),
        name="peer",
    )(x2d, ml2d, rw2d, woml, worw, gffn, gfin, wqt, keys, u, u, vt, vt)


def _prep_weights(norm_mix_g, w_in, ml_conv_w, ml_conv_b, ml_b_i, ml_b_f, ml_norm_g,
                  rw_mu, rw_w0, rw_w2, rw_a0, rw_a2, rw_g2, rw_k_k, rw_k_a, rw_r_k, rw_gn_g, rw_gn_b,
                  w_out, norm_ffn_g, peer_w_q, peer_sub_keys, peer_u, peer_v, norm_final_g):
    assert w_in.shape[0] == 1, "one layer"
    w = w_in[0]
    wg = jnp.pad(w[:, ML_QKVO:ML_QKVO + 2 * ML_HEADS], ((0, 0), (0, LANE - 2 * ML_HEADS)))
    wg_hi = wg.astype(BF16)
    wg_lo = (wg - wg_hi.astype(F32)).astype(BF16)
    zeros = jnp.zeros((DECAY_LORA, RW_WIDTH), F32)
    wwa = jnp.concatenate([jnp.concatenate([rw_w2[0], zeros], axis=1),
                           jnp.concatenate([zeros, rw_a2[0]], axis=1)], axis=0)
    gate_bias = jnp.pad(jnp.concatenate([ml_b_i[0], ml_b_f[0]]), (0, LANE - 2 * ML_HEADS))[None, :]
    row = lambda a: a.reshape(1, -1)
    return dict(
        g_mix=row(norm_mix_g[0]),
        wml=w[:, 0:ML_QKVO].astype(BF16),
        wg=jnp.stack([wg_hi, wg_lo]),
        wrw=w[:, ML_QKVO + 2 * ML_HEADS:].astype(BF16),
        conv_w=ml_conv_w[0], conv_b=row(ml_conv_b[0]), gate_bias=gate_bias, ml_norm_g=row(ml_norm_g[0]),
        mu=row(rw_mu[0]), w0=row(rw_w0[0]), wwa=wwa.astype(BF16), a0=row(rw_a0[0]),
        g2=rw_g2[0].astype(BF16), k_k=row(rw_k_k[0]), k_a=row(rw_k_a[0]), r_k=row(rw_r_k[0]),
        gn_g=row(rw_gn_g[0]), gn_b=row(rw_gn_b[0]),
        wo_ml=w_out[0, 0:ML_WIDTH].astype(BF16), wo_rw=w_out[0, ML_WIDTH:].astype(BF16),
        g_ffn=row(norm_ffn_g[0]), g_fin=row(norm_final_g),
        wqt=peer_w_q[0].T.astype(BF16),
        keys=peer_sub_keys[0].reshape(2 * PEER_HEADS, N_KEYS, PEER_HALF).astype(BF16),
        u=peer_u[0].astype(BF16),
        vt=peer_v[0].reshape(-1, PEER_EXPERT_CHUNK // 2, D_MODEL).transpose(0, 2, 1).astype(BF16),
    )


def _trunk(x, states, wp, tr_ml, tr_rw, tm_in, tm_peer, ec):
    b, t, d = x.shape
    c0, n0, m0, conv0, s0, shift0 = (s[0] for s in states)
    x2d = x.reshape(b * t, d)
    ml, gates, rw = _proj_in(x2d, wp["g_mix"], wp["wml"], wp["wg"], wp["wrw"], tm_in)
    ml3 = ml.reshape(b, t, ML_QKVO)
    rw3 = rw.reshape(b, t, RW_COLS)
    ml_out, c1, n1, m1, conv1 = _mlstm(
        ml3, gates.reshape(b, t, LANE), c0, n0, m0.reshape(b, 1, ML_HEADS), conv0,
        wp["conv_w"], wp["conv_b"], wp["gate_bias"], wp["ml_norm_g"], tr_ml)
    prev = jnp.concatenate([shift0[:, None, :], rw3[:, :-1]], axis=1)
    rw_out, s1 = _rwkv(
        rw3, prev, s0.reshape(b, RW_WIDTH, RW_HD), wp["mu"], wp["w0"], wp["wwa"], wp["a0"], wp["g2"],
        wp["k_k"], wp["k_a"], wp["r_k"], wp["gn_g"], wp["gn_b"], tr_rw)
    y = _peer(x2d, ml_out.reshape(b * t, ML_WIDTH), rw_out.reshape(b * t, RW_WIDTH),
              wp["wo_ml"], wp["wo_rw"], wp["g_ffn"], wp["g_fin"], wp["wqt"], wp["keys"],
              wp["u"], wp["vt"], tm_peer, ec)
    new_states = (c1[None], n1[None], m1.reshape(b, ML_HEADS)[None], conv1[None],
                  s1.reshape(b, RW_HEADS, RW_HD, RW_HD)[None], rw3[:, -1][None])
    return y.reshape(b, t, d), new_states


def kernel(x_prompt, x_sample, state_mlstm_C, state_mlstm_n, state_mlstm_m, state_mlstm_conv, state_rwkv_S, state_rwkv_shift, norm_mix_g, w_in, ml_conv_w, ml_conv_b, ml_b_i, ml_b_f, ml_norm_g, rw_mu, rw_w0, rw_w2, rw_a0, rw_a2, rw_g2, rw_k_k, rw_k_a, rw_r_k, rw_gn_g, rw_gn_b, w_out, norm_ffn_g, peer_w_q, peer_sub_keys, peer_u, peer_v, norm_final_g):
    wp = _prep_weights(norm_mix_g, w_in, ml_conv_w, ml_conv_b, ml_b_i, ml_b_f, ml_norm_g,
                       rw_mu, rw_w0, rw_w2, rw_a0, rw_a2, rw_g2, rw_k_k, rw_k_a, rw_r_k, rw_gn_g, rw_gn_b,
                       w_out, norm_ffn_g, peer_w_q, peer_sub_keys, peer_u, peer_v, norm_final_g)
    bp = x_prompt.shape[0]
    z = lambda *s: jnp.zeros((1, bp) + s, F32)
    prompt_states = (z(ML_HEADS, ML_HD, ML_HD), z(ML_HEADS, ML_HD), jnp.full((1, bp, ML_HEADS), M_INIT, F32),
                     z(CONV_W - 1, 2 * ML_WIDTH), z(RW_HEADS, RW_HD, RW_HD), z(RW_COLS))
    sample_states = (state_mlstm_C, state_mlstm_n, state_mlstm_m, state_mlstm_conv,
                     state_rwkv_S, state_rwkv_shift)
    tp, ts = x_prompt.shape[1], x_sample.shape[1]
    np_, ns = bp * tp, x_sample.shape[0] * ts
    y_p, st_p = _trunk(x_prompt, prompt_states, wp, min(tp, ML_CHUNK), min(tp, RW_CHUNK),
                       min(np_, PROJ_TILE), min(np_, PEER_TILE), PEER_EXPERT_CHUNK)
    y_s, st_s = _trunk(x_sample, sample_states, wp, min(ts, ML_CHUNK), min(ts, RW_CHUNK),
                       min(ns, PROJ_TILE), min(ns, PEER_TILE), PEER_EXPERT_CHUNK)
    return (y_p, y_s) + tuple(st_p) + tuple(st_s)
```

```python
import functools

import jax
import jax.numpy as jnp
from jax import lax
from jax.experimental import pallas as pl
from jax.experimental.pallas import tpu as pltpu

F32 = jnp.float32
BF16 = jnp.bfloat16

D_MODEL = 1024
ML_HEADS = 4
ML_HD = 128
ML_WIDTH = ML_HEADS * ML_HD
CONV_W = 4
RW_HEADS = 8
RW_HD = 64
RW_WIDTH = RW_HEADS * RW_HD
DECAY_LORA = 64
AAA_LORA = 64
GATE_LORA = 128
RW_COLS = 3 * RW_WIDTH + DECAY_LORA + AAA_LORA + GATE_LORA
ML_QKVO = 4 * ML_WIDTH
PEER_HEADS = 8
N_KEYS = 128
PEER_TOPK = 16
PEER_HALF = 128
NORM_EPS = 1e-6
ML_NORM_EPS = 1e-6
GN_EPS = RW_HD * 1e-5
M_INIT = -1e30
NEG_BIG = -1e30

LANE = 128
ML_CHUNK = 128
RW_CHUNK = 64
RW_GROUP = 4
PROJ_TILE = 256
PEER_TILE = 512
PEER_EXPERT_CHUNK = 1024
VMEM_LIMIT = 56 * 1024 * 1024


def _dot(a, b):
    return jnp.dot(a, b, preferred_element_type=F32)


def _dot_nt(a, b):
    return lax.dot_general(a, b, (((1,), (1,)), ((), ())), preferred_element_type=F32)


def _split_bf16(x, n):
    parts = []
    r = x
    for _ in range(n):
        p = r.astype(BF16)
        parts.append(p)
        r = r - p.astype(F32)
    return parts


def _dot_exact_lhs(mask_bf16, x, n):
    return sum(_dot(mask_bf16, p) for p in _split_bf16(x, n))


def _dot_exact_rhs(x, mask_bf16, n):
    return sum(_dot(p, mask_bf16) for p in _split_bf16(x, n))


def _sigmoid(x):
    return 1.0 / (1.0 + jnp.exp(-x))


def _softplus(x):
    return jnp.maximum(x, 0.0) + jnp.log1p(jnp.exp(-jnp.abs(x)))


def _iota(shape, dim):
    return lax.broadcasted_iota(jnp.int32, shape, dim)


def _pad_rows(x, rows, value=0.0):
    if x.shape[0] == rows:
        return x
    return jnp.concatenate([x, jnp.full((rows - x.shape[0], x.shape[1]), value, x.dtype)], axis=0)


def _proj_in_kernel(x_ref, g_ref, wml_ref, wg_ref, wrw_ref, ml_ref, gate_ref, rw_ref):
    x = x_ref[...]
    xn = x * lax.rsqrt(jnp.mean(x * x, axis=-1, keepdims=True) + NORM_EPS) * g_ref[...]
    xb = xn.astype(BF16)
    ml_ref[...] = _dot(xb, wml_ref[...])
    rw_ref[...] = _dot(xb, wrw_ref[...])
    xlo = (xn - xb.astype(F32)).astype(BF16)
    gate_ref[...] = _dot(xb, wg_ref[0]) + _dot(xlo, wg_ref[0]) + _dot(xb, wg_ref[1])


def _proj_in(x2d, g, wml, wg, wrw, tm):
    n = x2d.shape[0]
    const2 = lambda i: (0, 0)
    return pl.pallas_call(
        _proj_in_kernel,
        grid=(n // tm,),
        in_specs=[
            pl.BlockSpec((tm, D_MODEL), lambda i: (i, 0)),
            pl.BlockSpec((1, D_MODEL), const2),
            pl.BlockSpec((D_MODEL, ML_QKVO), const2),
            pl.BlockSpec((2, D_MODEL, LANE), lambda i: (0, 0, 0)),
            pl.BlockSpec((D_MODEL, RW_COLS), const2),
        ],
        out_specs=[
            pl.BlockSpec((tm, ML_QKVO), lambda i: (i, 0)),
            pl.BlockSpec((tm, LANE), lambda i: (i, 0)),
            pl.BlockSpec((tm, RW_COLS), lambda i: (i, 0)),
        ],
        out_shape=[
            jax.ShapeDtypeStruct((n, ML_QKVO), F32),
            jax.ShapeDtypeStruct((n, LANE), F32),
            jax.ShapeDtypeStruct((n, RW_COLS), F32),
        ],
        compiler_params=pltpu.CompilerParams(
            dimension_semantics=("arbitrary",), vmem_limit_bytes=VMEM_LIMIT),
        name="proj_in",
    )(x2d, g, wml, wg, wrw)


def _mlstm_kernel(ml_ref, gate_ref, c0_ref, n0_ref, m0_ref, conv0_ref, cw_ref, cb_ref, gb_ref, ng_ref,
                  out_ref, c1_ref, n1_ref, m1_ref, conv1_ref,
                  caug_ref, m_ref, ext_ref, *, tr):
    L = ML_CHUNK
    c = pl.program_id(1)
    nc = pl.num_programs(1)

    @pl.when(c == 0)
    def _init():
        for h in range(ML_HEADS):
            caug_ref[h, :, 0:ML_HD] = c0_ref[0, h]
            nrow = n0_ref[0, h:h + 1, :]
            caug_ref[h, :, ML_HD:2 * ML_HD] = jnp.broadcast_to(nrow, (ML_HD, ML_HD)).T
        m_ref[...] = jnp.zeros(m_ref.shape, F32)
        m_ref[:, 0:ML_HEADS] = m0_ref[0]
        ext_ref[5:8, :] = conv0_ref[0]

    ext_ref[8:8 + tr, :] = ml_ref[0, :, 0:2 * ML_WIDTH]
    acc = cb_ref[...] + ext_ref[5:5 + tr, :] * cw_ref[0:1, :]
    for j in range(1, CONV_W):
        acc = acc + ext_ref[5 + j:5 + j + tr, :] * cw_ref[j:j + 1, :]
    qk = acc * _sigmoid(acc)
    tail = ext_ref[tr + 5:tr + 8, :]
    ext_ref[5:8, :] = tail

    @pl.when(c == nc - 1)
    def _conv_out():
        conv1_ref[0] = tail

    q_all = _pad_rows(qk[:, 0:ML_WIDTH], L)
    k_all = _pad_rows(qk[:, ML_WIDTH:2 * ML_WIDTH] * (ML_HD ** -0.5), L)
    v_all = _pad_rows(ml_ref[0, :, 2 * ML_WIDTH:3 * ML_WIDTH], L)
    o_all = ml_ref[0, :, 3 * ML_WIDTH:4 * ML_WIDTH]

    g = gate_ref[0] + gb_ref[...]
    i_all = _pad_rows(g, L, NEG_BIG)
    lf_all = _pad_rows(-_softplus(-g), L, 0.0)
    row = _iota((L, L), 0)
    col = _iota((L, L), 1)
    causal = row >= col
    tri = jnp.where(causal, 1.0, 0.0).astype(BF16)
    b_col = _dot_exact_lhs(tri, lf_all, 3)
    b_t = b_col.T
    i_t = i_all.T

    ones = jnp.ones((L, ML_HD), BF16)
    for h in range(ML_HEADS):
        bc = b_col[:, ML_HEADS + h:ML_HEADS + h + 1]
        br = b_t[ML_HEADS + h:ML_HEADS + h + 1, :]
        ir = i_t[h:h + 1, :]
        m_prev = m_ref[:, h:h + 1]
        logd = jnp.where(causal, bc - br + ir, -jnp.inf)
        linter = bc + m_prev
        m_t = jnp.maximum(linter, jnp.max(logd, axis=1, keepdims=True))
        d = jnp.exp(logd - m_t)
        s_inter = jnp.exp(linter - m_t)
        sl = slice(h * ML_HD, (h + 1) * ML_HD)
        qh = q_all[:, sl].astype(BF16)
        k_t = k_all[:, sl].T
        vaug = jnp.concatenate([v_all[:, sl].astype(BF16), ones], axis=1)
        caug = caug_ref[h]
        s = _dot(qh, k_t.astype(BF16)) * d
        num = s_inter * _dot(qh, caug.astype(BF16)) + _dot(s.astype(BF16), vaug)
        den = num[:, ML_HD:2 * ML_HD]
        hh = num[:, 0:ML_HD] / jnp.maximum(jnp.abs(den), jnp.exp(-m_t))
        mu = jnp.mean(hh, axis=-1, keepdims=True)
        dv = hh - mu
        var = jnp.mean(dv * dv, axis=-1, keepdims=True)
        y = dv * lax.rsqrt(var + ML_NORM_EPS) * ng_ref[:, sl]
        out_ref[0, :, sl] = y[0:tr] * _sigmoid(o_all[:, sl])
        m_new = m_t[L - 1:L, :]
        b_last = bc[L - 1:L, :]
        s_state = jnp.exp(b_last + m_prev - m_new)
        w_row = jnp.exp(b_last - br + ir - m_new)
        caug_ref[h] = s_state * caug + _dot((k_t * w_row).astype(BF16), vaug)
        m_ref[:, h:h + 1] = m_new

    @pl.when(c == nc - 1)
    def _state_out():
        for h in range(ML_HEADS):
            caug = caug_ref[h]
            c1_ref[0, h] = caug[:, 0:ML_HD]
            n1_ref[0, h:h + 1, :] = caug[:, ML_HD:2 * ML_HD].T[0:1, :]
        m1_ref[0] = m_ref[:, 0:ML_HEADS]


def _mlstm(ml, gates, c0, n0, m0, conv0, cw, cb, gb, ng, tr):
    b, t, _ = ml.shape
    nc = t // tr
    bmap = lambda i, j: (i, 0, 0)
    const2 = lambda i, j: (0, 0)
    return pl.pallas_call(
        functools.partial(_mlstm_kernel, tr=tr),
        grid=(b, nc),
        in_specs=[
            pl.BlockSpec((1, tr, ML_QKVO), lambda i, j: (i, j, 0)),
            pl.BlockSpec((1, tr, LANE), lambda i, j: (i, j, 0)),
            pl.BlockSpec((1, ML_HEADS, ML_HD, ML_HD), lambda i, j: (i, 0, 0, 0)),
            pl.BlockSpec((1, ML_HEADS, ML_HD), bmap),
            pl.BlockSpec((1, 1, ML_HEADS), bmap),
            pl.BlockSpec((1, CONV_W - 1, 2 * ML_WIDTH), bmap),
            pl.BlockSpec((CONV_W, 2 * ML_WIDTH), const2),
            pl.BlockSpec((1, 2 * ML_WIDTH), const2),
            pl.BlockSpec((1, LANE), const2),
            pl.BlockSpec((1, ML_WIDTH), const2),
        ],
        out_specs=[
            pl.BlockSpec((1, tr, ML_WIDTH), lambda i, j: (i, j, 0)),
            pl.BlockSpec((1, ML_HEADS, ML_HD, ML_HD), lambda i, j: (i, 0, 0, 0)),
            pl.BlockSpec((1, ML_HEADS, ML_HD), bmap),
            pl.BlockSpec((1, 1, ML_HEADS), bmap),
            pl.BlockSpec((1, CONV_W - 1, 2 * ML_WIDTH), bmap),
        ],
        out_shape=[
            jax.ShapeDtypeStruct((b, t, ML_WIDTH), F32),
            jax.ShapeDtypeStruct((b, ML_HEADS, ML_HD, ML_HD), F32),
            jax.ShapeDtypeStruct((b, ML_HEADS, ML_HD), F32),
            jax.ShapeDtypeStruct((b, 1, ML_HEADS), F32),
            jax.ShapeDtypeStruct((b, CONV_W - 1, 2 * ML_WIDTH), F32),
        ],
        scratch_shapes=[
            pltpu.VMEM((ML_HEADS, ML_HD, 2 * ML_HD), F32),
            pltpu.VMEM((1, LANE), F32),
            pltpu.VMEM((tr + 8, 2 * ML_WIDTH), F32),
        ],
        compiler_params=pltpu.CompilerParams(
            dimension_semantics=("arbitrary", "arbitrary"), vmem_limit_bytes=VMEM_LIMIT),
        name="mlstm",
    )(ml, gates, c0, n0, m0, conv0, cw, cb, gb, ng)


def _rwkv_kernel(c_ref, p_ref, s0_ref, mu_ref, w0_ref, wwa_ref, a0_ref, g2_ref, kk_ref, ka_ref, rk_ref,
                 gng_ref, gnb_ref, out_ref, s1_ref, s_ref, *, tr):
    L = RW_CHUNK
    W = RW_WIDTH
    GW = RW_GROUP * RW_HD
    GL = RW_GROUP * L
    NGRP = RW_HEADS // RW_GROUP
    ci = pl.program_id(1)
    nc = pl.num_programs(1)

    r2 = _iota((GL, GW), 0)
    c2 = _iota((GL, GW), 1)
    bd = (r2 // L) == (c2 // RW_HD)
    s_lower = r2 > c2
    i_lower = r2 >= c2

    @pl.when(ci == 0)
    def _init():
        for gi in range(NGRP):
            x = s0_ref[0, gi * GW:(gi + 1) * GW, :]
            s_ref[gi] = jnp.where(bd, jnp.concatenate([x] * RW_GROUP, axis=1), 0.0)

    c = c_ref[0]
    xs = c + (p_ref[0] - c) * mu_ref[...]
    r = xs[:, 0:W]
    k = xs[:, W:2 * W]
    v = xs[:, 2 * W:3 * W]
    slab = xs[:, 3 * W:3 * W + DECAY_LORA + AAA_LORA]
    gd = xs[:, 3 * W + DECAY_LORA + AAA_LORA:]
    lane = _iota(slab.shape, 1)
    t_in = jnp.where(lane < DECAY_LORA, jnp.tanh(slab), slab)
    la = _dot(t_in.astype(BF16), wwa_ref[...])
    w_log = -_softplus(-(w0_ref[...] + la[:, 0:W])) - 0.5
    lw = -jnp.exp(w_log)
    a = _sigmoid(a0_ref[...] + la[:, W:2 * W])
    g = _dot(_sigmoid(gd).astype(BF16), g2_ref[...])

    rs = _iota((W, W), 0)
    cs = _iota((W, W), 1)
    seg = jnp.where((rs // RW_HD) == (cs // RW_HD), 1.0, 0.0).astype(BF16)

    kk = k * kk_ref[...]
    kn = k * (1.0 + (a - 1.0) * ka_ref[...])
    ss = _dot_exact_rhs(kk * kk, seg, 2)
    kap = kk / jnp.maximum(jnp.sqrt(ss), 1e-12)
    bonus = _dot_exact_rhs(r * kn * rk_ref[...], seg, 2) * v

    lw = _pad_rows(lw, L)
    kap = _pad_rows(kap, L)
    kn_p = _pad_rows(kn, L)
    a_p = _pad_rows(a, L)
    v_p = _pad_rows(v, L)
    r_p = _pad_rows(r, L)

    rl = _iota((L, L), 0)
    cl_ = _iota((L, L), 1)
    tri = jnp.where(rl >= cl_, 1.0, 0.0).astype(BF16)
    cum = _dot_exact_lhs(tri, lw, 3)
    cum_last = cum[L - 1:L, :]
    e_neg = jnp.exp(-cum)
    abar = -kap * jnp.exp(cum - lw)
    btil = kap * a_p * e_neg
    ktil = kn_p * e_neg
    rbar = r_p * jnp.exp(cum)
    e_rem = jnp.exp(cum_last - cum)
    bhat = kap * a_p * e_rem
    khat = kn_p * e_rem
    g_last = jnp.exp(cum_last)

    def stack(x):
        return jnp.where(bd, jnp.concatenate([x] * RW_GROUP, axis=0), 0.0).astype(BF16)

    ys = []
    for gi in range(NGRP):
        ls = slice(gi * GW, (gi + 1) * GW)
        a_s, b_s, k_s, r_s, v_s = (stack(t[:, ls]) for t in (abar, btil, ktil, rbar, v_p))
        bh_s, kh_s = stack(bhat[:, ls]), stack(khat[:, ls])
        ar = jnp.concatenate([a_s, r_s], axis=0)
        bk = jnp.concatenate([b_s, k_s], axis=0)
        p = _dot_nt(ar, bk)
        a_ab = jnp.where(s_lower, p[0:GL, 0:GL], 0.0)
        a_ak = jnp.where(s_lower, p[0:GL, GL:2 * GL], 0.0)
        r_b = jnp.where(i_lower, p[GL:2 * GL, 0:GL], 0.0)
        r_k = jnp.where(i_lower, p[GL:2 * GL, GL:2 * GL], 0.0)
        s = s_ref[gi]
        q0 = _dot_nt(ar, s.astype(BF16))
        u = q0[0:GL] + _dot(a_ak.astype(BF16), v_s)
        npow = a_ab
        lvls = L.bit_length() - 1
        for lvl in range(lvls):
            u = u + _dot(npow.astype(BF16), u.astype(BF16))
            if lvl < lvls - 1:
                npow = _dot(npow.astype(BF16), npow.astype(BF16))
        u_b = u.astype(BF16)
        y = q0[GL:2 * GL] + _dot(r_b.astype(BF16), u_b) + _dot(r_k.astype(BF16), v_s)
        yg = y[0:L]
        for j in range(1, RW_GROUP):
            yg = yg + y[j * L:(j + 1) * L]
        ys.append(yg)
        uv_t = jnp.concatenate([u, v_s.astype(F32)], axis=0).T.astype(BF16)
        bkh = jnp.concatenate([bh_s, kh_s], axis=0)
        s_ref[gi] = s * g_last[:, ls] + _dot(uv_t, bkh)

    y_all = jnp.concatenate(ys, axis=1)[0:tr]
    inv = 1.0 / RW_HD
    mu_ = _dot_exact_rhs(y_all, seg, 2) * inv
    dy = y_all - mu_
    var = _dot_exact_rhs(dy * dy, seg, 2) * inv
    yn = dy * lax.rsqrt(var + GN_EPS) * gng_ref[...] + gnb_ref[...]
    out_ref[0] = (yn + bonus) * g

    @pl.when(ci == nc - 1)
    def _state_out():
        for gi in range(NGRP):
            s = s_ref[gi]
            f = s[:, 0:RW_HD]
            for j in range(1, RW_GROUP):
                f = f + s[:, j * RW_HD:(j + 1) * RW_HD]
            s1_ref[0, gi * GW:(gi + 1) * GW, :] = f


def _rwkv(cols, prev, s0, mu, w0, wwa, a0, g2, kk, ka, rk, gng, gnb, tr):
    b, t, _ = cols.shape
    nc = t // tr
    const2 = lambda i, j: (0, 0)
    vec = pl.BlockSpec((1, RW_WIDTH), const2)
    return pl.pallas_call(
        functools.partial(_rwkv_kernel, tr=tr),
        grid=(b, nc),
        in_specs=[
            pl.BlockSpec((1, tr, RW_COLS), lambda i, j: (i, j, 0)),
            pl.BlockSpec((1, tr, RW_COLS), lambda i, j: (i, j, 0)),
            pl.BlockSpec((1, RW_WIDTH, RW_HD), lambda i, j: (i, 0, 0)),
            pl.BlockSpec((1, RW_COLS), const2),
            vec,
            pl.BlockSpec((DECAY_LORA + AAA_LORA, 2 * RW_WIDTH), const2),
            vec,
            pl.BlockSpec((GATE_LORA, RW_WIDTH), const2),
            vec, vec, vec, vec, vec,
        ],
        out_specs=[
            pl.BlockSpec((1, tr, RW_WIDTH), lambda i, j: (i, j, 0)),
            pl.BlockSpec((1, RW_WIDTH, RW_HD), lambda i, j: (i, 0, 0)),
        ],
        out_shape=[
            jax.ShapeDtypeStruct((b, t, RW_WIDTH), F32),
            jax.ShapeDtypeStruct((b, RW_WIDTH, RW_HD), F32),
        ],
        scratch_shapes=[pltpu.VMEM((RW_HEADS // RW_GROUP, RW_GROUP * RW_HD, RW_GROUP * RW_HD), F32)],
        compiler_params=pltpu.CompilerParams(
            dimension_semantics=("arbitrary", "arbitrary"), vmem_limit_bytes=VMEM_LIMIT),
        name="rwkv",
    )(cols, prev, s0, mu, w0, wwa, a0, g2, kk, ka, rk, gng, gnb)


def _extract_top(work, idx, n):
    rank = jnp.full(work.shape, 99.0, F32)
    vals = []
    for j in range(n):
        m = jnp.max(work, axis=0, keepdims=True)
        first = jnp.min(jnp.where(work == m, idx, 1e9), axis=0, keepdims=True)
        sel = idx == first
        rank = jnp.where(sel, float(j), rank)
        work = jnp.where(sel, -jnp.inf, work)
        vals.append(m)
    return vals, rank


def _rows_bf16(row, rows):
    packed = jnp.broadcast_to(row, (16, row.shape[1])).astype(BF16)
    return jnp.concatenate([packed] * (rows // 16), axis=0)


def _peer_kernel(x_ref, ml_ref, rw_ref, woml_ref, worw_ref, gffn_ref, gfin_ref, wqt_ref, keys_ref,
                 ua_ref, ub_ref, vta_ref, vtb_ref, y_ref,
                 ht_ref, qt_ref, rank_ref, e_ref, vals_ref, r2b_ref, e2b_ref, cnt_ref, gate_ref,
                 p_ref, acc_ref, *, tm, ec):
    e = pl.program_id(1)
    ne = pl.num_programs(1)
    ng = tm // LANE
    K = PEER_TOPK
    nslab = ec // N_KEYS
    act_ref = qt_ref

    @pl.when(e == 0)
    def _select():
        x1 = (x_ref[...] + _dot(ml_ref[...].astype(BF16), woml_ref[...])
              + _dot(rw_ref[...].astype(BF16), worw_ref[...]))
        y_ref[...] = x1
        h = x1 * lax.rsqrt(jnp.mean(x1 * x1, axis=-1, keepdims=True) + NORM_EPS) * gffn_ref[...]
        ht = h.T.astype(BF16)
        ht_ref[...] = ht
        qt_ref[...] = _dot(wqt_ref[...], ht)
        acc_ref[...] = jnp.zeros(acc_ref.shape, F32)

        def score_body(hp, carry):
            q = qt_ref[pl.ds(pl.multiple_of(hp * PEER_HALF, PEER_HALF), PEER_HALF), :]
            e_ref[hp] = _dot(keys_ref[hp], q.astype(BF16))
            return carry

        lax.fori_loop(0, 2 * PEER_HEADS, score_body, 0)

        key_idx = _iota((N_KEYS, LANE), 0).astype(F32)

        def topk_body(i, carry):
            hp = i // ng
            lanes = pl.ds(pl.multiple_of((i % ng) * LANE, LANE), LANE)
            s = e_ref[hp, :, lanes]
            vals, rank = _extract_top(s, key_idx, K)
            rank_ref[hp, :, lanes] = rank
            e_ref[hp, :, lanes] = jnp.exp(s - vals[0])
            vals_ref[hp, :, lanes] = jnp.concatenate(vals, axis=0)
            return carry

        lax.fori_loop(0, 2 * PEER_HEADS * ng, topk_body, 0)

        sub8 = _iota((8, LANE), 0)

        def cand_body(i, carry):
            h = i // ng
            lanes = pl.ds(pl.multiple_of((i % ng) * LANE, LANE), LANE)
            v1 = vals_ref[2 * h, :, lanes]
            v2 = vals_ref[2 * h + 1, :, lanes]
            slabs, idxs, slab_a = [], [], []
            for a_i in range(K):
                nb = K // (a_i + 1)
                for b0 in range(0, nb, 8):
                    rows = v1[a_i:a_i + 1, :] + v2[b0:b0 + 8, :]
                    bidx = sub8 + b0
                    slabs.append(jnp.where(bidx < nb, rows, -jnp.inf))
                    idxs.append((bidx + a_i * K).astype(F32))
                    slab_a.append(a_i)
            work = jnp.concatenate(slabs, axis=0)
            idx = jnp.concatenate(idxs, axis=0)
            top = v1[0:1, :] + v2[0:1, :]
            vals, rank = _extract_top(work, idx, K)
            z = jnp.exp(vals[0] - top)
            for j in range(1, K):
                z = z + jnp.exp(vals[j] - top)
            picked = jnp.where(rank < float(K), 1.0, 0.0)
            r1 = rank_ref[2 * h, :, lanes]
            cnt = jnp.zeros((N_KEYS, LANE), F32)
            for a_i in range(K):
                ca = None
                for si, sa in enumerate(slab_a):
                    if sa == a_i:
                        part = jnp.sum(picked[si * 8:(si + 1) * 8], axis=0, keepdims=True)
                        ca = part if ca is None else ca + part
                cnt = cnt + jnp.where(r1 == float(a_i), ca, 0.0)
            cnt_ref[h, :, lanes] = cnt
            e1 = e_ref[2 * h, :, lanes]
            gate_ref[h, :, lanes] = jnp.where(r1 < float(K), e1 / z, 0.0)
            r2b_ref[h, :, lanes] = rank_ref[2 * h + 1, :, lanes].astype(BF16)
            e2b_ref[h, :, lanes] = e_ref[2 * h + 1, :, lanes].astype(BF16)
            return carry

        lax.fori_loop(0, PEER_HEADS * ng, cand_body, 0)

    key1 = pl.ds(pl.multiple_of(e * nslab, nslab), nslab)
    hs = nslab // 2
    hrows = hs * N_KEYS
    for half in range(2):
        act_ref[half * hrows:(half + 1) * hrows, :] = _dot((ua_ref, ub_ref)[half][...], ht_ref[...])
    for half in range(2):
        for gi in range(ng):
            lanes = slice(gi * LANE, (gi + 1) * LANE)
            w = [jnp.zeros((N_KEYS, LANE), BF16) for _ in range(hs)]
            for h in range(PEER_HEADS):
                r2 = r2b_ref[h, :, lanes]
                e2 = e2b_ref[h, :, lanes]
                cw = cnt_ref[h, key1, lanes]
                gw = gate_ref[h, key1, lanes]
                for jj in range(hs):
                    j = half * hs + jj
                    hit = r2 < _rows_bf16(cw[j:j + 1, :], N_KEYS)
                    w[jj] = w[jj] + jnp.where(hit, e2, jnp.zeros_like(e2)) * _rows_bf16(gw[j:j + 1, :], N_KEYS)
            for jj in range(hs):
                rows = slice((half * hs + jj) * N_KEYS, (half * hs + jj + 1) * N_KEYS)
                act = act_ref[rows, lanes]
                gelu = 0.5 * act * (1.0 + lax.erf(act * 0.7071067811865476))
                p_ref[rows, lanes] = w[jj] * gelu.astype(BF16)
        acc_ref[...] += _dot((vta_ref, vtb_ref)[half][0], p_ref[half * hrows:(half + 1) * hrows, :])

    @pl.when(e == ne - 1)
    def _finish():
        x2 = y_ref[...] + acc_ref[...].T
        y_ref[...] = x2 * lax.rsqrt(jnp.mean(x2 * x2, axis=-1, keepdims=True) + NORM_EPS) * gfin_ref[...]


def _peer(x2d, ml2d, rw2d, woml, worw, gffn, gfin, wqt, keys, u, vt, tm, ec):
    n = x2d.shape[0]
    n_exp = u.shape[0]
    assert ec == 8 * N_KEYS and tm % LANE == 0 and n % tm == 0 and n_exp % ec == 0
    tok = lambda i, e: (i, 0)
    const2 = lambda i, e: (0, 0)
    qrows = 2 * PEER_HEADS * PEER_HALF
    eh = ec // 2
    once = pl.Buffered(1)
    return pl.pallas_call(
        functools.partial(_peer_kernel, tm=tm, ec=ec),
        grid=(n // tm, n_exp // ec),
        in_specs=[
            pl.BlockSpec((tm, D_MODEL), tok),
            pl.BlockSpec((tm, ML_WIDTH), tok),
            pl.BlockSpec((tm, RW_WIDTH), tok),
            pl.BlockSpec((ML_WIDTH, D_MODEL), const2, pipeline_mode=once),
            pl.BlockSpec((RW_WIDTH, D_MODEL), const2, pipeline_mode=once),
            pl.BlockSpec((1, D_MODEL), const2),
            pl.BlockSpec((1, D_MODEL), const2),
            pl.BlockSpec((qrows, D_MODEL), const2, pipeline_mode=once),
            pl.BlockSpec((2 * PEER_HEADS, N_KEYS, PEER_HALF), lambda i, e: (0, 0, 0), pipeline_mode=once),
            pl.BlockSpec((eh, D_MODEL), lambda i, e: (2 * e, 0)),
            pl.BlockSpec((eh, D_MODEL), lambda i, e: (2 * e + 1, 0)),
            pl.BlockSpec((1, D_MODEL, eh), lambda i, e: (2 * e, 0, 0)),
            pl.BlockSpec((1, D_MODEL, eh), lambda i, e: (2 * e + 1, 0, 0)),
        ],
        out_specs=pl.BlockSpec((tm, D_MODEL), tok),
        out_shape=jax.ShapeDtypeStruct((n, D_MODEL), F32),
        scratch_shapes=[
            pltpu.VMEM((D_MODEL, tm), BF16),
            pltpu.VMEM((qrows, tm), F32),
            pltpu.VMEM((2 * PEER_HEADS, N_KEYS, tm), F32),
            pltpu.VMEM((2 * PEER_HEADS, N_KEYS, tm), F32),
            pltpu.VMEM((2 * PEER_HEADS, PEER_TOPK, tm), F32),
            pltpu.VMEM((PEER_HEADS, N_KEYS, tm), BF16),
            pltpu.VMEM((PEER_HEADS, N_KEYS, tm), BF16),
            pltpu.VMEM((PEER_HEADS, N_KEYS, tm), F32),
            pltpu.VMEM((PEER_HEADS, N_KEYS, tm), F32),
            pltpu.VMEM((ec, tm), BF16),
            pltpu.VMEM((D_MODEL, tm), F32),
        ],
        compiler_params=pltpu.CompilerParams(
            dimension_semantics=("arbitrary", "arbitrary"), vmem_limit_bytes=VMEM_LIMIT),
        name="peer",
    )(x2d, ml2d, rw2d, woml, worw, gffn, gfin, wqt, keys, u, u, vt, vt)


def _prep_weights(norm_mix_g, w_in, ml_conv_w, ml_conv_b, ml_b_i, ml_b_f, ml_norm_g,
                  rw_mu, rw_w0, rw_w2, rw_a0, rw_a2, rw_g2, rw_k_k, rw_k_a, rw_r_k, rw_gn_g, rw_gn_b,
                  w_out, norm_ffn_g, peer_w_q, peer_sub_keys, peer_u, peer_v, norm_final_g):
    assert w_in.shape[0] == 1, "one layer"
    w = w_in[0]
    wg = jnp.pad(w[:, ML_QKVO:ML_QKVO + 2 * ML_HEADS], ((0, 0), (0, LANE - 2 * ML_HEADS)))
    wg_hi = wg.astype(BF16)
    wg_lo = (wg - wg_hi.astype(F32)).astype(BF16)
    zeros = jnp.zeros((DECAY_LORA, RW_WIDTH), F32)
    wwa = jnp.concatenate([jnp.concatenate([rw_w2[0], zeros], axis=1),
                           jnp.concatenate([zeros, rw_a2[0]], axis=1)], axis=0)
    gate_bias = jnp.pad(jnp.concatenate([ml_b_i[0], ml_b_f[0]]), (0, LANE - 2 * ML_HEADS))[None, :]
    row = lambda a: a.reshape(1, -1)
    return dict(
        g_mix=row(norm_mix_g[0]),
        wml=w[:, 0:ML_QKVO].astype(BF16),
        wg=jnp.stack([wg_hi, wg_lo]),
        wrw=w[:, ML_QKVO + 2 * ML_HEADS:].astype(BF16),
        conv_w=ml_conv_w[0], conv_b=row(ml_conv_b[0]), gate_bias=gate_bias, ml_norm_g=row(ml_norm_g[0]),
        mu=row(rw_mu[0]), w0=row(rw_w0[0]), wwa=wwa.astype(BF16), a0=row(rw_a0[0]),
        g2=rw_g2[0].astype(BF16), k_k=row(rw_k_k[0]), k_a=row(rw_k_a[0]), r_k=row(rw_r_k[0]),
        gn_g=row(rw_gn_g[0]), gn_b=row(rw_gn_b[0]),
        wo_ml=w_out[0, 0:ML_WIDTH].astype(BF16), wo_rw=w_out[0, ML_WIDTH:].astype(BF16),
        g_ffn=row(norm_ffn_g[0]), g_fin=row(norm_final_g),
        wqt=peer_w_q[0].T.astype(BF16),
        keys=peer_sub_keys[0].reshape(2 * PEER_HEADS, N_KEYS, PEER_HALF).astype(BF16),
        u=peer_u[0].astype(BF16),
        vt=peer_v[0].reshape(-1, PEER_EXPERT_CHUNK // 2, D_MODEL).transpose(0, 2, 1).astype(BF16),
    )


def _trunk(x, states, wp, tr_ml, tr_rw, tm_in, tm_peer, ec):
    b, t, d = x.shape
    c0, n0, m0, conv0, s0, shift0 = (s[0] for s in states)
    x2d = x.reshape(b * t, d)
    ml, gates, rw = _proj_in(x2d, wp["g_mix"], wp["wml"], wp["wg"], wp["wrw"], tm_in)
    ml3 = ml.reshape(b, t, ML_QKVO)
    rw3 = rw.reshape(b, t, RW_COLS)
    ml_out, c1, n1, m1, conv1 = _mlstm(
        ml3, gates.reshape(b, t, LANE), c0, n0, m0.reshape(b, 1, ML_HEADS), conv0,
        wp["conv_w"], wp["conv_b"], wp["gate_bias"], wp["ml_norm_g"], tr_ml)
    prev = jnp.concatenate([shift0[:, None, :], rw3[:, :-1]], axis=1)
    rw_out, s1 = _rwkv(
        rw3, prev, s0.reshape(b, RW_WIDTH, RW_HD), wp["mu"], wp["w0"], wp["wwa"], wp["a0"], wp["g2"],
        wp["k_k"], wp["k_a"], wp["r_k"], wp["gn_g"], wp["gn_b"], tr_rw)
    y = _peer(x2d, ml_out.reshape(b * t, ML_WIDTH), rw_out.reshape(b * t, RW_WIDTH),
              wp["wo_ml"], wp["wo_rw"], wp["g_ffn"], wp["g_fin"], wp["wqt"], wp["keys"],
              wp["u"], wp["vt"], tm_peer, ec)
    new_states = (c1[None], n1[None], m1.reshape(b, ML_HEADS)[None], conv1[None],
                  s1.reshape(b, RW_HEADS, RW_HD, RW_HD)[None], rw3[:, -1][None])
    return y.reshape(b, t, d), new_states


def kernel(x_prompt, x_sample, state_mlstm_C, state_mlstm_n, state_mlstm_m, state_mlstm_conv, state_rwkv_S, state_rwkv_shift, norm_mix_g, w_in, ml_conv_w, ml_conv_b, ml_b_i, ml_b_f, ml_norm_g, rw_mu, rw_w0, rw_w2, rw_a0, rw_a2, rw_g2, rw_k_k, rw_k_a, rw_r_k, rw_gn_g, rw_gn_b, w_out, norm_ffn_g, peer_w_q, peer_sub_keys, peer_u, peer_v, norm_final_g):
    wp = _prep_weights(norm_mix_g, w_in, ml_conv_w, ml_conv_b, ml_b_i, ml_b_f, ml_norm_g,
                       rw_mu, rw_w0, rw_w2, rw_a0, rw_a2, rw_g2, rw_k_k, rw_k_a, rw_r_k, rw_gn_g, rw_gn_b,
                       w_out, norm_ffn_g, peer_w_q, peer_sub_keys, peer_u, peer_v, norm_final_g)
    bp = x_prompt.shape[0]
    z = lambda *s: jnp.zeros((1, bp) + s, F32)
    prompt_states = (z(ML_HEADS, ML_HD, ML_HD), z(ML_HEADS, ML_HD), jnp.full((1, bp, ML_HEADS), M_INIT, F32),
                     z(CONV_W - 1, 2 * ML_WIDTH), z(RW_HEADS, RW_HD, RW_HD), z(RW_COLS))
    sample_states = (state_mlstm_C, state_mlstm_n, state_mlstm_m, state_mlstm_conv,
                     state_rwkv_S, state_rwkv_shift)
    tp, ts = x_prompt.shape[1], x_sample.shape[1]
    np_, ns = bp * tp, x_sample.shape[0] * ts
    y_p, st_p = _trunk(x_prompt, prompt_states, wp, min(tp, ML_CHUNK), min(tp, RW_CHUNK),
                       min(np_, PROJ_TILE), min(np_, PEER_TILE), PEER_EXPERT_CHUNK)
    y_s, st_s = _trunk(x_sample, sample_states, wp, min(ts, ML_CHUNK), min(ts, RW_CHUNK),
                       min(ns, PROJ_TILE), min(ns, PEER_TILE), PEER_EXPERT_CHUNK)
    return (y_p, y_s) + tuple(st_p) + tuple(st_s)
```

```python
import functools

import jax
import jax.numpy as jnp
from jax import lax
from jax.experimental import pallas as pl
from jax.experimental.pallas import tpu as pltpu

F32 = jnp.float32
BF16 = jnp.bfloat16

D_MODEL = 1024
ML_HEADS = 4
ML_HD = 128
ML_WIDTH = ML_HEADS * ML_HD
CONV_W = 4
RW_HEADS = 8
RW_HD = 64
RW_WIDTH = RW_HEADS * RW_HD
DECAY_LORA = 64
AAA_LORA = 64
GATE_LORA = 128
RW_COLS = 3 * RW_WIDTH + DECAY_LORA + AAA_LORA + GATE_LORA
ML_QKVO = 4 * ML_WIDTH
PEER_HEADS = 8
N_KEYS = 128
PEER_TOPK = 16
PEER_HALF = 128
NORM_EPS = 1e-6
ML_NORM_EPS = 1e-6
GN_EPS = RW_HD * 1e-5
M_INIT = -1e30
NEG_BIG = -1e30

LANE = 128
ML_CHUNK = 128
RW_CHUNK = 64
RW_GROUP = 4
PROJ_TILE = 256
PEER_TILE = 512
PEER_EXPERT_CHUNK = 1024
VMEM_LIMIT = 56 * 1024 * 1024


def _dot(a, b):
    return jnp.dot(a, b, preferred_element_type=F32)


def _dot_nt(a, b):
    return lax.dot_general(a, b, (((1,), (1,)), ((), ())), preferred_element_type=F32)


def _split_bf16(x, n):
    parts = []
    r = x
    for _ in range(n):
        p = r.astype(BF16)
        parts.append(p)
        r = r - p.astype(F32)
    return parts


def _dot_exact_lhs(mask_bf16, x, n):
    return sum(_dot(mask_bf16, p) for p in _split_bf16(x, n))


def _dot_exact_rhs(x, mask_bf16, n):
    return sum(_dot(p, mask_bf16) for p in _split_bf16(x, n))


def _sigmoid(x):
    return 1.0 / (1.0 + jnp.exp(-x))


def _softplus(x):
    return jnp.maximum(x, 0.0) + jnp.log1p(jnp.exp(-jnp.abs(x)))


def _iota(shape, dim):
    return lax.broadcasted_iota(jnp.int32, shape, dim)


def _pad_rows(x, rows, value=0.0):
    if x.shape[0] == rows:
        return x
    return jnp.concatenate([x, jnp.full((rows - x.shape[0], x.shape[1]), value, x.dtype)], axis=0)


def _proj_in_kernel(x_ref, g_ref, wml_ref, wg_ref, wrw_ref, ml_ref, gate_ref, rw_ref):
    x = x_ref[...]
    xn = x * lax.rsqrt(jnp.mean(x * x, axis=-1, keepdims=True) + NORM_EPS) * g_ref[...]
    xb = xn.astype(BF16)
    ml_ref[...] = _dot(xb, wml_ref[...])
    rw_ref[...] = _dot(xb, wrw_ref[...])
    xlo = (xn - xb.astype(F32)).astype(BF16)
    gate_ref[...] = _dot(xb, wg_ref[0]) + _dot(xlo, wg_ref[0]) + _dot(xb, wg_ref[1])


def _proj_in(x2d, g, wml, wg, wrw, tm):
    n = x2d.shape[0]
    const2 = lambda i: (0, 0)
    return pl.pallas_call(
        _proj_in_kernel,
        grid=(n // tm,),
        in_specs=[
            pl.BlockSpec((tm, D_MODEL), lambda i: (i, 0)),
            pl.BlockSpec((1, D_MODEL), const2),
            pl.BlockSpec((D_MODEL, ML_QKVO), const2),
            pl.BlockSpec((2, D_MODEL, LANE), lambda i: (0, 0, 0)),
            pl.BlockSpec((D_MODEL, RW_COLS), const2),
        ],
        out_specs=[
            pl.BlockSpec((tm, ML_QKVO), lambda i: (i, 0)),
            pl.BlockSpec((tm, LANE), lambda i: (i, 0)),
            pl.BlockSpec((tm, RW_COLS), lambda i: (i, 0)),
        ],
        out_shape=[
            jax.ShapeDtypeStruct((n, ML_QKVO), F32),
            jax.ShapeDtypeStruct((n, LANE), F32),
            jax.ShapeDtypeStruct((n, RW_COLS), F32),
        ],
        compiler_params=pltpu.CompilerParams(
            dimension_semantics=("arbitrary",), vmem_limit_bytes=VMEM_LIMIT),
        name="proj_in",
    )(x2d, g, wml, wg, wrw)


def _mlstm_kernel(ml_ref, gate_ref, c0_ref, n0_ref, m0_ref, conv0_ref, cw_ref, cb_ref, gb_ref, ng_ref,
                  out_ref, c1_ref, n1_ref, m1_ref, conv1_ref,
                  caug_ref, m_ref, ext_ref, *, tr):
    L = ML_CHUNK
    c = pl.program_id(1)
    nc = pl.num_programs(1)

    @pl.when(c == 0)
    def _init():
        for h in range(ML_HEADS):
            caug_ref[h, :, 0:ML_HD] = c0_ref[0, h]
            nrow = n0_ref[0, h:h + 1, :]
            caug_ref[h, :, ML_HD:2 * ML_HD] = jnp.broadcast_to(nrow, (ML_HD, ML_HD)).T
        m_ref[...] = jnp.zeros(m_ref.shape, F32)
        m_ref[:, 0:ML_HEADS] = m0_ref[0]
        ext_ref[5:8, :] = conv0_ref[0]

    ext_ref[8:8 + tr, :] = ml_ref[0, :, 0:2 * ML_WIDTH]
    acc = cb_ref[...] + ext_ref[5:5 + tr, :] * cw_ref[0:1, :]
    for j in range(1, CONV_W):
        acc = acc + ext_ref[5 + j:5 + j + tr, :] * cw_ref[j:j + 1, :]
    qk = acc * _sigmoid(acc)
    tail = ext_ref[tr + 5:tr + 8, :]
    ext_ref[5:8, :] = tail

    @pl.when(c == nc - 1)
    def _conv_out():
        conv1_ref[0] = tail

    q_all = _pad_rows(qk[:, 0:ML_WIDTH], L)
    k_all = _pad_rows(qk[:, ML_WIDTH:2 * ML_WIDTH] * (ML_HD ** -0.5), L)
    v_all = _pad_rows(ml_ref[0, :, 2 * ML_WIDTH:3 * ML_WIDTH], L)
    o_all = ml_ref[0, :, 3 * ML_WIDTH:4 * ML_WIDTH]

    g = gate_ref[0] + gb_ref[...]
    i_all = _pad_rows(g, L, NEG_BIG)
    lf_all = _pad_rows(-_softplus(-g), L, 0.0)
    row = _iota((L, L), 0)
    col = _iota((L, L), 1)
    causal = row >= col
    tri = jnp.where(causal, 1.0, 0.0).astype(BF16)
    b_col = _dot_exact_lhs(tri, lf_all, 3)
    b_t = b_col.T
    i_t = i_all.T

    ones = jnp.ones((L, ML_HD), BF16)
    for h in range(ML_HEADS):
        bc = b_col[:, ML_HEADS + h:ML_HEADS + h + 1]
        br = b_t[ML_HEADS + h:ML_HEADS + h + 1, :]
        ir = i_t[h:h + 1, :]
        m_prev = m_ref[:, h:h + 1]
        logd = jnp.where(causal, bc - br + ir, -jnp.inf)
        linter = bc + m_prev
        m_t = jnp.maximum(linter, jnp.max(logd, axis=1, keepdims=True))
        d = jnp.exp(logd - m_t)
        s_inter = jnp.exp(linter - m_t)
        sl = slice(h * ML_HD, (h + 1) * ML_HD)
        qh = q_all[:, sl].astype(BF16)
        k_t = k_all[:, sl].T
        vaug = jnp.concatenate([v_all[:, sl].astype(BF16), ones], axis=1)
        caug = caug_ref[h]
        s = _dot(qh, k_t.astype(BF16)) * d
        num = s_inter * _dot(qh, caug.astype(BF16)) + _dot(s.astype(BF16), vaug)
        den = num[:, ML_HD:2 * ML_HD]
        hh = num[:, 0:ML_HD] / jnp.maximum(jnp.abs(den), jnp.exp(-m_t))
        mu = jnp.mean(hh, axis=-1, keepdims=True)
        dv = hh - mu
        var = jnp.mean(dv * dv, axis=-1, keepdims=True)
        y = dv * lax.rsqrt(var + ML_NORM_EPS) * ng_ref[:, sl]
        out_ref[0, :, sl] = y[0:tr] * _sigmoid(o_all[:, sl])
        m_new = m_t[L - 1:L, :]
        b_last = bc[L - 1:L, :]
        s_state = jnp.exp(b_last + m_prev - m_new)
        w_row = jnp.exp(b_last - br + ir - m_new)
        caug_ref[h] = s_state * caug + _dot((k_t * w_row).astype(BF16), vaug)
        m_ref[:, h:h + 1] = m_new

    @pl.when(c == nc - 1)
    def _state_out():
        for h in range(ML_HEADS):
            caug = caug_ref[h]
            c1_ref[0, h] = caug[:, 0:ML_HD]
            n1_ref[0, h:h + 1, :] = caug[:, ML_HD:2 * ML_HD].T[0:1, :]
        m1_ref[0] = m_ref[:, 0:ML_HEADS]


def _mlstm(ml, gates, c0, n0, m0, conv0, cw, cb, gb, ng, tr):
    b, t, _ = ml.shape
    nc = t // tr
    bmap = lambda i, j: (i, 0, 0)
    const2 = lambda i, j: (0, 0)
    return pl.pallas_call(
        functools.partial(_mlstm_kernel, tr=tr),
        grid=(b, nc),
        in_specs=[
            pl.BlockSpec((1, tr, ML_QKVO), lambda i, j: (i, j, 0)),
            pl.BlockSpec((1, tr, LANE), lambda i, j: (i, j, 0)),
            pl.BlockSpec((1, ML_HEADS, ML_HD, ML_HD), lambda i, j: (i, 0, 0, 0)),
            pl.BlockSpec((1, ML_HEADS, ML_HD), bmap),
            pl.BlockSpec((1, 1, ML_HEADS), bmap),
            pl.BlockSpec((1, CONV_W - 1, 2 * ML_WIDTH), bmap),
            pl.BlockSpec((CONV_W, 2 * ML_WIDTH), const2),
            pl.BlockSpec((1, 2 * ML_WIDTH), const2),
            pl.BlockSpec((1, LANE), const2),
            pl.BlockSpec((1, ML_WIDTH), const2),
        ],
        out_specs=[
            pl.BlockSpec((1, tr, ML_WIDTH), lambda i, j: (i, j, 0)),
            pl.BlockSpec((1, ML_HEADS, ML_HD, ML_HD), lambda i, j: (i, 0, 0, 0)),
            pl.BlockSpec((1, ML_HEADS, ML_HD), bmap),
            pl.BlockSpec((1, 1, ML_HEADS), bmap),
            pl.BlockSpec((1, CONV_W - 1, 2 * ML_WIDTH), bmap),
        ],
        out_shape=[
            jax.ShapeDtypeStruct((b, t, ML_WIDTH), F32),
            jax.ShapeDtypeStruct((b, ML_HEADS, ML_HD, ML_HD), F32),
            jax.ShapeDtypeStruct((b, ML_HEADS, ML_HD), F32),
            jax.ShapeDtypeStruct((b, 1, ML_HEADS), F32),
            jax.ShapeDtypeStruct((b, CONV_W - 1, 2 * ML_WIDTH), F32),
        ],
        scratch_shapes=[
            pltpu.VMEM((ML_HEADS, ML_HD, 2 * ML_HD), F32),
            pltpu.VMEM((1, LANE), F32),
            pltpu.VMEM((tr + 8, 2 * ML_WIDTH), F32),
        ],
        compiler_params=pltpu.CompilerParams(
            dimension_semantics=("arbitrary", "arbitrary"), vmem_limit_bytes=VMEM_LIMIT),
        name="mlstm",
    )(ml, gates, c0, n0, m0, conv0, cw, cb, gb, ng)


def _rwkv_kernel(c_ref, p_ref, s0_ref, mu_ref, w0_ref, wwa_ref, a0_ref, g2_ref, kk_ref, ka_ref, rk_ref,
                 gng_ref, gnb_ref, out_ref, s1_ref, s_ref, *, tr):
    L = RW_CHUNK
    W = RW_WIDTH
    GW = RW_GROUP * RW_HD
    GL = RW_GROUP * L
    NGRP = RW_HEADS // RW_GROUP
    ci = pl.program_id(1)
    nc = pl.num_programs(1)

    r2 = _iota((GL, GW), 0)
    c2 = _iota((GL, GW), 1)
    bd = (r2 // L) == (c2 // RW_HD)
    s_lower = r2 > c2
    i_lower = r2 >= c2

    @pl.when(ci == 0)
    def _init():
        for gi in range(NGRP):
            x = s0_ref[0, gi * GW:(gi + 1) * GW, :]
            s_ref[gi] = jnp.where(bd, jnp.concatenate([x] * RW_GROUP, axis=1), 0.0)

    c = c_ref[0]
    xs = c + (p_ref[0] - c) * mu_ref[...]
    r = xs[:, 0:W]
    k = xs[:, W:2 * W]
    v = xs[:, 2 * W:3 * W]
    slab = xs[:, 3 * W:3 * W + DECAY_LORA + AAA_LORA]
    gd = xs[:, 3 * W + DECAY_LORA + AAA_LORA:]
    lane = _iota(slab.shape, 1)
    t_in = jnp.where(lane < DECAY_LORA, jnp.tanh(slab), slab)
    la = _dot(t_in.astype(BF16), wwa_ref[...])
    w_log = -_softplus(-(w0_ref[...] + la[:, 0:W])) - 0.5
    lw = -jnp.exp(w_log)
    a = _sigmoid(a0_ref[...] + la[:, W:2 * W])
    g = _dot(_sigmoid(gd).astype(BF16), g2_ref[...])

    rs = _iota((W, W), 0)
    cs = _iota((W, W), 1)
    seg = jnp.where((rs // RW_HD) == (cs // RW_HD), 1.0, 0.0).astype(BF16)

    kk = k * kk_ref[...]
    kn = k * (1.0 + (a - 1.0) * ka_ref[...])
    ss = _dot_exact_rhs(kk * kk, seg, 2)
    kap = kk / jnp.maximum(jnp.sqrt(ss), 1e-12)
    bonus = _dot_exact_rhs(r * kn * rk_ref[...], seg, 2) * v

    lw = _pad_rows(lw, L)
    kap = _pad_rows(kap, L)
    kn_p = _pad_rows(kn, L)
    a_p = _pad_rows(a, L)
    v_p = _pad_rows(v, L)
    r_p = _pad_rows(r, L)

    rl = _iota((L, L), 0)
    cl_ = _iota((L, L), 1)
    tri = jnp.where(rl >= cl_, 1.0, 0.0).astype(BF16)
    cum = _dot_exact_lhs(tri, lw, 3)
    cum_last = cum[L - 1:L, :]
    e_neg = jnp.exp(-cum)
    abar = -kap * jnp.exp(cum - lw)
    btil = kap * a_p * e_neg
    ktil = kn_p * e_neg
    rbar = r_p * jnp.exp(cum)
    e_rem = jnp.exp(cum_last - cum)
    bhat = kap * a_p * e_rem
    khat = kn_p * e_rem
    g_last = jnp.exp(cum_last)

    def stack(x):
        return jnp.where(bd, jnp.concatenate([x] * RW_GROUP, axis=0), 0.0).astype(BF16)

    ys = []
    for gi in range(NGRP):
        ls = slice(gi * GW, (gi + 1) * GW)
        a_s, b_s, k_s, r_s, v_s = (stack(t[:, ls]) for t in (abar, btil, ktil, rbar, v_p))
        bh_s, kh_s = stack(bhat[:, ls]), stack(khat[:, ls])
        ar = jnp.concatenate([a_s, r_s], axis=0)
        bk = jnp.concatenate([b_s, k_s], axis=0)
        p = _dot_nt(ar, bk)
        a_ab = jnp.where(s_lower, p[0:GL, 0:GL], 0.0)
        a_ak = jnp.where(s_lower, p[0:GL, GL:2 * GL], 0.0)
        r_b = jnp.where(i_lower, p[GL:2 * GL, 0:GL], 0.0)
        r_k = jnp.where(i_lower, p[GL:2 * GL, GL:2 * GL], 0.0)
        s = s_ref[gi]
        q0 = _dot_nt(ar, s.astype(BF16))
        u = q0[0:GL] + _dot(a_ak.astype(BF16), v_s)
        npow = a_ab
        lvls = L.bit_length() - 1
        for lvl in range(lvls):
            u = u + _dot(npow.astype(BF16), u.astype(BF16))
            if lvl < lvls - 1:
                npow = _dot(npow.astype(BF16), npow.astype(BF16))
        u_b = u.astype(BF16)
        y = q0[GL:2 * GL] + _dot(r_b.astype(BF16), u_b) + _dot(r_k.astype(BF16), v_s)
        yg = y[0:L]
        for j in range(1, RW_GROUP):
            yg = yg + y[j * L:(j + 1) * L]
        ys.append(yg)
        uv_t = jnp.concatenate([u, v_s.astype(F32)], axis=0).T.astype(BF16)
        bkh = jnp.concatenate([bh_s, kh_s], axis=0)
        s_ref[gi] = s * g_last[:, ls] + _dot(uv_t, bkh)

    y_all = jnp.concatenate(ys, axis=1)[0:tr]
    inv = 1.0 / RW_HD
    mu_ = _dot_exact_rhs(y_all, seg, 2) * inv
    dy = y_all - mu_
    var = _dot_exact_rhs(dy * dy, seg, 2) * inv
    yn = dy * lax.rsqrt(var + GN_EPS) * gng_ref[...] + gnb_ref[...]
    out_ref[0] = (yn + bonus) * g

    @pl.when(ci == nc - 1)
    def _state_out():
        for gi in range(NGRP):
            s = s_ref[gi]
            f = s[:, 0:RW_HD]
            for j in range(1, RW_GROUP):
                f = f + s[:, j * RW_HD:(j + 1) * RW_HD]
            s1_ref[0, gi * GW:(gi + 1) * GW, :] = f


def _rwkv(cols, prev, s0, mu, w0, wwa, a0, g2, kk, ka, rk, gng, gnb, tr):
    b, t, _ = cols.shape
    nc = t // tr
    const2 = lambda i, j: (0, 0)
    vec = pl.BlockSpec((1, RW_WIDTH), const2)
    return pl.pallas_call(
        functools.partial(_rwkv_kernel, tr=tr),
        grid=(b, nc),
        in_specs=[
            pl.BlockSpec((1, tr, RW_COLS), lambda i, j: (i, j, 0)),
            pl.BlockSpec((1, tr, RW_COLS), lambda i, j: (i, j, 0)),
            pl.BlockSpec((1, RW_WIDTH, RW_HD), lambda i, j: (i, 0, 0)),
            pl.BlockSpec((1, RW_COLS), const2),
            vec,
            pl.BlockSpec((DECAY_LORA + AAA_LORA, 2 * RW_WIDTH), const2),
            vec,
            pl.BlockSpec((GATE_LORA, RW_WIDTH), const2),
            vec, vec, vec, vec, vec,
        ],
        out_specs=[
            pl.BlockSpec((1, tr, RW_WIDTH), lambda i, j: (i, j, 0)),
            pl.BlockSpec((1, RW_WIDTH, RW_HD), lambda i, j: (i, 0, 0)),
        ],
        out_shape=[
            jax.ShapeDtypeStruct((b, t, RW_WIDTH), F32),
            jax.ShapeDtypeStruct((b, RW_WIDTH, RW_HD), F32),
        ],
        scratch_shapes=[pltpu.VMEM((RW_HEADS // RW_GROUP, RW_GROUP * RW_HD, RW_GROUP * RW_HD), F32)],
        compiler_params=pltpu.CompilerParams(
            dimension_semantics=("arbitrary", "arbitrary"), vmem_limit_bytes=VMEM_LIMIT),
        name="rwkv",
    )(cols, prev, s0, mu, w0, wwa, a0, g2, kk, ka, rk, gng, gnb)


def _extract_top(work, idx, n):
    rank = jnp.full(work.shape, 99.0, F32)
    vals = []
    for j in range(n):
        m = jnp.max(work, axis=0, keepdims=True)
        if idx is None:
            sel = work == m
        else:
            first = jnp.min(jnp.where(work == m, idx, 1e9), axis=0, keepdims=True)
            sel = idx == first
        rank = jnp.where(sel, float(j), rank)
        work = jnp.where(sel, -jnp.inf, work)
        vals.append(m)
    return vals, rank


def _miscount(rank, n):
    taken = jnp.sum(jnp.where(rank < float(n), 1.0, 0.0), axis=0, keepdims=True)
    return jnp.abs(taken - float(n))


def _rows_bf16(row, rows):
    packed = jnp.broadcast_to(row, (16, row.shape[1])).astype(BF16)
    return jnp.concatenate([packed] * (rows // 16), axis=0)


def _peer_kernel(x_ref, ml_ref, rw_ref, woml_ref, worw_ref, gffn_ref, gfin_ref, wqt_ref, keys_ref,
                 ua_ref, ub_ref, vta_ref, vtb_ref, y_ref,
                 ht_ref, qt_ref, rank_ref, e_ref, vals_ref, r2b_ref, e2b_ref, cnt_ref, gate_ref,
                 p_ref, acc_ref, *, tm, ec):
    e = pl.program_id(1)
    ne = pl.num_programs(1)
    ng = tm // LANE
    K = PEER_TOPK
    nslab = ec // N_KEYS
    act_ref = qt_ref

    @pl.when(e == 0)
    def _select():
        x1 = (x_ref[...] + _dot(ml_ref[...].astype(BF16), woml_ref[...])
              + _dot(rw_ref[...].astype(BF16), worw_ref[...]))
        y_ref[...] = x1
        h = x1 * lax.rsqrt(jnp.mean(x1 * x1, axis=-1, keepdims=True) + NORM_EPS) * gffn_ref[...]
        ht = h.T.astype(BF16)
        ht_ref[...] = ht
        qt_ref[...] = _dot(wqt_ref[...], ht)
        acc_ref[...] = jnp.zeros(acc_ref.shape, F32)

        def score_body(hp, carry):
            q = qt_ref[pl.ds(pl.multiple_of(hp * PEER_HALF, PEER_HALF), PEER_HALF), :]
            e_ref[hp] = _dot(keys_ref[hp], q.astype(BF16))
            return carry

        lax.fori_loop(0, 2 * PEER_HEADS, score_body, 0)

        key_idx = _iota((N_KEYS, LANE), 0).astype(F32)

        groups = [slice(gi * LANE, (gi + 1) * LANE) for gi in range(ng)]

        def topk_group(hp, lanes, exact_ties):
            s = e_ref[hp, :, lanes]
            vals, rank = _extract_top(s, key_idx if exact_ties else None, K)
            rank_ref[hp, :, lanes] = rank
            vals_ref[hp, :, lanes] = jnp.concatenate(vals, axis=0)
            return _miscount(rank, K)

        def topk_body(hp, carry):
            bad = topk_group(hp, groups[0], False)
            for lanes in groups[1:]:
                bad = jnp.maximum(bad, topk_group(hp, lanes, False))

            @pl.when(jnp.max(bad) > 0.0)
            def _redo():
                for lanes in groups:
                    topk_group(hp, lanes, True)

            return carry

        lax.fori_loop(0, 2 * PEER_HEADS, topk_body, 0)

        sub8 = _iota((8, LANE), 0)

        def cand_group(h, lanes, exact_ties):
            v1 = vals_ref[2 * h, :, lanes]
            v2 = vals_ref[2 * h + 1, :, lanes]
            slabs, idxs, slab_a = [], [], []
            for a_i in range(K):
                nb = K // (a_i + 1)
                for b0 in range(0, nb, 8):
                    rows = v1[a_i:a_i + 1, :] + v2[b0:b0 + 8, :]
                    bidx = sub8 + b0
                    slabs.append(jnp.where(bidx < nb, rows, -jnp.inf))
                    idxs.append((bidx + a_i * K).astype(F32))
                    slab_a.append(a_i)
            work = jnp.concatenate(slabs, axis=0)
            idx = jnp.concatenate(idxs, axis=0) if exact_ties else None
            top = v1[0:1, :] + v2[0:1, :]
            vals, rank = _extract_top(work, idx, K)
            z = jnp.exp(vals[0] - top)
            for j in range(1, K):
                z = z + jnp.exp(vals[j] - top)
            picked = jnp.where(rank < float(K), 1.0, 0.0)
            r1 = rank_ref[2 * h, :, lanes]
            cnt = jnp.zeros((N_KEYS, LANE), F32)
            for a_i in range(K):
                ca = None
                for si, sa in enumerate(slab_a):
                    if sa == a_i:
                        part = jnp.sum(picked[si * 8:(si + 1) * 8], axis=0, keepdims=True)
                        ca = part if ca is None else ca + part
                cnt = cnt + jnp.where(r1 == float(a_i), ca, 0.0)
            cnt_ref[h, :, lanes] = cnt
            e1 = jnp.exp(e_ref[2 * h, :, lanes] - v1[0:1, :])
            gate_ref[h, :, lanes] = jnp.where(r1 < float(K), e1 / z, 0.0)
            return _miscount(rank, K)

        def cand_body(h, carry):
            bad = cand_group(h, groups[0], False)
            for lanes in groups[1:]:
                bad = jnp.maximum(bad, cand_group(h, lanes, False))

            @pl.when(jnp.max(bad) > 0.0)
            def _redo():
                for lanes in groups:
                    cand_group(h, lanes, True)

            r2b_ref[h] = rank_ref[2 * h + 1].astype(BF16)
            e2b_ref[h] = jnp.exp(e_ref[2 * h + 1] - vals_ref[2 * h + 1, 0:1, :]).astype(BF16)
            return carry

        lax.fori_loop(0, PEER_HEADS, cand_body, 0)

    key1 = pl.ds(pl.multiple_of(e * nslab, nslab), nslab)
    hs = nslab // 2
    hrows = hs * N_KEYS
    for half in range(2):
        act_ref[half * hrows:(half + 1) * hrows, :] = _dot((ua_ref, ub_ref)[half][...], ht_ref[...])
    for half in range(2):
        for gi in range(ng):
            lanes = slice(gi * LANE, (gi + 1) * LANE)
            w = [jnp.zeros((N_KEYS, LANE), BF16) for _ in range(hs)]
            for h in range(PEER_HEADS):
                r2 = r2b_ref[h, :, lanes]
                e2 = e2b_ref[h, :, lanes]
                cw = cnt_ref[h, key1, lanes]
                gw = gate_ref[h, key1, lanes]
                for jj in range(hs):
                    j = half * hs + jj
                    hit = r2 < _rows_bf16(cw[j:j + 1, :], N_KEYS)
                    w[jj] = w[jj] + jnp.where(hit, e2, jnp.zeros_like(e2)) * _rows_bf16(gw[j:j + 1, :], N_KEYS)
            for jj in range(hs):
                rows = slice((half * hs + jj) * N_KEYS, (half * hs + jj + 1) * N_KEYS)
                act = act_ref[rows, lanes]
                gelu = 0.5 * act * (1.0 + lax.erf(act * 0.7071067811865476))
                p_ref[rows, lanes] = w[jj] * gelu.astype(BF16)
        acc_ref[...] += _dot((vta_ref, vtb_ref)[half][0], p_ref[half * hrows:(half + 1) * hrows, :])

    @pl.when(e == ne - 1)
    def _finish():
        x2 = y_ref[...] + acc_ref[...].T
        y_ref[...] = x2 * lax.rsqrt(jnp.mean(x2 * x2, axis=-1, keepdims=True) + NORM_EPS) * gfin_ref[...]


def _peer(x2d, ml2d, rw2d, woml, worw, gffn, gfin, wqt, keys, u, vt, tm, ec):
    n = x2d.shape[0]
    n_exp = u.shape[0]
    assert ec == 8 * N_KEYS and tm % LANE == 0 and n % tm == 0 and n_exp % ec == 0
    tok = lambda i, e: (i, 0)
    const2 = lambda i, e: (0, 0)
    qrows = 2 * PEER_HEADS * PEER_HALF
    eh = ec // 2
    once = pl.Buffered(1)
    return pl.pallas_call(
        functools.partial(_peer_kernel, tm=tm, ec=ec),
        grid=(n // tm, n_exp // ec),
        in_specs=[
            pl.BlockSpec((tm, D_MODEL), tok),
            pl.BlockSpec((tm, ML_WIDTH), tok),
            pl.BlockSpec((tm, RW_WIDTH), tok),
            pl.BlockSpec((ML_WIDTH, D_MODEL), const2, pipeline_mode=once),
            pl.BlockSpec((RW_WIDTH, D_MODEL), const2, pipeline_mode=once),
            pl.BlockSpec((1, D_MODEL), const2),
            pl.BlockSpec((1, D_MODEL), const2),
            pl.BlockSpec((qrows, D_MODEL), const2, pipeline_mode=once),
            pl.BlockSpec((2 * PEER_HEADS, N_KEYS, PEER_HALF), lambda i, e: (0, 0, 0), pipeline_mode=once),
            pl.BlockSpec((eh, D_MODEL), lambda i, e: (2 * e, 0)),
            pl.BlockSpec((eh, D_MODEL), lambda i, e: (2 * e + 1, 0)),
            pl.BlockSpec((1, D_MODEL, eh), lambda i, e: (2 * e, 0, 0)),
            pl.BlockSpec((1, D_MODEL, eh), lambda i, e: (2 * e + 1, 0, 0)),
        ],
        out_specs=pl.BlockSpec((tm, D_MODEL), tok),
        out_shape=jax.ShapeDtypeStruct((n, D_MODEL), F32),
        scratch_shapes=[
            pltpu.VMEM((D_MODEL, tm), BF16),
            pltpu.VMEM((qrows, tm), F32),
            pltpu.VMEM((2 * PEER_HEADS, N_KEYS, tm), F32),
            pltpu.VMEM((2 * PEER_HEADS, N_KEYS, tm), F32),
            pltpu.VMEM((2 * PEER_HEADS, PEER_TOPK, tm), F32),
            pltpu.VMEM((PEER_HEADS, N_KEYS, tm), BF16),
            pltpu.VMEM((PEER_HEADS, N_KEYS, tm), BF16),
            pltpu.VMEM((PEER_HEADS, N_KEYS, tm), F32),
            pltpu.VMEM((PEER_HEADS, N_KEYS, tm), F32),
            pltpu.VMEM((ec, tm), BF16),
            pltpu.VMEM((D_MODEL, tm), F32),
        ],
        compiler_params=pltpu.CompilerParams(
            dimension_semantics=("arbitrary", "arbitrary"), vmem_limit_bytes=VMEM_LIMIT),
        name="peer",
    )(x2d, ml2d, rw2d, woml, worw, gffn, gfin, wqt, keys, u, u, vt, vt)


def _prep_weights(norm_mix_g, w_in, ml_conv_w, ml_conv_b, ml_b_i, ml_b_f, ml_norm_g,
                  rw_mu, rw_w0, rw_w2, rw_a0, rw_a2, rw_g2, rw_k_k, rw_k_a, rw_r_k, rw_gn_g, rw_gn_b,
                  w_out, norm_ffn_g, peer_w_q, peer_sub_keys, peer_u, peer_v, norm_final_g):
    assert w_in.shape[0] == 1, "one layer"
    w = w_in[0]
    wg = jnp.pad(w[:, ML_QKVO:ML_QKVO + 2 * ML_HEADS], ((0, 0), (0, LANE - 2 * ML_HEADS)))
    wg_hi = wg.astype(BF16)
    wg_lo = (wg - wg_hi.astype(F32)).astype(BF16)
    zeros = jnp.zeros((DECAY_LORA, RW_WIDTH), F32)
    wwa = jnp.concatenate([jnp.concatenate([rw_w2[0], zeros], axis=1),
                           jnp.concatenate([zeros, rw_a2[0]], axis=1)], axis=0)
    gate_bias = jnp.pad(jnp.concatenate([ml_b_i[0], ml_b_f[0]]), (0, LANE - 2 * ML_HEADS))[None, :]
    row = lambda a: a.reshape(1, -1)
    return dict(
        g_mix=row(norm_mix_g[0]),
        wml=w[:, 0:ML_QKVO].astype(BF16),
        wg=jnp.stack([wg_hi, wg_lo]),
        wrw=w[:, ML_QKVO + 2 * ML_HEADS:].astype(BF16),
        conv_w=ml_conv_w[0], conv_b=row(ml_conv_b[0]), gate_bias=gate_bias, ml_norm_g=row(ml_norm_g[0]),
        mu=row(rw_mu[0]), w0=row(rw_w0[0]), wwa=wwa.astype(BF16), a0=row(rw_a0[0]),
        g2=rw_g2[0].astype(BF16), k_k=row(rw_k_k[0]), k_a=row(rw_k_a[0]), r_k=row(rw_r_k[0]),
        gn_g=row(rw_gn_g[0]), gn_b=row(rw_gn_b[0]),
        wo_ml=w_out[0, 0:ML_WIDTH].astype(BF16), wo_rw=w_out[0, ML_WIDTH:].astype(BF16),
        g_ffn=row(norm_ffn_g[0]), g_fin=row(norm_final_g),
        wqt=peer_w_q[0].T.astype(BF16),
        keys=peer_sub_keys[0].reshape(2 * PEER_HEADS, N_KEYS, PEER_HALF).astype(BF16),
        u=peer_u[0].astype(BF16),
        vt=peer_v[0].reshape(-1, PEER_EXPERT_CHUNK // 2, D_MODEL).transpose(0, 2, 1).astype(BF16),
    )


def _trunk(x, states, wp, tr_ml, tr_rw, tm_in, tm_peer, ec):
    b, t, d = x.shape
    c0, n0, m0, conv0, s0, shift0 = (s[0] for s in states)
    x2d = x.reshape(b * t, d)
    ml, gates, rw = _proj_in(x2d, wp["g_mix"], wp["wml"], wp["wg"], wp["wrw"], tm_in)
    ml3 = ml.reshape(b, t, ML_QKVO)
    rw3 = rw.reshape(b, t, RW_COLS)
    ml_out, c1, n1, m1, conv1 = _mlstm(
        ml3, gates.reshape(b, t, LANE), c0, n0, m0.reshape(b, 1, ML_HEADS), conv0,
        wp["conv_w"], wp["conv_b"], wp["gate_bias"], wp["ml_norm_g"], tr_ml)
    prev = jnp.concatenate([shift0[:, None, :], rw3[:, :-1]], axis=1)
    rw_out, s1 = _rwkv(
        rw3, prev, s0.reshape(b, RW_WIDTH, RW_HD), wp["mu"], wp["w0"], wp["wwa"], wp["a0"], wp["g2"],
        wp["k_k"], wp["k_a"], wp["r_k"], wp["gn_g"], wp["gn_b"], tr_rw)
    y = _peer(x2d, ml_out.reshape(b * t, ML_WIDTH), rw_out.reshape(b * t, RW_WIDTH),
              wp["wo_ml"], wp["wo_rw"], wp["g_ffn"], wp["g_fin"], wp["wqt"], wp["keys"],
              wp["u"], wp["vt"], tm_peer, ec)
    new_states = (c1[None], n1[None], m1.reshape(b, ML_HEADS)[None], conv1[None],
                  s1.reshape(b, RW_HEADS, RW_HD, RW_HD)[None], rw3[:, -1][None])
    return y.reshape(b, t, d), new_states


def kernel(x_prompt, x_sample, state_mlstm_C, state_mlstm_n, state_mlstm_m, state_mlstm_conv, state_rwkv_S, state_rwkv_shift, norm_mix_g, w_in, ml_conv_w, ml_conv_b, ml_b_i, ml_b_f, ml_norm_g, rw_mu, rw_w0, rw_w2, rw_a0, rw_a2, rw_g2, rw_k_k, rw_k_a, rw_r_k, rw_gn_g, rw_gn_b, w_out, norm_ffn_g, peer_w_q, peer_sub_keys, peer_u, peer_v, norm_final_g):
    wp = _prep_weights(norm_mix_g, w_in, ml_conv_w, ml_conv_b, ml_b_i, ml_b_f, ml_norm_g,
                       rw_mu, rw_w0, rw_w2, rw_a0, rw_a2, rw_g2, rw_k_k, rw_k_a, rw_r_k, rw_gn_g, rw_gn_b,
                       w_out, norm_ffn_g, peer_w_q, peer_sub_keys, peer_u, peer_v, norm_final_g)
    bp = x_prompt.shape[0]
    z = lambda *s: jnp.zeros((1, bp) + s, F32)
    prompt_states = (z(ML_HEADS, ML_HD, ML_HD), z(ML_HEADS, ML_HD), jnp.full((1, bp, ML_HEADS), M_INIT, F32),
                     z(CONV_W - 1, 2 * ML_WIDTH), z(RW_HEADS, RW_HD, RW_HD), z(RW_COLS))
    sample_states = (state_mlstm_C, state_mlstm_n, state_mlstm_m, state_mlstm_conv,
                     state_rwkv_S, state_rwkv_shift)
    tp, ts = x_prompt.shape[1], x_sample.shape[1]
    np_, ns = bp * tp, x_sample.shape[0] * ts
    y_p, st_p = _trunk(x_prompt, prompt_states, wp, min(tp, ML_CHUNK), min(tp, RW_CHUNK),
                       min(np_, PROJ_TILE), min(np_, PEER_TILE), PEER_EXPERT_CHUNK)
    y_s, st_s = _trunk(x_sample, sample_states, wp, min(ts, ML_CHUNK), min(ts, RW_CHUNK),
                       min(ns, PROJ_TILE), min(ns, PEER_TILE), PEER_EXPERT_CHUNK)
    return (y_p, y_s) + tuple(st_p) + tuple(st_s)
```

```python
import functools

import jax
import jax.numpy as jnp
from jax import lax
from jax.experimental import pallas as pl
from jax.experimental.pallas import tpu as pltpu

F32 = jnp.float32
BF16 = jnp.bfloat16

D_MODEL = 1024
ML_HEADS = 4
ML_HD = 128
ML_WIDTH = ML_HEADS * ML_HD
CONV_W = 4
RW_HEADS = 8
RW_HD = 64
RW_WIDTH = RW_HEADS * RW_HD
DECAY_LORA = 64
AAA_LORA = 64
GATE_LORA = 128
RW_COLS = 3 * RW_WIDTH + DECAY_LORA + AAA_LORA + GATE_LORA
ML_QKVO = 4 * ML_WIDTH
PEER_HEADS = 8
N_KEYS = 128
PEER_TOPK = 16
PEER_HALF = 128
NORM_EPS = 1e-6
ML_NORM_EPS = 1e-6
GN_EPS = RW_HD * 1e-5
M_INIT = -1e30
NEG_BIG = -1e30

LANE = 128
ML_CHUNK = 128
RW_CHUNK = 64
RW_GROUP = 4
RW_BATCH = 4
PROJ_TILE = 256
PEER_TILE = 512
PEER_EXPERT_CHUNK = 1024
VMEM_LIMIT = 56 * 1024 * 1024


def _dot(a, b):
    return jnp.dot(a, b, preferred_element_type=F32)


def _dot_nt(a, b):
    return lax.dot_general(a, b, (((1,), (1,)), ((), ())), preferred_element_type=F32)


def _split_bf16(x, n):
    parts = []
    r = x
    for _ in range(n):
        p = r.astype(BF16)
        parts.append(p)
        r = r - p.astype(F32)
    return parts


def _dot_exact_lhs(mask_bf16, x, n):
    return sum(_dot(mask_bf16, p) for p in _split_bf16(x, n))


def _dot_exact_rhs(x, mask_bf16, n):
    return sum(_dot(p, mask_bf16) for p in _split_bf16(x, n))


def _sigmoid(x):
    return 1.0 / (1.0 + jnp.exp(-x))


def _softplus(x):
    return jnp.maximum(x, 0.0) + jnp.log1p(jnp.exp(-jnp.abs(x)))


def _iota(shape, dim):
    return lax.broadcasted_iota(jnp.int32, shape, dim)


def _pad_rows(x, rows, value=0.0):
    if x.shape[0] == rows:
        return x
    return jnp.concatenate([x, jnp.full((rows - x.shape[0], x.shape[1]), value, x.dtype)], axis=0)


def _proj_in_kernel(x_ref, g_ref, wml_ref, wg_ref, wrw_ref, ml_ref, gate_ref, rw_ref):
    x = x_ref[...]
    xn = x * lax.rsqrt(jnp.mean(x * x, axis=-1, keepdims=True) + NORM_EPS) * g_ref[...]
    xb = xn.astype(BF16)
    ml_ref[...] = _dot(xb, wml_ref[...])
    rw_ref[...] = _dot(xb, wrw_ref[...])
    xlo = (xn - xb.astype(F32)).astype(BF16)
    gate_ref[...] = _dot(xb, wg_ref[0]) + _dot(xlo, wg_ref[0]) + _dot(xb, wg_ref[1])


def _proj_in(x2d, g, wml, wg, wrw, tm):
    n = x2d.shape[0]
    const2 = lambda i: (0, 0)
    return pl.pallas_call(
        _proj_in_kernel,
        grid=(n // tm,),
        in_specs=[
            pl.BlockSpec((tm, D_MODEL), lambda i: (i, 0)),
            pl.BlockSpec((1, D_MODEL), const2),
            pl.BlockSpec((D_MODEL, ML_QKVO), const2),
            pl.BlockSpec((2, D_MODEL, LANE), lambda i: (0, 0, 0)),
            pl.BlockSpec((D_MODEL, RW_COLS), const2),
        ],
        out_specs=[
            pl.BlockSpec((tm, ML_QKVO), lambda i: (i, 0)),
            pl.BlockSpec((tm, LANE), lambda i: (i, 0)),
            pl.BlockSpec((tm, RW_COLS), lambda i: (i, 0)),
        ],
        out_shape=[
            jax.ShapeDtypeStruct((n, ML_QKVO), F32),
            jax.ShapeDtypeStruct((n, LANE), F32),
            jax.ShapeDtypeStruct((n, RW_COLS), F32),
        ],
        compiler_params=pltpu.CompilerParams(
            dimension_semantics=("arbitrary",), vmem_limit_bytes=VMEM_LIMIT),
        name="proj_in",
    )(x2d, g, wml, wg, wrw)


def _mlstm_kernel(ml_ref, gate_ref, c0_ref, n0_ref, m0_ref, conv0_ref, cw_ref, cb_ref, gb_ref, ng_ref,
                  out_ref, c1_ref, n1_ref, m1_ref, conv1_ref,
                  caug_ref, m_ref, ext_ref, *, tr):
    L = ML_CHUNK
    c = pl.program_id(1)
    nc = pl.num_programs(1)

    @pl.when(c == 0)
    def _init():
        for h in range(ML_HEADS):
            caug_ref[h, :, 0:ML_HD] = c0_ref[0, h]
            nrow = n0_ref[0, h:h + 1, :]
            caug_ref[h, :, ML_HD:2 * ML_HD] = jnp.broadcast_to(nrow, (ML_HD, ML_HD)).T
        m_ref[...] = jnp.zeros(m_ref.shape, F32)
        m_ref[:, 0:ML_HEADS] = m0_ref[0]
        ext_ref[5:8, :] = conv0_ref[0]

    ext_ref[8:8 + tr, :] = ml_ref[0, :, 0:2 * ML_WIDTH]
    acc = cb_ref[...] + ext_ref[5:5 + tr, :] * cw_ref[0:1, :]
    for j in range(1, CONV_W):
        acc = acc + ext_ref[5 + j:5 + j + tr, :] * cw_ref[j:j + 1, :]
    qk = acc * _sigmoid(acc)
    tail = ext_ref[tr + 5:tr + 8, :]
    ext_ref[5:8, :] = tail

    @pl.when(c == nc - 1)
    def _conv_out():
        conv1_ref[0] = tail

    q_all = _pad_rows(qk[:, 0:ML_WIDTH], L)
    k_all = _pad_rows(qk[:, ML_WIDTH:2 * ML_WIDTH] * (ML_HD ** -0.5), L)
    v_all = _pad_rows(ml_ref[0, :, 2 * ML_WIDTH:3 * ML_WIDTH], L)
    o_all = ml_ref[0, :, 3 * ML_WIDTH:4 * ML_WIDTH]

    g = gate_ref[0] + gb_ref[...]
    i_all = _pad_rows(g, L, NEG_BIG)
    lf_all = _pad_rows(-_softplus(-g), L, 0.0)
    row = _iota((L, L), 0)
    col = _iota((L, L), 1)
    causal = row >= col
    tri = jnp.where(causal, 1.0, 0.0).astype(BF16)
    b_col = _dot_exact_lhs(tri, lf_all, 3)
    b_t = b_col.T
    i_t = i_all.T

    ones = jnp.ones((L, ML_HD), BF16)
    for h in range(ML_HEADS):
        bc = b_col[:, ML_HEADS + h:ML_HEADS + h + 1]
        br = b_t[ML_HEADS + h:ML_HEADS + h + 1, :]
        ir = i_t[h:h + 1, :]
        m_prev = m_ref[:, h:h + 1]
        logd = jnp.where(causal, bc - br + ir, -jnp.inf)
        linter = bc + m_prev
        m_t = jnp.maximum(linter, jnp.max(logd, axis=1, keepdims=True))
        d = jnp.exp(logd - m_t)
        s_inter = jnp.exp(linter - m_t)
        sl = slice(h * ML_HD, (h + 1) * ML_HD)
        qh = q_all[:, sl].astype(BF16)
        k_t = k_all[:, sl].T
        vaug = jnp.concatenate([v_all[:, sl].astype(BF16), ones], axis=1)
        caug = caug_ref[h]
        s = _dot(qh, k_t.astype(BF16)) * d
        num = s_inter * _dot(qh, caug.astype(BF16)) + _dot(s.astype(BF16), vaug)
        den = num[:, ML_HD:2 * ML_HD]
        hh = num[:, 0:ML_HD] / jnp.maximum(jnp.abs(den), jnp.exp(-m_t))
        mu = jnp.mean(hh, axis=-1, keepdims=True)
        dv = hh - mu
        var = jnp.mean(dv * dv, axis=-1, keepdims=True)
        y = dv * lax.rsqrt(var + ML_NORM_EPS) * ng_ref[:, sl]
        out_ref[0, :, sl] = y[0:tr] * _sigmoid(o_all[:, sl])
        m_new = m_t[L - 1:L, :]
        b_last = bc[L - 1:L, :]
        s_state = jnp.exp(b_last + m_prev - m_new)
        w_row = jnp.exp(b_last - br + ir - m_new)
        caug_ref[h] = s_state * caug + _dot((k_t * w_row).astype(BF16), vaug)
        m_ref[:, h:h + 1] = m_new

    @pl.when(c == nc - 1)
    def _state_out():
        for h in range(ML_HEADS):
            caug = caug_ref[h]
            c1_ref[0, h] = caug[:, 0:ML_HD]
            n1_ref[0, h:h + 1, :] = caug[:, ML_HD:2 * ML_HD].T[0:1, :]
        m1_ref[0] = m_ref[:, 0:ML_HEADS]


def _mlstm(ml, gates, c0, n0, m0, conv0, cw, cb, gb, ng, tr):
    b, t, _ = ml.shape
    nc = t // tr
    bmap = lambda i, j: (i, 0, 0)
    const2 = lambda i, j: (0, 0)
    return pl.pallas_call(
        functools.partial(_mlstm_kernel, tr=tr),
        grid=(b, nc),
        in_specs=[
            pl.BlockSpec((1, tr, ML_QKVO), lambda i, j: (i, j, 0)),
            pl.BlockSpec((1, tr, LANE), lambda i, j: (i, j, 0)),
            pl.BlockSpec((1, ML_HEADS, ML_HD, ML_HD), lambda i, j: (i, 0, 0, 0)),
            pl.BlockSpec((1, ML_HEADS, ML_HD), bmap),
            pl.BlockSpec((1, 1, ML_HEADS), bmap),
            pl.BlockSpec((1, CONV_W - 1, 2 * ML_WIDTH), bmap),
            pl.BlockSpec((CONV_W, 2 * ML_WIDTH), const2),
            pl.BlockSpec((1, 2 * ML_WIDTH), const2),
            pl.BlockSpec((1, LANE), const2),
            pl.BlockSpec((1, ML_WIDTH), const2),
        ],
        out_specs=[
            pl.BlockSpec((1, tr, ML_WIDTH), lambda i, j: (i, j, 0)),
            pl.BlockSpec((1, ML_HEADS, ML_HD, ML_HD), lambda i, j: (i, 0, 0, 0)),
            pl.BlockSpec((1, ML_HEADS, ML_HD), bmap),
            pl.BlockSpec((1, 1, ML_HEADS), bmap),
            pl.BlockSpec((1, CONV_W - 1, 2 * ML_WIDTH), bmap),
        ],
        out_shape=[
            jax.ShapeDtypeStruct((b, t, ML_WIDTH), F32),
            jax.ShapeDtypeStruct((b, ML_HEADS, ML_HD, ML_HD), F32),
            jax.ShapeDtypeStruct((b, ML_HEADS, ML_HD), F32),
            jax.ShapeDtypeStruct((b, 1, ML_HEADS), F32),
            jax.ShapeDtypeStruct((b, CONV_W - 1, 2 * ML_WIDTH), F32),
        ],
        scratch_shapes=[
            pltpu.VMEM((ML_HEADS, ML_HD, 2 * ML_HD), F32),
            pltpu.VMEM((1, LANE), F32),
            pltpu.VMEM((tr + 8, 2 * ML_WIDTH), F32),
        ],
        compiler_params=pltpu.CompilerParams(
            dimension_semantics=("arbitrary", "arbitrary"), vmem_limit_bytes=VMEM_LIMIT),
        name="mlstm",
    )(ml, gates, c0, n0, m0, conv0, cw, cb, gb, ng)


def _rwkv_kernel(c_ref, p_ref, s0_ref, mu_ref, w0_ref, wwa_ref, a0_ref, g2_ref, kk_ref, ka_ref, rk_ref,
                 gng_ref, gnb_ref, out_ref, s1_ref, s_ref, *, tr, nb):
    L = RW_CHUNK
    W = RW_WIDTH
    GW = RW_GROUP * RW_HD
    GL = RW_GROUP * L
    NGRP = RW_HEADS // RW_GROUP
    ci = pl.program_id(1)
    nc = pl.num_programs(1)

    r2 = _iota((GL, GW), 0)
    c2 = _iota((GL, GW), 1)
    bd = (r2 // L) == (c2 // RW_HD)
    s_lower = r2 > c2
    i_lower = r2 >= c2

    @pl.when(ci == 0)
    def _init():
        for bi in range(nb):
            for gi in range(NGRP):
                x = s0_ref[bi, gi * GW:(gi + 1) * GW, :]
                s_ref[bi * NGRP + gi] = jnp.where(bd, jnp.concatenate([x] * RW_GROUP, axis=1), 0.0)

    c = c_ref[...].reshape(nb * tr, RW_COLS)
    xs = c + (p_ref[...].reshape(nb * tr, RW_COLS) - c) * mu_ref[...]
    r = xs[:, 0:W]
    k = xs[:, W:2 * W]
    v = xs[:, 2 * W:3 * W]
    slab = xs[:, 3 * W:3 * W + DECAY_LORA + AAA_LORA]
    gd = xs[:, 3 * W + DECAY_LORA + AAA_LORA:]
    lane = _iota(slab.shape, 1)
    t_in = jnp.where(lane < DECAY_LORA, jnp.tanh(slab), slab)
    la = _dot(t_in.astype(BF16), wwa_ref[...])
    w_log = -_softplus(-(w0_ref[...] + la[:, 0:W])) - 0.5
    lw_all = -jnp.exp(w_log)
    a_all = _sigmoid(a0_ref[...] + la[:, W:2 * W])
    g = _dot(_sigmoid(gd).astype(BF16), g2_ref[...])

    rs = _iota((W, W), 0)
    cs = _iota((W, W), 1)
    seg = jnp.where((rs // RW_HD) == (cs // RW_HD), 1.0, 0.0).astype(BF16)

    kk = k * kk_ref[...]
    kn_all = k * (1.0 + (a_all - 1.0) * ka_ref[...])
    ss = _dot_exact_rhs(kk * kk, seg, 2)
    kap_all = kk / jnp.maximum(jnp.sqrt(ss), 1e-12)
    bonus = _dot_exact_rhs(r * kn_all * rk_ref[...], seg, 2) * v

    rl = _iota((L, L), 0)
    cl_ = _iota((L, L), 1)
    tri = jnp.where(rl >= cl_, 1.0, 0.0).astype(BF16)

    def stack(x):
        return jnp.where(bd, jnp.concatenate([x] * RW_GROUP, axis=0), 0.0).astype(BF16)

    y_rows = []
    for bi in range(nb):
        rows = slice(bi * tr, (bi + 1) * tr)
        lw = _pad_rows(lw_all[rows], L)
        kap = _pad_rows(kap_all[rows], L)
        kn_p = _pad_rows(kn_all[rows], L)
        a_p = _pad_rows(a_all[rows], L)
        v_p = _pad_rows(v[rows], L)
        r_p = _pad_rows(r[rows], L)

        cum = _dot_exact_lhs(tri, lw, 3)
        cum_last = cum[L - 1:L, :]
        e_neg = jnp.exp(-cum)
        abar = -kap * jnp.exp(cum - lw)
        btil = kap * a_p * e_neg
        ktil = kn_p * e_neg
        rbar = r_p * jnp.exp(cum)
        e_rem = jnp.exp(cum_last - cum)
        bhat = kap * a_p * e_rem
        khat = kn_p * e_rem
        g_last = jnp.exp(cum_last)

        ys = []
        for gi in range(NGRP):
            ls = slice(gi * GW, (gi + 1) * GW)
            a_s, b_s, k_s, r_s, v_s = (stack(t[:, ls]) for t in (abar, btil, ktil, rbar, v_p))
            bh_s, kh_s = stack(bhat[:, ls]), stack(khat[:, ls])
            ar = jnp.concatenate([a_s, r_s], axis=0)
            bk = jnp.concatenate([b_s, k_s], axis=0)
            p = _dot_nt(ar, bk)
            a_ab = jnp.where(s_lower, p[0:GL, 0:GL], 0.0)
            a_ak = jnp.where(s_lower, p[0:GL, GL:2 * GL], 0.0)
            r_b = jnp.where(i_lower, p[GL:2 * GL, 0:GL], 0.0)
            r_k = jnp.where(i_lower, p[GL:2 * GL, GL:2 * GL], 0.0)
            s = s_ref[bi * NGRP + gi]
            q0 = _dot_nt(ar, s.astype(BF16))
            u = q0[0:GL] + _dot(a_ak.astype(BF16), v_s)
            npow = a_ab
            lvls = L.bit_length() - 1
            for lvl in range(lvls):
                u = u + _dot(npow.astype(BF16), u.astype(BF16))
                if lvl < lvls - 1:
                    npow = _dot(npow.astype(BF16), npow.astype(BF16))
            u_b = u.astype(BF16)
            y = q0[GL:2 * GL] + _dot(r_b.astype(BF16), u_b) + _dot(r_k.astype(BF16), v_s)
            yg = y[0:L]
            for j in range(1, RW_GROUP):
                yg = yg + y[j * L:(j + 1) * L]
            ys.append(yg)
            uv_t = jnp.concatenate([u, v_s.astype(F32)], axis=0).T.astype(BF16)
            bkh = jnp.concatenate([bh_s, kh_s], axis=0)
            s_ref[bi * NGRP + gi] = s * g_last[:, ls] + _dot(uv_t, bkh)
        y_rows.append(jnp.concatenate(ys, axis=1)[0:tr])

    y_all = jnp.concatenate(y_rows, axis=0) if nb > 1 else y_rows[0]
    inv = 1.0 / RW_HD
    mu_ = _dot_exact_rhs(y_all, seg, 2) * inv
    dy = y_all - mu_
    var = _dot_exact_rhs(dy * dy, seg, 2) * inv
    yn = dy * lax.rsqrt(var + GN_EPS) * gng_ref[...] + gnb_ref[...]
    out_ref[...] = ((yn + bonus) * g).reshape(nb, tr, W)

    @pl.when(ci == nc - 1)
    def _state_out():
        for bi in range(nb):
            for gi in range(NGRP):
                s = s_ref[bi * NGRP + gi]
                f = s[:, 0:RW_HD]
                for j in range(1, RW_GROUP):
                    f = f + s[:, j * RW_HD:(j + 1) * RW_HD]
                s1_ref[bi, gi * GW:(gi + 1) * GW, :] = f


def _rwkv(cols, prev, s0, mu, w0, wwa, a0, g2, kk, ka, rk, gng, gnb, tr, nb):
    b, t, _ = cols.shape
    nc = t // tr
    assert b % nb == 0 and t % tr == 0
    const2 = lambda i, j: (0, 0)
    vec = pl.BlockSpec((1, RW_WIDTH), const2)
    return pl.pallas_call(
        functools.partial(_rwkv_kernel, tr=tr, nb=nb),
        grid=(b // nb, nc),
        in_specs=[
            pl.BlockSpec((nb, tr, RW_COLS), lambda i, j: (i, j, 0)),
            pl.BlockSpec((nb, tr, RW_COLS), lambda i, j: (i, j, 0)),
            pl.BlockSpec((nb, RW_WIDTH, RW_HD), lambda i, j: (i, 0, 0)),
            pl.BlockSpec((1, RW_COLS), const2),
            vec,
            pl.BlockSpec((DECAY_LORA + AAA_LORA, 2 * RW_WIDTH), const2),
            vec,
            pl.BlockSpec((GATE_LORA, RW_WIDTH), const2),
            vec, vec, vec, vec, vec,
        ],
        out_specs=[
            pl.BlockSpec((nb, tr, RW_WIDTH), lambda i, j: (i, j, 0)),
            pl.BlockSpec((nb, RW_WIDTH, RW_HD), lambda i, j: (i, 0, 0)),
        ],
        out_shape=[
            jax.ShapeDtypeStruct((b, t, RW_WIDTH), F32),
            jax.ShapeDtypeStruct((b, RW_WIDTH, RW_HD), F32),
        ],
        scratch_shapes=[pltpu.VMEM((nb * (RW_HEADS // RW_GROUP), RW_GROUP * RW_HD, RW_GROUP * RW_HD), F32)],
        compiler_params=pltpu.CompilerParams(
            dimension_semantics=("arbitrary", "arbitrary"), vmem_limit_bytes=VMEM_LIMIT),
        name="rwkv",
    )(cols, prev, s0, mu, w0, wwa, a0, g2, kk, ka, rk, gng, gnb)


def _extract_top(work, idx, n):
    rank = jnp.full(work.shape, 99.0, F32)
    vals = []
    for j in range(n):
        m = jnp.max(work, axis=0, keepdims=True)
        if idx is None:
            sel = work == m
        else:
            first = jnp.min(jnp.where(work == m, idx, 1e9), axis=0, keepdims=True)
            sel = idx == first
        rank = jnp.where(sel, float(j), rank)
        work = jnp.where(sel, -jnp.inf, work)
        vals.append(m)
    return vals, rank


def _miscount(rank, n):
    taken = jnp.sum(jnp.where(rank < float(n), 1.0, 0.0), axis=0, keepdims=True)
    return jnp.abs(taken - float(n))


def _rows_bf16(row, rows):
    packed = jnp.broadcast_to(row, (16, row.shape[1])).astype(BF16)
    return jnp.concatenate([packed] * (rows // 16), axis=0)


def _peer_kernel(x_ref, ml_ref, rw_ref, woml_ref, worw_ref, gffn_ref, gfin_ref, wqt_ref, keys_ref,
                 ua_ref, ub_ref, vta_ref, vtb_ref, y_ref,
                 ht_ref, qt_ref, rank_ref, e_ref, vals_ref, r2b_ref, e2b_ref, cnt_ref, gate_ref,
                 p_ref, acc_ref, *, tm, ec):
    e = pl.program_id(1)
    ne = pl.num_programs(1)
    ng = tm // LANE
    K = PEER_TOPK
    nslab = ec // N_KEYS
    act_ref = qt_ref

    @pl.when(e == 0)
    def _select():
        x1 = (x_ref[...] + _dot(ml_ref[...].astype(BF16), woml_ref[...])
              + _dot(rw_ref[...].astype(BF16), worw_ref[...]))
        y_ref[...] = x1
        h = x1 * lax.rsqrt(jnp.mean(x1 * x1, axis=-1, keepdims=True) + NORM_EPS) * gffn_ref[...]
        ht = h.T.astype(BF16)
        ht_ref[...] = ht
        qt_ref[...] = _dot(wqt_ref[...], ht)
        acc_ref[...] = jnp.zeros(acc_ref.shape, F32)

        def score_body(hp, carry):
            q = qt_ref[pl.ds(pl.multiple_of(hp * PEER_HALF, PEER_HALF), PEER_HALF), :]
            e_ref[hp] = _dot(keys_ref[hp], q.astype(BF16))
            return carry

        lax.fori_loop(0, 2 * PEER_HEADS, score_body, 0)

        key_idx = _iota((N_KEYS, LANE), 0).astype(F32)
        groups = [slice(gi * LANE, (gi + 1) * LANE) for gi in range(ng)]

        def topk_group(hp, lanes, exact_ties):
            s = e_ref[hp, :, lanes]
            vals, rank = _extract_top(s, key_idx if exact_ties else None, K)
            rank_ref[hp, :, lanes] = rank
            vals_ref[hp, :, lanes] = jnp.concatenate(vals, axis=0)
            return _miscount(rank, K)

        def topk_body(hp, carry):
            bad = topk_group(hp, groups[0], False)
            for lanes in groups[1:]:
                bad = jnp.maximum(bad, topk_group(hp, lanes, False))

            @pl.when(jnp.max(bad) > 0.0)
            def _redo():
                for lanes in groups:
                    topk_group(hp, lanes, True)

            return carry

        lax.fori_loop(0, 2 * PEER_HEADS, topk_body, 0)

        sub8 = _iota((8, LANE), 0)

        def cand_group(h, lanes, exact_ties):
            v1 = vals_ref[2 * h, :, lanes]
            v2 = vals_ref[2 * h + 1, :, lanes]
            slabs, idxs, slab_a = [], [], []
            for a_i in range(K):
                nb = K // (a_i + 1)
                for b0 in range(0, nb, 8):
                    rows = v1[a_i:a_i + 1, :] + v2[b0:b0 + 8, :]
                    bidx = sub8 + b0
                    slabs.append(jnp.where(bidx < nb, rows, -jnp.inf))
                    idxs.append((bidx + a_i * K).astype(F32))
                    slab_a.append(a_i)
            work = jnp.concatenate(slabs, axis=0)
            idx = jnp.concatenate(idxs, axis=0) if exact_ties else None
            top = v1[0:1, :] + v2[0:1, :]
            vals, rank = _extract_top(work, idx, K)
            z = jnp.exp(vals[0] - top)
            for j in range(1, K):
                z = z + jnp.exp(vals[j] - top)
            picked = jnp.where(rank < float(K), 1.0, 0.0)
            r1 = rank_ref[2 * h, :, lanes]
            cnt = jnp.zeros((N_KEYS, LANE), F32)
            for a_i in range(K):
                ca = None
                for si, sa in enumerate(slab_a):
                    if sa == a_i:
                        part = jnp.sum(picked[si * 8:(si + 1) * 8], axis=0, keepdims=True)
                        ca = part if ca is None else ca + part
                cnt = cnt + jnp.where(r1 == float(a_i), ca, 0.0)
            cnt_ref[h, :, lanes] = cnt
            e1 = jnp.exp(e_ref[2 * h, :, lanes] - v1[0:1, :])
            gate_ref[h, :, lanes] = jnp.where(r1 < float(K), e1 / z, 0.0)
            return _miscount(rank, K)

        def cand_body(h, carry):
            bad = cand_group(h, groups[0], False)
            for lanes in groups[1:]:
                bad = jnp.maximum(bad, cand_group(h, lanes, False))

            @pl.when(jnp.max(bad) > 0.0)
            def _redo():
                for lanes in groups:
                    cand_group(h, lanes, True)

            r2b_ref[h] = rank_ref[2 * h + 1].astype(BF16)
            e2b_ref[h] = jnp.exp(e_ref[2 * h + 1] - vals_ref[2 * h + 1, 0:1, :]).astype(BF16)
            return carry

        lax.fori_loop(0, PEER_HEADS, cand_body, 0)

    key1 = pl.ds(pl.multiple_of(e * nslab, nslab), nslab)
    hs = nslab // 2
    hrows = hs * N_KEYS
    for half in range(2):
        act_ref[half * hrows:(half + 1) * hrows, :] = _dot((ua_ref, ub_ref)[half][...], ht_ref[...])
    for half in range(2):
        for gi in range(ng):
            lanes = slice(gi * LANE, (gi + 1) * LANE)
            w = [jnp.zeros((N_KEYS, LANE), BF16) for _ in range(hs)]
            for h in range(PEER_HEADS):
                r2 = r2b_ref[h, :, lanes]
                e2 = e2b_ref[h, :, lanes]
                cw = cnt_ref[h, key1, lanes]
                gw = gate_ref[h, key1, lanes]
                for jj in range(hs):
                    j = half * hs + jj
                    hit = r2 < _rows_bf16(cw[j:j + 1, :], N_KEYS)
                    w[jj] = w[jj] + jnp.where(hit, e2, jnp.zeros_like(e2)) * _rows_bf16(gw[j:j + 1, :], N_KEYS)
            for jj in range(hs):
                rows = slice((half * hs + jj) * N_KEYS, (half * hs + jj + 1) * N_KEYS)
                act = act_ref[rows, lanes]
                gelu = 0.5 * act * (1.0 + lax.erf(act * 0.7071067811865476))
                p_ref[rows, lanes] = w[jj] * gelu.astype(BF16)
        acc_ref[...] += _dot((vta_ref, vtb_ref)[half][0], p_ref[half * hrows:(half + 1) * hrows, :])

    @pl.when(e == ne - 1)
    def _finish():
        x2 = y_ref[...] + acc_ref[...].T
        y_ref[...] = x2 * lax.rsqrt(jnp.mean(x2 * x2, axis=-1, keepdims=True) + NORM_EPS) * gfin_ref[...]


def _peer(x2d, ml2d, rw2d, woml, worw, gffn, gfin, wqt, keys, u, vt, tm, ec):
    n = x2d.shape[0]
    n_exp = u.shape[0]
    assert ec == 8 * N_KEYS and tm % LANE == 0 and n % tm == 0 and n_exp % ec == 0
    tok = lambda i, e: (i, 0)
    const2 = lambda i, e: (0, 0)
    qrows = 2 * PEER_HEADS * PEER_HALF
    eh = ec // 2
    once = pl.Buffered(1)
    return pl.pallas_call(
        functools.partial(_peer_kernel, tm=tm, ec=ec),
        grid=(n // tm, n_exp // ec),
        in_specs=[
            pl.BlockSpec((tm, D_MODEL), tok),
            pl.BlockSpec((tm, ML_WIDTH), tok),
            pl.BlockSpec((tm, RW_WIDTH), tok),
            pl.BlockSpec((ML_WIDTH, D_MODEL), const2, pipeline_mode=once),
            pl.BlockSpec((RW_WIDTH, D_MODEL), const2, pipeline_mode=once),
            pl.BlockSpec((1, D_MODEL), const2),
            pl.BlockSpec((1, D_MODEL), const2),
            pl.BlockSpec((qrows, D_MODEL), const2, pipeline_mode=once),
            pl.BlockSpec((2 * PEER_HEADS, N_KEYS, PEER_HALF), lambda i, e: (0, 0, 0), pipeline_mode=once),
            pl.BlockSpec((eh, D_MODEL), lambda i, e: (2 * e, 0)),
            pl.BlockSpec((eh, D_MODEL), lambda i, e: (2 * e + 1, 0)),
            pl.BlockSpec((1, D_MODEL, eh), lambda i, e: (2 * e, 0, 0)),
            pl.BlockSpec((1, D_MODEL, eh), lambda i, e: (2 * e + 1, 0, 0)),
        ],
        out_specs=pl.BlockSpec((tm, D_MODEL), tok),
        out_shape=jax.ShapeDtypeStruct((n, D_MODEL), F32),
        scratch_shapes=[
            pltpu.VMEM((D_MODEL, tm), BF16),
            pltpu.VMEM((qrows, tm), F32),
            pltpu.VMEM((2 * PEER_HEADS, N_KEYS, tm), F32),
            pltpu.VMEM((2 * PEER_HEADS, N_KEYS, tm), F32),
            pltpu.VMEM((2 * PEER_HEADS, PEER_TOPK, tm), F32),
            pltpu.VMEM((PEER_HEADS, N_KEYS, tm), BF16),
            pltpu.VMEM((PEER_HEADS, N_KEYS, tm), BF16),
            pltpu.VMEM((PEER_HEADS, N_KEYS, tm), F32),
            pltpu.VMEM((PEER_HEADS, N_KEYS, tm), F32),
            pltpu.VMEM((ec, tm), BF16),
            pltpu.VMEM((D_MODEL, tm), F32),
        ],
        compiler_params=pltpu.CompilerParams(
            dimension_semantics=("arbitrary", "arbitrary"), vmem_limit_bytes=VMEM_LIMIT),
        name="peer",
    )(x2d, ml2d, rw2d, woml, worw, gffn, gfin, wqt, keys, u, u, vt, vt)


def _prep_weights(norm_mix_g, w_in, ml_conv_w, ml_conv_b, ml_b_i, ml_b_f, ml_norm_g,
                  rw_mu, rw_w0, rw_w2, rw_a0, rw_a2, rw_g2, rw_k_k, rw_k_a, rw_r_k, rw_gn_g, rw_gn_b,
                  w_out, norm_ffn_g, peer_w_q, peer_sub_keys, peer_u, peer_v, norm_final_g):
    assert w_in.shape[0] == 1, "one layer"
    w = w_in[0]
    wg = jnp.pad(w[:, ML_QKVO:ML_QKVO + 2 * ML_HEADS], ((0, 0), (0, LANE - 2 * ML_HEADS)))
    wg_hi = wg.astype(BF16)
    wg_lo = (wg - wg_hi.astype(F32)).astype(BF16)
    zeros = jnp.zeros((DECAY_LORA, RW_WIDTH), F32)
    wwa = jnp.concatenate([jnp.concatenate([rw_w2[0], zeros], axis=1),
                           jnp.concatenate([zeros, rw_a2[0]], axis=1)], axis=0)
    gate_bias = jnp.pad(jnp.concatenate([ml_b_i[0], ml_b_f[0]]), (0, LANE - 2 * ML_HEADS))[None, :]
    row = lambda a: a.reshape(1, -1)
    return dict(
        g_mix=row(norm_mix_g[0]),
        wml=w[:, 0:ML_QKVO].astype(BF16),
        wg=jnp.stack([wg_hi, wg_lo]),
        wrw=w[:, ML_QKVO + 2 * ML_HEADS:].astype(BF16),
        conv_w=ml_conv_w[0], conv_b=row(ml_conv_b[0]), gate_bias=gate_bias, ml_norm_g=row(ml_norm_g[0]),
        mu=row(rw_mu[0]), w0=row(rw_w0[0]), wwa=wwa.astype(BF16), a0=row(rw_a0[0]),
        g2=rw_g2[0].astype(BF16), k_k=row(rw_k_k[0]), k_a=row(rw_k_a[0]), r_k=row(rw_r_k[0]),
        gn_g=row(rw_gn_g[0]), gn_b=row(rw_gn_b[0]),
        wo_ml=w_out[0, 0:ML_WIDTH].astype(BF16), wo_rw=w_out[0, ML_WIDTH:].astype(BF16),
        g_ffn=row(norm_ffn_g[0]), g_fin=row(norm_final_g),
        wqt=peer_w_q[0].T.astype(BF16),
        keys=peer_sub_keys[0].reshape(2 * PEER_HEADS, N_KEYS, PEER_HALF).astype(BF16),
        u=peer_u[0].astype(BF16),
        vt=peer_v[0].reshape(-1, PEER_EXPERT_CHUNK // 2, D_MODEL).transpose(0, 2, 1).astype(BF16),
    )


def _trunk(x, states, wp, tr_ml, tr_rw, tm_in, tm_peer, ec):
    b, t, d = x.shape
    c0, n0, m0, conv0, s0, shift0 = (s[0] for s in states)
    x2d = x.reshape(b * t, d)
    ml, gates, rw = _proj_in(x2d, wp["g_mix"], wp["wml"], wp["wg"], wp["wrw"], tm_in)
    ml3 = ml.reshape(b, t, ML_QKVO)
    rw3 = rw.reshape(b, t, RW_COLS)
    ml_out, c1, n1, m1, conv1 = _mlstm(
        ml3, gates.reshape(b, t, LANE), c0, n0, m0.reshape(b, 1, ML_HEADS), conv0,
        wp["conv_w"], wp["conv_b"], wp["gate_bias"], wp["ml_norm_g"], tr_ml)
    prev = jnp.concatenate([shift0[:, None, :], rw3[:, :-1]], axis=1)
    rw_out, s1 = _rwkv(
        rw3, prev, s0.reshape(b, RW_WIDTH, RW_HD), wp["mu"], wp["w0"], wp["wwa"], wp["a0"], wp["g2"],
        wp["k_k"], wp["k_a"], wp["r_k"], wp["gn_g"], wp["gn_b"], tr_rw, min(b, RW_BATCH))
    y = _peer(x2d, ml_out.reshape(b * t, ML_WIDTH), rw_out.reshape(b * t, RW_WIDTH),
              wp["wo_ml"], wp["wo_rw"], wp["g_ffn"], wp["g_fin"], wp["wqt"], wp["keys"],
              wp["u"], wp["vt"], tm_peer, ec)
    new_states = (c1[None], n1[None], m1.reshape(b, ML_HEADS)[None], conv1[None],
                  s1.reshape(b, RW_HEADS, RW_HD, RW_HD)[None], rw3[:, -1][None])
    return y.reshape(b, t, d), new_states


def kernel(x_prompt, x_sample, state_mlstm_C, state_mlstm_n, state_mlstm_m, state_mlstm_conv, state_rwkv_S, state_rwkv_shift, norm_mix_g, w_in, ml_conv_w, ml_conv_b, ml_b_i, ml_b_f, ml_norm_g, rw_mu, rw_w0, rw_w2, rw_a0, rw_a2, rw_g2, rw_k_k, rw_k_a, rw_r_k, rw_gn_g, rw_gn_b, w_out, norm_ffn_g, peer_w_q, peer_sub_keys, peer_u, peer_v, norm_final_g):
    wp = _prep_weights(norm_mix_g, w_in, ml_conv_w, ml_conv_b, ml_b_i, ml_b_f, ml_norm_g,
                       rw_mu, rw_w0, rw_w2, rw_a0, rw_a2, rw_g2, rw_k_k, rw_k_a, rw_r_k, rw_gn_g, rw_gn_b,
                       w_out, norm_ffn_g, peer_w_q, peer_sub_keys, peer_u, peer_v, norm_final_g)
    bp = x_prompt.shape[0]
    z = lambda *s: jnp.zeros((1, bp) + s, F32)
    prompt_states = (z(ML_HEADS, ML_HD, ML_HD), z(ML_HEADS, ML_HD), jnp.full((1, bp, ML_HEADS), M_INIT, F32),
                     z(CONV_W - 1, 2 * ML_WIDTH), z(RW_HEADS, RW_HD, RW_HD), z(RW_COLS))
    sample_states = (state_mlstm_C, state_mlstm_n, state_mlstm_m, state_mlstm_conv,
                     state_rwkv_S, state_rwkv_shift)
    tp, ts = x_prompt.shape[1], x_sample.shape[1]
    np_, ns = bp * tp, x_sample.shape[0] * ts
    y_p, st_p = _trunk(x_prompt, prompt_states, wp, min(tp, ML_CHUNK), min(tp, RW_CHUNK),
                       min(np_, PROJ_TILE), min(np_, PEER_TILE), PEER_EXPERT_CHUNK)
    y_s, st_s = _trunk(x_sample, sample_states, wp, min(ts, ML_CHUNK), min(ts, RW_CHUNK),
                       min(ns, PROJ_TILE), min(ns, PEER_TILE), PEER_EXPERT_CHUNK)
    return (y_p, y_s) + tuple(st_p) + tuple(st_s)
```

```python
import functools

import jax
import jax.numpy as jnp
from jax import lax
from jax.experimental import pallas as pl
from jax.experimental.pallas import tpu as pltpu

F32 = jnp.float32
BF16 = jnp.bfloat16

D_MODEL = 1024
ML_HEADS = 4
ML_HD = 128
ML_WIDTH = ML_HEADS * ML_HD
CONV_W = 4
RW_HEADS = 8
RW_HD = 64
RW_WIDTH = RW_HEADS * RW_HD
DECAY_LORA = 64
AAA_LORA = 64
GATE_LORA = 128
RW_COLS = 3 * RW_WIDTH + DECAY_LORA + AAA_LORA + GATE_LORA
ML_QKVO = 4 * ML_WIDTH
PEER_HEADS = 8
N_KEYS = 128
PEER_TOPK = 16
PEER_HALF = 128
NORM_EPS = 1e-6
ML_NORM_EPS = 1e-6
GN_EPS = RW_HD * 1e-5
M_INIT = -1e30
NEG_BIG = -1e30

LANE = 128
ML_CHUNK = 128
RW_CHUNK = 64
RW_GROUP = 4
RW_BATCH = 4
PROJ_TILE = 256
PEER_TILE = 512
PEER_EXPERT_CHUNK = 1024
VMEM_LIMIT = 56 * 1024 * 1024


def _dot(a, b):
    return jnp.dot(a, b, preferred_element_type=F32)


def _dot_nt(a, b):
    return lax.dot_general(a, b, (((1,), (1,)), ((), ())), preferred_element_type=F32)


def _split_bf16(x, n):
    parts = []
    r = x
    for _ in range(n):
        p = r.astype(BF16)
        parts.append(p)
        r = r - p.astype(F32)
    return parts


def _dot_exact_lhs(mask_bf16, x, n):
    return sum(_dot(mask_bf16, p) for p in _split_bf16(x, n))


def _dot_exact_rhs(x, mask_bf16, n):
    return sum(_dot(p, mask_bf16) for p in _split_bf16(x, n))


def _sigmoid(x):
    return 1.0 / (1.0 + jnp.exp(-x))


def _softplus(x):
    return jnp.maximum(x, 0.0) + jnp.log1p(jnp.exp(-jnp.abs(x)))


def _iota(shape, dim):
    return lax.broadcasted_iota(jnp.int32, shape, dim)


def _pad_rows(x, rows, value=0.0):
    if x.shape[0] == rows:
        return x
    return jnp.concatenate([x, jnp.full((rows - x.shape[0], x.shape[1]), value, x.dtype)], axis=0)


def _proj_in_kernel(x_ref, g_ref, wml_ref, wg_ref, wrw_ref, ml_ref, gate_ref, rw_ref):
    x = x_ref[...]
    xn = x * lax.rsqrt(jnp.mean(x * x, axis=-1, keepdims=True) + NORM_EPS) * g_ref[...]
    xb = xn.astype(BF16)
    ml_ref[...] = _dot(xb, wml_ref[...])
    rw_ref[...] = _dot(xb, wrw_ref[...])
    xlo = (xn - xb.astype(F32)).astype(BF16)
    gate_ref[...] = _dot(xb, wg_ref[0]) + _dot(xlo, wg_ref[0]) + _dot(xb, wg_ref[1])


def _proj_in(x2d, g, wml, wg, wrw, tm):
    n = x2d.shape[0]
    const2 = lambda i: (0, 0)
    return pl.pallas_call(
        _proj_in_kernel,
        grid=(n // tm,),
        in_specs=[
            pl.BlockSpec((tm, D_MODEL), lambda i: (i, 0)),
            pl.BlockSpec((1, D_MODEL), const2),
            pl.BlockSpec((D_MODEL, ML_QKVO), const2),
            pl.BlockSpec((2, D_MODEL, LANE), lambda i: (0, 0, 0)),
            pl.BlockSpec((D_MODEL, RW_COLS), const2),
        ],
        out_specs=[
            pl.BlockSpec((tm, ML_QKVO), lambda i: (i, 0)),
            pl.BlockSpec((tm, LANE), lambda i: (i, 0)),
            pl.BlockSpec((tm, RW_COLS), lambda i: (i, 0)),
        ],
        out_shape=[
            jax.ShapeDtypeStruct((n, ML_QKVO), F32),
            jax.ShapeDtypeStruct((n, LANE), F32),
            jax.ShapeDtypeStruct((n, RW_COLS), F32),
        ],
        compiler_params=pltpu.CompilerParams(
            dimension_semantics=("arbitrary",), vmem_limit_bytes=VMEM_LIMIT),
        name="proj_in",
    )(x2d, g, wml, wg, wrw)


def _mlstm_kernel(ml_ref, gate_ref, c0_ref, n0_ref, m0_ref, conv0_ref, cw_ref, cb_ref, gb_ref, ng_ref,
                  out_ref, c1_ref, n1_ref, m1_ref, conv1_ref,
                  caug_ref, m_ref, ext_ref, *, tr):
    L = ML_CHUNK
    c = pl.program_id(1)
    nc = pl.num_programs(1)

    @pl.when(c == 0)
    def _init():
        for h in range(ML_HEADS):
            caug_ref[h, :, 0:ML_HD] = c0_ref[0, h]
            nrow = n0_ref[0, h:h + 1, :]
            caug_ref[h, :, ML_HD:2 * ML_HD] = jnp.broadcast_to(nrow, (ML_HD, ML_HD)).T
        m_ref[...] = jnp.zeros(m_ref.shape, F32)
        m_ref[:, 0:ML_HEADS] = m0_ref[0]
        ext_ref[5:8, :] = conv0_ref[0]

    ext_ref[8:8 + tr, :] = ml_ref[0, :, 0:2 * ML_WIDTH]
    acc = cb_ref[...] + ext_ref[5:5 + tr, :] * cw_ref[0:1, :]
    for j in range(1, CONV_W):
        acc = acc + ext_ref[5 + j:5 + j + tr, :] * cw_ref[j:j + 1, :]
    qk = acc * _sigmoid(acc)
    tail = ext_ref[tr + 5:tr + 8, :]
    ext_ref[5:8, :] = tail

    @pl.when(c == nc - 1)
    def _conv_out():
        conv1_ref[0] = tail

    q_all = _pad_rows(qk[:, 0:ML_WIDTH], L)
    k_all = _pad_rows(qk[:, ML_WIDTH:2 * ML_WIDTH] * (ML_HD ** -0.5), L)
    v_all = _pad_rows(ml_ref[0, :, 2 * ML_WIDTH:3 * ML_WIDTH], L)
    o_all = ml_ref[0, :, 3 * ML_WIDTH:4 * ML_WIDTH]

    g = gate_ref[0] + gb_ref[...]
    i_all = _pad_rows(g, L, NEG_BIG)
    lf_all = _pad_rows(-_softplus(-g), L, 0.0)
    row = _iota((L, L), 0)
    col = _iota((L, L), 1)
    causal = row >= col
    tri = jnp.where(causal, 1.0, 0.0).astype(BF16)
    b_col = _dot_exact_lhs(tri, lf_all, 3)
    b_t = b_col.T
    i_t = i_all.T

    ones = jnp.ones((L, ML_HD), BF16)
    for h in range(ML_HEADS):
        bc = b_col[:, ML_HEADS + h:ML_HEADS + h + 1]
        br = b_t[ML_HEADS + h:ML_HEADS + h + 1, :]
        ir = i_t[h:h + 1, :]
        m_prev = m_ref[:, h:h + 1]
        logd = jnp.where(causal, bc - br + ir, -jnp.inf)
        linter = bc + m_prev
        m_t = jnp.maximum(linter, jnp.max(logd, axis=1, keepdims=True))
        d = jnp.exp(logd - m_t)
        s_inter = jnp.exp(linter - m_t)
        sl = slice(h * ML_HD, (h + 1) * ML_HD)
        qh = q_all[:, sl].astype(BF16)
        k_t = k_all[:, sl].T
        vaug = jnp.concatenate([v_all[:, sl].astype(BF16), ones], axis=1)
        caug = caug_ref[h]
        s = _dot(qh, k_t.astype(BF16)) * d
        num = s_inter * _dot(qh, caug.astype(BF16)) + _dot(s.astype(BF16), vaug)
        den = num[:, ML_HD:2 * ML_HD]
        hh = num[:, 0:ML_HD] / jnp.maximum(jnp.abs(den), jnp.exp(-m_t))
        mu = jnp.mean(hh, axis=-1, keepdims=True)
        dv = hh - mu
        var = jnp.mean(dv * dv, axis=-1, keepdims=True)
        y = dv * lax.rsqrt(var + ML_NORM_EPS) * ng_ref[:, sl]
        out_ref[0, :, sl] = y[0:tr] * _sigmoid(o_all[:, sl])
        m_new = m_t[L - 1:L, :]
        b_last = bc[L - 1:L, :]
        s_state = jnp.exp(b_last + m_prev - m_new)
        w_row = jnp.exp(b_last - br + ir - m_new)
        caug_ref[h] = s_state * caug + _dot((k_t * w_row).astype(BF16), vaug)
        m_ref[:, h:h + 1] = m_new

    @pl.when(c == nc - 1)
    def _state_out():
        for h in range(ML_HEADS):
            caug = caug_ref[h]
            c1_ref[0, h] = caug[:, 0:ML_HD]
            n1_ref[0, h:h + 1, :] = caug[:, ML_HD:2 * ML_HD].T[0:1, :]
        m1_ref[0] = m_ref[:, 0:ML_HEADS]


def _mlstm(ml, gates, c0, n0, m0, conv0, cw, cb, gb, ng, tr):
    b, t, _ = ml.shape
    nc = t // tr
    bmap = lambda i, j: (i, 0, 0)
    const2 = lambda i, j: (0, 0)
    return pl.pallas_call(
        functools.partial(_mlstm_kernel, tr=tr),
        grid=(b, nc),
        in_specs=[
            pl.BlockSpec((1, tr, ML_QKVO), lambda i, j: (i, j, 0)),
            pl.BlockSpec((1, tr, LANE), lambda i, j: (i, j, 0)),
            pl.BlockSpec((1, ML_HEADS, ML_HD, ML_HD), lambda i, j: (i, 0, 0, 0)),
            pl.BlockSpec((1, ML_HEADS, ML_HD), bmap),
            pl.BlockSpec((1, 1, ML_HEADS), bmap),
            pl.BlockSpec((1, CONV_W - 1, 2 * ML_WIDTH), bmap),
            pl.BlockSpec((CONV_W, 2 * ML_WIDTH), const2),
            pl.BlockSpec((1, 2 * ML_WIDTH), const2),
            pl.BlockSpec((1, LANE), const2),
            pl.BlockSpec((1, ML_WIDTH), const2),
        ],
        out_specs=[
            pl.BlockSpec((1, tr, ML_WIDTH), lambda i, j: (i, j, 0)),
            pl.BlockSpec((1, ML_HEADS, ML_HD, ML_HD), lambda i, j: (i, 0, 0, 0)),
            pl.BlockSpec((1, ML_HEADS, ML_HD), bmap),
            pl.BlockSpec((1, 1, ML_HEADS), bmap),
            pl.BlockSpec((1, CONV_W - 1, 2 * ML_WIDTH), bmap),
        ],
        out_shape=[
            jax.ShapeDtypeStruct((b, t, ML_WIDTH), F32),
            jax.ShapeDtypeStruct((b, ML_HEADS, ML_HD, ML_HD), F32),
            jax.ShapeDtypeStruct((b, ML_HEADS, ML_HD), F32),
            jax.ShapeDtypeStruct((b, 1, ML_HEADS), F32),
            jax.ShapeDtypeStruct((b, CONV_W - 1, 2 * ML_WIDTH), F32),
        ],
        scratch_shapes=[
            pltpu.VMEM((ML_HEADS, ML_HD, 2 * ML_HD), F32),
            pltpu.VMEM((1, LANE), F32),
            pltpu.VMEM((tr + 8, 2 * ML_WIDTH), F32),
        ],
        compiler_params=pltpu.CompilerParams(
            dimension_semantics=("arbitrary", "arbitrary"), vmem_limit_bytes=VMEM_LIMIT),
        name="mlstm",
    )(ml, gates, c0, n0, m0, conv0, cw, cb, gb, ng)


def _rwkv_kernel(c_ref, p_ref, s0_ref, mu_ref, w0_ref, wwa_ref, a0_ref, g2_ref, kk_ref, ka_ref, rk_ref,
                 gng_ref, gnb_ref, out_ref, s1_ref, s_ref, *, tr, nb):
    L = RW_CHUNK
    W = RW_WIDTH
    GW = RW_GROUP * RW_HD
    GL = RW_GROUP * L
    NGRP = RW_HEADS // RW_GROUP
    ci = pl.program_id(1)
    nc = pl.num_programs(1)

    r2 = _iota((GL, GW), 0)
    c2 = _iota((GL, GW), 1)
    bd = (r2 // L) == (c2 // RW_HD)
    s_lower = r2 > c2
    i_lower = r2 >= c2

    @pl.when(ci == 0)
    def _init():
        for bi in range(nb):
            for gi in range(NGRP):
                x = s0_ref[bi, gi * GW:(gi + 1) * GW, :]
                s_ref[bi * NGRP + gi] = jnp.where(bd, jnp.concatenate([x] * RW_GROUP, axis=1), 0.0)

    c = c_ref[...].reshape(nb * tr, RW_COLS)
    xs = c + (p_ref[...].reshape(nb * tr, RW_COLS) - c) * mu_ref[...]
    r = xs[:, 0:W]
    k = xs[:, W:2 * W]
    v = xs[:, 2 * W:3 * W]
    slab = xs[:, 3 * W:3 * W + DECAY_LORA + AAA_LORA]
    gd = xs[:, 3 * W + DECAY_LORA + AAA_LORA:]
    lane = _iota(slab.shape, 1)
    t_in = jnp.where(lane < DECAY_LORA, jnp.tanh(slab), slab)
    la = _dot(t_in.astype(BF16), wwa_ref[...])
    w_log = -_softplus(-(w0_ref[...] + la[:, 0:W])) - 0.5
    lw_all = -jnp.exp(w_log)
    a_all = _sigmoid(a0_ref[...] + la[:, W:2 * W])
    g = _dot(_sigmoid(gd).astype(BF16), g2_ref[...])

    rs = _iota((W, W), 0)
    cs = _iota((W, W), 1)
    seg = jnp.where((rs // RW_HD) == (cs // RW_HD), 1.0, 0.0).astype(BF16)

    kk = k * kk_ref[...]
    kn_all = k * (1.0 + (a_all - 1.0) * ka_ref[...])
    ss = _dot_exact_rhs(kk * kk, seg, 2)
    kap_all = kk / jnp.maximum(jnp.sqrt(ss), 1e-12)
    bonus = _dot_exact_rhs(r * kn_all * rk_ref[...], seg, 2) * v

    rl = _iota((L, L), 0)
    cl_ = _iota((L, L), 1)
    tri = jnp.where(rl >= cl_, 1.0, 0.0).astype(BF16)

    def stack(x):
        return jnp.where(bd, jnp.concatenate([x] * RW_GROUP, axis=0), 0.0).astype(BF16)

    y_rows = []
    for bi in range(nb):
        rows = slice(bi * tr, (bi + 1) * tr)
        lw = _pad_rows(lw_all[rows], L)
        kap = _pad_rows(kap_all[rows], L)
        kn_p = _pad_rows(kn_all[rows], L)
        a_p = _pad_rows(a_all[rows], L)
        v_p = _pad_rows(v[rows], L)
        r_p = _pad_rows(r[rows], L)

        cum = _dot_exact_lhs(tri, lw, 3)
        cum_last = cum[L - 1:L, :]
        e_neg = jnp.exp(-cum)
        abar = -kap * jnp.exp(cum - lw)
        btil = kap * a_p * e_neg
        ktil = kn_p * e_neg
        rbar = r_p * jnp.exp(cum)
        e_rem = jnp.exp(cum_last - cum)
        bhat = kap * a_p * e_rem
        khat = kn_p * e_rem
        g_last = jnp.exp(cum_last)

        ys = []
        for gi in range(NGRP):
            ls = slice(gi * GW, (gi + 1) * GW)
            a_s, b_s, k_s, r_s, v_s = (stack(t[:, ls]) for t in (abar, btil, ktil, rbar, v_p))
            bh_s, kh_s = stack(bhat[:, ls]), stack(khat[:, ls])
            ar = jnp.concatenate([a_s, r_s], axis=0)
            bk = jnp.concatenate([b_s, k_s], axis=0)
            p = _dot_nt(ar, bk)
            a_ab = jnp.where(s_lower, p[0:GL, 0:GL], 0.0)
            a_ak = jnp.where(s_lower, p[0:GL, GL:2 * GL], 0.0)
            r_b = jnp.where(i_lower, p[GL:2 * GL, 0:GL], 0.0)
            r_k = jnp.where(i_lower, p[GL:2 * GL, GL:2 * GL], 0.0)
            s = s_ref[bi * NGRP + gi]
            q0 = _dot_nt(ar, s.astype(BF16))
            u = q0[0:GL] + _dot(a_ak.astype(BF16), v_s)
            npow = a_ab
            lvls = L.bit_length() - 1
            for lvl in range(lvls):
                u = u + _dot(npow.astype(BF16), u.astype(BF16))
                if lvl < lvls - 1:
                    npow = _dot(npow.astype(BF16), npow.astype(BF16))
            u_b = u.astype(BF16)
            y = q0[GL:2 * GL] + _dot(r_b.astype(BF16), u_b) + _dot(r_k.astype(BF16), v_s)
            yg = y[0:L]
            for j in range(1, RW_GROUP):
                yg = yg + y[j * L:(j + 1) * L]
            ys.append(yg)
            uv_t = jnp.concatenate([u, v_s.astype(F32)], axis=0).T.astype(BF16)
            bkh = jnp.concatenate([bh_s, kh_s], axis=0)
            s_ref[bi * NGRP + gi] = s * g_last[:, ls] + _dot(uv_t, bkh)
        y_rows.append(jnp.concatenate(ys, axis=1)[0:tr])

    y_all = jnp.concatenate(y_rows, axis=0) if nb > 1 else y_rows[0]
    inv = 1.0 / RW_HD
    mu_ = _dot_exact_rhs(y_all, seg, 2) * inv
    dy = y_all - mu_
    var = _dot_exact_rhs(dy * dy, seg, 2) * inv
    yn = dy * lax.rsqrt(var + GN_EPS) * gng_ref[...] + gnb_ref[...]
    out_ref[...] = ((yn + bonus) * g).reshape(nb, tr, W)

    @pl.when(ci == nc - 1)
    def _state_out():
        for bi in range(nb):
            for gi in range(NGRP):
                s = s_ref[bi * NGRP + gi]
                f = s[:, 0:RW_HD]
                for j in range(1, RW_GROUP):
                    f = f + s[:, j * RW_HD:(j + 1) * RW_HD]
                s1_ref[bi, gi * GW:(gi + 1) * GW, :] = f


def _rwkv(cols, prev, s0, mu, w0, wwa, a0, g2, kk, ka, rk, gng, gnb, tr, nb):
    b, t, _ = cols.shape
    nc = t // tr
    assert b % nb == 0 and t % tr == 0
    const2 = lambda i, j: (0, 0)
    vec = pl.BlockSpec((1, RW_WIDTH), const2)
    return pl.pallas_call(
        functools.partial(_rwkv_kernel, tr=tr, nb=nb),
        grid=(b // nb, nc),
        in_specs=[
            pl.BlockSpec((nb, tr, RW_COLS), lambda i, j: (i, j, 0)),
            pl.BlockSpec((nb, tr, RW_COLS), lambda i, j: (i, j, 0)),
            pl.BlockSpec((nb, RW_WIDTH, RW_HD), lambda i, j: (i, 0, 0)),
            pl.BlockSpec((1, RW_COLS), const2),
            vec,
            pl.BlockSpec((DECAY_LORA + AAA_LORA, 2 * RW_WIDTH), const2),
            vec,
            pl.BlockSpec((GATE_LORA, RW_WIDTH), const2),
            vec, vec, vec, vec, vec,
        ],
        out_specs=[
            pl.BlockSpec((nb, tr, RW_WIDTH), lambda i, j: (i, j, 0)),
            pl.BlockSpec((nb, RW_WIDTH, RW_HD), lambda i, j: (i, 0, 0)),
        ],
        out_shape=[
            jax.ShapeDtypeStruct((b, t, RW_WIDTH), F32),
            jax.ShapeDtypeStruct((b, RW_WIDTH, RW_HD), F32),
        ],
        scratch_shapes=[pltpu.VMEM((nb * (RW_HEADS // RW_GROUP), RW_GROUP * RW_HD, RW_GROUP * RW_HD), F32)],
        compiler_params=pltpu.CompilerParams(
            dimension_semantics=("arbitrary", "arbitrary"), vmem_limit_bytes=VMEM_LIMIT),
        name="rwkv",
    )(cols, prev, s0, mu, w0, wwa, a0, g2, kk, ka, rk, gng, gnb)


def _extract_top(work, idx, n):
    rank = jnp.full(work.shape, 99.0, F32)
    vals = []
    for j in range(n):
        m = jnp.max(work, axis=0, keepdims=True)
        if idx is None:
            sel = work == m
        else:
            first = jnp.min(jnp.where(work == m, idx, 1e9), axis=0, keepdims=True)
            sel = idx == first
        rank = jnp.where(sel, float(j), rank)
        work = jnp.where(sel, -jnp.inf, work)
        vals.append(m)
    return vals, rank


def _miscount(rank, n):
    taken = jnp.sum(jnp.where(rank < float(n), 1.0, 0.0), axis=0, keepdims=True)
    return jnp.abs(taken - float(n))


def _peer_kernel(x_ref, ml_ref, rw_ref, woml_ref, worw_ref, gffn_ref, gfin_ref, wqt_ref, keys_ref,
                 ua_ref, ub_ref, vta_ref, vtb_ref, y_ref,
                 ht_ref, qt_ref, rank_ref, e_ref, vals_ref, cnt_ref, gate_ref,
                 p_ref, acc_ref, *, tm, ec):
    e = pl.program_id(1)
    ne = pl.num_programs(1)
    ng = tm // LANE
    K = PEER_TOPK
    nslab = ec // N_KEYS
    act_ref = qt_ref

    @pl.when(e == 0)
    def _select():
        x1 = (x_ref[...] + _dot(ml_ref[...].astype(BF16), woml_ref[...])
              + _dot(rw_ref[...].astype(BF16), worw_ref[...]))
        y_ref[...] = x1
        h = x1 * lax.rsqrt(jnp.mean(x1 * x1, axis=-1, keepdims=True) + NORM_EPS) * gffn_ref[...]
        ht = h.T.astype(BF16)
        ht_ref[...] = ht
        qt_ref[...] = _dot(wqt_ref[...], ht)
        acc_ref[...] = jnp.zeros(acc_ref.shape, F32)

        def score_body(hp, carry):
            q = qt_ref[pl.ds(pl.multiple_of(hp * PEER_HALF, PEER_HALF), PEER_HALF), :]
            e_ref[hp] = _dot(keys_ref[hp], q.astype(BF16))
            return carry

        lax.fori_loop(0, 2 * PEER_HEADS, score_body, 0)

        key_idx = _iota((N_KEYS, LANE), 0).astype(F32)
        groups = [slice(gi * LANE, (gi + 1) * LANE) for gi in range(ng)]

        def topk_group(hp, lanes, exact_ties):
            s = e_ref[hp, :, lanes]
            vals, rank = _extract_top(s, key_idx if exact_ties else None, K)
            rank_ref[hp, :, lanes] = rank
            vals_ref[hp, :, lanes] = jnp.concatenate(vals, axis=0)
            return _miscount(rank, K)

        def topk_body(hp, carry):
            bad = topk_group(hp, groups[0], False)
            for lanes in groups[1:]:
                bad = jnp.maximum(bad, topk_group(hp, lanes, False))

            @pl.when(jnp.max(bad) > 0.0)
            def _redo():
                for lanes in groups:
                    topk_group(hp, lanes, True)

            return carry

        lax.fori_loop(0, 2 * PEER_HEADS, topk_body, 0)

        sub8 = _iota((8, LANE), 0)

        def cand_group(h, lanes, exact_ties):
            v1 = vals_ref[2 * h, :, lanes]
            v2 = vals_ref[2 * h + 1, :, lanes]
            slabs, idxs, slab_a = [], [], []
            for a_i in range(K):
                nb = K // (a_i + 1)
                for b0 in range(0, nb, 8):
                    rows = v1[a_i:a_i + 1, :] + v2[b0:b0 + 8, :]
                    bidx = sub8 + b0
                    slabs.append(jnp.where(bidx < nb, rows, -jnp.inf))
                    idxs.append((bidx + a_i * K).astype(F32))
                    slab_a.append(a_i)
            work = jnp.concatenate(slabs, axis=0)
            idx = jnp.concatenate(idxs, axis=0) if exact_ties else None
            top = v1[0:1, :] + v2[0:1, :]
            vals, rank = _extract_top(work, idx, K)
            z = jnp.exp(vals[0] - top)
            for j in range(1, K):
                z = z + jnp.exp(vals[j] - top)
            picked = jnp.where(rank < float(K), 1.0, 0.0)
            r1 = rank_ref[2 * h, :, lanes]
            cnt = jnp.zeros((N_KEYS, LANE), F32)
            for a_i in range(K):
                ca = None
                for si, sa in enumerate(slab_a):
                    if sa == a_i:
                        part = jnp.sum(picked[si * 8:(si + 1) * 8], axis=0, keepdims=True)
                        ca = part if ca is None else ca + part
                cnt = cnt + jnp.where(r1 == float(a_i), ca, 0.0)
            cnt_ref[h, :, lanes] = cnt
            e1 = jnp.exp(e_ref[2 * h, :, lanes] - v1[0:1, :])
            gate_ref[h, :, lanes] = jnp.where(r1 < float(K), e1 / z, 0.0)
            return _miscount(rank, K)

        def cand_body(h, carry):
            bad = cand_group(h, groups[0], False)
            for lanes in groups[1:]:
                bad = jnp.maximum(bad, cand_group(h, lanes, False))

            @pl.when(jnp.max(bad) > 0.0)
            def _redo():
                for lanes in groups:
                    cand_group(h, lanes, True)

            e_ref[2 * h + 1] = jnp.exp(e_ref[2 * h + 1] - vals_ref[2 * h + 1, 0:1, :])
            return carry

        lax.fori_loop(0, PEER_HEADS, cand_body, 0)

    key1 = pl.ds(pl.multiple_of(e * nslab, nslab), nslab)
    hs = nslab // 2
    hrows = hs * N_KEYS
    for half in range(2):
        act_ref[half * hrows:(half + 1) * hrows, :] = _dot((ua_ref, ub_ref)[half][...], ht_ref[...])
    for half in range(2):
        for gi in range(ng):
            lanes = slice(gi * LANE, (gi + 1) * LANE)
            cw = [cnt_ref[h, key1, lanes] for h in range(PEER_HEADS)]
            gw = [gate_ref[h, key1, lanes] for h in range(PEER_HEADS)]
            for jp in range(0, hs, 2):
                for k2 in (slice(0, N_KEYS // 2), slice(N_KEYS // 2, N_KEYS)):
                    w = [jnp.zeros((N_KEYS // 2, LANE), F32) for _ in range(2)]
                    for h in range(PEER_HEADS):
                        r2 = rank_ref[2 * h + 1, k2, lanes]
                        e2 = e_ref[2 * h + 1, k2, lanes]
                        for jj in range(2):
                            j = half * hs + jp + jj
                            w[jj] = w[jj] + jnp.where(r2 < cw[h][j:j + 1, :], e2, 0.0) * gw[h][j:j + 1, :]
                    for jj in range(2):
                        j = half * hs + jp + jj
                        rows = slice(j * N_KEYS + k2.start, j * N_KEYS + k2.stop)
                        act = act_ref[rows, lanes]
                        gelu = 0.5 * act * (1.0 + lax.erf(act * 0.7071067811865476))
                        p_ref[rows, lanes] = (w[jj] * gelu).astype(BF16)
        acc_ref[...] += _dot((vta_ref, vtb_ref)[half][0], p_ref[half * hrows:(half + 1) * hrows, :])

    @pl.when(e == ne - 1)
    def _finish():
        x2 = y_ref[...] + acc_ref[...].T
        y_ref[...] = x2 * lax.rsqrt(jnp.mean(x2 * x2, axis=-1, keepdims=True) + NORM_EPS) * gfin_ref[...]


def _peer(x2d, ml2d, rw2d, woml, worw, gffn, gfin, wqt, keys, u, vt, tm, ec):
    n = x2d.shape[0]
    n_exp = u.shape[0]
    assert ec == 8 * N_KEYS and tm % LANE == 0 and n % tm == 0 and n_exp % ec == 0
    tok = lambda i, e: (i, 0)
    const2 = lambda i, e: (0, 0)
    qrows = 2 * PEER_HEADS * PEER_HALF
    eh = ec // 2
    once = pl.Buffered(1)
    return pl.pallas_call(
        functools.partial(_peer_kernel, tm=tm, ec=ec),
        grid=(n // tm, n_exp // ec),
        in_specs=[
            pl.BlockSpec((tm, D_MODEL), tok),
            pl.BlockSpec((tm, ML_WIDTH), tok),
            pl.BlockSpec((tm, RW_WIDTH), tok),
            pl.BlockSpec((ML_WIDTH, D_MODEL), const2, pipeline_mode=once),
            pl.BlockSpec((RW_WIDTH, D_MODEL), const2, pipeline_mode=once),
            pl.BlockSpec((1, D_MODEL), const2),
            pl.BlockSpec((1, D_MODEL), const2),
            pl.BlockSpec((qrows, D_MODEL), const2, pipeline_mode=once),
            pl.BlockSpec((2 * PEER_HEADS, N_KEYS, PEER_HALF), lambda i, e: (0, 0, 0), pipeline_mode=once),
            pl.BlockSpec((eh, D_MODEL), lambda i, e: (2 * e, 0)),
            pl.BlockSpec((eh, D_MODEL), lambda i, e: (2 * e + 1, 0)),
            pl.BlockSpec((1, D_MODEL, eh), lambda i, e: (2 * e, 0, 0)),
            pl.BlockSpec((1, D_MODEL, eh), lambda i, e: (2 * e + 1, 0, 0)),
        ],
        out_specs=pl.BlockSpec((tm, D_MODEL), tok),
        out_shape=jax.ShapeDtypeStruct((n, D_MODEL), F32),
        scratch_shapes=[
            pltpu.VMEM((D_MODEL, tm), BF16),
            pltpu.VMEM((qrows, tm), F32),
            pltpu.VMEM((2 * PEER_HEADS, N_KEYS, tm), F32),
            pltpu.VMEM((2 * PEER_HEADS, N_KEYS, tm), F32),
            pltpu.VMEM((2 * PEER_HEADS, PEER_TOPK, tm), F32),
            pltpu.VMEM((PEER_HEADS, N_KEYS, tm), F32),
            pltpu.VMEM((PEER_HEADS, N_KEYS, tm), F32),
            pltpu.VMEM((ec, tm), BF16),
            pltpu.VMEM((D_MODEL, tm), F32),
        ],
        compiler_params=pltpu.CompilerParams(
            dimension_semantics=("arbitrary", "arbitrary"), vmem_limit_bytes=VMEM_LIMIT),
        name="peer",
    )(x2d, ml2d, rw2d, woml, worw, gffn, gfin, wqt, keys, u, u, vt, vt)


def _prep_weights(norm_mix_g, w_in, ml_conv_w, ml_conv_b, ml_b_i, ml_b_f, ml_norm_g,
                  rw_mu, rw_w0, rw_w2, rw_a0, rw_a2, rw_g2, rw_k_k, rw_k_a, rw_r_k, rw_gn_g, rw_gn_b,
                  w_out, norm_ffn_g, peer_w_q, peer_sub_keys, peer_u, peer_v, norm_final_g):
    assert w_in.shape[0] == 1, "one layer"
    w = w_in[0]
    wg = jnp.pad(w[:, ML_QKVO:ML_QKVO + 2 * ML_HEADS], ((0, 0), (0, LANE - 2 * ML_HEADS)))
    wg_hi = wg.astype(BF16)
    wg_lo = (wg - wg_hi.astype(F32)).astype(BF16)
    zeros = jnp.zeros((DECAY_LORA, RW_WIDTH), F32)
    wwa = jnp.concatenate([jnp.concatenate([rw_w2[0], zeros], axis=1),
                           jnp.concatenate([zeros, rw_a2[0]], axis=1)], axis=0)
    gate_bias = jnp.pad(jnp.concatenate([ml_b_i[0], ml_b_f[0]]), (0, LANE - 2 * ML_HEADS))[None, :]
    row = lambda a: a.reshape(1, -1)
    return dict(
        g_mix=row(norm_mix_g[0]),
        wml=w[:, 0:ML_QKVO].astype(BF16),
        wg=jnp.stack([wg_hi, wg_lo]),
        wrw=w[:, ML_QKVO + 2 * ML_HEADS:].astype(BF16),
        conv_w=ml_conv_w[0], conv_b=row(ml_conv_b[0]), gate_bias=gate_bias, ml_norm_g=row(ml_norm_g[0]),
        mu=row(rw_mu[0]), w0=row(rw_w0[0]), wwa=wwa.astype(BF16), a0=row(rw_a0[0]),
        g2=rw_g2[0].astype(BF16), k_k=row(rw_k_k[0]), k_a=row(rw_k_a[0]), r_k=row(rw_r_k[0]),
        gn_g=row(rw_gn_g[0]), gn_b=row(rw_gn_b[0]),
        wo_ml=w_out[0, 0:ML_WIDTH].astype(BF16), wo_rw=w_out[0, ML_WIDTH:].astype(BF16),
        g_ffn=row(norm_ffn_g[0]), g_fin=row(norm_final_g),
        wqt=peer_w_q[0].T.astype(BF16),
        keys=peer_sub_keys[0].reshape(2 * PEER_HEADS, N_KEYS, PEER_HALF).astype(BF16),
        u=peer_u[0].astype(BF16),
        vt=peer_v[0].reshape(-1, PEER_EXPERT_CHUNK // 2, D_MODEL).transpose(0, 2, 1).astype(BF16),
    )


def _trunk(x, states, wp, tr_ml, tr_rw, tm_in, tm_peer, ec):
    b, t, d = x.shape
    c0, n0, m0, conv0, s0, shift0 = (s[0] for s in states)
    x2d = x.reshape(b * t, d)
    ml, gates, rw = _proj_in(x2d, wp["g_mix"], wp["wml"], wp["wg"], wp["wrw"], tm_in)
    ml3 = ml.reshape(b, t, ML_QKVO)
    rw3 = rw.reshape(b, t, RW_COLS)
    ml_out, c1, n1, m1, conv1 = _mlstm(
        ml3, gates.reshape(b, t, LANE), c0, n0, m0.reshape(b, 1, ML_HEADS), conv0,
        wp["conv_w"], wp["conv_b"], wp["gate_bias"], wp["ml_norm_g"], tr_ml)
    prev = jnp.concatenate([shift0[:, None, :], rw3[:, :-1]], axis=1)
    rw_out, s1 = _rwkv(
        rw3, prev, s0.reshape(b, RW_WIDTH, RW_HD), wp["mu"], wp["w0"], wp["wwa"], wp["a0"], wp["g2"],
        wp["k_k"], wp["k_a"], wp["r_k"], wp["gn_g"], wp["gn_b"], tr_rw, min(b, RW_BATCH))
    y = _peer(x2d, ml_out.reshape(b * t, ML_WIDTH), rw_out.reshape(b * t, RW_WIDTH),
              wp["wo_ml"], wp["wo_rw"], wp["g_ffn"], wp["g_fin"], wp["wqt"], wp["keys"],
              wp["u"], wp["vt"], tm_peer, ec)
    new_states = (c1[None], n1[None], m1.reshape(b, ML_HEADS)[None], conv1[None],
                  s1.reshape(b, RW_HEADS, RW_HD, RW_HD)[None], rw3[:, -1][None])
    return y.reshape(b, t, d), new_states


def kernel(x_prompt, x_sample, state_mlstm_C, state_mlstm_n, state_mlstm_m, state_mlstm_conv, state_rwkv_S, state_rwkv_shift, norm_mix_g, w_in, ml_conv_w, ml_conv_b, ml_b_i, ml_b_f, ml_norm_g, rw_mu, rw_w0, rw_w2, rw_a0, rw_a2, rw_g2, rw_k_k, rw_k_a, rw_r_k, rw_gn_g, rw_gn_b, w_out, norm_ffn_g, peer_w_q, peer_sub_keys, peer_u, peer_v, norm_final_g):
    wp = _prep_weights(norm_mix_g, w_in, ml_conv_w, ml_conv_b, ml_b_i, ml_b_f, ml_norm_g,
                       rw_mu, rw_w0, rw_w2, rw_a0, rw_a2, rw_g2, rw_k_k, rw_k_a, rw_r_k, rw_gn_g, rw_gn_b,
                       w_out, norm_ffn_g, peer_w_q, peer_sub_keys, peer_u, peer_v, norm_final_g)
    bp = x_prompt.shape[0]
    z = lambda *s: jnp.zeros((1, bp) + s, F32)
    prompt_states = (z(ML_HEADS, ML_HD, ML_HD), z(ML_HEADS, ML_HD), jnp.full((1, bp, ML_HEADS), M_INIT, F32),
                     z(CONV_W - 1, 2 * ML_WIDTH), z(RW_HEADS, RW_HD, RW_HD), z(RW_COLS))
    sample_states = (state_mlstm_C, state_mlstm_n, state_mlstm_m, state_mlstm_conv,
                     state_rwkv_S, state_rwkv_shift)
    tp, ts = x_prompt.shape[1], x_sample.shape[1]
    np_, ns = bp * tp, x_sample.shape[0] * ts
    y_p, st_p = _trunk(x_prompt, prompt_states, wp, min(tp, ML_CHUNK), min(tp, RW_CHUNK),
                       min(np_, PROJ_TILE), min(np_, PEER_TILE), PEER_EXPERT_CHUNK)
    y_s, st_s = _trunk(x_sample, sample_states, wp, min(ts, ML_CHUNK), min(ts, RW_CHUNK),
                       min(ns, PROJ_TILE), min(ns, PEER_TILE), PEER_EXPERT_CHUNK)
    return (y_p, y_s) + tuple(st_p) + tuple(st_s)
```

```python
import functools

import jax
import jax.numpy as jnp
from jax import lax
from jax.experimental import pallas as pl
from jax.experimental.pallas import tpu as pltpu

F32 = jnp.float32
BF16 = jnp.bfloat16

D_MODEL = 1024
ML_HEADS = 4
ML_HD = 128
ML_WIDTH = ML_HEADS * ML_HD
CONV_W = 4
RW_HEADS = 8
RW_HD = 64
RW_WIDTH = RW_HEADS * RW_HD
DECAY_LORA = 64
AAA_LORA = 64
GATE_LORA = 128
RW_COLS = 3 * RW_WIDTH + DECAY_LORA + AAA_LORA + GATE_LORA
ML_QKVO = 4 * ML_WIDTH
PEER_HEADS = 8
N_KEYS = 128
PEER_TOPK = 16
PEER_HALF = 128
NORM_EPS = 1e-6
ML_NORM_EPS = 1e-6
GN_EPS = RW_HD * 1e-5
M_INIT = -1e30
NEG_BIG = -1e30

LANE = 128
ML_CHUNK = 128
RW_CHUNK = 64
RW_GROUP = 4
RW_MIN_CHUNK = 16
RW_STEP_ROWS = 256
PROJ_TILE = 256
PEER_TILE = 512
PEER_EXPERT_CHUNK = 1024
VMEM_LIMIT = 56 * 1024 * 1024


def _dot(a, b):
    return jnp.dot(a, b, preferred_element_type=F32)


def _dot_nt(a, b):
    return lax.dot_general(a, b, (((1,), (1,)), ((), ())), preferred_element_type=F32)


def _split_bf16(x, n):
    parts = []
    r = x
    for _ in range(n):
        p = r.astype(BF16)
        parts.append(p)
        r = r - p.astype(F32)
    return parts


def _dot_exact_lhs(mask_bf16, x, n):
    return sum(_dot(mask_bf16, p) for p in _split_bf16(x, n))


def _dot_exact_rhs(x, mask_bf16, n):
    return sum(_dot(p, mask_bf16) for p in _split_bf16(x, n))


def _sigmoid(x):
    return 1.0 / (1.0 + jnp.exp(-x))


def _softplus(x):
    return jnp.maximum(x, 0.0) + jnp.log1p(jnp.exp(-jnp.abs(x)))


def _iota(shape, dim):
    return lax.broadcasted_iota(jnp.int32, shape, dim)


def _pad_rows(x, rows, value=0.0):
    if x.shape[0] == rows:
        return x
    return jnp.concatenate([x, jnp.full((rows - x.shape[0], x.shape[1]), value, x.dtype)], axis=0)


def _proj_in_kernel(x_ref, g_ref, wml_ref, wg_ref, wrw_ref, ml_ref, gate_ref, rw_ref):
    x = x_ref[...]
    xn = x * lax.rsqrt(jnp.mean(x * x, axis=-1, keepdims=True) + NORM_EPS) * g_ref[...]
    xb = xn.astype(BF16)
    ml_ref[...] = _dot(xb, wml_ref[...])
    rw_ref[...] = _dot(xb, wrw_ref[...])
    xlo = (xn - xb.astype(F32)).astype(BF16)
    gate_ref[...] = _dot(xb, wg_ref[0]) + _dot(xlo, wg_ref[0]) + _dot(xb, wg_ref[1])


def _proj_in(x2d, g, wml, wg, wrw, tm):
    n = x2d.shape[0]
    const2 = lambda i: (0, 0)
    return pl.pallas_call(
        _proj_in_kernel,
        grid=(n // tm,),
        in_specs=[
            pl.BlockSpec((tm, D_MODEL), lambda i: (i, 0)),
            pl.BlockSpec((1, D_MODEL), const2),
            pl.BlockSpec((D_MODEL, ML_QKVO), const2),
            pl.BlockSpec((2, D_MODEL, LANE), lambda i: (0, 0, 0)),
            pl.BlockSpec((D_MODEL, RW_COLS), const2),
        ],
        out_specs=[
            pl.BlockSpec((tm, ML_QKVO), lambda i: (i, 0)),
            pl.BlockSpec((tm, LANE), lambda i: (i, 0)),
            pl.BlockSpec((tm, RW_COLS), lambda i: (i, 0)),
        ],
        out_shape=[
            jax.ShapeDtypeStruct((n, ML_QKVO), F32),
            jax.ShapeDtypeStruct((n, LANE), F32),
            jax.ShapeDtypeStruct((n, RW_COLS), F32),
        ],
        compiler_params=pltpu.CompilerParams(
            dimension_semantics=("arbitrary",), vmem_limit_bytes=VMEM_LIMIT),
        name="proj_in",
    )(x2d, g, wml, wg, wrw)


def _mlstm_kernel(ml_ref, gate_ref, c0_ref, n0_ref, m0_ref, conv0_ref, cw_ref, cb_ref, gb_ref, ng_ref,
                  out_ref, c1_ref, n1_ref, m1_ref, conv1_ref,
                  caug_ref, m_ref, ext_ref, *, tr):
    L = ML_CHUNK
    c = pl.program_id(1)
    nc = pl.num_programs(1)

    @pl.when(c == 0)
    def _init():
        for h in range(ML_HEADS):
            caug_ref[h, :, 0:ML_HD] = c0_ref[0, h]
            nrow = n0_ref[0, h:h + 1, :]
            caug_ref[h, :, ML_HD:2 * ML_HD] = jnp.broadcast_to(nrow, (ML_HD, ML_HD)).T
        m_ref[...] = jnp.zeros(m_ref.shape, F32)
        m_ref[:, 0:ML_HEADS] = m0_ref[0]
        ext_ref[5:8, :] = conv0_ref[0]

    ext_ref[8:8 + tr, :] = ml_ref[0, :, 0:2 * ML_WIDTH]
    acc = cb_ref[...] + ext_ref[5:5 + tr, :] * cw_ref[0:1, :]
    for j in range(1, CONV_W):
        acc = acc + ext_ref[5 + j:5 + j + tr, :] * cw_ref[j:j + 1, :]
    qk = acc * _sigmoid(acc)
    tail = ext_ref[tr + 5:tr + 8, :]
    ext_ref[5:8, :] = tail

    @pl.when(c == nc - 1)
    def _conv_out():
        conv1_ref[0] = tail

    q_all = _pad_rows(qk[:, 0:ML_WIDTH], L)
    k_all = _pad_rows(qk[:, ML_WIDTH:2 * ML_WIDTH] * (ML_HD ** -0.5), L)
    v_all = _pad_rows(ml_ref[0, :, 2 * ML_WIDTH:3 * ML_WIDTH], L)
    o_all = ml_ref[0, :, 3 * ML_WIDTH:4 * ML_WIDTH]

    g = gate_ref[0] + gb_ref[...]
    i_all = _pad_rows(g, L, NEG_BIG)
    lf_all = _pad_rows(-_softplus(-g), L, 0.0)
    row = _iota((L, L), 0)
    col = _iota((L, L), 1)
    causal = row >= col
    tri = jnp.where(causal, 1.0, 0.0).astype(BF16)
    b_col = _dot_exact_lhs(tri, lf_all, 3)
    b_t = b_col.T
    i_t = i_all.T

    ones = jnp.ones((L, ML_HD), BF16)
    for h in range(ML_HEADS):
        bc = b_col[:, ML_HEADS + h:ML_HEADS + h + 1]
        br = b_t[ML_HEADS + h:ML_HEADS + h + 1, :]
        ir = i_t[h:h + 1, :]
        m_prev = m_ref[:, h:h + 1]
        logd = jnp.where(causal, bc - br + ir, -jnp.inf)
        linter = bc + m_prev
        m_t = jnp.maximum(linter, jnp.max(logd, axis=1, keepdims=True))
        d = jnp.exp(logd - m_t)
        s_inter = jnp.exp(linter - m_t)
        sl = slice(h * ML_HD, (h + 1) * ML_HD)
        qh = q_all[:, sl].astype(BF16)
        k_t = k_all[:, sl].T
        vaug = jnp.concatenate([v_all[:, sl].astype(BF16), ones], axis=1)
        caug = caug_ref[h]
        s = _dot(qh, k_t.astype(BF16)) * d
        num = s_inter * _dot(qh, caug.astype(BF16)) + _dot(s.astype(BF16), vaug)
        den = num[:, ML_HD:2 * ML_HD]
        hh = num[:, 0:ML_HD] / jnp.maximum(jnp.abs(den), jnp.exp(-m_t))
        mu = jnp.mean(hh, axis=-1, keepdims=True)
        dv = hh - mu
        var = jnp.mean(dv * dv, axis=-1, keepdims=True)
        y = dv * lax.rsqrt(var + ML_NORM_EPS) * ng_ref[:, sl]
        out_ref[0, :, sl] = y[0:tr] * _sigmoid(o_all[:, sl])
        m_new = m_t[L - 1:L, :]
        b_last = bc[L - 1:L, :]
        s_state = jnp.exp(b_last + m_prev - m_new)
        w_row = jnp.exp(b_last - br + ir - m_new)
        caug_ref[h] = s_state * caug + _dot((k_t * w_row).astype(BF16), vaug)
        m_ref[:, h:h + 1] = m_new

    @pl.when(c == nc - 1)
    def _state_out():
        for h in range(ML_HEADS):
            caug = caug_ref[h]
            c1_ref[0, h] = caug[:, 0:ML_HD]
            n1_ref[0, h:h + 1, :] = caug[:, ML_HD:2 * ML_HD].T[0:1, :]
        m1_ref[0] = m_ref[:, 0:ML_HEADS]


def _mlstm(ml, gates, c0, n0, m0, conv0, cw, cb, gb, ng, tr):
    b, t, _ = ml.shape
    nc = t // tr
    bmap = lambda i, j: (i, 0, 0)
    const2 = lambda i, j: (0, 0)
    return pl.pallas_call(
        functools.partial(_mlstm_kernel, tr=tr),
        grid=(b, nc),
        in_specs=[
            pl.BlockSpec((1, tr, ML_QKVO), lambda i, j: (i, j, 0)),
            pl.BlockSpec((1, tr, LANE), lambda i, j: (i, j, 0)),
            pl.BlockSpec((1, ML_HEADS, ML_HD, ML_HD), lambda i, j: (i, 0, 0, 0)),
            pl.BlockSpec((1, ML_HEADS, ML_HD), bmap),
            pl.BlockSpec((1, 1, ML_HEADS), bmap),
            pl.BlockSpec((1, CONV_W - 1, 2 * ML_WIDTH), bmap),
            pl.BlockSpec((CONV_W, 2 * ML_WIDTH), const2),
            pl.BlockSpec((1, 2 * ML_WIDTH), const2),
            pl.BlockSpec((1, LANE), const2),
            pl.BlockSpec((1, ML_WIDTH), const2),
        ],
        out_specs=[
            pl.BlockSpec((1, tr, ML_WIDTH), lambda i, j: (i, j, 0)),
            pl.BlockSpec((1, ML_HEADS, ML_HD, ML_HD), lambda i, j: (i, 0, 0, 0)),
            pl.BlockSpec((1, ML_HEADS, ML_HD), bmap),
            pl.BlockSpec((1, 1, ML_HEADS), bmap),
            pl.BlockSpec((1, CONV_W - 1, 2 * ML_WIDTH), bmap),
        ],
        out_shape=[
            jax.ShapeDtypeStruct((b, t, ML_WIDTH), F32),
            jax.ShapeDtypeStruct((b, ML_HEADS, ML_HD, ML_HD), F32),
            jax.ShapeDtypeStruct((b, ML_HEADS, ML_HD), F32),
            jax.ShapeDtypeStruct((b, 1, ML_HEADS), F32),
            jax.ShapeDtypeStruct((b, CONV_W - 1, 2 * ML_WIDTH), F32),
        ],
        scratch_shapes=[
            pltpu.VMEM((ML_HEADS, ML_HD, 2 * ML_HD), F32),
            pltpu.VMEM((1, LANE), F32),
            pltpu.VMEM((tr + 8, 2 * ML_WIDTH), F32),
        ],
        compiler_params=pltpu.CompilerParams(
            dimension_semantics=("arbitrary", "arbitrary"), vmem_limit_bytes=VMEM_LIMIT),
        name="mlstm",
    )(ml, gates, c0, n0, m0, conv0, cw, cb, gb, ng)


def _rwkv_kernel(c_ref, sh_ref, s0_ref, mu_ref, w0_ref, wwa_ref, a0_ref, g2_ref, kk_ref, ka_ref, rk_ref,
                 gng_ref, gnb_ref, out_ref, s1_ref, s_ref, last_ref, *, tr, nb, L):
    W = RW_WIDTH
    GW = RW_GROUP * RW_HD
    GL = RW_GROUP * L
    NGRP = RW_HEADS // RW_GROUP
    ci = pl.program_id(1)
    nc = pl.num_programs(1)

    r2 = _iota((GL, GW), 0)
    c2 = _iota((GL, GW), 1)
    bd = (r2 // L) == (c2 // RW_HD)
    rg = _iota((GL, GL), 0)
    cg = _iota((GL, GL), 1)
    s_lower = rg > cg
    i_lower = rg >= cg

    @pl.when(ci == 0)
    def _init():
        sbd = (_iota((GW, GW), 0) // RW_HD) == (_iota((GW, GW), 1) // RW_HD)
        for bi in range(nb):
            last_ref[bi, 0:1, :] = sh_ref[bi]
            for gi in range(NGRP):
                x = s0_ref[bi, gi * GW:(gi + 1) * GW, :]
                s_ref[bi * NGRP + gi] = jnp.where(sbd, jnp.concatenate([x] * RW_GROUP, axis=1), 0.0)

    c = c_ref[...].reshape(nb * tr, RW_COLS)
    prev = pltpu.roll(c, 1, axis=0)
    rowid = _iota(c.shape, 0)
    for bi in range(nb):
        prev = jnp.where(rowid == bi * tr, last_ref[bi, 0:1, :], prev)
        last_ref[bi, 0:1, :] = c[(bi + 1) * tr - 1:(bi + 1) * tr, :]
    xs = c + (prev - c) * mu_ref[...]
    r = xs[:, 0:W]
    k = xs[:, W:2 * W]
    v = xs[:, 2 * W:3 * W]
    slab = xs[:, 3 * W:3 * W + DECAY_LORA + AAA_LORA]
    gd = xs[:, 3 * W + DECAY_LORA + AAA_LORA:]
    lane = _iota(slab.shape, 1)
    t_in = jnp.where(lane < DECAY_LORA, jnp.tanh(slab), slab)
    la = _dot(t_in.astype(BF16), wwa_ref[...])
    w_log = -_softplus(-(w0_ref[...] + la[:, 0:W])) - 0.5
    lw_all = -jnp.exp(w_log)
    a_all = _sigmoid(a0_ref[...] + la[:, W:2 * W])
    g = _dot(_sigmoid(gd).astype(BF16), g2_ref[...])

    rs = _iota((W, W), 0)
    cs = _iota((W, W), 1)
    seg = jnp.where((rs // RW_HD) == (cs // RW_HD), 1.0, 0.0).astype(BF16)

    kk = k * kk_ref[...]
    kn_all = k * (1.0 + (a_all - 1.0) * ka_ref[...])
    ss = _dot_exact_rhs(kk * kk, seg, 2)
    kap_all = kk / jnp.maximum(jnp.sqrt(ss), 1e-12)
    bonus = _dot_exact_rhs(r * kn_all * rk_ref[...], seg, 2) * v

    rl = _iota((L, L), 0)
    cl_ = _iota((L, L), 1)
    tri = jnp.where(rl >= cl_, 1.0, 0.0).astype(BF16)

    def stack(x):
        return jnp.where(bd, jnp.concatenate([x] * RW_GROUP, axis=0), 0.0).astype(BF16)

    y_rows = []
    for bi in range(nb):
        rows = slice(bi * tr, (bi + 1) * tr)
        lw = _pad_rows(lw_all[rows], L)
        kap = _pad_rows(kap_all[rows], L)
        kn_p = _pad_rows(kn_all[rows], L)
        a_p = _pad_rows(a_all[rows], L)
        v_p = _pad_rows(v[rows], L)
        r_p = _pad_rows(r[rows], L)

        cum = _dot_exact_lhs(tri, lw, 3)
        cum_last = cum[L - 1:L, :]
        e_neg = jnp.exp(-cum)
        abar = -kap * jnp.exp(cum - lw)
        btil = kap * a_p * e_neg
        ktil = kn_p * e_neg
        rbar = r_p * jnp.exp(cum)
        e_rem = jnp.exp(cum_last - cum)
        bhat = kap * a_p * e_rem
        khat = kn_p * e_rem
        g_last = jnp.exp(cum_last)

        ys = []
        for gi in range(NGRP):
            ls = slice(gi * GW, (gi + 1) * GW)
            a_s, b_s, k_s, r_s, v_s = (stack(t[:, ls]) for t in (abar, btil, ktil, rbar, v_p))
            bh_s, kh_s = stack(bhat[:, ls]), stack(khat[:, ls])
            ar = jnp.concatenate([a_s, r_s], axis=0)
            bk = jnp.concatenate([b_s, k_s], axis=0)
            p = _dot_nt(ar, bk)
            a_ab = jnp.where(s_lower, p[0:GL, 0:GL], 0.0)
            a_ak = jnp.where(s_lower, p[0:GL, GL:2 * GL], 0.0)
            r_b = jnp.where(i_lower, p[GL:2 * GL, 0:GL], 0.0)
            r_k = jnp.where(i_lower, p[GL:2 * GL, GL:2 * GL], 0.0)
            s = s_ref[bi * NGRP + gi]
            q0 = _dot_nt(ar, s.astype(BF16))
            u = q0[0:GL] + _dot(a_ak.astype(BF16), v_s)
            npow = a_ab
            lvls = L.bit_length() - 1
            for lvl in range(lvls):
                u = u + _dot(npow.astype(BF16), u.astype(BF16))
                if lvl < lvls - 1:
                    npow = _dot(npow.astype(BF16), npow.astype(BF16))
            u_b = u.astype(BF16)
            y = q0[GL:2 * GL] + _dot(r_b.astype(BF16), u_b) + _dot(r_k.astype(BF16), v_s)
            yg = y[0:L]
            for j in range(1, RW_GROUP):
                yg = yg + y[j * L:(j + 1) * L]
            ys.append(yg)
            uv_t = jnp.concatenate([u, v_s.astype(F32)], axis=0).T.astype(BF16)
            bkh = jnp.concatenate([bh_s, kh_s], axis=0)
            s_ref[bi * NGRP + gi] = s * g_last[:, ls] + _dot(uv_t, bkh)
        y_rows.append(jnp.concatenate(ys, axis=1)[0:tr])

    y_all = jnp.concatenate(y_rows, axis=0) if nb > 1 else y_rows[0]
    inv = 1.0 / RW_HD
    mu_ = _dot_exact_rhs(y_all, seg, 2) * inv
    dy = y_all - mu_
    var = _dot_exact_rhs(dy * dy, seg, 2) * inv
    yn = dy * lax.rsqrt(var + GN_EPS) * gng_ref[...] + gnb_ref[...]
    out_ref[...] = ((yn + bonus) * g).reshape(nb, tr, W)

    @pl.when(ci == nc - 1)
    def _state_out():
        for bi in range(nb):
            for gi in range(NGRP):
                s = s_ref[bi * NGRP + gi]
                f = s[:, 0:RW_HD]
                for j in range(1, RW_GROUP):
                    f = f + s[:, j * RW_HD:(j + 1) * RW_HD]
                s1_ref[bi, gi * GW:(gi + 1) * GW, :] = f


def _rwkv(cols, shift0, s0, mu, w0, wwa, a0, g2, kk, ka, rk, gng, gnb, tr, nb, chunk):
    b, t, _ = cols.shape
    nc = t // tr
    assert b % nb == 0 and t % tr == 0 and tr <= chunk
    const2 = lambda i, j: (0, 0)
    vec = pl.BlockSpec((1, RW_WIDTH), const2)
    return pl.pallas_call(
        functools.partial(_rwkv_kernel, tr=tr, nb=nb, L=chunk),
        grid=(b // nb, nc),
        in_specs=[
            pl.BlockSpec((nb, tr, RW_COLS), lambda i, j: (i, j, 0)),
            pl.BlockSpec((nb, 1, RW_COLS), lambda i, j: (i, 0, 0)),
            pl.BlockSpec((nb, RW_WIDTH, RW_HD), lambda i, j: (i, 0, 0)),
            pl.BlockSpec((1, RW_COLS), const2),
            vec,
            pl.BlockSpec((DECAY_LORA + AAA_LORA, 2 * RW_WIDTH), const2),
            vec,
            pl.BlockSpec((GATE_LORA, RW_WIDTH), const2),
            vec, vec, vec, vec, vec,
        ],
        out_specs=[
            pl.BlockSpec((nb, tr, RW_WIDTH), lambda i, j: (i, j, 0)),
            pl.BlockSpec((nb, RW_WIDTH, RW_HD), lambda i, j: (i, 0, 0)),
        ],
        out_shape=[
            jax.ShapeDtypeStruct((b, t, RW_WIDTH), F32),
            jax.ShapeDtypeStruct((b, RW_WIDTH, RW_HD), F32),
        ],
        scratch_shapes=[pltpu.VMEM((nb * (RW_HEADS // RW_GROUP), RW_GROUP * RW_HD, RW_GROUP * RW_HD), F32),
                        pltpu.VMEM((nb, 8, RW_COLS), F32)],
        compiler_params=pltpu.CompilerParams(
            dimension_semantics=("arbitrary", "arbitrary"), vmem_limit_bytes=VMEM_LIMIT),
        name="rwkv",
    )(cols, shift0, s0, mu, w0, wwa, a0, g2, kk, ka, rk, gng, gnb)


def _extract_top(work, idx, n):
    rank = jnp.full(work.shape, 99.0, F32)
    vals = []
    for j in range(n):
        m = jnp.max(work, axis=0, keepdims=True)
        if idx is None:
            sel = work == m
        else:
            first = jnp.min(jnp.where(work == m, idx, 1e9), axis=0, keepdims=True)
            sel = idx == first
        rank = jnp.where(sel, float(j), rank)
        work = jnp.where(sel, -jnp.inf, work)
        vals.append(m)
    return vals, rank


def _miscount(rank, n):
    taken = jnp.sum(jnp.where(rank < float(n), 1.0, 0.0), axis=0, keepdims=True)
    return jnp.abs(taken - float(n))


def _rows_bf16(row, rows):
    packed = jnp.broadcast_to(row, (16, row.shape[1])).astype(BF16)
    return jnp.concatenate([packed] * (rows // 16), axis=0)


def _peer_kernel(x_ref, ml_ref, rw_ref, woml_ref, worw_ref, gffn_ref, gfin_ref, wqt_ref, keys_ref,
                 ua_ref, ub_ref, vta_ref, vtb_ref, y_ref,
                 ht_ref, qt_ref, rank_ref, e_ref, vals_ref, r2b_ref, e2b_ref, cnt_ref, gate_ref,
                 p_ref, acc_ref, *, tm, ec):
    e = pl.program_id(1)
    ne = pl.num_programs(1)
    ng = tm // LANE
    K = PEER_TOPK
    nslab = ec // N_KEYS
    act_ref = qt_ref

    @pl.when(e == 0)
    def _select():
        x1 = (x_ref[...] + _dot(ml_ref[...].astype(BF16), woml_ref[...])
              + _dot(rw_ref[...].astype(BF16), worw_ref[...]))
        y_ref[...] = x1
        h = x1 * lax.rsqrt(jnp.mean(x1 * x1, axis=-1, keepdims=True) + NORM_EPS) * gffn_ref[...]
        ht = h.T.astype(BF16)
        ht_ref[...] = ht
        qt_ref[...] = _dot(wqt_ref[...], ht)
        acc_ref[...] = jnp.zeros(acc_ref.shape, F32)

        def score_body(hp, carry):
            q = qt_ref[pl.ds(pl.multiple_of(hp * PEER_HALF, PEER_HALF), PEER_HALF), :]
            e_ref[hp] = _dot(keys_ref[hp], q.astype(BF16))
            return carry

        lax.fori_loop(0, 2 * PEER_HEADS, score_body, 0)

        key_idx = _iota((N_KEYS, LANE), 0).astype(F32)
        groups = [slice(gi * LANE, (gi + 1) * LANE) for gi in range(ng)]

        def topk_group(hp, lanes, exact_ties):
            s = e_ref[hp, :, lanes]
            vals, rank = _extract_top(s, key_idx if exact_ties else None, K)
            rank_ref[hp, :, lanes] = rank
            vals_ref[hp, :, lanes] = jnp.concatenate(vals, axis=0)
            return _miscount(rank, K)

        def topk_body(hp, carry):
            bad = topk_group(hp, groups[0], False)
            for lanes in groups[1:]:
                bad = jnp.maximum(bad, topk_group(hp, lanes, False))

            @pl.when(jnp.max(bad) > 0.0)
            def _redo():
                for lanes in groups:
                    topk_group(hp, lanes, True)

            return carry

        lax.fori_loop(0, 2 * PEER_HEADS, topk_body, 0)

        sub8 = _iota((8, LANE), 0)

        def cand_group(h, lanes, exact_ties):
            v1 = vals_ref[2 * h, :, lanes]
            v2 = vals_ref[2 * h + 1, :, lanes]
            slabs, idxs, slab_a = [], [], []
            for a_i in range(K):
                nb = K // (a_i + 1)
                for b0 in range(0, nb, 8):
                    rows = v1[a_i:a_i + 1, :] + v2[b0:b0 + 8, :]
                    bidx = sub8 + b0
                    slabs.append(jnp.where(bidx < nb, rows, -jnp.inf))
                    idxs.append((bidx + a_i * K).astype(F32))
                    slab_a.append(a_i)
            work = jnp.concatenate(slabs, axis=0)
            idx = jnp.concatenate(idxs, axis=0) if exact_ties else None
            top = v1[0:1, :] + v2[0:1, :]
            vals, rank = _extract_top(work, idx, K)
            z = jnp.exp(vals[0] - top)
            for j in range(1, K):
                z = z + jnp.exp(vals[j] - top)
            picked = jnp.where(rank < float(K), 1.0, 0.0)
            r1 = rank_ref[2 * h, :, lanes]
            cnt = jnp.zeros((N_KEYS, LANE), F32)
            for a_i in range(K):
                ca = None
                for si, sa in enumerate(slab_a):
                    if sa == a_i:
                        part = jnp.sum(picked[si * 8:(si + 1) * 8], axis=0, keepdims=True)
                        ca = part if ca is None else ca + part
                cnt = cnt + jnp.where(r1 == float(a_i), ca, 0.0)
            cnt_ref[h, :, lanes] = cnt
            e1 = jnp.exp(e_ref[2 * h, :, lanes] - v1[0:1, :])
            gate_ref[h, :, lanes] = jnp.where(r1 < float(K), e1 / z, 0.0)
            return _miscount(rank, K)

        def cand_body(h, carry):
            bad = cand_group(h, groups[0], False)
            for lanes in groups[1:]:
                bad = jnp.maximum(bad, cand_group(h, lanes, False))

            @pl.when(jnp.max(bad) > 0.0)
            def _redo():
                for lanes in groups:
                    cand_group(h, lanes, True)

            r2b_ref[h] = rank_ref[2 * h + 1].astype(BF16)
            e2b_ref[h] = jnp.exp(e_ref[2 * h + 1] - vals_ref[2 * h + 1, 0:1, :]).astype(BF16)
            return carry

        lax.fori_loop(0, PEER_HEADS, cand_body, 0)

    key1 = pl.ds(pl.multiple_of(e * nslab, nslab), nslab)
    hs = nslab // 2
    hrows = hs * N_KEYS
    for half in range(2):
        act_ref[half * hrows:(half + 1) * hrows, :] = _dot((ua_ref, ub_ref)[half][...], ht_ref[...])
    for half in range(2):
        for gi in range(ng):
            lanes = slice(gi * LANE, (gi + 1) * LANE)
            w = [jnp.zeros((N_KEYS, LANE), BF16) for _ in range(hs)]
            for h in range(PEER_HEADS):
                r2 = r2b_ref[h, :, lanes]
                e2 = e2b_ref[h, :, lanes]
                cw = cnt_ref[h, key1, lanes]
                gw = gate_ref[h, key1, lanes]
                for jj in range(hs):
                    j = half * hs + jj
                    hit = r2 < _rows_bf16(cw[j:j + 1, :], N_KEYS)
                    w[jj] = w[jj] + jnp.where(hit, e2, jnp.zeros_like(e2)) * _rows_bf16(gw[j:j + 1, :], N_KEYS)
            for jj in range(hs):
                rows = slice((half * hs + jj) * N_KEYS, (half * hs + jj + 1) * N_KEYS)
                act = act_ref[rows, lanes]
                gelu = 0.5 * act * (1.0 + lax.erf(act * 0.7071067811865476))
                p_ref[rows, lanes] = w[jj] * gelu.astype(BF16)
    acc_ref[...] += (_dot(vta_ref[0], p_ref[0:hrows, :]) + _dot(vtb_ref[0], p_ref[hrows:2 * hrows, :]))

    @pl.when(e == ne - 1)
    def _finish():
        x2 = y_ref[...] + acc_ref[...].T
        y_ref[...] = x2 * lax.rsqrt(jnp.mean(x2 * x2, axis=-1, keepdims=True) + NORM_EPS) * gfin_ref[...]


def _peer(x2d, ml2d, rw2d, woml, worw, gffn, gfin, wqt, keys, u, vt, tm, ec):
    n = x2d.shape[0]
    n_exp = u.shape[0]
    assert ec == 8 * N_KEYS and tm % LANE == 0 and n % tm == 0 and n_exp % ec == 0
    tok = lambda i, e: (i, 0)
    const2 = lambda i, e: (0, 0)
    qrows = 2 * PEER_HEADS * PEER_HALF
    eh = ec // 2
    once = pl.Buffered(1)
    return pl.pallas_call(
        functools.partial(_peer_kernel, tm=tm, ec=ec),
        grid=(n // tm, n_exp // ec),
        in_specs=[
            pl.BlockSpec((tm, D_MODEL), tok),
            pl.BlockSpec((tm, ML_WIDTH), tok),
            pl.BlockSpec((tm, RW_WIDTH), tok),
            pl.BlockSpec((ML_WIDTH, D_MODEL), const2, pipeline_mode=once),
            pl.BlockSpec((RW_WIDTH, D_MODEL), const2, pipeline_mode=once),
            pl.BlockSpec((1, D_MODEL), const2),
            pl.BlockSpec((1, D_MODEL), const2),
            pl.BlockSpec((qrows, D_MODEL), const2, pipeline_mode=once),
            pl.BlockSpec((2 * PEER_HEADS, N_KEYS, PEER_HALF), lambda i, e: (0, 0, 0), pipeline_mode=once),
            pl.BlockSpec((eh, D_MODEL), lambda i, e: (2 * e, 0)),
            pl.BlockSpec((eh, D_MODEL), lambda i, e: (2 * e + 1, 0)),
            pl.BlockSpec((1, D_MODEL, eh), lambda i, e: (2 * e, 0, 0)),
            pl.BlockSpec((1, D_MODEL, eh), lambda i, e: (2 * e + 1, 0, 0)),
        ],
        out_specs=pl.BlockSpec((tm, D_MODEL), tok),
        out_shape=jax.ShapeDtypeStruct((n, D_MODEL), F32),
        scratch_shapes=[
            pltpu.VMEM((D_MODEL, tm), BF16),
            pltpu.VMEM((qrows, tm), F32),
            pltpu.VMEM((2 * PEER_HEADS, N_KEYS, tm), F32),
            pltpu.VMEM((2 * PEER_HEADS, N_KEYS, tm), F32),
            pltpu.VMEM((2 * PEER_HEADS, PEER_TOPK, tm), F32),
            pltpu.VMEM((PEER_HEADS, N_KEYS, tm), BF16),
            pltpu.VMEM((PEER_HEADS, N_KEYS, tm), BF16),
            pltpu.VMEM((PEER_HEADS, N_KEYS, tm), F32),
            pltpu.VMEM((PEER_HEADS, N_KEYS, tm), F32),
            pltpu.VMEM((ec, tm), BF16),
            pltpu.VMEM((D_MODEL, tm), F32),
        ],
        compiler_params=pltpu.CompilerParams(
            dimension_semantics=("arbitrary", "arbitrary"), vmem_limit_bytes=VMEM_LIMIT),
        name="peer",
    )(x2d, ml2d, rw2d, woml, worw, gffn, gfin, wqt, keys, u, u, vt, vt)


def _prep_weights(norm_mix_g, w_in, ml_conv_w, ml_conv_b, ml_b_i, ml_b_f, ml_norm_g,
                  rw_mu, rw_w0, rw_w2, rw_a0, rw_a2, rw_g2, rw_k_k, rw_k_a, rw_r_k, rw_gn_g, rw_gn_b,
                  w_out, norm_ffn_g, peer_w_q, peer_sub_keys, peer_u, peer_v, norm_final_g):
    assert w_in.shape[0] == 1, "one layer"
    w = w_in[0]
    wg = jnp.pad(w[:, ML_QKVO:ML_QKVO + 2 * ML_HEADS], ((0, 0), (0, LANE - 2 * ML_HEADS)))
    wg_hi = wg.astype(BF16)
    wg_lo = (wg - wg_hi.astype(F32)).astype(BF16)
    zeros = jnp.zeros((DECAY_LORA, RW_WIDTH), F32)
    wwa = jnp.concatenate([jnp.concatenate([rw_w2[0], zeros], axis=1),
                           jnp.concatenate([zeros, rw_a2[0]], axis=1)], axis=0)
    gate_bias = jnp.pad(jnp.concatenate([ml_b_i[0], ml_b_f[0]]), (0, LANE - 2 * ML_HEADS))[None, :]
    row = lambda a: a.reshape(1, -1)
    return dict(
        g_mix=row(norm_mix_g[0]),
        wml=w[:, 0:ML_QKVO].astype(BF16),
        wg=jnp.stack([wg_hi, wg_lo]),
        wrw=w[:, ML_QKVO + 2 * ML_HEADS:].astype(BF16),
        conv_w=ml_conv_w[0], conv_b=row(ml_conv_b[0]), gate_bias=gate_bias, ml_norm_g=row(ml_norm_g[0]),
        mu=row(rw_mu[0]), w0=row(rw_w0[0]), wwa=wwa.astype(BF16), a0=row(rw_a0[0]),
        g2=rw_g2[0].astype(BF16), k_k=row(rw_k_k[0]), k_a=row(rw_k_a[0]), r_k=row(rw_r_k[0]),
        gn_g=row(rw_gn_g[0]), gn_b=row(rw_gn_b[0]),
        wo_ml=w_out[0, 0:ML_WIDTH].astype(BF16), wo_rw=w_out[0, ML_WIDTH:].astype(BF16),
        g_ffn=row(norm_ffn_g[0]), g_fin=row(norm_final_g),
        wqt=peer_w_q[0].T.astype(BF16),
        keys=peer_sub_keys[0].reshape(2 * PEER_HEADS, N_KEYS, PEER_HALF).astype(BF16),
        u=peer_u[0].astype(BF16),
        vt=peer_v[0].reshape(-1, PEER_EXPERT_CHUNK // 2, D_MODEL).transpose(0, 2, 1).astype(BF16),
    )


def _rwkv_tiling(t):
    tr = min(t, RW_CHUNK)
    chunk = max(tr, RW_MIN_CHUNK)
    return tr, chunk, RW_STEP_ROWS // chunk // 2 if chunk < RW_CHUNK else RW_STEP_ROWS // chunk


def _trunk(x, states, wp, tr_ml, tr_rw, rw_chunk, rw_batch, tm_in, tm_peer, ec):
    b, t, d = x.shape
    c0, n0, m0, conv0, s0, shift0 = (s[0] for s in states)
    x2d = x.reshape(b * t, d)
    ml, gates, rw = _proj_in(x2d, wp["g_mix"], wp["wml"], wp["wg"], wp["wrw"], tm_in)
    ml3 = ml.reshape(b, t, ML_QKVO)
    rw3 = rw.reshape(b, t, RW_COLS)
    ml_out, c1, n1, m1, conv1 = _mlstm(
        ml3, gates.reshape(b, t, LANE), c0, n0, m0.reshape(b, 1, ML_HEADS), conv0,
        wp["conv_w"], wp["conv_b"], wp["gate_bias"], wp["ml_norm_g"], tr_ml)
    rw_out, s1 = _rwkv(
        rw3, shift0[:, None, :], s0.reshape(b, RW_WIDTH, RW_HD), wp["mu"], wp["w0"], wp["wwa"], wp["a0"], wp["g2"],
        wp["k_k"], wp["k_a"], wp["r_k"], wp["gn_g"], wp["gn_b"], tr_rw, min(b, rw_batch), rw_chunk)
    y = _peer(x2d, ml_out.reshape(b * t, ML_WIDTH), rw_out.reshape(b * t, RW_WIDTH),
              wp["wo_ml"], wp["wo_rw"], wp["g_ffn"], wp["g_fin"], wp["wqt"], wp["keys"],
              wp["u"], wp["vt"], tm_peer, ec)
    new_states = (c1[None], n1[None], m1.reshape(b, ML_HEADS)[None], conv1[None],
                  s1.reshape(b, RW_HEADS, RW_HD, RW_HD)[None], rw3[:, -1][None])
    return y.reshape(b, t, d), new_states


def kernel(x_prompt, x_sample, state_mlstm_C, state_mlstm_n, state_mlstm_m, state_mlstm_conv, state_rwkv_S, state_rwkv_shift, norm_mix_g, w_in, ml_conv_w, ml_conv_b, ml_b_i, ml_b_f, ml_norm_g, rw_mu, rw_w0, rw_w2, rw_a0, rw_a2, rw_g2, rw_k_k, rw_k_a, rw_r_k, rw_gn_g, rw_gn_b, w_out, norm_ffn_g, peer_w_q, peer_sub_keys, peer_u, peer_v, norm_final_g):
    wp = _prep_weights(norm_mix_g, w_in, ml_conv_w, ml_conv_b, ml_b_i, ml_b_f, ml_norm_g,
                       rw_mu, rw_w0, rw_w2, rw_a0, rw_a2, rw_g2, rw_k_k, rw_k_a, rw_r_k, rw_gn_g, rw_gn_b,
                       w_out, norm_ffn_g, peer_w_q, peer_sub_keys, peer_u, peer_v, norm_final_g)
    bp = x_prompt.shape[0]
    z = lambda *s: jnp.zeros((1, bp) + s, F32)
    prompt_states = (z(ML_HEADS, ML_HD, ML_HD), z(ML_HEADS, ML_HD), jnp.full((1, bp, ML_HEADS), M_INIT, F32),
                     z(CONV_W - 1, 2 * ML_WIDTH), z(RW_HEADS, RW_HD, RW_HD), z(RW_COLS))
    sample_states = (state_mlstm_C, state_mlstm_n, state_mlstm_m, state_mlstm_conv,
                     state_rwkv_S, state_rwkv_shift)
    tp, ts = x_prompt.shape[1], x_sample.shape[1]
    np_, ns = bp * tp, x_sample.shape[0] * ts
    y_p, st_p = _trunk(x_prompt, prompt_states, wp, min(tp, ML_CHUNK), *_rwkv_tiling(tp),
                       min(np_, PROJ_TILE), min(np_, PEER_TILE), PEER_EXPERT_CHUNK)
    y_s, st_s = _trunk(x_sample, sample_states, wp, min(ts, ML_CHUNK), *_rwkv_tiling(ts),
                       min(ns, PROJ_TILE), min(ns, PEER_TILE), PEER_EXPERT_CHUNK)
    return (y_p, y_s) + tuple(st_p) + tuple(st_s)
```

```python
import functools

import jax
import jax.numpy as jnp
from jax import lax
from jax.experimental import pallas as pl
from jax.experimental.pallas import tpu as pltpu

F32 = jnp.float32
BF16 = jnp.bfloat16

D_MODEL = 1024
ML_HEADS = 4
ML_HD = 128
ML_WIDTH = ML_HEADS * ML_HD
CONV_W = 4
RW_HEADS = 8
RW_HD = 64
RW_WIDTH = RW_HEADS * RW_HD
DECAY_LORA = 64
AAA_LORA = 64
GATE_LORA = 128
RW_COLS = 3 * RW_WIDTH + DECAY_LORA + AAA_LORA + GATE_LORA
ML_QKVO = 4 * ML_WIDTH
PEER_HEADS = 8
N_KEYS = 128
PEER_TOPK = 16
PEER_HALF = 128
NORM_EPS = 1e-6
ML_NORM_EPS = 1e-6
GN_EPS = RW_HD * 1e-5
M_INIT = -1e30
NEG_BIG = -1e30

LANE = 128
ML_CHUNK = 128
RW_CHUNK = 64
RW_GROUP = 4
RW_MIN_CHUNK = 16
RW_STEP_ROWS = 256
PROJ_TILE = 256
PEER_TILE = 512
PEER_EXPERT_CHUNK = 1024
VMEM_LIMIT = 56 * 1024 * 1024


def _dot(a, b):
    return jnp.dot(a, b, preferred_element_type=F32)


def _dot_nt(a, b):
    return lax.dot_general(a, b, (((1,), (1,)), ((), ())), preferred_element_type=F32)


def _split_bf16(x, n):
    parts = []
    r = x
    for _ in range(n):
        p = r.astype(BF16)
        parts.append(p)
        r = r - p.astype(F32)
    return parts


def _dot_exact_lhs(mask_bf16, x, n):
    return sum(_dot(mask_bf16, p) for p in _split_bf16(x, n))


def _dot_exact_rhs(x, mask_bf16, n):
    return sum(_dot(p, mask_bf16) for p in _split_bf16(x, n))


def _sigmoid(x):
    return 1.0 / (1.0 + jnp.exp(-x))


def _softplus(x):
    return jnp.maximum(x, 0.0) + jnp.log1p(jnp.exp(-jnp.abs(x)))


def _iota(shape, dim):
    return lax.broadcasted_iota(jnp.int32, shape, dim)


def _pad_rows(x, rows, value=0.0):
    if x.shape[0] == rows:
        return x
    return jnp.concatenate([x, jnp.full((rows - x.shape[0], x.shape[1]), value, x.dtype)], axis=0)


def _proj_in_kernel(x_ref, g_ref, wml_ref, wg_ref, wrw_ref, ml_ref, gate_ref, rw_ref):
    x = x_ref[...]
    xn = x * lax.rsqrt(jnp.mean(x * x, axis=-1, keepdims=True) + NORM_EPS) * g_ref[...]
    xb = xn.astype(BF16)
    ml_ref[...] = _dot(xb, wml_ref[...])
    rw_ref[...] = _dot(xb, wrw_ref[...])
    xlo = (xn - xb.astype(F32)).astype(BF16)
    gate_ref[...] = _dot(xb, wg_ref[0]) + _dot(xlo, wg_ref[0]) + _dot(xb, wg_ref[1])


def _proj_in(x2d, g, wml, wg, wrw, tm):
    n = x2d.shape[0]
    const2 = lambda i: (0, 0)
    return pl.pallas_call(
        _proj_in_kernel,
        grid=(n // tm,),
        in_specs=[
            pl.BlockSpec((tm, D_MODEL), lambda i: (i, 0)),
            pl.BlockSpec((1, D_MODEL), const2),
            pl.BlockSpec((D_MODEL, ML_QKVO), const2),
            pl.BlockSpec((2, D_MODEL, LANE), lambda i: (0, 0, 0)),
            pl.BlockSpec((D_MODEL, RW_COLS), const2),
        ],
        out_specs=[
            pl.BlockSpec((tm, ML_QKVO), lambda i: (i, 0)),
            pl.BlockSpec((tm, LANE), lambda i: (i, 0)),
            pl.BlockSpec((tm, RW_COLS), lambda i: (i, 0)),
        ],
        out_shape=[
            jax.ShapeDtypeStruct((n, ML_QKVO), F32),
            jax.ShapeDtypeStruct((n, LANE), F32),
            jax.ShapeDtypeStruct((n, RW_COLS), F32),
        ],
        compiler_params=pltpu.CompilerParams(
            dimension_semantics=("arbitrary",), vmem_limit_bytes=VMEM_LIMIT),
        name="proj_in",
    )(x2d, g, wml, wg, wrw)


def _mlstm_kernel(ml_ref, gate_ref, c0_ref, n0_ref, m0_ref, conv0_ref, cw_ref, cb_ref, gb_ref, ng_ref,
                  out_ref, c1_ref, n1_ref, m1_ref, conv1_ref,
                  caug_ref, m_ref, ext_ref, *, tr):
    L = ML_CHUNK
    c = pl.program_id(1)
    nc = pl.num_programs(1)

    @pl.when(c == 0)
    def _init():
        for h in range(ML_HEADS):
            caug_ref[h, :, 0:ML_HD] = c0_ref[0, h]
            nrow = n0_ref[0, h:h + 1, :]
            caug_ref[h, :, ML_HD:2 * ML_HD] = jnp.broadcast_to(nrow, (ML_HD, ML_HD)).T
        m_ref[...] = jnp.zeros(m_ref.shape, F32)
        m_ref[:, 0:ML_HEADS] = m0_ref[0]
        ext_ref[5:8, :] = conv0_ref[0]

    ext_ref[8:8 + tr, :] = ml_ref[0, :, 0:2 * ML_WIDTH]
    acc = cb_ref[...] + ext_ref[5:5 + tr, :] * cw_ref[0:1, :]
    for j in range(1, CONV_W):
        acc = acc + ext_ref[5 + j:5 + j + tr, :] * cw_ref[j:j + 1, :]
    qk = acc * _sigmoid(acc)
    tail = ext_ref[tr + 5:tr + 8, :]
    ext_ref[5:8, :] = tail

    @pl.when(c == nc - 1)
    def _conv_out():
        conv1_ref[0] = tail

    q_all = _pad_rows(qk[:, 0:ML_WIDTH], L)
    k_all = _pad_rows(qk[:, ML_WIDTH:2 * ML_WIDTH] * (ML_HD ** -0.5), L)
    v_all = _pad_rows(ml_ref[0, :, 2 * ML_WIDTH:3 * ML_WIDTH], L)
    o_all = ml_ref[0, :, 3 * ML_WIDTH:4 * ML_WIDTH]

    g = gate_ref[0] + gb_ref[...]
    i_all = _pad_rows(g, L, NEG_BIG)
    lf_all = _pad_rows(-_softplus(-g), L, 0.0)
    row = _iota((L, L), 0)
    col = _iota((L, L), 1)
    causal = row >= col
    tri = jnp.where(causal, 1.0, 0.0).astype(BF16)
    b_col = _dot_exact_lhs(tri, lf_all, 3)
    b_t = b_col.T
    i_t = i_all.T

    ones = jnp.ones((L, ML_HD), BF16)
    for h in range(ML_HEADS):
        bc = b_col[:, ML_HEADS + h:ML_HEADS + h + 1]
        br = b_t[ML_HEADS + h:ML_HEADS + h + 1, :]
        ir = i_t[h:h + 1, :]
        m_prev = m_ref[:, h:h + 1]
        logd = jnp.where(causal, bc - br + ir, -jnp.inf)
        linter = bc + m_prev
        m_t = jnp.maximum(linter, jnp.max(logd, axis=1, keepdims=True))
        d = jnp.exp(logd - m_t)
        s_inter = jnp.exp(linter - m_t)
        sl = slice(h * ML_HD, (h + 1) * ML_HD)
        qh = q_all[:, sl].astype(BF16)
        k_t = k_all[:, sl].T
        vaug = jnp.concatenate([v_all[:, sl].astype(BF16), ones], axis=1)
        caug = caug_ref[h]
        s = _dot(qh, k_t.astype(BF16)) * d
        num = s_inter * _dot(qh, caug.astype(BF16)) + _dot(s.astype(BF16), vaug)
        den = num[:, ML_HD:2 * ML_HD]
        hh = num[:, 0:ML_HD] / jnp.maximum(jnp.abs(den), jnp.exp(-m_t))
        mu = jnp.mean(hh, axis=-1, keepdims=True)
        dv = hh - mu
        var = jnp.mean(dv * dv, axis=-1, keepdims=True)
        y = dv * lax.rsqrt(var + ML_NORM_EPS) * ng_ref[:, sl]
        out_ref[0, :, sl] = y[0:tr] * _sigmoid(o_all[:, sl])
        m_new = m_t[L - 1:L, :]
        b_last = bc[L - 1:L, :]
        s_state = jnp.exp(b_last + m_prev - m_new)
        w_row = jnp.exp(b_last - br + ir - m_new)
        caug_ref[h] = s_state * caug + _dot((k_t * w_row).astype(BF16), vaug)
        m_ref[:, h:h + 1] = m_new

    @pl.when(c == nc - 1)
    def _state_out():
        for h in range(ML_HEADS):
            caug = caug_ref[h]
            c1_ref[0, h] = caug[:, 0:ML_HD]
            n1_ref[0, h:h + 1, :] = caug[:, ML_HD:2 * ML_HD].T[0:1, :]
        m1_ref[0] = m_ref[:, 0:ML_HEADS]


def _mlstm(ml, gates, c0, n0, m0, conv0, cw, cb, gb, ng, tr):
    b, t, _ = ml.shape
    nc = t // tr
    bmap = lambda i, j: (i, 0, 0)
    const2 = lambda i, j: (0, 0)
    return pl.pallas_call(
        functools.partial(_mlstm_kernel, tr=tr),
        grid=(b, nc),
        in_specs=[
            pl.BlockSpec((1, tr, ML_QKVO), lambda i, j: (i, j, 0)),
            pl.BlockSpec((1, tr, LANE), lambda i, j: (i, j, 0)),
            pl.BlockSpec((1, ML_HEADS, ML_HD, ML_HD), lambda i, j: (i, 0, 0, 0)),
            pl.BlockSpec((1, ML_HEADS, ML_HD), bmap),
            pl.BlockSpec((1, 1, ML_HEADS), bmap),
            pl.BlockSpec((1, CONV_W - 1, 2 * ML_WIDTH), bmap),
            pl.BlockSpec((CONV_W, 2 * ML_WIDTH), const2),
            pl.BlockSpec((1, 2 * ML_WIDTH), const2),
            pl.BlockSpec((1, LANE), const2),
            pl.BlockSpec((1, ML_WIDTH), const2),
        ],
        out_specs=[
            pl.BlockSpec((1, tr, ML_WIDTH), lambda i, j: (i, j, 0)),
            pl.BlockSpec((1, ML_HEADS, ML_HD, ML_HD), lambda i, j: (i, 0, 0, 0)),
            pl.BlockSpec((1, ML_HEADS, ML_HD), bmap),
            pl.BlockSpec((1, 1, ML_HEADS), bmap),
            pl.BlockSpec((1, CONV_W - 1, 2 * ML_WIDTH), bmap),
        ],
        out_shape=[
            jax.ShapeDtypeStruct((b, t, ML_WIDTH), F32),
            jax.ShapeDtypeStruct((b, ML_HEADS, ML_HD, ML_HD), F32),
            jax.ShapeDtypeStruct((b, ML_HEADS, ML_HD), F32),
            jax.ShapeDtypeStruct((b, 1, ML_HEADS), F32),
            jax.ShapeDtypeStruct((b, CONV_W - 1, 2 * ML_WIDTH), F32),
        ],
        scratch_shapes=[
            pltpu.VMEM((ML_HEADS, ML_HD, 2 * ML_HD), F32),
            pltpu.VMEM((1, LANE), F32),
            pltpu.VMEM((tr + 8, 2 * ML_WIDTH), F32),
        ],
        compiler_params=pltpu.CompilerParams(
            dimension_semantics=("arbitrary", "arbitrary"), vmem_limit_bytes=VMEM_LIMIT),
        name="mlstm",
    )(ml, gates, c0, n0, m0, conv0, cw, cb, gb, ng)


def _rwkv_kernel(c_ref, sh_ref, s0_ref, mu_ref, w0_ref, wwa_ref, a0_ref, g2_ref, kk_ref, ka_ref, rk_ref,
                 gng_ref, gnb_ref, out_ref, s1_ref, s_ref, last_ref, *, tr, nb, L):
    W = RW_WIDTH
    GW = RW_GROUP * RW_HD
    GL = RW_GROUP * L
    NGRP = RW_HEADS // RW_GROUP
    ci = pl.program_id(1)
    nc = pl.num_programs(1)

    r2 = _iota((GL, GW), 0)
    c2 = _iota((GL, GW), 1)
    bd = (r2 // L) == (c2 // RW_HD)
    rg = _iota((GL, GL), 0)
    cg = _iota((GL, GL), 1)
    s_lower = rg > cg
    i_lower = rg >= cg

    @pl.when(ci == 0)
    def _init():
        sbd = (_iota((GW, GW), 0) // RW_HD) == (_iota((GW, GW), 1) // RW_HD)
        for bi in range(nb):
            last_ref[bi, 0:1, :] = sh_ref[bi]
            for gi in range(NGRP):
                x = s0_ref[bi, gi * GW:(gi + 1) * GW, :]
                s_ref[bi * NGRP + gi] = jnp.where(sbd, jnp.concatenate([x] * RW_GROUP, axis=1), 0.0)

    c = c_ref[...].reshape(nb * tr, RW_COLS)
    prev = pltpu.roll(c, 1, axis=0)
    rowid = _iota(c.shape, 0)
    for bi in range(nb):
        prev = jnp.where(rowid == bi * tr, last_ref[bi, 0:1, :], prev)
        last_ref[bi, 0:1, :] = c[(bi + 1) * tr - 1:(bi + 1) * tr, :]
    xs = c + (prev - c) * mu_ref[...]
    r = xs[:, 0:W]
    k = xs[:, W:2 * W]
    v = xs[:, 2 * W:3 * W]
    slab = xs[:, 3 * W:3 * W + DECAY_LORA + AAA_LORA]
    gd = xs[:, 3 * W + DECAY_LORA + AAA_LORA:]
    lane = _iota(slab.shape, 1)
    t_in = jnp.where(lane < DECAY_LORA, jnp.tanh(slab), slab)
    la = _dot(t_in.astype(BF16), wwa_ref[...])
    w_log = -_softplus(-(w0_ref[...] + la[:, 0:W])) - 0.5
    lw_all = -jnp.exp(w_log)
    a_all = _sigmoid(a0_ref[...] + la[:, W:2 * W])
    g = _dot(_sigmoid(gd).astype(BF16), g2_ref[...])

    rs = _iota((W, W), 0)
    cs = _iota((W, W), 1)
    seg = jnp.where((rs // RW_HD) == (cs // RW_HD), 1.0, 0.0).astype(BF16)

    kk = k * kk_ref[...]
    kn_all = k * (1.0 + (a_all - 1.0) * ka_ref[...])
    ss = _dot_exact_rhs(kk * kk, seg, 2)
    kap_all = kk / jnp.maximum(jnp.sqrt(ss), 1e-12)
    bonus = _dot_exact_rhs(r * kn_all * rk_ref[...], seg, 2) * v

    rl = _iota((L, L), 0)
    cl_ = _iota((L, L), 1)
    tri = jnp.where(rl >= cl_, 1.0, 0.0).astype(BF16)

    def stack(x):
        return jnp.where(bd, jnp.concatenate([x] * RW_GROUP, axis=0), 0.0).astype(BF16)

    seqs = []
    for bi in range(nb):
        rows = slice(bi * tr, (bi + 1) * tr)
        lw = _pad_rows(lw_all[rows], L)
        kap = _pad_rows(kap_all[rows], L)
        kn_p = _pad_rows(kn_all[rows], L)
        a_p = _pad_rows(a_all[rows], L)
        v_p = _pad_rows(v[rows], L)
        r_p = _pad_rows(r[rows], L)
        cum = _dot_exact_lhs(tri, lw, 3)
        cum_last = cum[L - 1:L, :]
        e_neg = jnp.exp(-cum)
        e_rem = jnp.exp(cum_last - cum)
        seqs.append(dict(
            abar=-kap * jnp.exp(cum - lw), btil=kap * a_p * e_neg, ktil=kn_p * e_neg, rbar=r_p * jnp.exp(cum),
            bhat=kap * a_p * e_rem, khat=kn_p * e_rem, v=v_p, g_last=jnp.exp(cum_last)))

    chains = [(bi, gi) for bi in range(nb) for gi in range(NGRP)]
    ch = []
    for bi, gi in chains:
        q = seqs[bi]
        ls = slice(gi * GW, (gi + 1) * GW)
        a_s, b_s, k_s, r_s, v_s = (stack(q[n][:, ls]) for n in ("abar", "btil", "ktil", "rbar", "v"))
        ch.append(dict(
            ar=jnp.concatenate([a_s, r_s], axis=0),
            bk=jnp.concatenate([b_s, k_s], axis=0),
            bkh=jnp.concatenate([stack(q["bhat"][:, ls]), stack(q["khat"][:, ls])], axis=0),
            v_s=v_s, g_last=q["g_last"][:, ls], s=s_ref[bi * NGRP + gi]))
    for c_ in ch:
        p = _dot_nt(c_["ar"], c_["bk"])
        c_["npow"] = jnp.where(s_lower, p[0:GL, 0:GL], 0.0)
        c_["a_ak"] = jnp.where(s_lower, p[0:GL, GL:2 * GL], 0.0).astype(BF16)
        c_["r_b"] = jnp.where(i_lower, p[GL:2 * GL, 0:GL], 0.0).astype(BF16)
        c_["r_k"] = jnp.where(i_lower, p[GL:2 * GL, GL:2 * GL], 0.0).astype(BF16)
    for c_ in ch:
        c_["q0"] = _dot_nt(c_["ar"], c_["s"].astype(BF16))
    for c_ in ch:
        c_["u"] = c_["q0"][0:GL] + _dot(c_["a_ak"], c_["v_s"])
    lvls = L.bit_length() - 1
    for lvl in range(lvls):
        for c_ in ch:
            nb16 = c_["npow"].astype(BF16)
            c_["u"] = c_["u"] + _dot(nb16, c_["u"].astype(BF16))
            if lvl < lvls - 1:
                c_["npow"] = _dot(nb16, nb16)
    for c_ in ch:
        y = (c_["q0"][GL:2 * GL] + _dot(c_["r_b"], c_["u"].astype(BF16)) + _dot(c_["r_k"], c_["v_s"]))
        yg = y[0:L]
        for j in range(1, RW_GROUP):
            yg = yg + y[j * L:(j + 1) * L]
        c_["yg"] = yg
    for (bi, gi), c_ in zip(chains, ch):
        uv_t = jnp.concatenate([c_["u"], c_["v_s"].astype(F32)], axis=0).T.astype(BF16)
        s_ref[bi * NGRP + gi] = c_["s"] * c_["g_last"] + _dot(uv_t, c_["bkh"])
    y_rows = [jnp.concatenate([ch[bi * NGRP + gi]["yg"] for gi in range(NGRP)], axis=1)[0:tr]
              for bi in range(nb)]

    y_all = jnp.concatenate(y_rows, axis=0) if nb > 1 else y_rows[0]
    inv = 1.0 / RW_HD
    mu_ = _dot_exact_rhs(y_all, seg, 2) * inv
    dy = y_all - mu_
    var = _dot_exact_rhs(dy * dy, seg, 2) * inv
    yn = dy * lax.rsqrt(var + GN_EPS) * gng_ref[...] + gnb_ref[...]
    out_ref[...] = ((yn + bonus) * g).reshape(nb, tr, W)

    @pl.when(ci == nc - 1)
    def _state_out():
        for bi in range(nb):
            for gi in range(NGRP):
                s = s_ref[bi * NGRP + gi]
                f = s[:, 0:RW_HD]
                for j in range(1, RW_GROUP):
                    f = f + s[:, j * RW_HD:(j + 1) * RW_HD]
                s1_ref[bi, gi * GW:(gi + 1) * GW, :] = f


def _rwkv(cols, shift0, s0, mu, w0, wwa, a0, g2, kk, ka, rk, gng, gnb, tr, nb, chunk):
    b, t, _ = cols.shape
    nc = t // tr
    assert b % nb == 0 and t % tr == 0 and tr <= chunk
    const2 = lambda i, j: (0, 0)
    vec = pl.BlockSpec((1, RW_WIDTH), const2)
    return pl.pallas_call(
        functools.partial(_rwkv_kernel, tr=tr, nb=nb, L=chunk),
        grid=(b // nb, nc),
        in_specs=[
            pl.BlockSpec((nb, tr, RW_COLS), lambda i, j: (i, j, 0)),
            pl.BlockSpec((nb, 1, RW_COLS), lambda i, j: (i, 0, 0)),
            pl.BlockSpec((nb, RW_WIDTH, RW_HD), lambda i, j: (i, 0, 0)),
            pl.BlockSpec((1, RW_COLS), const2),
            vec,
            pl.BlockSpec((DECAY_LORA + AAA_LORA, 2 * RW_WIDTH), const2),
            vec,
            pl.BlockSpec((GATE_LORA, RW_WIDTH), const2),
            vec, vec, vec, vec, vec,
        ],
        out_specs=[
            pl.BlockSpec((nb, tr, RW_WIDTH), lambda i, j: (i, j, 0)),
            pl.BlockSpec((nb, RW_WIDTH, RW_HD), lambda i, j: (i, 0, 0)),
        ],
        out_shape=[
            jax.ShapeDtypeStruct((b, t, RW_WIDTH), F32),
            jax.ShapeDtypeStruct((b, RW_WIDTH, RW_HD), F32),
        ],
        scratch_shapes=[pltpu.VMEM((nb * (RW_HEADS // RW_GROUP), RW_GROUP * RW_HD, RW_GROUP * RW_HD), F32),
                        pltpu.VMEM((nb, 8, RW_COLS), F32)],
        compiler_params=pltpu.CompilerParams(
            dimension_semantics=("arbitrary", "arbitrary"), vmem_limit_bytes=VMEM_LIMIT),
        name="rwkv",
    )(cols, shift0, s0, mu, w0, wwa, a0, g2, kk, ka, rk, gng, gnb)


def _extract_top(work, idx, n):
    rank = jnp.full(work.shape, 99.0, F32)
    vals = []
    for j in range(n):
        m = jnp.max(work, axis=0, keepdims=True)
        if idx is None:
            sel = work == m
        else:
            first = jnp.min(jnp.where(work == m, idx, 1e9), axis=0, keepdims=True)
            sel = idx == first
        rank = jnp.where(sel, float(j), rank)
        work = jnp.where(sel, -jnp.inf, work)
        vals.append(m)
    return vals, rank


def _miscount(rank, n):
    taken = jnp.sum(jnp.where(rank < float(n), 1.0, 0.0), axis=0, keepdims=True)
    return jnp.abs(taken - float(n))


def _rows_bf16(row, rows):
    packed = jnp.broadcast_to(row, (16, row.shape[1])).astype(BF16)
    return jnp.concatenate([packed] * (rows // 16), axis=0)


def _peer_kernel(x_ref, ml_ref, rw_ref, woml_ref, worw_ref, gffn_ref, gfin_ref, wqt_ref, keys_ref,
                 ua_ref, ub_ref, vta_ref, vtb_ref, y_ref,
                 ht_ref, qt_ref, rank_ref, e_ref, vals_ref, r2b_ref, e2b_ref, cnt_ref, gate_ref,
                 p_ref, acc_ref, *, tm, ec):
    e = pl.program_id(1)
    ne = pl.num_programs(1)
    ng = tm // LANE
    K = PEER_TOPK
    nslab = ec // N_KEYS
    act_ref = qt_ref

    @pl.when(e == 0)
    def _select():
        x1 = (x_ref[...] + _dot(ml_ref[...].astype(BF16), woml_ref[...])
              + _dot(rw_ref[...].astype(BF16), worw_ref[...]))
        y_ref[...] = x1
        h = x1 * lax.rsqrt(jnp.mean(x1 * x1, axis=-1, keepdims=True) + NORM_EPS) * gffn_ref[...]
        ht = h.T.astype(BF16)
        ht_ref[...] = ht
        qt_ref[...] = _dot(wqt_ref[...], ht)
        acc_ref[...] = jnp.zeros(acc_ref.shape, F32)

        def score_body(hp, carry):
            q = qt_ref[pl.ds(pl.multiple_of(hp * PEER_HALF, PEER_HALF), PEER_HALF), :]
            e_ref[hp] = _dot(keys_ref[hp], q.astype(BF16))
            return carry

        lax.fori_loop(0, 2 * PEER_HEADS, score_body, 0)

        key_idx = _iota((N_KEYS, LANE), 0).astype(F32)
        groups = [slice(gi * LANE, (gi + 1) * LANE) for gi in range(ng)]

        def topk_group(hp, lanes, exact_ties):
            s = e_ref[hp, :, lanes]
            vals, rank = _extract_top(s, key_idx if exact_ties else None, K)
            rank_ref[hp, :, lanes] = rank
            vals_ref[hp, :, lanes] = jnp.concatenate(vals, axis=0)
            return _miscount(rank, K)

        def topk_body(hp, carry):
            bad = topk_group(hp, groups[0], False)
            for lanes in groups[1:]:
                bad = jnp.maximum(bad, topk_group(hp, lanes, False))

            @pl.when(jnp.max(bad) > 0.0)
            def _redo():
                for lanes in groups:
                    topk_group(hp, lanes, True)

            return carry

        lax.fori_loop(0, 2 * PEER_HEADS, topk_body, 0)

        sub8 = _iota((8, LANE), 0)

        def cand_group(h, lanes, exact_ties):
            v1 = vals_ref[2 * h, :, lanes]
            v2 = vals_ref[2 * h + 1, :, lanes]
            slabs, idxs, slab_a = [], [], []
            for a_i in range(K):
                nb = K // (a_i + 1)
                for b0 in range(0, nb, 8):
                    rows = v1[a_i:a_i + 1, :] + v2[b0:b0 + 8, :]
                    bidx = sub8 + b0
                    slabs.append(jnp.where(bidx < nb, rows, -jnp.inf))
                    idxs.append((bidx + a_i * K).astype(F32))
                    slab_a.append(a_i)
            work = jnp.concatenate(slabs, axis=0)
            idx = jnp.concatenate(idxs, axis=0) if exact_ties else None
            top = v1[0:1, :] + v2[0:1, :]
            vals, rank = _extract_top(work, idx, K)
            z = jnp.exp(vals[0] - top)
            for j in range(1, K):
                z = z + jnp.exp(vals[j] - top)
            picked = jnp.where(rank < float(K), 1.0, 0.0)
            r1 = rank_ref[2 * h, :, lanes]
            cnt = jnp.zeros((N_KEYS, LANE), F32)
            for a_i in range(K):
                ca = None
                for si, sa in enumerate(slab_a):
                    if sa == a_i:
                        part = jnp.sum(picked[si * 8:(si + 1) * 8], axis=0, keepdims=True)
                        ca = part if ca is None else ca + part
                cnt = cnt + jnp.where(r1 == float(a_i), ca, 0.0)
            cnt_ref[h, :, lanes] = cnt
            e1 = jnp.exp(e_ref[2 * h, :, lanes] - v1[0:1, :])
            gate_ref[h, :, lanes] = jnp.where(r1 < float(K), e1 / z, 0.0)
            return _miscount(rank, K)

        def cand_body(h, carry):
            bad = cand_group(h, groups[0], False)
            for lanes in groups[1:]:
                bad = jnp.maximum(bad, cand_group(h, lanes, False))

            @pl.when(jnp.max(bad) > 0.0)
            def _redo():
                for lanes in groups:
                    cand_group(h, lanes, True)

            r2b_ref[h] = rank_ref[2 * h + 1].astype(BF16)
            e2b_ref[h] = jnp.exp(e_ref[2 * h + 1] - vals_ref[2 * h + 1, 0:1, :]).astype(BF16)
            return carry

        lax.fori_loop(0, PEER_HEADS, cand_body, 0)

    key1 = pl.ds(pl.multiple_of(e * nslab, nslab), nslab)
    hs = nslab // 2
    hrows = hs * N_KEYS
    for half in range(2):
        act_ref[half * hrows:(half + 1) * hrows, :] = _dot((ua_ref, ub_ref)[half][...], ht_ref[...])
    for half in range(2):
        for gi in range(ng):
            lanes = slice(gi * LANE, (gi + 1) * LANE)
            w = [jnp.zeros((N_KEYS, LANE), BF16) for _ in range(hs)]
            for h in range(PEER_HEADS):
                r2 = r2b_ref[h, :, lanes]
                e2 = e2b_ref[h, :, lanes]
                cw = cnt_ref[h, key1, lanes]
                gw = gate_ref[h, key1, lanes]
                for jj in range(hs):
                    j = half * hs + jj
                    hit = r2 < _rows_bf16(cw[j:j + 1, :], N_KEYS)
                    w[jj] = w[jj] + jnp.where(hit, e2, jnp.zeros_like(e2)) * _rows_bf16(gw[j:j + 1, :], N_KEYS)
            for jj in range(hs):
                rows = slice((half * hs + jj) * N_KEYS, (half * hs + jj + 1) * N_KEYS)
                act = act_ref[rows, lanes]
                gelu = 0.5 * act * (1.0 + lax.erf(act * 0.7071067811865476))
                p_ref[rows, lanes] = w[jj] * gelu.astype(BF16)
        acc_ref[...] += _dot((vta_ref, vtb_ref)[half][0], p_ref[half * hrows:(half + 1) * hrows, :])

    @pl.when(e == ne - 1)
    def _finish():
        x2 = y_ref[...] + acc_ref[...].T
        y_ref[...] = x2 * lax.rsqrt(jnp.mean(x2 * x2, axis=-1, keepdims=True) + NORM_EPS) * gfin_ref[...]


def _peer(x2d, ml2d, rw2d, woml, worw, gffn, gfin, wqt, keys, u, vt, tm, ec):
    n = x2d.shape[0]
    n_exp = u.shape[0]
    assert ec == 8 * N_KEYS and tm % LANE == 0 and n % tm == 0 and n_exp % ec == 0
    tok = lambda i, e: (i, 0)
    const2 = lambda i, e: (0, 0)
    qrows = 2 * PEER_HEADS * PEER_HALF
    eh = ec // 2
    once = pl.Buffered(1)
    return pl.pallas_call(
        functools.partial(_peer_kernel, tm=tm, ec=ec),
        grid=(n // tm, n_exp // ec),
        in_specs=[
            pl.BlockSpec((tm, D_MODEL), tok),
            pl.BlockSpec((tm, ML_WIDTH), tok),
            pl.BlockSpec((tm, RW_WIDTH), tok),
            pl.BlockSpec((ML_WIDTH, D_MODEL), const2, pipeline_mode=once),
            pl.BlockSpec((RW_WIDTH, D_MODEL), const2, pipeline_mode=once),
            pl.BlockSpec((1, D_MODEL), const2),
            pl.BlockSpec((1, D_MODEL), const2),
            pl.BlockSpec((qrows, D_MODEL), const2, pipeline_mode=once),
            pl.BlockSpec((2 * PEER_HEADS, N_KEYS, PEER_HALF), lambda i, e: (0, 0, 0), pipeline_mode=once),
            pl.BlockSpec((eh, D_MODEL), lambda i, e: (2 * e, 0)),
            pl.BlockSpec((eh, D_MODEL), lambda i, e: (2 * e + 1, 0)),
            pl.BlockSpec((1, D_MODEL, eh), lambda i, e: (2 * e, 0, 0)),
            pl.BlockSpec((1, D_MODEL, eh), lambda i, e: (2 * e + 1, 0, 0)),
        ],
        out_specs=pl.BlockSpec((tm, D_MODEL), tok),
        out_shape=jax.ShapeDtypeStruct((n, D_MODEL), F32),
        scratch_shapes=[
            pltpu.VMEM((D_MODEL, tm), BF16),
            pltpu.VMEM((qrows, tm), F32),
            pltpu.VMEM((2 * PEER_HEADS, N_KEYS, tm), F32),
            pltpu.VMEM((2 * PEER_HEADS, N_KEYS, tm), F32),
            pltpu.VMEM((2 * PEER_HEADS, PEER_TOPK, tm), F32),
            pltpu.VMEM((PEER_HEADS, N_KEYS, tm), BF16),
            pltpu.VMEM((PEER_HEADS, N_KEYS, tm), BF16),
            pltpu.VMEM((PEER_HEADS, N_KEYS, tm), F32),
            pltpu.VMEM((PEER_HEADS, N_KEYS, tm), F32),
            pltpu.VMEM((ec, tm), BF16),
            pltpu.VMEM((D_MODEL, tm), F32),
        ],
        compiler_params=pltpu.CompilerParams(
            dimension_semantics=("arbitrary", "arbitrary"), vmem_limit_bytes=VMEM_LIMIT),
        name="peer",
    )(x2d, ml2d, rw2d, woml, worw, gffn, gfin, wqt, keys, u, u, vt, vt)


def _prep_weights(norm_mix_g, w_in, ml_conv_w, ml_conv_b, ml_b_i, ml_b_f, ml_norm_g,
                  rw_mu, rw_w0, rw_w2, rw_a0, rw_a2, rw_g2, rw_k_k, rw_k_a, rw_r_k, rw_gn_g, rw_gn_b,
                  w_out, norm_ffn_g, peer_w_q, peer_sub_keys, peer_u, peer_v, norm_final_g):
    assert w_in.shape[0] == 1, "one layer"
    w = w_in[0]
    wg = jnp.pad(w[:, ML_QKVO:ML_QKVO + 2 * ML_HEADS], ((0, 0), (0, LANE - 2 * ML_HEADS)))
    wg_hi = wg.astype(BF16)
    wg_lo = (wg - wg_hi.astype(F32)).astype(BF16)
    zeros = jnp.zeros((DECAY_LORA, RW_WIDTH), F32)
    wwa = jnp.concatenate([jnp.concatenate([rw_w2[0], zeros], axis=1),
                           jnp.concatenate([zeros, rw_a2[0]], axis=1)], axis=0)
    gate_bias = jnp.pad(jnp.concatenate([ml_b_i[0], ml_b_f[0]]), (0, LANE - 2 * ML_HEADS))[None, :]
    row = lambda a: a.reshape(1, -1)
    return dict(
        g_mix=row(norm_mix_g[0]),
        wml=w[:, 0:ML_QKVO].astype(BF16),
        wg=jnp.stack([wg_hi, wg_lo]),
        wrw=w[:, ML_QKVO + 2 * ML_HEADS:].astype(BF16),
        conv_w=ml_conv_w[0], conv_b=row(ml_conv_b[0]), gate_bias=gate_bias, ml_norm_g=row(ml_norm_g[0]),
        mu=row(rw_mu[0]), w0=row(rw_w0[0]), wwa=wwa.astype(BF16), a0=row(rw_a0[0]),
        g2=rw_g2[0].astype(BF16), k_k=row(rw_k_k[0]), k_a=row(rw_k_a[0]), r_k=row(rw_r_k[0]),
        gn_g=row(rw_gn_g[0]), gn_b=row(rw_gn_b[0]),
        wo_ml=w_out[0, 0:ML_WIDTH].astype(BF16), wo_rw=w_out[0, ML_WIDTH:].astype(BF16),
        g_ffn=row(norm_ffn_g[0]), g_fin=row(norm_final_g),
        wqt=peer_w_q[0].T.astype(BF16),
        keys=peer_sub_keys[0].reshape(2 * PEER_HEADS, N_KEYS, PEER_HALF).astype(BF16),
        u=peer_u[0].astype(BF16),
        vt=peer_v[0].reshape(-1, PEER_EXPERT_CHUNK // 2, D_MODEL).transpose(0, 2, 1).astype(BF16),
    )


def _rwkv_tiling(t):
    tr = min(t, RW_CHUNK)
    chunk = max(tr, RW_MIN_CHUNK)
    return tr, chunk, RW_STEP_ROWS // chunk // 2 if chunk < RW_CHUNK else RW_STEP_ROWS // chunk


def _trunk(x, states, wp, tr_ml, tr_rw, rw_chunk, rw_batch, tm_in, tm_peer, ec):
    b, t, d = x.shape
    c0, n0, m0, conv0, s0, shift0 = (s[0] for s in states)
    x2d = x.reshape(b * t, d)
    ml, gates, rw = _proj_in(x2d, wp["g_mix"], wp["wml"], wp["wg"], wp["wrw"], tm_in)
    ml3 = ml.reshape(b, t, ML_QKVO)
    rw3 = rw.reshape(b, t, RW_COLS)
    ml_out, c1, n1, m1, conv1 = _mlstm(
        ml3, gates.reshape(b, t, LANE), c0, n0, m0.reshape(b, 1, ML_HEADS), conv0,
        wp["conv_w"], wp["conv_b"], wp["gate_bias"], wp["ml_norm_g"], tr_ml)
    rw_out, s1 = _rwkv(
        rw3, shift0[:, None, :], s0.reshape(b, RW_WIDTH, RW_HD), wp["mu"], wp["w0"], wp["wwa"], wp["a0"], wp["g2"],
        wp["k_k"], wp["k_a"], wp["r_k"], wp["gn_g"], wp["gn_b"], tr_rw, min(b, rw_batch), rw_chunk)
    y = _peer(x2d, ml_out.reshape(b * t, ML_WIDTH), rw_out.reshape(b * t, RW_WIDTH),
              wp["wo_ml"], wp["wo_rw"], wp["g_ffn"], wp["g_fin"], wp["wqt"], wp["keys"],
              wp["u"], wp["vt"], tm_peer, ec)
    new_states = (c1[None], n1[None], m1.reshape(b, ML_HEADS)[None], conv1[None],
                  s1.reshape(b, RW_HEADS, RW_HD, RW_HD)[None], rw3[:, -1][None])
    return y.reshape(b, t, d), new_states


def kernel(x_prompt, x_sample, state_mlstm_C, state_mlstm_n, state_mlstm_m, state_mlstm_conv, state_rwkv_S, state_rwkv_shift, norm_mix_g, w_in, ml_conv_w, ml_conv_b, ml_b_i, ml_b_f, ml_norm_g, rw_mu, rw_w0, rw_w2, rw_a0, rw_a2, rw_g2, rw_k_k, rw_k_a, rw_r_k, rw_gn_g, rw_gn_b, w_out, norm_ffn_g, peer_w_q, peer_sub_keys, peer_u, peer_v, norm_final_g):
    wp = _prep_weights(norm_mix_g, w_in, ml_conv_w, ml_conv_b, ml_b_i, ml_b_f, ml_norm_g,
                       rw_mu, rw_w0, rw_w2, rw_a0, rw_a2, rw_g2, rw_k_k, rw_k_a, rw_r_k, rw_gn_g, rw_gn_b,
                       w_out, norm_ffn_g, peer_w_q, peer_sub_keys, peer_u, peer_v, norm_final_g)
    bp = x_prompt.shape[0]
    z = lambda *s: jnp.zeros((1, bp) + s, F32)
    prompt_states = (z(ML_HEADS, ML_HD, ML_HD), z(ML_HEADS, ML_HD), jnp.full((1, bp, ML_HEADS), M_INIT, F32),
                     z(CONV_W - 1, 2 * ML_WIDTH), z(RW_HEADS, RW_HD, RW_HD), z(RW_COLS))
    sample_states = (state_mlstm_C, state_mlstm_n, state_mlstm_m, state_mlstm_conv,
                     state_rwkv_S, state_rwkv_shift)
    tp, ts = x_prompt.shape[1], x_sample.shape[1]
    np_, ns = bp * tp, x_sample.shape[0] * ts
    y_p, st_p = _trunk(x_prompt, prompt_states, wp, min(tp, ML_CHUNK), *_rwkv_tiling(tp),
                       min(np_, PROJ_TILE), min(np_, PEER_TILE), PEER_EXPERT_CHUNK)
    y_s, st_s = _trunk(x_sample, sample_states, wp, min(ts, ML_CHUNK), *_rwkv_tiling(ts),
                       min(ns, PROJ_TILE), min(ns, PEER_TILE), PEER_EXPERT_CHUNK)
    return (y_p, y_s) + tuple(st_p) + tuple(st_s)
```

```python
import functools

import jax
import jax.numpy as jnp
from jax import lax
from jax.experimental import pallas as pl
from jax.experimental.pallas import tpu as pltpu

F32 = jnp.float32
BF16 = jnp.bfloat16

D_MODEL = 1024
ML_HEADS = 4
ML_HD = 128
ML_WIDTH = ML_HEADS * ML_HD
CONV_W = 4
RW_HEADS = 8
RW_HD = 64
RW_WIDTH = RW_HEADS * RW_HD
DECAY_LORA = 64
AAA_LORA = 64
GATE_LORA = 128
RW_COLS = 3 * RW_WIDTH + DECAY_LORA + AAA_LORA + GATE_LORA
ML_QKVO = 4 * ML_WIDTH
PEER_HEADS = 8
N_KEYS = 128
PEER_TOPK = 16
PEER_HALF = 128
NORM_EPS = 1e-6
ML_NORM_EPS = 1e-6
GN_EPS = RW_HD * 1e-5
M_INIT = -1e30
NEG_BIG = -1e30

LANE = 128
ML_CHUNK = 128
RW_CHUNK = 64
RW_GROUP = 4
RW_MIN_CHUNK = 16
RW_STEP_ROWS = 256
PROJ_TILE = 256
PEER_TILE = 512
PEER_EXPERT_CHUNK = 1024
VMEM_LIMIT = 56 * 1024 * 1024


def _dot(a, b):
    return jnp.dot(a, b, preferred_element_type=F32)


def _dot_nt(a, b):
    return lax.dot_general(a, b, (((1,), (1,)), ((), ())), preferred_element_type=F32)


def _split_bf16(x, n):
    parts = []
    r = x
    for _ in range(n):
        p = r.astype(BF16)
        parts.append(p)
        r = r - p.astype(F32)
    return parts


def _dot_exact_lhs(mask_bf16, x, n):
    return sum(_dot(mask_bf16, p) for p in _split_bf16(x, n))


def _dot_exact_rhs(x, mask_bf16, n):
    return sum(_dot(p, mask_bf16) for p in _split_bf16(x, n))


def _sigmoid(x):
    return 1.0 / (1.0 + jnp.exp(-x))


def _softplus(x):
    return jnp.maximum(x, 0.0) + jnp.log1p(jnp.exp(-jnp.abs(x)))


def _iota(shape, dim):
    return lax.broadcasted_iota(jnp.int32, shape, dim)


def _pad_rows(x, rows, value=0.0):
    if x.shape[0] == rows:
        return x
    return jnp.concatenate([x, jnp.full((rows - x.shape[0], x.shape[1]), value, x.dtype)], axis=0)


def _proj_in_kernel(x_ref, g_ref, wml_ref, wg_ref, wrw_ref, ml_ref, gate_ref, rw_ref):
    x = x_ref[...]
    xn = x * lax.rsqrt(jnp.mean(x * x, axis=-1, keepdims=True) + NORM_EPS) * g_ref[...]
    xb = xn.astype(BF16)
    ml_ref[...] = _dot(xb, wml_ref[...])
    rw_ref[...] = _dot(xb, wrw_ref[...])
    xlo = (xn - xb.astype(F32)).astype(BF16)
    gate_ref[...] = _dot(xb, wg_ref[0]) + _dot(xlo, wg_ref[0]) + _dot(xb, wg_ref[1])


def _proj_in(x2d, g, wml, wg, wrw, tm):
    n = x2d.shape[0]
    const2 = lambda i: (0, 0)
    return pl.pallas_call(
        _proj_in_kernel,
        grid=(n // tm,),
        in_specs=[
            pl.BlockSpec((tm, D_MODEL), lambda i: (i, 0)),
            pl.BlockSpec((1, D_MODEL), const2),
            pl.BlockSpec((D_MODEL, ML_QKVO), const2),
            pl.BlockSpec((2, D_MODEL, LANE), lambda i: (0, 0, 0)),
            pl.BlockSpec((D_MODEL, RW_COLS), const2),
        ],
        out_specs=[
            pl.BlockSpec((tm, ML_QKVO), lambda i: (i, 0)),
            pl.BlockSpec((tm, LANE), lambda i: (i, 0)),
            pl.BlockSpec((tm, RW_COLS), lambda i: (i, 0)),
        ],
        out_shape=[
            jax.ShapeDtypeStruct((n, ML_QKVO), F32),
            jax.ShapeDtypeStruct((n, LANE), F32),
            jax.ShapeDtypeStruct((n, RW_COLS), F32),
        ],
        compiler_params=pltpu.CompilerParams(
            dimension_semantics=("arbitrary",), vmem_limit_bytes=VMEM_LIMIT),
        name="proj_in",
    )(x2d, g, wml, wg, wrw)


def _mlstm_kernel(ml_ref, gate_ref, c0_ref, n0_ref, m0_ref, conv0_ref, cw_ref, cb_ref, gb_ref, ng_ref,
                  out_ref, c1_ref, n1_ref, m1_ref, conv1_ref,
                  caug_ref, m_ref, ext_ref, *, tr):
    L = ML_CHUNK
    c = pl.program_id(1)
    nc = pl.num_programs(1)

    @pl.when(c == 0)
    def _init():
        for h in range(ML_HEADS):
            caug_ref[h, :, 0:ML_HD] = c0_ref[0, h]
            nrow = n0_ref[0, h:h + 1, :]
            caug_ref[h, :, ML_HD:2 * ML_HD] = jnp.broadcast_to(nrow, (ML_HD, ML_HD)).T
        m_ref[...] = jnp.zeros(m_ref.shape, F32)
        m_ref[:, 0:ML_HEADS] = m0_ref[0]
        ext_ref[5:8, :] = conv0_ref[0]

    ext_ref[8:8 + tr, :] = ml_ref[0, :, 0:2 * ML_WIDTH]
    acc = cb_ref[...] + ext_ref[5:5 + tr, :] * cw_ref[0:1, :]
    for j in range(1, CONV_W):
        acc = acc + ext_ref[5 + j:5 + j + tr, :] * cw_ref[j:j + 1, :]
    qk = acc * _sigmoid(acc)
    tail = ext_ref[tr + 5:tr + 8, :]
    ext_ref[5:8, :] = tail

    @pl.when(c == nc - 1)
    def _conv_out():
        conv1_ref[0] = tail

    q_all = _pad_rows(qk[:, 0:ML_WIDTH], L)
    k_all = _pad_rows(qk[:, ML_WIDTH:2 * ML_WIDTH] * (ML_HD ** -0.5), L)
    v_all = _pad_rows(ml_ref[0, :, 2 * ML_WIDTH:3 * ML_WIDTH], L)
    o_all = ml_ref[0, :, 3 * ML_WIDTH:4 * ML_WIDTH]

    g = gate_ref[0] + gb_ref[...]
    i_all = _pad_rows(g, L, NEG_BIG)
    lf_all = _pad_rows(-_softplus(-g), L, 0.0)
    row = _iota((L, L), 0)
    col = _iota((L, L), 1)
    causal = row >= col
    tri = jnp.where(causal, 1.0, 0.0).astype(BF16)
    b_col = _dot_exact_lhs(tri, lf_all, 3)
    b_t = b_col.T
    i_t = i_all.T

    ones = jnp.ones((L, ML_HD), BF16)
    for h in range(ML_HEADS):
        bc = b_col[:, ML_HEADS + h:ML_HEADS + h + 1]
        br = b_t[ML_HEADS + h:ML_HEADS + h + 1, :]
        ir = i_t[h:h + 1, :]
        m_prev = m_ref[:, h:h + 1]
        logd = jnp.where(causal, bc - br + ir, -jnp.inf)
        linter = bc + m_prev
        m_t = jnp.maximum(linter, jnp.max(logd, axis=1, keepdims=True))
        d = jnp.exp(logd - m_t)
        s_inter = jnp.exp(linter - m_t)
        sl = slice(h * ML_HD, (h + 1) * ML_HD)
        qh = q_all[:, sl].astype(BF16)
        k_t = k_all[:, sl].T
        vaug = jnp.concatenate([v_all[:, sl].astype(BF16), ones], axis=1)
        caug = caug_ref[h]
        s = _dot(qh, k_t.astype(BF16)) * d
        num = s_inter * _dot(qh, caug.astype(BF16)) + _dot(s.astype(BF16), vaug)
        den = num[:, ML_HD:2 * ML_HD]
        hh = num[:, 0:ML_HD] / jnp.maximum(jnp.abs(den), jnp.exp(-m_t))
        mu = jnp.mean(hh, axis=-1, keepdims=True)
        dv = hh - mu
        var = jnp.mean(dv * dv, axis=-1, keepdims=True)
        y = dv * lax.rsqrt(var + ML_NORM_EPS) * ng_ref[:, sl]
        out_ref[0, :, sl] = y[0:tr] * _sigmoid(o_all[:, sl])
        m_new = m_t[L - 1:L, :]
        b_last = bc[L - 1:L, :]
        s_state = jnp.exp(b_last + m_prev - m_new)
        w_row = jnp.exp(b_last - br + ir - m_new)
        caug_ref[h] = s_state * caug + _dot((k_t * w_row).astype(BF16), vaug)
        m_ref[:, h:h + 1] = m_new

    @pl.when(c == nc - 1)
    def _state_out():
        for h in range(ML_HEADS):
            caug = caug_ref[h]
            c1_ref[0, h] = caug[:, 0:ML_HD]
            n1_ref[0, h:h + 1, :] = caug[:, ML_HD:2 * ML_HD].T[0:1, :]
        m1_ref[0] = m_ref[:, 0:ML_HEADS]


def _mlstm(ml, gates, c0, n0, m0, conv0, cw, cb, gb, ng, tr):
    b, t, _ = ml.shape
    nc = t // tr
    bmap = lambda i, j: (i, 0, 0)
    const2 = lambda i, j: (0, 0)
    return pl.pallas_call(
        functools.partial(_mlstm_kernel, tr=tr),
        grid=(b, nc),
        in_specs=[
            pl.BlockSpec((1, tr, ML_QKVO), lambda i, j: (i, j, 0)),
            pl.BlockSpec((1, tr, LANE), lambda i, j: (i, j, 0)),
            pl.BlockSpec((1, ML_HEADS, ML_HD, ML_HD), lambda i, j: (i, 0, 0, 0)),
            pl.BlockSpec((1, ML_HEADS, ML_HD), bmap),
            pl.BlockSpec((1, 1, ML_HEADS), bmap),
            pl.BlockSpec((1, CONV_W - 1, 2 * ML_WIDTH), bmap),
            pl.BlockSpec((CONV_W, 2 * ML_WIDTH), const2),
            pl.BlockSpec((1, 2 * ML_WIDTH), const2),
            pl.BlockSpec((1, LANE), const2),
            pl.BlockSpec((1, ML_WIDTH), const2),
        ],
        out_specs=[
            pl.BlockSpec((1, tr, ML_WIDTH), lambda i, j: (i, j, 0)),
            pl.BlockSpec((1, ML_HEADS, ML_HD, ML_HD), lambda i, j: (i, 0, 0, 0)),
            pl.BlockSpec((1, ML_HEADS, ML_HD), bmap),
            pl.BlockSpec((1, 1, ML_HEADS), bmap),
            pl.BlockSpec((1, CONV_W - 1, 2 * ML_WIDTH), bmap),
        ],
        out_shape=[
            jax.ShapeDtypeStruct((b, t, ML_WIDTH), F32),
            jax.ShapeDtypeStruct((b, ML_HEADS, ML_HD, ML_HD), F32),
            jax.ShapeDtypeStruct((b, ML_HEADS, ML_HD), F32),
            jax.ShapeDtypeStruct((b, 1, ML_HEADS), F32),
            jax.ShapeDtypeStruct((b, CONV_W - 1, 2 * ML_WIDTH), F32),
        ],
        scratch_shapes=[
            pltpu.VMEM((ML_HEADS, ML_HD, 2 * ML_HD), F32),
            pltpu.VMEM((1, LANE), F32),
            pltpu.VMEM((tr + 8, 2 * ML_WIDTH), F32),
        ],
        compiler_params=pltpu.CompilerParams(
            dimension_semantics=("arbitrary", "arbitrary"), vmem_limit_bytes=VMEM_LIMIT),
        name="mlstm",
    )(ml, gates, c0, n0, m0, conv0, cw, cb, gb, ng)


def _rwkv_kernel(c_ref, sh_ref, s0_ref, mu_ref, w0_ref, wwa_ref, a0_ref, g2_ref, kk_ref, ka_ref, rk_ref,
                 gng_ref, gnb_ref, out_ref, s1_ref, s_ref, last_ref, *, tr, nb, L):
    W = RW_WIDTH
    GW = RW_GROUP * RW_HD
    GL = RW_GROUP * L
    NGRP = RW_HEADS // RW_GROUP
    ci = pl.program_id(1)
    nc = pl.num_programs(1)

    r2 = _iota((GL, GW), 0)
    c2 = _iota((GL, GW), 1)
    bd = (r2 // L) == (c2 // RW_HD)
    rg = _iota((GL, GL), 0)
    cg = _iota((GL, GL), 1)
    s_lower = rg > cg
    i_lower = rg >= cg

    @pl.when(ci == 0)
    def _init():
        sbd = (_iota((GW, GW), 0) // RW_HD) == (_iota((GW, GW), 1) // RW_HD)
        for bi in range(nb):
            last_ref[bi, 0:1, :] = sh_ref[bi]
            for gi in range(NGRP):
                x = s0_ref[bi, gi * GW:(gi + 1) * GW, :]
                s_ref[bi * NGRP + gi] = jnp.where(sbd, jnp.concatenate([x] * RW_GROUP, axis=1), 0.0)

    c = c_ref[...].reshape(nb * tr, RW_COLS)
    prev = pltpu.roll(c, 1, axis=0)
    rowid = _iota(c.shape, 0)
    for bi in range(nb):
        prev = jnp.where(rowid == bi * tr, last_ref[bi, 0:1, :], prev)
        last_ref[bi, 0:1, :] = c[(bi + 1) * tr - 1:(bi + 1) * tr, :]
    xs = c + (prev - c) * mu_ref[...]
    r = xs[:, 0:W]
    k = xs[:, W:2 * W]
    v = xs[:, 2 * W:3 * W]
    slab = xs[:, 3 * W:3 * W + DECAY_LORA + AAA_LORA]
    gd = xs[:, 3 * W + DECAY_LORA + AAA_LORA:]
    lane = _iota(slab.shape, 1)
    t_in = jnp.where(lane < DECAY_LORA, jnp.tanh(slab), slab)
    la = _dot(t_in.astype(BF16), wwa_ref[...])
    w_log = -_softplus(-(w0_ref[...] + la[:, 0:W])) - 0.5
    lw_all = -jnp.exp(w_log)
    a_all = _sigmoid(a0_ref[...] + la[:, W:2 * W])
    g = _dot(_sigmoid(gd).astype(BF16), g2_ref[...])

    rs = _iota((W, W), 0)
    cs = _iota((W, W), 1)
    seg = jnp.where((rs // RW_HD) == (cs // RW_HD), 1.0, 0.0).astype(BF16)

    kk = k * kk_ref[...]
    kn_all = k * (1.0 + (a_all - 1.0) * ka_ref[...])
    ss = _dot_exact_rhs(kk * kk, seg, 2)
    kap_all = kk / jnp.maximum(jnp.sqrt(ss), 1e-12)
    bonus = _dot_exact_rhs(r * kn_all * rk_ref[...], seg, 2) * v

    rl = _iota((L, L), 0)
    cl_ = _iota((L, L), 1)
    tri = jnp.where(rl >= cl_, 1.0, 0.0).astype(BF16)

    def stack(x):
        return jnp.where(bd, jnp.concatenate([x] * RW_GROUP, axis=0), 0.0).astype(BF16)

    seqs = []
    for bi in range(nb):
        rows = slice(bi * tr, (bi + 1) * tr)
        lw = _pad_rows(lw_all[rows], L)
        kap = _pad_rows(kap_all[rows], L)
        kn_p = _pad_rows(kn_all[rows], L)
        a_p = _pad_rows(a_all[rows], L)
        v_p = _pad_rows(v[rows], L)
        r_p = _pad_rows(r[rows], L)
        cum = _dot_exact_lhs(tri, lw, 3)
        cum_last = cum[L - 1:L, :]
        e_neg = jnp.exp(-cum)
        e_rem = jnp.exp(cum_last - cum)
        seqs.append(dict(
            abar=-kap * jnp.exp(cum - lw), btil=kap * a_p * e_neg, ktil=kn_p * e_neg, rbar=r_p * jnp.exp(cum),
            bhat=kap * a_p * e_rem, khat=kn_p * e_rem, v=v_p, g_last=jnp.exp(cum_last)))

    chains = [(bi, gi) for bi in range(nb) for gi in range(NGRP)]
    ch = []
    for bi, gi in chains:
        q = seqs[bi]
        ls = slice(gi * GW, (gi + 1) * GW)
        a_s, b_s, k_s, r_s, v_s = (stack(q[n][:, ls]) for n in ("abar", "btil", "ktil", "rbar", "v"))
        ch.append(dict(
            ar=jnp.concatenate([a_s, r_s], axis=0),
            bk=jnp.concatenate([b_s, k_s], axis=0),
            bkh=jnp.concatenate([stack(q["bhat"][:, ls]), stack(q["khat"][:, ls])], axis=0),
            v_s=v_s, g_last=q["g_last"][:, ls], s=s_ref[bi * NGRP + gi]))
    for c_ in ch:
        p = _dot_nt(c_["ar"], c_["bk"])
        c_["npow"] = jnp.where(s_lower, p[0:GL, 0:GL], 0.0)
        c_["a_ak"] = jnp.where(s_lower, p[0:GL, GL:2 * GL], 0.0).astype(BF16)
        c_["r_b"] = jnp.where(i_lower, p[GL:2 * GL, 0:GL], 0.0).astype(BF16)
        c_["r_k"] = jnp.where(i_lower, p[GL:2 * GL, GL:2 * GL], 0.0).astype(BF16)
    for c_ in ch:
        c_["q0"] = _dot_nt(c_["ar"], c_["s"].astype(BF16))
    for c_ in ch:
        c_["u"] = c_["q0"][0:GL] + _dot(c_["a_ak"], c_["v_s"])
    lvls = L.bit_length() - 1
    for lvl in range(lvls):
        for c_ in ch:
            nb16 = c_["npow"].astype(BF16)
            c_["u"] = c_["u"] + _dot(nb16, c_["u"].astype(BF16))
            if lvl < lvls - 1:
                c_["npow"] = _dot(nb16, nb16)
    for c_ in ch:
        y = (c_["q0"][GL:2 * GL] + _dot(c_["r_b"], c_["u"].astype(BF16)) + _dot(c_["r_k"], c_["v_s"]))
        yg = y[0:L]
        for j in range(1, RW_GROUP):
            yg = yg + y[j * L:(j + 1) * L]
        c_["yg"] = yg
    for (bi, gi), c_ in zip(chains, ch):
        uv_t = jnp.concatenate([c_["u"], c_["v_s"].astype(F32)], axis=0).T.astype(BF16)
        s_ref[bi * NGRP + gi] = c_["s"] * c_["g_last"] + _dot(uv_t, c_["bkh"])
    y_rows = [jnp.concatenate([ch[bi * NGRP + gi]["yg"] for gi in range(NGRP)], axis=1)[0:tr]
              for bi in range(nb)]

    y_all = jnp.concatenate(y_rows, axis=0) if nb > 1 else y_rows[0]
    inv = 1.0 / RW_HD
    mu_ = _dot_exact_rhs(y_all, seg, 2) * inv
    dy = y_all - mu_
    var = _dot_exact_rhs(dy * dy, seg, 2) * inv
    yn = dy * lax.rsqrt(var + GN_EPS) * gng_ref[...] + gnb_ref[...]
    out_ref[...] = ((yn + bonus) * g).reshape(nb, tr, W)

    @pl.when(ci == nc - 1)
    def _state_out():
        for bi in range(nb):
            for gi in range(NGRP):
                s = s_ref[bi * NGRP + gi]
                f = s[:, 0:RW_HD]
                for j in range(1, RW_GROUP):
                    f = f + s[:, j * RW_HD:(j + 1) * RW_HD]
                s1_ref[bi, gi * GW:(gi + 1) * GW, :] = f


def _rwkv(cols, shift0, s0, mu, w0, wwa, a0, g2, kk, ka, rk, gng, gnb, tr, nb, chunk):
    b, t, _ = cols.shape
    nc = t // tr
    assert b % nb == 0 and t % tr == 0 and tr <= chunk
    const2 = lambda i, j: (0, 0)
    vec = pl.BlockSpec((1, RW_WIDTH), const2)
    return pl.pallas_call(
        functools.partial(_rwkv_kernel, tr=tr, nb=nb, L=chunk),
        grid=(b // nb, nc),
        in_specs=[
            pl.BlockSpec((nb, tr, RW_COLS), lambda i, j: (i, j, 0)),
            pl.BlockSpec((nb, 1, RW_COLS), lambda i, j: (i, 0, 0)),
            pl.BlockSpec((nb, RW_WIDTH, RW_HD), lambda i, j: (i, 0, 0)),
            pl.BlockSpec((1, RW_COLS), const2),
            vec,
            pl.BlockSpec((DECAY_LORA + AAA_LORA, 2 * RW_WIDTH), const2),
            vec,
            pl.BlockSpec((GATE_LORA, RW_WIDTH), const2),
            vec, vec, vec, vec, vec,
        ],
        out_specs=[
            pl.BlockSpec((nb, tr, RW_WIDTH), lambda i, j: (i, j, 0)),
            pl.BlockSpec((nb, RW_WIDTH, RW_HD), lambda i, j: (i, 0, 0)),
        ],
        out_shape=[
            jax.ShapeDtypeStruct((b, t, RW_WIDTH), F32),
            jax.ShapeDtypeStruct((b, RW_WIDTH, RW_HD), F32),
        ],
        scratch_shapes=[pltpu.VMEM((nb * (RW_HEADS // RW_GROUP), RW_GROUP * RW_HD, RW_GROUP * RW_HD), F32),
                        pltpu.VMEM((nb, 8, RW_COLS), F32)],
        compiler_params=pltpu.CompilerParams(
            dimension_semantics=("arbitrary", "arbitrary"), vmem_limit_bytes=VMEM_LIMIT),
        name="rwkv",
    )(cols, shift0, s0, mu, w0, wwa, a0, g2, kk, ka, rk, gng, gnb)


def _extract_top(works, idx, n):
    ranks = [jnp.full(w.shape, 99.0, F32) for w in works]
    vals = [[] for _ in works]
    for j in range(n):
        ms = [jnp.max(w, axis=0, keepdims=True) for w in works]
        if idx is None:
            sels = [w == m for w, m in zip(works, ms)]
        else:
            firsts = [jnp.min(jnp.where(w == m, idx, 1e9), axis=0, keepdims=True) for w, m in zip(works, ms)]
            sels = [idx == f for f in firsts]
        ranks = [jnp.where(sel, float(j), r) for sel, r in zip(sels, ranks)]
        works = [jnp.where(sel, -jnp.inf, w) for sel, w in zip(sels, works)]
        for v, m in zip(vals, ms):
            v.append(m)
    return vals, ranks


def _miscount(rank, n):
    taken = jnp.sum(jnp.where(rank < float(n), 1.0, 0.0), axis=0, keepdims=True)
    return jnp.abs(taken - float(n))


def _rows_bf16(row, rows):
    packed = jnp.broadcast_to(row, (16, row.shape[1])).astype(BF16)
    return jnp.concatenate([packed] * (rows // 16), axis=0)


def _peer_kernel(x_ref, ml_ref, rw_ref, woml_ref, worw_ref, gffn_ref, gfin_ref, wqt_ref, keys_ref,
                 ua_ref, ub_ref, vta_ref, vtb_ref, y_ref,
                 ht_ref, qt_ref, rank_ref, e_ref, vals_ref, r2b_ref, e2b_ref, cnt_ref, gate_ref,
                 p_ref, acc_ref, *, tm, ec):
    e = pl.program_id(1)
    ne = pl.num_programs(1)
    ng = tm // LANE
    K = PEER_TOPK
    nslab = ec // N_KEYS
    act_ref = qt_ref

    @pl.when(e == 0)
    def _select():
        x1 = (x_ref[...] + _dot(ml_ref[...].astype(BF16), woml_ref[...])
              + _dot(rw_ref[...].astype(BF16), worw_ref[...]))
        y_ref[...] = x1
        h = x1 * lax.rsqrt(jnp.mean(x1 * x1, axis=-1, keepdims=True) + NORM_EPS) * gffn_ref[...]
        ht = h.T.astype(BF16)
        ht_ref[...] = ht
        qt_ref[...] = _dot(wqt_ref[...], ht)
        acc_ref[...] = jnp.zeros(acc_ref.shape, F32)

        def score_body(hp, carry):
            q = qt_ref[pl.ds(pl.multiple_of(hp * PEER_HALF, PEER_HALF), PEER_HALF), :]
            e_ref[hp] = _dot(keys_ref[hp], q.astype(BF16))
            return carry

        lax.fori_loop(0, 2 * PEER_HEADS, score_body, 0)

        key_idx = _iota((N_KEYS, LANE), 0).astype(F32)
        groups = [slice(gi * LANE, (gi + 1) * LANE) for gi in range(ng)]

        def topk_groups(hp, exact_ties):
            scores = [e_ref[hp, :, lanes] for lanes in groups]
            vals, ranks = _extract_top(scores, key_idx if exact_ties else None, K)
            bad = None
            for lanes, v, rank in zip(groups, vals, ranks):
                rank_ref[hp, :, lanes] = rank
                vals_ref[hp, :, lanes] = jnp.concatenate(v, axis=0)
                miss = _miscount(rank, K)
                bad = miss if bad is None else jnp.maximum(bad, miss)
            return bad

        def topk_body(hp, carry):
            bad = topk_groups(hp, False)

            @pl.when(jnp.max(bad) > 0.0)
            def _redo():
                topk_groups(hp, True)

            return carry

        lax.fori_loop(0, 2 * PEER_HEADS, topk_body, 0)

        sub8 = _iota((8, LANE), 0)

        def cand_groups(h, exact_ties):
            works, v1s = [], []
            slab_a = [a_i for a_i in range(K) for _ in range(0, K // (a_i + 1), 8)]
            for lanes in groups:
                v1 = vals_ref[2 * h, :, lanes]
                v2 = vals_ref[2 * h + 1, :, lanes]
                slabs = []
                for a_i in range(K):
                    nb = K // (a_i + 1)
                    for b0 in range(0, nb, 8):
                        rows = v1[a_i:a_i + 1, :] + v2[b0:b0 + 8, :]
                        slabs.append(jnp.where(sub8 + b0 < nb, rows, -jnp.inf))
                works.append(jnp.concatenate(slabs, axis=0))
                v1s.append(v1)
            idx = None
            if exact_ties:
                idx = jnp.concatenate([(sub8 + (b0 + a_i * K)).astype(F32) for a_i in range(K)
                                       for b0 in range(0, K // (a_i + 1), 8)], axis=0)
            vals, ranks = _extract_top(works, idx, K)
            bad = None
            for lanes, v1, v, rank in zip(groups, v1s, vals, ranks):
                z = jnp.ones_like(v[0])
                for j in range(1, K):
                    z = z + jnp.exp(v[j] - v[0])
                picked = jnp.where(rank < float(K), 1.0, 0.0)
                r1 = rank_ref[2 * h, :, lanes]
                cnt = jnp.zeros((N_KEYS, LANE), F32)
                for a_i in range(K):
                    ca = None
                    for si, sa in enumerate(slab_a):
                        if sa == a_i:
                            part = jnp.sum(picked[si * 8:(si + 1) * 8], axis=0, keepdims=True)
                            ca = part if ca is None else ca + part
                    cnt = jnp.where(r1 == float(a_i), ca, cnt)
                cnt_ref[h, :, lanes] = cnt
                e1 = jnp.exp(e_ref[2 * h, :, lanes] - v1[0:1, :])
                gate_ref[h, :, lanes] = jnp.where(r1 < float(K), e1 / z, 0.0)
                miss = _miscount(rank, K)
                bad = miss if bad is None else jnp.maximum(bad, miss)
            return bad

        def cand_body(h, carry):
            bad = cand_groups(h, False)

            @pl.when(jnp.max(bad) > 0.0)
            def _redo():
                cand_groups(h, True)

            r2b_ref[h] = rank_ref[2 * h + 1].astype(BF16)
            e2b_ref[h] = jnp.exp(e_ref[2 * h + 1] - vals_ref[2 * h + 1, 0:1, :]).astype(BF16)
            return carry

        lax.fori_loop(0, PEER_HEADS, cand_body, 0)

    key1 = pl.ds(pl.multiple_of(e * nslab, nslab), nslab)
    hs = nslab // 2
    hrows = hs * N_KEYS
    for half in range(2):
        act_ref[half * hrows:(half + 1) * hrows, :] = _dot((ua_ref, ub_ref)[half][...], ht_ref[...])
    def gate_half(half):
        for gi in range(ng):
            lanes = slice(gi * LANE, (gi + 1) * LANE)
            w = [jnp.zeros((N_KEYS, LANE), BF16) for _ in range(hs)]
            for h in range(PEER_HEADS):
                r2 = r2b_ref[h, :, lanes]
                e2 = e2b_ref[h, :, lanes]
                cw = cnt_ref[h, key1, lanes]
                gw = gate_ref[h, key1, lanes]
                for jj in range(hs):
                    j = half * hs + jj
                    hit = r2 < _rows_bf16(cw[j:j + 1, :], N_KEYS)
                    w[jj] = w[jj] + jnp.where(hit, e2, jnp.zeros_like(e2)) * _rows_bf16(gw[j:j + 1, :], N_KEYS)
            for jj in range(hs):
                rows = slice((half * hs + jj) * N_KEYS, (half * hs + jj + 1) * N_KEYS)
                act = act_ref[rows, lanes]
                gelu = 0.5 * act * (1.0 + lax.erf(act * 0.7071067811865476))
                p_ref[rows, lanes] = w[jj] * gelu.astype(BF16)

    for half in range(2):
        pl.when(e >= 0)(functools.partial(gate_half, half))
        acc_ref[...] += _dot((vta_ref, vtb_ref)[half][0], p_ref[half * hrows:(half + 1) * hrows, :])

    @pl.when(e == ne - 1)
    def _finish():
        x2 = y_ref[...] + acc_ref[...].T
        y_ref[...] = x2 * lax.rsqrt(jnp.mean(x2 * x2, axis=-1, keepdims=True) + NORM_EPS) * gfin_ref[...]


def _peer(x2d, ml2d, rw2d, woml, worw, gffn, gfin, wqt, keys, u, vt, tm, ec):
    n = x2d.shape[0]
    n_exp = u.shape[0]
    assert ec == 8 * N_KEYS and tm % LANE == 0 and n % tm == 0 and n_exp % ec == 0
    tok = lambda i, e: (i, 0)
    const2 = lambda i, e: (0, 0)
    qrows = 2 * PEER_HEADS * PEER_HALF
    eh = ec // 2
    once = pl.Buffered(1)
    return pl.pallas_call(
        functools.partial(_peer_kernel, tm=tm, ec=ec),
        grid=(n // tm, n_exp // ec),
        in_specs=[
            pl.BlockSpec((tm, D_MODEL), tok),
            pl.BlockSpec((tm, ML_WIDTH), tok),
            pl.BlockSpec((tm, RW_WIDTH), tok),
            pl.BlockSpec((ML_WIDTH, D_MODEL), const2, pipeline_mode=once),
            pl.BlockSpec((RW_WIDTH, D_MODEL), const2, pipeline_mode=once),
            pl.BlockSpec((1, D_MODEL), const2),
            pl.BlockSpec((1, D_MODEL), const2),
            pl.BlockSpec((qrows, D_MODEL), const2, pipeline_mode=once),
            pl.BlockSpec((2 * PEER_HEADS, N_KEYS, PEER_HALF), lambda i, e: (0, 0, 0), pipeline_mode=once),
            pl.BlockSpec((eh, D_MODEL), lambda i, e: (2 * e, 0)),
            pl.BlockSpec((eh, D_MODEL), lambda i, e: (2 * e + 1, 0)),
            pl.BlockSpec((1, D_MODEL, eh), lambda i, e: (2 * e, 0, 0)),
            pl.BlockSpec((1, D_MODEL, eh), lambda i, e: (2 * e + 1, 0, 0)),
        ],
        out_specs=pl.BlockSpec((tm, D_MODEL), tok),
        out_shape=jax.ShapeDtypeStruct((n, D_MODEL), F32),
        scratch_shapes=[
            pltpu.VMEM((D_MODEL, tm), BF16),
            pltpu.VMEM((qrows, tm), F32),
            pltpu.VMEM((2 * PEER_HEADS, N_KEYS, tm), F32),
            pltpu.VMEM((2 * PEER_HEADS, N_KEYS, tm), F32),
            pltpu.VMEM((2 * PEER_HEADS, PEER_TOPK, tm), F32),
            pltpu.VMEM((PEER_HEADS, N_KEYS, tm), BF16),
            pltpu.VMEM((PEER_HEADS, N_KEYS, tm), BF16),
            pltpu.VMEM((PEER_HEADS, N_KEYS, tm), F32),
            pltpu.VMEM((PEER_HEADS, N_KEYS, tm), F32),
            pltpu.VMEM((ec, tm), BF16),
            pltpu.VMEM((D_MODEL, tm), F32),
        ],
        compiler_params=pltpu.CompilerParams(
            dimension_semantics=("arbitrary", "arbitrary"), vmem_limit_bytes=VMEM_LIMIT),
        name="peer",
    )(x2d, ml2d, rw2d, woml, worw, gffn, gfin, wqt, keys, u, u, vt, vt)


def _prep_weights(norm_mix_g, w_in, ml_conv_w, ml_conv_b, ml_b_i, ml_b_f, ml_norm_g,
                  rw_mu, rw_w0, rw_w2, rw_a0, rw_a2, rw_g2, rw_k_k, rw_k_a, rw_r_k, rw_gn_g, rw_gn_b,
                  w_out, norm_ffn_g, peer_w_q, peer_sub_keys, peer_u, peer_v, norm_final_g):
    assert w_in.shape[0] == 1, "one layer"
    w = w_in[0]
    wg = jnp.pad(w[:, ML_QKVO:ML_QKVO + 2 * ML_HEADS], ((0, 0), (0, LANE - 2 * ML_HEADS)))
    wg_hi = wg.astype(BF16)
    wg_lo = (wg - wg_hi.astype(F32)).astype(BF16)
    zeros = jnp.zeros((DECAY_LORA, RW_WIDTH), F32)
    wwa = jnp.concatenate([jnp.concatenate([rw_w2[0], zeros], axis=1),
                           jnp.concatenate([zeros, rw_a2[0]], axis=1)], axis=0)
    gate_bias = jnp.pad(jnp.concatenate([ml_b_i[0], ml_b_f[0]]), (0, LANE - 2 * ML_HEADS))[None, :]
    row = lambda a: a.reshape(1, -1)
    return dict(
        g_mix=row(norm_mix_g[0]),
        wml=w[:, 0:ML_QKVO].astype(BF16),
        wg=jnp.stack([wg_hi, wg_lo]),
        wrw=w[:, ML_QKVO + 2 * ML_HEADS:].astype(BF16),
        conv_w=ml_conv_w[0], conv_b=row(ml_conv_b[0]), gate_bias=gate_bias, ml_norm_g=row(ml_norm_g[0]),
        mu=row(rw_mu[0]), w0=row(rw_w0[0]), wwa=wwa.astype(BF16), a0=row(rw_a0[0]),
        g2=rw_g2[0].astype(BF16), k_k=row(rw_k_k[0]), k_a=row(rw_k_a[0]), r_k=row(rw_r_k[0]),
        gn_g=row(rw_gn_g[0]), gn_b=row(rw_gn_b[0]),
        wo_ml=w_out[0, 0:ML_WIDTH].astype(BF16), wo_rw=w_out[0, ML_WIDTH:].astype(BF16),
        g_ffn=row(norm_ffn_g[0]), g_fin=row(norm_final_g),
        wqt=peer_w_q[0].T.astype(BF16),
        keys=peer_sub_keys[0].reshape(2 * PEER_HEADS, N_KEYS, PEER_HALF).astype(BF16),
        u=peer_u[0].astype(BF16),
        vt=peer_v[0].reshape(-1, PEER_EXPERT_CHUNK // 2, D_MODEL).transpose(0, 2, 1).astype(BF16),
    )


def _rwkv_tiling(t):
    tr = min(t, RW_CHUNK)
    chunk = max(tr, RW_MIN_CHUNK)
    return tr, chunk, RW_STEP_ROWS // chunk // 2 if chunk < RW_CHUNK else RW_STEP_ROWS // chunk


def _trunk(x, states, wp, tr_ml, tr_rw, rw_chunk, rw_batch, tm_in, tm_peer, ec):
    b, t, d = x.shape
    c0, n0, m0, conv0, s0, shift0 = (s[0] for s in states)
    x2d = x.reshape(b * t, d)
    ml, gates, rw = _proj_in(x2d, wp["g_mix"], wp["wml"], wp["wg"], wp["wrw"], tm_in)
    ml3 = ml.reshape(b, t, ML_QKVO)
    rw3 = rw.reshape(b, t, RW_COLS)
    ml_out, c1, n1, m1, conv1 = _mlstm(
        ml3, gates.reshape(b, t, LANE), c0, n0, m0.reshape(b, 1, ML_HEADS), conv0,
        wp["conv_w"], wp["conv_b"], wp["gate_bias"], wp["ml_norm_g"], tr_ml)
    rw_out, s1 = _rwkv(
        rw3, shift0[:, None, :], s0.reshape(b, RW_WIDTH, RW_HD), wp["mu"], wp["w0"], wp["wwa"], wp["a0"], wp["g2"],
        wp["k_k"], wp["k_a"], wp["r_k"], wp["gn_g"], wp["gn_b"], tr_rw, min(b, rw_batch), rw_chunk)
    y = _peer(x2d, ml_out.reshape(b * t, ML_WIDTH), rw_out.reshape(b * t, RW_WIDTH),
              wp["wo_ml"], wp["wo_rw"], wp["g_ffn"], wp["g_fin"], wp["wqt"], wp["keys"],
              wp["u"], wp["vt"], tm_peer, ec)
    new_states = (c1[None], n1[None], m1.reshape(b, ML_HEADS)[None], conv1[None],
                  s1.reshape(b, RW_HEADS, RW_HD, RW_HD)[None], rw3[:, -1][None])
    return y.reshape(b, t, d), new_states


def kernel(x_prompt, x_sample, state_mlstm_C, state_mlstm_n, state_mlstm_m, state_mlstm_conv, state_rwkv_S, state_rwkv_shift, norm_mix_g, w_in, ml_conv_w, ml_conv_b, ml_b_i, ml_b_f, ml_norm_g, rw_mu, rw_w0, rw_w2, rw_a0, rw_a2, rw_g2, rw_k_k, rw_k_a, rw_r_k, rw_gn_g, rw_gn_b, w_out, norm_ffn_g, peer_w_q, peer_sub_keys, peer_u, peer_v, norm_final_g):
    wp = _prep_weights(norm_mix_g, w_in, ml_conv_w, ml_conv_b, ml_b_i, ml_b_f, ml_norm_g,
                       rw_mu, rw_w0, rw_w2, rw_a0, rw_a2, rw_g2, rw_k_k, rw_k_a, rw_r_k, rw_gn_g, rw_gn_b,
                       w_out, norm_ffn_g, peer_w_q, peer_sub_keys, peer_u, peer_v, norm_final_g)
    bp = x_prompt.shape[0]
    z = lambda *s: jnp.zeros((1, bp) + s, F32)
    prompt_states = (z(ML_HEADS, ML_HD, ML_HD), z(ML_HEADS, ML_HD), jnp.full((1, bp, ML_HEADS), M_INIT, F32),
                     z(CONV_W - 1, 2 * ML_WIDTH), z(RW_HEADS, RW_HD, RW_HD), z(RW_COLS))
    sample_states = (state_mlstm_C, state_mlstm_n, state_mlstm_m, state_mlstm_conv,
                     state_rwkv_S, state_rwkv_shift)
    tp, ts = x_prompt.shape[1], x_sample.shape[1]
    np_, ns = bp * tp, x_sample.shape[0] * ts
    y_p, st_p = _trunk(x_prompt, prompt_states, wp, min(tp, ML_CHUNK), *_rwkv_tiling(tp),
                       min(np_, PROJ_TILE), min(np_, PEER_TILE), PEER_EXPERT_CHUNK)
    y_s, st_s = _trunk(x_sample, sample_states, wp, min(ts, ML_CHUNK), *_rwkv_tiling(ts),
                       min(ns, PROJ_TILE), min(ns, PEER_TILE), PEER_EXPERT_CHUNK)
    return (y_p, y_s) + tuple(st_p) + tuple(st_s)
```

```python
import functools

import jax
import jax.numpy as jnp
from jax import lax
from jax.experimental import pallas as pl
from jax.experimental.pallas import tpu as pltpu

F32 = jnp.float32
BF16 = jnp.bfloat16

D_MODEL = 1024
ML_HEADS = 4
ML_HD = 128
ML_WIDTH = ML_HEADS * ML_HD
CONV_W = 4
RW_HEADS = 8
RW_HD = 64
RW_WIDTH = RW_HEADS * RW_HD
DECAY_LORA = 64
AAA_LORA = 64
GATE_LORA = 128
RW_COLS = 3 * RW_WIDTH + DECAY_LORA + AAA_LORA + GATE_LORA
ML_QKVO = 4 * ML_WIDTH
PEER_HEADS = 8
N_KEYS = 128
PEER_TOPK = 16
PEER_HALF = 128
NORM_EPS = 1e-6
ML_NORM_EPS = 1e-6
GN_EPS = RW_HD * 1e-5
M_INIT = -1e30
NEG_BIG = -1e30

LANE = 128
ML_CHUNK = 128
RW_CHUNK = 64
RW_GROUP = 4
RW_MIN_CHUNK = 16
RW_STEP_ROWS = 256
PROJ_TILE = 512
PEER_TILE = 512
PEER_EXPERT_CHUNK = 1024
VMEM_LIMIT = 56 * 1024 * 1024


def _dot(a, b):
    return jnp.dot(a, b, preferred_element_type=F32)


def _dot_nt(a, b):
    return lax.dot_general(a, b, (((1,), (1,)), ((), ())), preferred_element_type=F32)


def _split_bf16(x, n):
    parts = []
    r = x
    for _ in range(n):
        p = r.astype(BF16)
        parts.append(p)
        r = r - p.astype(F32)
    return parts


def _dot_exact_lhs(mask_bf16, x, n):
    return sum(_dot(mask_bf16, p) for p in _split_bf16(x, n))


def _dot_exact_rhs(x, mask_bf16, n):
    return sum(_dot(p, mask_bf16) for p in _split_bf16(x, n))


def _sigmoid(x):
    return 1.0 / (1.0 + jnp.exp(-x))


def _softplus(x):
    return jnp.maximum(x, 0.0) + jnp.log1p(jnp.exp(-jnp.abs(x)))


def _iota(shape, dim):
    return lax.broadcasted_iota(jnp.int32, shape, dim)


def _pad_rows(x, rows, value=0.0):
    if x.shape[0] == rows:
        return x
    return jnp.concatenate([x, jnp.full((rows - x.shape[0], x.shape[1]), value, x.dtype)], axis=0)


def _proj_in_kernel(x_ref, g_ref, wml_ref, wg_ref, wrw_ref, ml_ref, gate_ref, rw_ref):
    x = x_ref[...]
    xn = x * lax.rsqrt(jnp.mean(x * x, axis=-1, keepdims=True) + NORM_EPS) * g_ref[...]
    xb = xn.astype(BF16)
    ml_ref[...] = _dot(xb, wml_ref[...])
    rw_ref[...] = _dot(xb, wrw_ref[...])
    xlo = (xn - xb.astype(F32)).astype(BF16)
    gate_ref[...] = _dot(xb, wg_ref[0]) + _dot(xlo, wg_ref[0]) + _dot(xb, wg_ref[1])


def _proj_in(x2d, g, wml, wg, wrw, tm):
    n = x2d.shape[0]
    const2 = lambda i: (0, 0)
    return pl.pallas_call(
        _proj_in_kernel,
        grid=(n // tm,),
        in_specs=[
            pl.BlockSpec((tm, D_MODEL), lambda i: (i, 0)),
            pl.BlockSpec((1, D_MODEL), const2),
            pl.BlockSpec((D_MODEL, ML_QKVO), const2),
            pl.BlockSpec((2, D_MODEL, LANE), lambda i: (0, 0, 0)),
            pl.BlockSpec((D_MODEL, RW_COLS), const2),
        ],
        out_specs=[
            pl.BlockSpec((tm, ML_QKVO), lambda i: (i, 0)),
            pl.BlockSpec((tm, LANE), lambda i: (i, 0)),
            pl.BlockSpec((tm, RW_COLS), lambda i: (i, 0)),
        ],
        out_shape=[
            jax.ShapeDtypeStruct((n, ML_QKVO), F32),
            jax.ShapeDtypeStruct((n, LANE), F32),
            jax.ShapeDtypeStruct((n, RW_COLS), F32),
        ],
        compiler_params=pltpu.CompilerParams(
            dimension_semantics=("arbitrary",), vmem_limit_bytes=VMEM_LIMIT),
        name="proj_in",
    )(x2d, g, wml, wg, wrw)


def _mlstm_kernel(ml_ref, gate_ref, c0_ref, n0_ref, m0_ref, conv0_ref, cw_ref, cb_ref, gb_ref, ng_ref,
                  out_ref, c1_ref, n1_ref, m1_ref, conv1_ref,
                  caug_ref, m_ref, ext_ref, *, tr):
    L = ML_CHUNK
    c = pl.program_id(1)
    nc = pl.num_programs(1)

    @pl.when(c == 0)
    def _init():
        for h in range(ML_HEADS):
            caug_ref[h, :, 0:ML_HD] = c0_ref[0, h]
            nrow = n0_ref[0, h:h + 1, :]
            caug_ref[h, :, ML_HD:2 * ML_HD] = jnp.broadcast_to(nrow, (ML_HD, ML_HD)).T
        m_ref[...] = jnp.zeros(m_ref.shape, F32)
        m_ref[:, 0:ML_HEADS] = m0_ref[0]
        ext_ref[5:8, :] = conv0_ref[0]

    ext_ref[8:8 + tr, :] = ml_ref[0, :, 0:2 * ML_WIDTH]
    acc = cb_ref[...] + ext_ref[5:5 + tr, :] * cw_ref[0:1, :]
    for j in range(1, CONV_W):
        acc = acc + ext_ref[5 + j:5 + j + tr, :] * cw_ref[j:j + 1, :]
    qk = acc * _sigmoid(acc)
    tail = ext_ref[tr + 5:tr + 8, :]
    ext_ref[5:8, :] = tail

    @pl.when(c == nc - 1)
    def _conv_out():
        conv1_ref[0] = tail

    q_all = _pad_rows(qk[:, 0:ML_WIDTH], L)
    k_all = _pad_rows(qk[:, ML_WIDTH:2 * ML_WIDTH] * (ML_HD ** -0.5), L)
    v_all = _pad_rows(ml_ref[0, :, 2 * ML_WIDTH:3 * ML_WIDTH], L)
    o_all = ml_ref[0, :, 3 * ML_WIDTH:4 * ML_WIDTH]

    g = gate_ref[0] + gb_ref[...]
    i_all = _pad_rows(g, L, NEG_BIG)
    lf_all = _pad_rows(-_softplus(-g), L, 0.0)
    row = _iota((L, L), 0)
    col = _iota((L, L), 1)
    causal = row >= col
    tri = jnp.where(causal, 1.0, 0.0).astype(BF16)
    b_col = _dot_exact_lhs(tri, lf_all, 3)
    b_t = b_col.T
    i_t = i_all.T

    ones = jnp.ones((L, ML_HD), BF16)
    for h in range(ML_HEADS):
        bc = b_col[:, ML_HEADS + h:ML_HEADS + h + 1]
        br = b_t[ML_HEADS + h:ML_HEADS + h + 1, :]
        ir = i_t[h:h + 1, :]
        m_prev = m_ref[:, h:h + 1]
        logd = jnp.where(causal, bc - br + ir, -jnp.inf)
        linter = bc + m_prev
        m_t = jnp.maximum(linter, jnp.max(logd, axis=1, keepdims=True))
        d = jnp.exp(logd - m_t)
        s_inter = jnp.exp(linter - m_t)
        sl = slice(h * ML_HD, (h + 1) * ML_HD)
        qh = q_all[:, sl].astype(BF16)
        k_t = k_all[:, sl].T
        vaug = jnp.concatenate([v_all[:, sl].astype(BF16), ones], axis=1)
        caug = caug_ref[h]
        s = _dot(qh, k_t.astype(BF16)) * d
        num = s_inter * _dot(qh, caug.astype(BF16)) + _dot(s.astype(BF16), vaug)
        den = num[:, ML_HD:2 * ML_HD]
        hh = num[:, 0:ML_HD] / jnp.maximum(jnp.abs(den), jnp.exp(-m_t))
        mu = jnp.mean(hh, axis=-1, keepdims=True)
        dv = hh - mu
        var = jnp.mean(dv * dv, axis=-1, keepdims=True)
        y = dv * lax.rsqrt(var + ML_NORM_EPS) * ng_ref[:, sl]
        out_ref[0, :, sl] = y[0:tr] * _sigmoid(o_all[:, sl])
        m_new = m_t[L - 1:L, :]
        b_last = bc[L - 1:L, :]
        s_state = jnp.exp(b_last + m_prev - m_new)
        w_row = jnp.exp(b_last - br + ir - m_new)
        caug_ref[h] = s_state * caug + _dot((k_t * w_row).astype(BF16), vaug)
        m_ref[:, h:h + 1] = m_new

    @pl.when(c == nc - 1)
    def _state_out():
        for h in range(ML_HEADS):
            caug = caug_ref[h]
            c1_ref[0, h] = caug[:, 0:ML_HD]
            n1_ref[0, h:h + 1, :] = caug[:, ML_HD:2 * ML_HD].T[0:1, :]
        m1_ref[0] = m_ref[:, 0:ML_HEADS]


def _mlstm(ml, gates, c0, n0, m0, conv0, cw, cb, gb, ng, tr):
    b, t, _ = ml.shape
    nc = t // tr
    bmap = lambda i, j: (i, 0, 0)
    const2 = lambda i, j: (0, 0)
    return pl.pallas_call(
        functools.partial(_mlstm_kernel, tr=tr),
        grid=(b, nc),
        in_specs=[
            pl.BlockSpec((1, tr, ML_QKVO), lambda i, j: (i, j, 0)),
            pl.BlockSpec((1, tr, LANE), lambda i, j: (i, j, 0)),
            pl.BlockSpec((1, ML_HEADS, ML_HD, ML_HD), lambda i, j: (i, 0, 0, 0)),
            pl.BlockSpec((1, ML_HEADS, ML_HD), bmap),
            pl.BlockSpec((1, 1, ML_HEADS), bmap),
            pl.BlockSpec((1, CONV_W - 1, 2 * ML_WIDTH), bmap),
            pl.BlockSpec((CONV_W, 2 * ML_WIDTH), const2),
            pl.BlockSpec((1, 2 * ML_WIDTH), const2),
            pl.BlockSpec((1, LANE), const2),
            pl.BlockSpec((1, ML_WIDTH), const2),
        ],
        out_specs=[
            pl.BlockSpec((1, tr, ML_WIDTH), lambda i, j: (i, j, 0)),
            pl.BlockSpec((1, ML_HEADS, ML_HD, ML_HD), lambda i, j: (i, 0, 0, 0)),
            pl.BlockSpec((1, ML_HEADS, ML_HD), bmap),
            pl.BlockSpec((1, 1, ML_HEADS), bmap),
            pl.BlockSpec((1, CONV_W - 1, 2 * ML_WIDTH), bmap),
        ],
        out_shape=[
            jax.ShapeDtypeStruct((b, t, ML_WIDTH), F32),
            jax.ShapeDtypeStruct((b, ML_HEADS, ML_HD, ML_HD), F32),
            jax.ShapeDtypeStruct((b, ML_HEADS, ML_HD), F32),
            jax.ShapeDtypeStruct((b, 1, ML_HEADS), F32),
            jax.ShapeDtypeStruct((b, CONV_W - 1, 2 * ML_WIDTH), F32),
        ],
        scratch_shapes=[
            pltpu.VMEM((ML_HEADS, ML_HD, 2 * ML_HD), F32),
            pltpu.VMEM((1, LANE), F32),
            pltpu.VMEM((tr + 8, 2 * ML_WIDTH), F32),
        ],
        compiler_params=pltpu.CompilerParams(
            dimension_semantics=("arbitrary", "arbitrary"), vmem_limit_bytes=VMEM_LIMIT),
        name="mlstm",
    )(ml, gates, c0, n0, m0, conv0, cw, cb, gb, ng)


def _rwkv_kernel(c_ref, sh_ref, s0_ref, mu_ref, w0_ref, wwa_ref, a0_ref, g2_ref, kk_ref, ka_ref, rk_ref,
                 gng_ref, gnb_ref, out_ref, s1_ref, s_ref, last_ref, *, tr, nb, L):
    W = RW_WIDTH
    GW = RW_GROUP * RW_HD
    GL = RW_GROUP * L
    NGRP = RW_HEADS // RW_GROUP
    ci = pl.program_id(1)
    nc = pl.num_programs(1)

    r2 = _iota((GL, GW), 0)
    c2 = _iota((GL, GW), 1)
    bd = (r2 // L) == (c2 // RW_HD)
    rg = _iota((GL, GL), 0)
    cg = _iota((GL, GL), 1)
    s_lower = rg > cg
    i_lower = rg >= cg

    @pl.when(ci == 0)
    def _init():
        sbd = (_iota((GW, GW), 0) // RW_HD) == (_iota((GW, GW), 1) // RW_HD)
        for bi in range(nb):
            last_ref[bi, 0:1, :] = sh_ref[bi]
            for gi in range(NGRP):
                x = s0_ref[bi, gi * GW:(gi + 1) * GW, :]
                s_ref[bi * NGRP + gi] = jnp.where(sbd, jnp.concatenate([x] * RW_GROUP, axis=1), 0.0)

    c = c_ref[...].reshape(nb * tr, RW_COLS)
    prev = pltpu.roll(c, 1, axis=0)
    rowid = _iota(c.shape, 0)
    for bi in range(nb):
        prev = jnp.where(rowid == bi * tr, last_ref[bi, 0:1, :], prev)
        last_ref[bi, 0:1, :] = c[(bi + 1) * tr - 1:(bi + 1) * tr, :]
    xs = c + (prev - c) * mu_ref[...]
    r = xs[:, 0:W]
    k = xs[:, W:2 * W]
    v = xs[:, 2 * W:3 * W]
    slab = xs[:, 3 * W:3 * W + DECAY_LORA + AAA_LORA]
    gd = xs[:, 3 * W + DECAY_LORA + AAA_LORA:]
    lane = _iota(slab.shape, 1)
    t_in = jnp.where(lane < DECAY_LORA, jnp.tanh(slab), slab)
    la = _dot(t_in.astype(BF16), wwa_ref[...])
    w_log = -_softplus(-(w0_ref[...] + la[:, 0:W])) - 0.5
    lw_all = -jnp.exp(w_log)
    a_all = _sigmoid(a0_ref[...] + la[:, W:2 * W])
    g = _dot(_sigmoid(gd).astype(BF16), g2_ref[...])

    rs = _iota((W, W), 0)
    cs = _iota((W, W), 1)
    seg = jnp.where((rs // RW_HD) == (cs // RW_HD), 1.0, 0.0).astype(BF16)

    kk = k * kk_ref[...]
    kn_all = k * (1.0 + (a_all - 1.0) * ka_ref[...])
    ss = _dot_exact_rhs(kk * kk, seg, 2)
    kap_all = kk / jnp.maximum(jnp.sqrt(ss), 1e-12)
    bonus = _dot_exact_rhs(r * kn_all * rk_ref[...], seg, 2) * v

    rl = _iota((L, L), 0)
    cl_ = _iota((L, L), 1)
    tri = jnp.where(rl >= cl_, 1.0, 0.0).astype(BF16)

    def stack(x):
        return jnp.where(bd, jnp.concatenate([x] * RW_GROUP, axis=0), 0.0).astype(BF16)

    seqs = []
    for bi in range(nb):
        rows = slice(bi * tr, (bi + 1) * tr)
        lw = _pad_rows(lw_all[rows], L)
        kap = _pad_rows(kap_all[rows], L)
        kn_p = _pad_rows(kn_all[rows], L)
        a_p = _pad_rows(a_all[rows], L)
        v_p = _pad_rows(v[rows], L)
        r_p = _pad_rows(r[rows], L)
        cum = _dot_exact_lhs(tri, lw, 3)
        cum_last = cum[L - 1:L, :]
        e_neg = jnp.exp(-cum)
        e_rem = jnp.exp(cum_last - cum)
        seqs.append(dict(
            abar=-kap * jnp.exp(cum - lw), btil=kap * a_p * e_neg, ktil=kn_p * e_neg, rbar=r_p * jnp.exp(cum),
            bhat=kap * a_p * e_rem, khat=kn_p * e_rem, v=v_p, g_last=jnp.exp(cum_last)))

    chains = [(bi, gi) for bi in range(nb) for gi in range(NGRP)]
    ch = []
    for bi, gi in chains:
        q = seqs[bi]
        ls = slice(gi * GW, (gi + 1) * GW)
        a_s, b_s, k_s, r_s, v_s = (stack(q[n][:, ls]) for n in ("abar", "btil", "ktil", "rbar", "v"))
        ch.append(dict(
            ar=jnp.concatenate([a_s, r_s], axis=0),
            bk=jnp.concatenate([b_s, k_s], axis=0),
            bkh=jnp.concatenate([stack(q["bhat"][:, ls]), stack(q["khat"][:, ls])], axis=0),
            v_s=v_s, g_last=q["g_last"][:, ls], s=s_ref[bi * NGRP + gi]))
    for c_ in ch:
        p = _dot_nt(c_["ar"], c_["bk"])
        c_["npow"] = jnp.where(s_lower, p[0:GL, 0:GL], 0.0)
        c_["a_ak"] = jnp.where(s_lower, p[0:GL, GL:2 * GL], 0.0).astype(BF16)
        c_["r_b"] = jnp.where(i_lower, p[GL:2 * GL, 0:GL], 0.0).astype(BF16)
        c_["r_k"] = jnp.where(i_lower, p[GL:2 * GL, GL:2 * GL], 0.0).astype(BF16)
    for c_ in ch:
        c_["q0"] = _dot_nt(c_["ar"], c_["s"].astype(BF16))
    for c_ in ch:
        c_["u"] = c_["q0"][0:GL] + _dot(c_["a_ak"], c_["v_s"])
    lvls = L.bit_length() - 1
    for lvl in range(lvls):
        for c_ in ch:
            nb16 = c_["npow"].astype(BF16)
            c_["u"] = c_["u"] + _dot(nb16, c_["u"].astype(BF16))
            if lvl < lvls - 1:
                c_["npow"] = _dot(nb16, nb16)
    for c_ in ch:
        y = (c_["q0"][GL:2 * GL] + _dot(c_["r_b"], c_["u"].astype(BF16)) + _dot(c_["r_k"], c_["v_s"]))
        yg = y[0:L]
        for j in range(1, RW_GROUP):
            yg = yg + y[j * L:(j + 1) * L]
        c_["yg"] = yg
    for (bi, gi), c_ in zip(chains, ch):
        uv_t = jnp.concatenate([c_["u"], c_["v_s"].astype(F32)], axis=0).T.astype(BF16)
        s_ref[bi * NGRP + gi] = c_["s"] * c_["g_last"] + _dot(uv_t, c_["bkh"])
    y_rows = [jnp.concatenate([ch[bi * NGRP + gi]["yg"] for gi in range(NGRP)], axis=1)[0:tr]
              for bi in range(nb)]

    y_all = jnp.concatenate(y_rows, axis=0) if nb > 1 else y_rows[0]
    inv = 1.0 / RW_HD
    mu_ = _dot_exact_rhs(y_all, seg, 2) * inv
    dy = y_all - mu_
    var = _dot_exact_rhs(dy * dy, seg, 2) * inv
    yn = dy * lax.rsqrt(var + GN_EPS) * gng_ref[...] + gnb_ref[...]
    out_ref[...] = ((yn + bonus) * g).reshape(nb, tr, W)

    @pl.when(ci == nc - 1)
    def _state_out():
        for bi in range(nb):
            for gi in range(NGRP):
                s = s_ref[bi * NGRP + gi]
                f = s[:, 0:RW_HD]
                for j in range(1, RW_GROUP):
                    f = f + s[:, j * RW_HD:(j + 1) * RW_HD]
                s1_ref[bi, gi * GW:(gi + 1) * GW, :] = f


def _rwkv(cols, shift0, s0, mu, w0, wwa, a0, g2, kk, ka, rk, gng, gnb, tr, nb, chunk):
    b, t, _ = cols.shape
    nc = t // tr
    assert b % nb == 0 and t % tr == 0 and tr <= chunk
    const2 = lambda i, j: (0, 0)
    vec = pl.BlockSpec((1, RW_WIDTH), const2)
    return pl.pallas_call(
        functools.partial(_rwkv_kernel, tr=tr, nb=nb, L=chunk),
        grid=(b // nb, nc),
        in_specs=[
            pl.BlockSpec((nb, tr, RW_COLS), lambda i, j: (i, j, 0)),
            pl.BlockSpec((nb, 1, RW_COLS), lambda i, j: (i, 0, 0)),
            pl.BlockSpec((nb, RW_WIDTH, RW_HD), lambda i, j: (i, 0, 0)),
            pl.BlockSpec((1, RW_COLS), const2),
            vec,
            pl.BlockSpec((DECAY_LORA + AAA_LORA, 2 * RW_WIDTH), const2),
            vec,
            pl.BlockSpec((GATE_LORA, RW_WIDTH), const2),
            vec, vec, vec, vec, vec,
        ],
        out_specs=[
            pl.BlockSpec((nb, tr, RW_WIDTH), lambda i, j: (i, j, 0)),
            pl.BlockSpec((nb, RW_WIDTH, RW_HD), lambda i, j: (i, 0, 0)),
        ],
        out_shape=[
            jax.ShapeDtypeStruct((b, t, RW_WIDTH), F32),
            jax.ShapeDtypeStruct((b, RW_WIDTH, RW_HD), F32),
        ],
        scratch_shapes=[pltpu.VMEM((nb * (RW_HEADS // RW_GROUP), RW_GROUP * RW_HD, RW_GROUP * RW_HD), F32),
                        pltpu.VMEM((nb, 8, RW_COLS), F32)],
        compiler_params=pltpu.CompilerParams(
            dimension_semantics=("arbitrary", "arbitrary"), vmem_limit_bytes=VMEM_LIMIT),
        name="rwkv",
    )(cols, shift0, s0, mu, w0, wwa, a0, g2, kk, ka, rk, gng, gnb)


def _extract_top(works, idx, n):
    ranks = [jnp.full(w.shape, 99.0, F32) for w in works]
    vals = [[] for _ in works]
    for j in range(n):
        ms = [jnp.max(w, axis=0, keepdims=True) for w in works]
        if idx is None:
            sels = [w == m for w, m in zip(works, ms)]
        else:
            firsts = [jnp.min(jnp.where(w == m, idx, 1e9), axis=0, keepdims=True) for w, m in zip(works, ms)]
            sels = [idx == f for f in firsts]
        ranks = [jnp.where(sel, float(j), r) for sel, r in zip(sels, ranks)]
        works = [jnp.where(sel, -jnp.inf, w) for sel, w in zip(sels, works)]
        for v, m in zip(vals, ms):
            v.append(m)
    return vals, ranks


def _miscount(rank, n):
    taken = jnp.sum(jnp.where(rank < float(n), 1.0, 0.0), axis=0, keepdims=True)
    return jnp.abs(taken - float(n))


def _rows_bf16(row, rows):
    packed = jnp.broadcast_to(row, (16, row.shape[1])).astype(BF16)
    return jnp.concatenate([packed] * (rows // 16), axis=0)


def _peer_kernel(x_ref, ml_ref, rw_ref, woml_ref, worw_ref, gffn_ref, gfin_ref, wqt_ref, keys_ref,
                 ua_ref, ub_ref, vta_ref, vtb_ref, y_ref,
                 ht_ref, qt_ref, rank_ref, e_ref, vals_ref, r2b_ref, e2b_ref, cnt_ref, gate_ref,
                 p_ref, acc_ref, *, tm, ec):
    e = pl.program_id(1)
    ne = pl.num_programs(1)
    ng = tm // LANE
    K = PEER_TOPK
    nslab = ec // N_KEYS
    act_ref = qt_ref

    @pl.when(e == 0)
    def _select():
        x1 = (x_ref[...] + _dot(ml_ref[...].astype(BF16), woml_ref[...])
              + _dot(rw_ref[...].astype(BF16), worw_ref[...]))
        y_ref[...] = x1
        h = x1 * lax.rsqrt(jnp.mean(x1 * x1, axis=-1, keepdims=True) + NORM_EPS) * gffn_ref[...]
        ht = h.T.astype(BF16)
        ht_ref[...] = ht
        qt_ref[...] = _dot(wqt_ref[...], ht)
        acc_ref[...] = jnp.zeros(acc_ref.shape, F32)

        def score_body(hp, carry):
            q = qt_ref[pl.ds(pl.multiple_of(hp * PEER_HALF, PEER_HALF), PEER_HALF), :]
            e_ref[hp] = _dot(keys_ref[hp], q.astype(BF16))
            return carry

        lax.fori_loop(0, 2 * PEER_HEADS, score_body, 0)

        key_idx = _iota((N_KEYS, LANE), 0).astype(F32)
        groups = [slice(gi * LANE, (gi + 1) * LANE) for gi in range(ng)]

        def topk_groups(hp, exact_ties):
            scores = [e_ref[hp, :, lanes] for lanes in groups]
            vals, ranks = _extract_top(scores, key_idx if exact_ties else None, K)
            bad = None
            for lanes, v, rank in zip(groups, vals, ranks):
                rank_ref[hp, :, lanes] = rank
                vals_ref[hp, :, lanes] = jnp.concatenate(v, axis=0)
                miss = _miscount(rank, K)
                bad = miss if bad is None else jnp.maximum(bad, miss)
            return bad

        def topk_body(hp, carry):
            bad = topk_groups(hp, False)

            @pl.when(jnp.max(bad) > 0.0)
            def _redo():
                topk_groups(hp, True)

            return carry

        lax.fori_loop(0, 2 * PEER_HEADS, topk_body, 0)

        sub8 = _iota((8, LANE), 0)

        def cand_groups(h, exact_ties):
            works, v1s = [], []
            slab_a = [a_i for a_i in range(K) for _ in range(0, K // (a_i + 1), 8)]
            for lanes in groups:
                v1 = vals_ref[2 * h, :, lanes]
                v2 = vals_ref[2 * h + 1, :, lanes]
                slabs = []
                for a_i in range(K):
                    nb = K // (a_i + 1)
                    for b0 in range(0, nb, 8):
                        rows = v1[a_i:a_i + 1, :] + v2[b0:b0 + 8, :]
                        slabs.append(jnp.where(sub8 + b0 < nb, rows, -jnp.inf))
                works.append(jnp.concatenate(slabs, axis=0))
                v1s.append(v1)
            idx = None
            if exact_ties:
                idx = jnp.concatenate([(sub8 + (b0 + a_i * K)).astype(F32) for a_i in range(K)
                                       for b0 in range(0, K // (a_i + 1), 8)], axis=0)
            vals, ranks = _extract_top(works, idx, K)
            bad = None
            for lanes, v1, v, rank in zip(groups, v1s, vals, ranks):
                z = jnp.ones_like(v[0])
                for j in range(1, K):
                    z = z + jnp.exp(v[j] - v[0])
                picked = jnp.where(rank < float(K), 1.0, 0.0)
                r1 = rank_ref[2 * h, :, lanes]
                cnt = jnp.zeros((N_KEYS, LANE), F32)
                for a_i in range(K):
                    ca = None
                    for si, sa in enumerate(slab_a):
                        if sa == a_i:
                            part = jnp.sum(picked[si * 8:(si + 1) * 8], axis=0, keepdims=True)
                            ca = part if ca is None else ca + part
                    cnt = jnp.where(r1 == float(a_i), ca, cnt)
                cnt_ref[h, :, lanes] = cnt
                e1 = jnp.exp(e_ref[2 * h, :, lanes] - v1[0:1, :])
                gate_ref[h, :, lanes] = jnp.where(r1 < float(K), e1 / z, 0.0)
                miss = _miscount(rank, K)
                bad = miss if bad is None else jnp.maximum(bad, miss)
            return bad

        def cand_body(h, carry):
            bad = cand_groups(h, False)

            @pl.when(jnp.max(bad) > 0.0)
            def _redo():
                cand_groups(h, True)

            r2b_ref[h] = rank_ref[2 * h + 1].astype(BF16)
            e2b_ref[h] = jnp.exp(e_ref[2 * h + 1] - vals_ref[2 * h + 1, 0:1, :]).astype(BF16)
            return carry

        lax.fori_loop(0, PEER_HEADS, cand_body, 0)

    key1 = pl.ds(pl.multiple_of(e * nslab, nslab), nslab)
    hs = nslab // 2
    hrows = hs * N_KEYS
    for half in range(2):
        act_ref[half * hrows:(half + 1) * hrows, :] = _dot((ua_ref, ub_ref)[half][...], ht_ref[...])
    def gate_half(half):
        for gi in range(ng):
            lanes = slice(gi * LANE, (gi + 1) * LANE)
            w = [jnp.zeros((N_KEYS, LANE), BF16) for _ in range(hs)]
            for h in range(PEER_HEADS):
                r2 = r2b_ref[h, :, lanes]
                e2 = e2b_ref[h, :, lanes]
                cw = cnt_ref[h, key1, lanes]
                gw = gate_ref[h, key1, lanes]
                for jj in range(hs):
                    j = half * hs + jj
                    hit = r2 < _rows_bf16(cw[j:j + 1, :], N_KEYS)
                    w[jj] = w[jj] + jnp.where(hit, e2, jnp.zeros_like(e2)) * _rows_bf16(gw[j:j + 1, :], N_KEYS)
            for jj in range(hs):
                rows = slice((half * hs + jj) * N_KEYS, (half * hs + jj + 1) * N_KEYS)
                act = act_ref[rows, lanes]
                gelu = 0.5 * act * (1.0 + lax.erf(act * 0.7071067811865476))
                p_ref[rows, lanes] = w[jj] * gelu.astype(BF16)

    for half in range(2):
        gate_half(half)
        acc_ref[...] += _dot((vta_ref, vtb_ref)[half][0], p_ref[half * hrows:(half + 1) * hrows, :])

    @pl.when(e == ne - 1)
    def _finish():
        x2 = y_ref[...] + acc_ref[...].T
        y_ref[...] = x2 * lax.rsqrt(jnp.mean(x2 * x2, axis=-1, keepdims=True) + NORM_EPS) * gfin_ref[...]


def _peer(x2d, ml2d, rw2d, woml, worw, gffn, gfin, wqt, keys, u, vt, tm, ec):
    n = x2d.shape[0]
    n_exp = u.shape[0]
    assert ec == 8 * N_KEYS and tm % LANE == 0 and n % tm == 0 and n_exp % ec == 0
    tok = lambda i, e: (i, 0)
    const2 = lambda i, e: (0, 0)
    qrows = 2 * PEER_HEADS * PEER_HALF
    eh = ec // 2
    once = pl.Buffered(1)
    return pl.pallas_call(
        functools.partial(_peer_kernel, tm=tm, ec=ec),
        grid=(n // tm, n_exp // ec),
        in_specs=[
            pl.BlockSpec((tm, D_MODEL), tok),
            pl.BlockSpec((tm, ML_WIDTH), tok),
            pl.BlockSpec((tm, RW_WIDTH), tok),
            pl.BlockSpec((ML_WIDTH, D_MODEL), const2, pipeline_mode=once),
            pl.BlockSpec((RW_WIDTH, D_MODEL), const2, pipeline_mode=once),
            pl.BlockSpec((1, D_MODEL), const2),
            pl.BlockSpec((1, D_MODEL), const2),
            pl.BlockSpec((qrows, D_MODEL), const2, pipeline_mode=once),
            pl.BlockSpec((2 * PEER_HEADS, N_KEYS, PEER_HALF), lambda i, e: (0, 0, 0), pipeline_mode=once),
            pl.BlockSpec((eh, D_MODEL), lambda i, e: (2 * e, 0)),
            pl.BlockSpec((eh, D_MODEL), lambda i, e: (2 * e + 1, 0)),
            pl.BlockSpec((1, D_MODEL, eh), lambda i, e: (2 * e, 0, 0)),
            pl.BlockSpec((1, D_MODEL, eh), lambda i, e: (2 * e + 1, 0, 0)),
        ],
        out_specs=pl.BlockSpec((tm, D_MODEL), tok),
        out_shape=jax.ShapeDtypeStruct((n, D_MODEL), F32),
        scratch_shapes=[
            pltpu.VMEM((D_MODEL, tm), BF16),
            pltpu.VMEM((qrows, tm), F32),
            pltpu.VMEM((2 * PEER_HEADS, N_KEYS, tm), F32),
            pltpu.VMEM((2 * PEER_HEADS, N_KEYS, tm), F32),
            pltpu.VMEM((2 * PEER_HEADS, PEER_TOPK, tm), F32),
            pltpu.VMEM((PEER_HEADS, N_KEYS, tm), BF16),
            pltpu.VMEM((PEER_HEADS, N_KEYS, tm), BF16),
            pltpu.VMEM((PEER_HEADS, N_KEYS, tm), F32),
            pltpu.VMEM((PEER_HEADS, N_KEYS, tm), F32),
            pltpu.VMEM((ec, tm), BF16),
            pltpu.VMEM((D_MODEL, tm), F32),
        ],
        compiler_params=pltpu.CompilerParams(
            dimension_semantics=("arbitrary", "arbitrary"), vmem_limit_bytes=VMEM_LIMIT),
        name="peer",
    )(x2d, ml2d, rw2d, woml, worw, gffn, gfin, wqt, keys, u, u, vt, vt)


def _prep_weights(norm_mix_g, w_in, ml_conv_w, ml_conv_b, ml_b_i, ml_b_f, ml_norm_g,
                  rw_mu, rw_w0, rw_w2, rw_a0, rw_a2, rw_g2, rw_k_k, rw_k_a, rw_r_k, rw_gn_g, rw_gn_b,
                  w_out, norm_ffn_g, peer_w_q, peer_sub_keys, peer_u, peer_v, norm_final_g):
    assert w_in.shape[0] == 1, "one layer"
    w = w_in[0]
    wg = jnp.pad(w[:, ML_QKVO:ML_QKVO + 2 * ML_HEADS], ((0, 0), (0, LANE - 2 * ML_HEADS)))
    wg_hi = wg.astype(BF16)
    wg_lo = (wg - wg_hi.astype(F32)).astype(BF16)
    zeros = jnp.zeros((DECAY_LORA, RW_WIDTH), F32)
    wwa = jnp.concatenate([jnp.concatenate([rw_w2[0], zeros], axis=1),
                           jnp.concatenate([zeros, rw_a2[0]], axis=1)], axis=0)
    gate_bias = jnp.pad(jnp.concatenate([ml_b_i[0], ml_b_f[0]]), (0, LANE - 2 * ML_HEADS))[None, :]
    row = lambda a: a.reshape(1, -1)
    return dict(
        g_mix=row(norm_mix_g[0]),
        wml=w[:, 0:ML_QKVO].astype(BF16),
        wg=jnp.stack([wg_hi, wg_lo]),
        wrw=w[:, ML_QKVO + 2 * ML_HEADS:].astype(BF16),
        conv_w=ml_conv_w[0], conv_b=row(ml_conv_b[0]), gate_bias=gate_bias, ml_norm_g=row(ml_norm_g[0]),
        mu=row(rw_mu[0]), w0=row(rw_w0[0]), wwa=wwa.astype(BF16), a0=row(rw_a0[0]),
        g2=rw_g2[0].astype(BF16), k_k=row(rw_k_k[0]), k_a=row(rw_k_a[0]), r_k=row(rw_r_k[0]),
        gn_g=row(rw_gn_g[0]), gn_b=row(rw_gn_b[0]),
        wo_ml=w_out[0, 0:ML_WIDTH].astype(BF16), wo_rw=w_out[0, ML_WIDTH:].astype(BF16),
        g_ffn=row(norm_ffn_g[0]), g_fin=row(norm_final_g),
        wqt=peer_w_q[0].T.astype(BF16),
        keys=peer_sub_keys[0].reshape(2 * PEER_HEADS, N_KEYS, PEER_HALF).astype(BF16),
        u=peer_u[0].astype(BF16),
        vt=peer_v[0].reshape(-1, PEER_EXPERT_CHUNK // 2, D_MODEL).transpose(0, 2, 1).astype(BF16),
    )


def _rwkv_tiling(t):
    tr = min(t, RW_CHUNK)
    chunk = max(tr, RW_MIN_CHUNK)
    return tr, chunk, RW_STEP_ROWS // chunk // 2 if chunk < RW_CHUNK else RW_STEP_ROWS // chunk


def _trunk(x, states, wp, tr_ml, tr_rw, rw_chunk, rw_batch, tm_in, tm_peer, ec):
    b, t, d = x.shape
    c0, n0, m0, conv0, s0, shift0 = (s[0] for s in states)
    x2d = x.reshape(b * t, d)
    ml, gates, rw = _proj_in(x2d, wp["g_mix"], wp["wml"], wp["wg"], wp["wrw"], tm_in)
    ml3 = ml.reshape(b, t, ML_QKVO)
    rw3 = rw.reshape(b, t, RW_COLS)
    ml_out, c1, n1, m1, conv1 = _mlstm(
        ml3, gates.reshape(b, t, LANE), c0, n0, m0.reshape(b, 1, ML_HEADS), conv0,
        wp["conv_w"], wp["conv_b"], wp["gate_bias"], wp["ml_norm_g"], tr_ml)
    rw_out, s1 = _rwkv(
        rw3, shift0[:, None, :], s0.reshape(b, RW_WIDTH, RW_HD), wp["mu"], wp["w0"], wp["wwa"], wp["a0"], wp["g2"],
        wp["k_k"], wp["k_a"], wp["r_k"], wp["gn_g"], wp["gn_b"], tr_rw, min(b, rw_batch), rw_chunk)
    y = _peer(x2d, ml_out.reshape(b * t, ML_WIDTH), rw_out.reshape(b * t, RW_WIDTH),
              wp["wo_ml"], wp["wo_rw"], wp["g_ffn"], wp["g_fin"], wp["wqt"], wp["keys"],
              wp["u"], wp["vt"], tm_peer, ec)
    new_states = (c1[None], n1[None], m1.reshape(b, ML_HEADS)[None], conv1[None],
                  s1.reshape(b, RW_HEADS, RW_HD, RW_HD)[None], rw3[:, -1][None])
    return y.reshape(b, t, d), new_states


def kernel(x_prompt, x_sample, state_mlstm_C, state_mlstm_n, state_mlstm_m, state_mlstm_conv, state_rwkv_S, state_rwkv_shift, norm_mix_g, w_in, ml_conv_w, ml_conv_b, ml_b_i, ml_b_f, ml_norm_g, rw_mu, rw_w0, rw_w2, rw_a0, rw_a2, rw_g2, rw_k_k, rw_k_a, rw_r_k, rw_gn_g, rw_gn_b, w_out, norm_ffn_g, peer_w_q, peer_sub_keys, peer_u, peer_v, norm_final_g):
    wp = _prep_weights(norm_mix_g, w_in, ml_conv_w, ml_conv_b, ml_b_i, ml_b_f, ml_norm_g,
                       rw_mu, rw_w0, rw_w2, rw_a0, rw_a2, rw_g2, rw_k_k, rw_k_a, rw_r_k, rw_gn_g, rw_gn_b,
                       w_out, norm_ffn_g, peer_w_q, peer_sub_keys, peer_u, peer_v, norm_final_g)
    bp = x_prompt.shape[0]
    z = lambda *s: jnp.zeros((1, bp) + s, F32)
    prompt_states = (z(ML_HEADS, ML_HD, ML_HD), z(ML_HEADS, ML_HD), jnp.full((1, bp, ML_HEADS), M_INIT, F32),
                     z(CONV_W - 1, 2 * ML_WIDTH), z(RW_HEADS, RW_HD, RW_HD), z(RW_COLS))
    sample_states = (state_mlstm_C, state_mlstm_n, state_mlstm_m, state_mlstm_conv,
                     state_rwkv_S, state_rwkv_shift)
    tp, ts = x_prompt.shape[1], x_sample.shape[1]
    np_, ns = bp * tp, x_sample.shape[0] * ts
    y_p, st_p = _trunk(x_prompt, prompt_states, wp, min(tp, ML_CHUNK), *_rwkv_tiling(tp),
                       min(np_, PROJ_TILE), min(np_, PEER_TILE), PEER_EXPERT_CHUNK)
    y_s, st_s = _trunk(x_sample, sample_states, wp, min(ts, ML_CHUNK), *_rwkv_tiling(ts),
                       min(ns, PROJ_TILE), min(ns, PEER_TILE), PEER_EXPERT_CHUNK)
    return (y_p, y_s) + tuple(st_p) + tuple(st_s)
```

```python
import functools

import jax
import jax.numpy as jnp
from jax import lax
from jax.experimental import pallas as pl
from jax.experimental.pallas import tpu as pltpu

F32 = jnp.float32
BF16 = jnp.bfloat16

D_MODEL = 1024
ML_HEADS = 4
ML_HD = 128
ML_WIDTH = ML_HEADS * ML_HD
CONV_W = 4
RW_HEADS = 8
RW_HD = 64
RW_WIDTH = RW_HEADS * RW_HD
DECAY_LORA = 64
AAA_LORA = 64
GATE_LORA = 128
RW_COLS = 3 * RW_WIDTH + DECAY_LORA + AAA_LORA + GATE_LORA
ML_QKVO = 4 * ML_WIDTH
PEER_HEADS = 8
N_KEYS = 128
PEER_TOPK = 16
PEER_HALF = 128
NORM_EPS = 1e-6
ML_NORM_EPS = 1e-6
GN_EPS = RW_HD * 1e-5
M_INIT = -1e30
NEG_BIG = -1e30

LANE = 128
ML_CHUNK = 128
ML_MIN_CHUNK = 16
ML_STEP_ROWS = 128
RW_CHUNK = 64
RW_GROUP = 4
RW_MIN_CHUNK = 16
RW_STEP_ROWS = 256
PROJ_TILE = 512
PEER_TILE = 512
PEER_EXPERT_CHUNK = 1024
VMEM_LIMIT = 56 * 1024 * 1024


def _dot(a, b):
    return jnp.dot(a, b, preferred_element_type=F32)


def _dot_nt(a, b):
    return lax.dot_general(a, b, (((1,), (1,)), ((), ())), preferred_element_type=F32)


def _split_bf16(x, n):
    parts = []
    r = x
    for _ in range(n):
        p = r.astype(BF16)
        parts.append(p)
        r = r - p.astype(F32)
    return parts


def _dot_exact_lhs(mask_bf16, x, n):
    return sum(_dot(mask_bf16, p) for p in _split_bf16(x, n))


def _dot_exact_rhs(x, mask_bf16, n):
    return sum(_dot(p, mask_bf16) for p in _split_bf16(x, n))


def _sigmoid(x):
    return 1.0 / (1.0 + jnp.exp(-x))


def _softplus(x):
    return jnp.maximum(x, 0.0) + jnp.log1p(jnp.exp(-jnp.abs(x)))


def _iota(shape, dim):
    return lax.broadcasted_iota(jnp.int32, shape, dim)


def _pad_rows(x, rows, value=0.0):
    if x.shape[0] == rows:
        return x
    return jnp.concatenate([x, jnp.full((rows - x.shape[0], x.shape[1]), value, x.dtype)], axis=0)


def _proj_in_kernel(x_ref, g_ref, wml_ref, wg_ref, wrw_ref, ml_ref, gate_ref, rw_ref):
    x = x_ref[...]
    xn = x * lax.rsqrt(jnp.mean(x * x, axis=-1, keepdims=True) + NORM_EPS) * g_ref[...]
    xb = xn.astype(BF16)
    ml_ref[...] = _dot(xb, wml_ref[...])
    rw_ref[...] = _dot(xb, wrw_ref[...])
    xlo = (xn - xb.astype(F32)).astype(BF16)
    gate_ref[...] = _dot(xb, wg_ref[0]) + _dot(xlo, wg_ref[0]) + _dot(xb, wg_ref[1])


def _proj_in(x2d, g, wml, wg, wrw, tm):
    n = x2d.shape[0]
    const2 = lambda i: (0, 0)
    return pl.pallas_call(
        _proj_in_kernel,
        grid=(n // tm,),
        in_specs=[
            pl.BlockSpec((tm, D_MODEL), lambda i: (i, 0)),
            pl.BlockSpec((1, D_MODEL), const2),
            pl.BlockSpec((D_MODEL, ML_QKVO), const2),
            pl.BlockSpec((2, D_MODEL, LANE), lambda i: (0, 0, 0)),
            pl.BlockSpec((D_MODEL, RW_COLS), const2),
        ],
        out_specs=[
            pl.BlockSpec((tm, ML_QKVO), lambda i: (i, 0)),
            pl.BlockSpec((tm, LANE), lambda i: (i, 0)),
            pl.BlockSpec((tm, RW_COLS), lambda i: (i, 0)),
        ],
        out_shape=[
            jax.ShapeDtypeStruct((n, ML_QKVO), F32),
            jax.ShapeDtypeStruct((n, LANE), F32),
            jax.ShapeDtypeStruct((n, RW_COLS), F32),
        ],
        compiler_params=pltpu.CompilerParams(
            dimension_semantics=("arbitrary",), vmem_limit_bytes=VMEM_LIMIT),
        name="proj_in",
    )(x2d, g, wml, wg, wrw)


def _mlstm_kernel(ml_ref, gate_ref, c0_ref, n0_ref, m0_ref, conv0_ref, cw_ref, cb_ref, gb_ref, ng_ref,
                  out_ref, c1_ref, n1_ref, m1_ref, conv1_ref,
                  caug_ref, m_ref, ext_ref, *, tr, nb, L):
    c = pl.program_id(1)
    nc = pl.num_programs(1)

    @pl.when(c == 0)
    def _init():
        m_ref[...] = jnp.zeros(m_ref.shape, F32)
        for bi in range(nb):
            for h in range(ML_HEADS):
                caug_ref[bi * ML_HEADS + h, :, 0:ML_HD] = c0_ref[bi, h]
                nrow = n0_ref[bi, h:h + 1, :]
                caug_ref[bi * ML_HEADS + h, :, ML_HD:2 * ML_HD] = jnp.broadcast_to(nrow, (ML_HD, ML_HD)).T
            m_ref[bi:bi + 1, 0:ML_HEADS] = m0_ref[bi]
            ext_ref[bi, 5:8, :] = conv0_ref[bi]

    row = _iota((L, L), 0)
    col = _iota((L, L), 1)
    causal = row >= col
    tri = jnp.where(causal, 1.0, 0.0).astype(BF16)
    ones = jnp.ones((L, ML_HD), BF16)

    seqs = []
    for bi in range(nb):
        ext_ref[bi, 8:8 + tr, :] = ml_ref[bi, :, 0:2 * ML_WIDTH]
        acc = cb_ref[...] + ext_ref[bi, 5:5 + tr, :] * cw_ref[0:1, :]
        for j in range(1, CONV_W):
            acc = acc + ext_ref[bi, 5 + j:5 + j + tr, :] * cw_ref[j:j + 1, :]
        qk = acc * _sigmoid(acc)
        tail = ext_ref[bi, tr + 5:tr + 8, :]
        ext_ref[bi, 5:8, :] = tail

        @pl.when(c == nc - 1)
        def _conv_out():
            conv1_ref[bi] = tail

        g = gate_ref[bi] + gb_ref[...]
        i_all = _pad_rows(g, L, NEG_BIG)
        b_col = _dot_exact_lhs(tri, _pad_rows(-_softplus(-g), L, 0.0), 3)
        seqs.append(dict(
            q=_pad_rows(qk[:, 0:ML_WIDTH], L),
            k=_pad_rows(qk[:, ML_WIDTH:2 * ML_WIDTH] * (ML_HD ** -0.5), L),
            v=_pad_rows(ml_ref[bi, :, 2 * ML_WIDTH:3 * ML_WIDTH], L),
            b_col=b_col, b_t=b_col.T, i_t=i_all.T))

    chains = [(bi, h) for bi in range(nb) for h in range(ML_HEADS)]
    ch = []
    for bi, h in chains:
        q = seqs[bi]
        sl = slice(h * ML_HD, (h + 1) * ML_HD)
        bc = q["b_col"][:, ML_HEADS + h:ML_HEADS + h + 1]
        br = q["b_t"][ML_HEADS + h:ML_HEADS + h + 1, :]
        ir = q["i_t"][h:h + 1, :]
        m_prev = m_ref[bi:bi + 1, h:h + 1]
        logd = jnp.where(causal, bc - br + ir, -jnp.inf)
        linter = bc + m_prev
        m_t = jnp.maximum(linter, jnp.max(logd, axis=1, keepdims=True))
        m_new = m_t[L - 1:L, :]
        b_last = bc[L - 1:L, :]
        ch.append(dict(
            sl=sl, m_t=m_t, d=jnp.exp(logd - m_t), s_inter=jnp.exp(linter - m_t), m_new=m_new,
            s_state=jnp.exp(b_last + m_prev - m_new), w_row=jnp.exp(b_last - br + ir - m_new),
            qh=q["q"][:, sl].astype(BF16), k_t=q["k"][:, sl].T,
            vaug=jnp.concatenate([q["v"][:, sl].astype(BF16), ones], axis=1),
            caug=caug_ref[bi * ML_HEADS + h]))
    for c_ in ch:
        c_["s"] = (_dot(c_["qh"], c_["k_t"].astype(BF16)) * c_["d"]).astype(BF16)
    for c_ in ch:
        c_["num"] = c_["s_inter"] * _dot(c_["qh"], c_["caug"].astype(BF16)) + _dot(c_["s"], c_["vaug"])
    for (bi, h), c_ in zip(chains, ch):
        num = c_["num"]
        den = num[:, ML_HD:2 * ML_HD]
        hh = num[:, 0:ML_HD] / jnp.maximum(jnp.abs(den), jnp.exp(-c_["m_t"]))
        mu = jnp.mean(hh, axis=-1, keepdims=True)
        dv = hh - mu
        var = jnp.mean(dv * dv, axis=-1, keepdims=True)
        y = dv * lax.rsqrt(var + ML_NORM_EPS) * ng_ref[:, c_["sl"]]
        o = ml_ref[bi, :, 3 * ML_WIDTH + h * ML_HD:3 * ML_WIDTH + (h + 1) * ML_HD]
        out_ref[bi, :, c_["sl"]] = y[0:tr] * _sigmoid(o)
    for (bi, h), c_ in zip(chains, ch):
        upd = _dot((c_["k_t"] * c_["w_row"]).astype(BF16), c_["vaug"])
        caug_ref[bi * ML_HEADS + h] = c_["s_state"] * c_["caug"] + upd
        m_ref[bi:bi + 1, h:h + 1] = c_["m_new"]

    @pl.when(c == nc - 1)
    def _state_out():
        for bi in range(nb):
            for h in range(ML_HEADS):
                caug = caug_ref[bi * ML_HEADS + h]
                c1_ref[bi, h] = caug[:, 0:ML_HD]
                n1_ref[bi, h:h + 1, :] = caug[:, ML_HD:2 * ML_HD].T[0:1, :]
            m1_ref[bi] = m_ref[bi:bi + 1, 0:ML_HEADS]


def _mlstm(ml, gates, c0, n0, m0, conv0, cw, cb, gb, ng, tr, nb, chunk):
    b, t, _ = ml.shape
    nc = t // tr
    assert b % nb == 0 and t % tr == 0 and nb <= 8 and tr <= chunk
    bmap = lambda i, j: (i, 0, 0)
    const2 = lambda i, j: (0, 0)
    return pl.pallas_call(
        functools.partial(_mlstm_kernel, tr=tr, nb=nb, L=chunk),
        grid=(b // nb, nc),
        in_specs=[
            pl.BlockSpec((nb, tr, ML_QKVO), lambda i, j: (i, j, 0)),
            pl.BlockSpec((nb, tr, LANE), lambda i, j: (i, j, 0)),
            pl.BlockSpec((nb, ML_HEADS, ML_HD, ML_HD), lambda i, j: (i, 0, 0, 0)),
            pl.BlockSpec((nb, ML_HEADS, ML_HD), bmap),
            pl.BlockSpec((nb, 1, ML_HEADS), bmap),
            pl.BlockSpec((nb, CONV_W - 1, 2 * ML_WIDTH), bmap),
            pl.BlockSpec((CONV_W, 2 * ML_WIDTH), const2),
            pl.BlockSpec((1, 2 * ML_WIDTH), const2),
            pl.BlockSpec((1, LANE), const2),
            pl.BlockSpec((1, ML_WIDTH), const2),
        ],
        out_specs=[
            pl.BlockSpec((nb, tr, ML_WIDTH), lambda i, j: (i, j, 0)),
            pl.BlockSpec((nb, ML_HEADS, ML_HD, ML_HD), lambda i, j: (i, 0, 0, 0)),
            pl.BlockSpec((nb, ML_HEADS, ML_HD), bmap),
            pl.BlockSpec((nb, 1, ML_HEADS), bmap),
            pl.BlockSpec((nb, CONV_W - 1, 2 * ML_WIDTH), bmap),
        ],
        out_shape=[
            jax.ShapeDtypeStruct((b, t, ML_WIDTH), F32),
            jax.ShapeDtypeStruct((b, ML_HEADS, ML_HD, ML_HD), F32),
            jax.ShapeDtypeStruct((b, ML_HEADS, ML_HD), F32),
            jax.ShapeDtypeStruct((b, 1, ML_HEADS), F32),
            jax.ShapeDtypeStruct((b, CONV_W - 1, 2 * ML_WIDTH), F32),
        ],
        scratch_shapes=[
            pltpu.VMEM((nb * ML_HEADS, ML_HD, 2 * ML_HD), F32),
            pltpu.VMEM((8, LANE), F32),
            pltpu.VMEM((nb, tr + 8, 2 * ML_WIDTH), F32),
        ],
        compiler_params=pltpu.CompilerParams(
            dimension_semantics=("arbitrary", "arbitrary"), vmem_limit_bytes=VMEM_LIMIT),
        name="mlstm",
    )(ml, gates, c0, n0, m0, conv0, cw, cb, gb, ng)


def _rwkv_kernel(c_ref, sh_ref, s0_ref, mu_ref, w0_ref, wwa_ref, a0_ref, g2_ref, kk_ref, ka_ref, rk_ref,
                 gng_ref, gnb_ref, out_ref, s1_ref, s_ref, last_ref, *, tr, nb, L):
    W = RW_WIDTH
    GW = RW_GROUP * RW_HD
    GL = RW_GROUP * L
    NGRP = RW_HEADS // RW_GROUP
    ci = pl.program_id(1)
    nc = pl.num_programs(1)

    r2 = _iota((GL, GW), 0)
    c2 = _iota((GL, GW), 1)
    bd = (r2 // L) == (c2 // RW_HD)
    rg = _iota((GL, GL), 0)
    cg = _iota((GL, GL), 1)
    s_lower = rg > cg
    i_lower = rg >= cg

    @pl.when(ci == 0)
    def _init():
        sbd = (_iota((GW, GW), 0) // RW_HD) == (_iota((GW, GW), 1) // RW_HD)
        for bi in range(nb):
            last_ref[bi, 0:1, :] = sh_ref[bi]
            for gi in range(NGRP):
                x = s0_ref[bi, gi * GW:(gi + 1) * GW, :]
                s_ref[bi * NGRP + gi] = jnp.where(sbd, jnp.concatenate([x] * RW_GROUP, axis=1), 0.0)

    c = c_ref[...].reshape(nb * tr, RW_COLS)
    prev = pltpu.roll(c, 1, axis=0)
    rowid = _iota(c.shape, 0)
    for bi in range(nb):
        prev = jnp.where(rowid == bi * tr, last_ref[bi, 0:1, :], prev)
        last_ref[bi, 0:1, :] = c[(bi + 1) * tr - 1:(bi + 1) * tr, :]
    xs = c + (prev - c) * mu_ref[...]
    r = xs[:, 0:W]
    k = xs[:, W:2 * W]
    v = xs[:, 2 * W:3 * W]
    slab = xs[:, 3 * W:3 * W + DECAY_LORA + AAA_LORA]
    gd = xs[:, 3 * W + DECAY_LORA + AAA_LORA:]
    lane = _iota(slab.shape, 1)
    t_in = jnp.where(lane < DECAY_LORA, jnp.tanh(slab), slab)
    la = _dot(t_in.astype(BF16), wwa_ref[...])
    w_log = -_softplus(-(w0_ref[...] + la[:, 0:W])) - 0.5
    lw_all = -jnp.exp(w_log)
    a_all = _sigmoid(a0_ref[...] + la[:, W:2 * W])
    g = _dot(_sigmoid(gd).astype(BF16), g2_ref[...])

    rs = _iota((W, W), 0)
    cs = _iota((W, W), 1)
    seg = jnp.where((rs // RW_HD) == (cs // RW_HD), 1.0, 0.0).astype(BF16)

    kk = k * kk_ref[...]
    kn_all = k * (1.0 + (a_all - 1.0) * ka_ref[...])
    ss = _dot_exact_rhs(kk * kk, seg, 2)
    kap_all = kk / jnp.maximum(jnp.sqrt(ss), 1e-12)
    bonus = _dot_exact_rhs(r * kn_all * rk_ref[...], seg, 2) * v

    rl = _iota((L, L), 0)
    cl_ = _iota((L, L), 1)
    tri = jnp.where(rl >= cl_, 1.0, 0.0).astype(BF16)

    def stack(x):
        return jnp.where(bd, jnp.concatenate([x] * RW_GROUP, axis=0), 0.0).astype(BF16)

    seqs = []
    for bi in range(nb):
        rows = slice(bi * tr, (bi + 1) * tr)
        lw = _pad_rows(lw_all[rows], L)
        kap = _pad_rows(kap_all[rows], L)
        kn_p = _pad_rows(kn_all[rows], L)
        a_p = _pad_rows(a_all[rows], L)
        v_p = _pad_rows(v[rows], L)
        r_p = _pad_rows(r[rows], L)
        cum = _dot_exact_lhs(tri, lw, 3)
        cum_last = cum[L - 1:L, :]
        e_neg = jnp.exp(-cum)
        e_rem = jnp.exp(cum_last - cum)
        seqs.append(dict(
            abar=-kap * jnp.exp(cum - lw), btil=kap * a_p * e_neg, ktil=kn_p * e_neg, rbar=r_p * jnp.exp(cum),
            bhat=kap * a_p * e_rem, khat=kn_p * e_rem, v=v_p, g_last=jnp.exp(cum_last)))

    chains = [(bi, gi) for bi in range(nb) for gi in range(NGRP)]
    ch = []
    for bi, gi in chains:
        q = seqs[bi]
        ls = slice(gi * GW, (gi + 1) * GW)
        a_s, b_s, k_s, r_s, v_s = (stack(q[n][:, ls]) for n in ("abar", "btil", "ktil", "rbar", "v"))
        ch.append(dict(
            ar=jnp.concatenate([a_s, r_s], axis=0),
            bk=jnp.concatenate([b_s, k_s], axis=0),
            bkh=jnp.concatenate([stack(q["bhat"][:, ls]), stack(q["khat"][:, ls])], axis=0),
            v_s=v_s, g_last=q["g_last"][:, ls], s=s_ref[bi * NGRP + gi]))
    for c_ in ch:
        p = _dot_nt(c_["ar"], c_["bk"])
        c_["npow"] = jnp.where(s_lower, p[0:GL, 0:GL], 0.0)
        c_["a_ak"] = jnp.where(s_lower, p[0:GL, GL:2 * GL], 0.0).astype(BF16)
        c_["r_b"] = jnp.where(i_lower, p[GL:2 * GL, 0:GL], 0.0).astype(BF16)
        c_["r_k"] = jnp.where(i_lower, p[GL:2 * GL, GL:2 * GL], 0.0).astype(BF16)
    for c_ in ch:
        c_["q0"] = _dot_nt(c_["ar"], c_["s"].astype(BF16))
    for c_ in ch:
        c_["u"] = c_["q0"][0:GL] + _dot(c_["a_ak"], c_["v_s"])
    lvls = L.bit_length() - 1
    for lvl in range(lvls):
        for c_ in ch:
            nb16 = c_["npow"].astype(BF16)
            c_["u"] = c_["u"] + _dot(nb16, c_["u"].astype(BF16))
            if lvl < lvls - 1:
                c_["npow"] = _dot(nb16, nb16)
    for c_ in ch:
        y = (c_["q0"][GL:2 * GL] + _dot(c_["r_b"], c_["u"].astype(BF16)) + _dot(c_["r_k"], c_["v_s"]))
        yg = y[0:L]
        for j in range(1, RW_GROUP):
            yg = yg + y[j * L:(j + 1) * L]
        c_["yg"] = yg
    for (bi, gi), c_ in zip(chains, ch):
        uv_t = jnp.concatenate([c_["u"], c_["v_s"].astype(F32)], axis=0).T.astype(BF16)
        s_ref[bi * NGRP + gi] = c_["s"] * c_["g_last"] + _dot(uv_t, c_["bkh"])
    y_rows = [jnp.concatenate([ch[bi * NGRP + gi]["yg"] for gi in range(NGRP)], axis=1)[0:tr]
              for bi in range(nb)]

    y_all = jnp.concatenate(y_rows, axis=0) if nb > 1 else y_rows[0]
    inv = 1.0 / RW_HD
    mu_ = _dot_exact_rhs(y_all, seg, 2) * inv
    dy = y_all - mu_
    var = _dot_exact_rhs(dy * dy, seg, 2) * inv
    yn = dy * lax.rsqrt(var + GN_EPS) * gng_ref[...] + gnb_ref[...]
    out_ref[...] = ((yn + bonus) * g).reshape(nb, tr, W)

    @pl.when(ci == nc - 1)
    def _state_out():
        for bi in range(nb):
            for gi in range(NGRP):
                s = s_ref[bi * NGRP + gi]
                f = s[:, 0:RW_HD]
                for j in range(1, RW_GROUP):
                    f = f + s[:, j * RW_HD:(j + 1) * RW_HD]
                s1_ref[bi, gi * GW:(gi + 1) * GW, :] = f


def _rwkv(cols, shift0, s0, mu, w0, wwa, a0, g2, kk, ka, rk, gng, gnb, tr, nb, chunk):
    b, t, _ = cols.shape
    nc = t // tr
    assert b % nb == 0 and t % tr == 0 and tr <= chunk
    const2 = lambda i, j: (0, 0)
    vec = pl.BlockSpec((1, RW_WIDTH), const2)
    return pl.pallas_call(
        functools.partial(_rwkv_kernel, tr=tr, nb=nb, L=chunk),
        grid=(b // nb, nc),
        in_specs=[
            pl.BlockSpec((nb, tr, RW_COLS), lambda i, j: (i, j, 0)),
            pl.BlockSpec((nb, 1, RW_COLS), lambda i, j: (i, 0, 0)),
            pl.BlockSpec((nb, RW_WIDTH, RW_HD), lambda i, j: (i, 0, 0)),
            pl.BlockSpec((1, RW_COLS), const2),
            vec,
            pl.BlockSpec((DECAY_LORA + AAA_LORA, 2 * RW_WIDTH), const2),
            vec,
            pl.BlockSpec((GATE_LORA, RW_WIDTH), const2),
            vec, vec, vec, vec, vec,
        ],
        out_specs=[
            pl.BlockSpec((nb, tr, RW_WIDTH), lambda i, j: (i, j, 0)),
            pl.BlockSpec((nb, RW_WIDTH, RW_HD), lambda i, j: (i, 0, 0)),
        ],
        out_shape=[
            jax.ShapeDtypeStruct((b, t, RW_WIDTH), F32),
            jax.ShapeDtypeStruct((b, RW_WIDTH, RW_HD), F32),
        ],
        scratch_shapes=[pltpu.VMEM((nb * (RW_HEADS // RW_GROUP), RW_GROUP * RW_HD, RW_GROUP * RW_HD), F32),
                        pltpu.VMEM((nb, 8, RW_COLS), F32)],
        compiler_params=pltpu.CompilerParams(
            dimension_semantics=("arbitrary", "arbitrary"), vmem_limit_bytes=VMEM_LIMIT),
        name="rwkv",
    )(cols, shift0, s0, mu, w0, wwa, a0, g2, kk, ka, rk, gng, gnb)


def _extract_top(works, idx, n):
    ranks = [jnp.full(w.shape, 99.0, F32) for w in works]
    vals = [[] for _ in works]
    for j in range(n):
        ms = [jnp.max(w, axis=0, keepdims=True) for w in works]
        if idx is None:
            sels = [w == m for w, m in zip(works, ms)]
        else:
            firsts = [jnp.min(jnp.where(w == m, idx, 1e9), axis=0, keepdims=True) for w, m in zip(works, ms)]
            sels = [idx == f for f in firsts]
        ranks = [jnp.where(sel, float(j), r) for sel, r in zip(sels, ranks)]
        works = [jnp.where(sel, -jnp.inf, w) for sel, w in zip(sels, works)]
        for v, m in zip(vals, ms):
            v.append(m)
    return vals, ranks


def _miscount(rank, n):
    taken = jnp.sum(jnp.where(rank < float(n), 1.0, 0.0), axis=0, keepdims=True)
    return jnp.abs(taken - float(n))


def _rows_bf16(row, rows):
    packed = jnp.broadcast_to(row, (16, row.shape[1])).astype(BF16)
    return jnp.concatenate([packed] * (rows // 16), axis=0)


def _peer_kernel(x_ref, ml_ref, rw_ref, woml_ref, worw_ref, gffn_ref, gfin_ref, wqt_ref, keys_ref,
                 ua_ref, ub_ref, vta_ref, vtb_ref, y_ref,
                 ht_ref, qt_ref, rank_ref, e_ref, vals_ref, r2b_ref, e2b_ref, cnt_ref, gate_ref,
                 p_ref, acc_ref, *, tm, ec):
    e = pl.program_id(1)
    ne = pl.num_programs(1)
    ng = tm // LANE
    K = PEER_TOPK
    nslab = ec // N_KEYS
    act_ref = qt_ref

    @pl.when(e == 0)
    def _select():
        x1 = (x_ref[...] + _dot(ml_ref[...].astype(BF16), woml_ref[...])
              + _dot(rw_ref[...].astype(BF16), worw_ref[...]))
        y_ref[...] = x1
        h = x1 * lax.rsqrt(jnp.mean(x1 * x1, axis=-1, keepdims=True) + NORM_EPS) * gffn_ref[...]
        ht = h.T.astype(BF16)
        ht_ref[...] = ht
        qt_ref[...] = _dot(wqt_ref[...], ht)
        acc_ref[...] = jnp.zeros(acc_ref.shape, F32)

        def score_body(hp, carry):
            q = qt_ref[pl.ds(pl.multiple_of(hp * PEER_HALF, PEER_HALF), PEER_HALF), :]
            e_ref[hp] = _dot(keys_ref[hp], q.astype(BF16))
            return carry

        lax.fori_loop(0, 2 * PEER_HEADS, score_body, 0)

        key_idx = _iota((N_KEYS, LANE), 0).astype(F32)
        groups = [slice(gi * LANE, (gi + 1) * LANE) for gi in range(ng)]

        def topk_groups(hp, exact_ties):
            scores = [e_ref[hp, :, lanes] for lanes in groups]
            vals, ranks = _extract_top(scores, key_idx if exact_ties else None, K)
            bad = None
            for lanes, v, rank in zip(groups, vals, ranks):
                rank_ref[hp, :, lanes] = rank
                vals_ref[hp, :, lanes] = jnp.concatenate(v, axis=0)
                miss = _miscount(rank, K)
                bad = miss if bad is None else jnp.maximum(bad, miss)
            return bad

        def topk_body(hp, carry):
            bad = topk_groups(hp, False)

            @pl.when(jnp.max(bad) > 0.0)
            def _redo():
                topk_groups(hp, True)

            return carry

        lax.fori_loop(0, 2 * PEER_HEADS, topk_body, 0)

        sub8 = _iota((8, LANE), 0)

        def cand_groups(h, exact_ties):
            works, v1s = [], []
            slab_a = [a_i for a_i in range(K) for _ in range(0, K // (a_i + 1), 8)]
            for lanes in groups:
                v1 = vals_ref[2 * h, :, lanes]
                v2 = vals_ref[2 * h + 1, :, lanes]
                slabs = []
                for a_i in range(K):
                    nb = K // (a_i + 1)
                    for b0 in range(0, nb, 8):
                        rows = v1[a_i:a_i + 1, :] + v2[b0:b0 + 8, :]
                        slabs.append(jnp.where(sub8 + b0 < nb, rows, -jnp.inf))
                works.append(jnp.concatenate(slabs, axis=0))
                v1s.append(v1)
            idx = None
            if exact_ties:
                idx = jnp.concatenate([(sub8 + (b0 + a_i * K)).astype(F32) for a_i in range(K)
                                       for b0 in range(0, K // (a_i + 1), 8)], axis=0)
            vals, ranks = _extract_top(works, idx, K)
            bad = None
            for lanes, v1, v, rank in zip(groups, v1s, vals, ranks):
                z = jnp.ones_like(v[0])
                for j in range(1, K):
                    z = z + jnp.exp(v[j] - v[0])
                picked = jnp.where(rank < float(K), 1.0, 0.0)
                r1 = rank_ref[2 * h, :, lanes]
                cnt = jnp.zeros((N_KEYS, LANE), F32)
                for a_i in range(K):
                    ca = None
                    for si, sa in enumerate(slab_a):
                        if sa == a_i:
                            part = jnp.sum(picked[si * 8:(si + 1) * 8], axis=0, keepdims=True)
                            ca = part if ca is None else ca + part
                    cnt = jnp.where(r1 == float(a_i), ca, cnt)
                cnt_ref[h, :, lanes] = cnt
                e1 = jnp.exp(e_ref[2 * h, :, lanes] - v1[0:1, :])
                gate_ref[h, :, lanes] = jnp.where(r1 < float(K), e1 / z, 0.0)
                miss = _miscount(rank, K)
                bad = miss if bad is None else jnp.maximum(bad, miss)
            return bad

        def cand_body(h, carry):
            bad = cand_groups(h, False)

            @pl.when(jnp.max(bad) > 0.0)
            def _redo():
                cand_groups(h, True)

            r2b_ref[h] = rank_ref[2 * h + 1].astype(BF16)
            e2b_ref[h] = jnp.exp(e_ref[2 * h + 1] - vals_ref[2 * h + 1, 0:1, :]).astype(BF16)
            return carry

        lax.fori_loop(0, PEER_HEADS, cand_body, 0)

    key1 = pl.ds(pl.multiple_of(e * nslab, nslab), nslab)
    hs = nslab // 2
    hrows = hs * N_KEYS
    for half in range(2):
        act_ref[half * hrows:(half + 1) * hrows, :] = _dot((ua_ref, ub_ref)[half][...], ht_ref[...])
    def gate_half(half):
        for gi in range(ng):
            lanes = slice(gi * LANE, (gi + 1) * LANE)
            w = [jnp.zeros((N_KEYS, LANE), BF16) for _ in range(hs)]
            for h in range(PEER_HEADS):
                r2 = r2b_ref[h, :, lanes]
                e2 = e2b_ref[h, :, lanes]
                cw = cnt_ref[h, key1, lanes]
                gw = gate_ref[h, key1, lanes]
                for jj in range(hs):
                    j = half * hs + jj
                    hit = r2 < _rows_bf16(cw[j:j + 1, :], N_KEYS)
                    w[jj] = w[jj] + jnp.where(hit, e2, jnp.zeros_like(e2)) * _rows_bf16(gw[j:j + 1, :], N_KEYS)
            for jj in range(hs):
                rows = slice((half * hs + jj) * N_KEYS, (half * hs + jj + 1) * N_KEYS)
                act = act_ref[rows, lanes]
                gelu = 0.5 * act * (1.0 + lax.erf(act * 0.7071067811865476))
                p_ref[rows, lanes] = w[jj] * gelu.astype(BF16)

    for half in range(2):
        gate_half(half)
        acc_ref[...] += _dot((vta_ref, vtb_ref)[half][0], p_ref[half * hrows:(half + 1) * hrows, :])

    @pl.when(e == ne - 1)
    def _finish():
        x2 = y_ref[...] + acc_ref[...].T
        y_ref[...] = x2 * lax.rsqrt(jnp.mean(x2 * x2, axis=-1, keepdims=True) + NORM_EPS) * gfin_ref[...]


def _peer(x2d, ml2d, rw2d, woml, worw, gffn, gfin, wqt, keys, u, vt, tm, ec):
    n = x2d.shape[0]
    n_exp = u.shape[0]
    assert ec == 8 * N_KEYS and tm % LANE == 0 and n % tm == 0 and n_exp % ec == 0
    tok = lambda i, e: (i, 0)
    const2 = lambda i, e: (0, 0)
    qrows = 2 * PEER_HEADS * PEER_HALF
    eh = ec // 2
    once = pl.Buffered(1)
    return pl.pallas_call(
        functools.partial(_peer_kernel, tm=tm, ec=ec),
        grid=(n // tm, n_exp // ec),
        in_specs=[
            pl.BlockSpec((tm, D_MODEL), tok),
            pl.BlockSpec((tm, ML_WIDTH), tok),
            pl.BlockSpec((tm, RW_WIDTH), tok),
            pl.BlockSpec((ML_WIDTH, D_MODEL), const2, pipeline_mode=once),
            pl.BlockSpec((RW_WIDTH, D_MODEL), const2, pipeline_mode=once),
            pl.BlockSpec((1, D_MODEL), const2),
            pl.BlockSpec((1, D_MODEL), const2),
            pl.BlockSpec((qrows, D_MODEL), const2, pipeline_mode=once),
            pl.BlockSpec((2 * PEER_HEADS, N_KEYS, PEER_HALF), lambda i, e: (0, 0, 0), pipeline_mode=once),
            pl.BlockSpec((eh, D_MODEL), lambda i, e: (2 * e, 0)),
            pl.BlockSpec((eh, D_MODEL), lambda i, e: (2 * e + 1, 0)),
            pl.BlockSpec((1, D_MODEL, eh), lambda i, e: (2 * e, 0, 0)),
            pl.BlockSpec((1, D_MODEL, eh), lambda i, e: (2 * e + 1, 0, 0)),
        ],
        out_specs=pl.BlockSpec((tm, D_MODEL), tok),
        out_shape=jax.ShapeDtypeStruct((n, D_MODEL), F32),
        scratch_shapes=[
            pltpu.VMEM((D_MODEL, tm), BF16),
            pltpu.VMEM((qrows, tm), F32),
            pltpu.VMEM((2 * PEER_HEADS, N_KEYS, tm), F32),
            pltpu.VMEM((2 * PEER_HEADS, N_KEYS, tm), F32),
            pltpu.VMEM((2 * PEER_HEADS, PEER_TOPK, tm), F32),
            pltpu.VMEM((PEER_HEADS, N_KEYS, tm), BF16),
            pltpu.VMEM((PEER_HEADS, N_KEYS, tm), BF16),
            pltpu.VMEM((PEER_HEADS, N_KEYS, tm), F32),
            pltpu.VMEM((PEER_HEADS, N_KEYS, tm), F32),
            pltpu.VMEM((ec, tm), BF16),
            pltpu.VMEM((D_MODEL, tm), F32),
        ],
        compiler_params=pltpu.CompilerParams(
            dimension_semantics=("arbitrary", "arbitrary"), vmem_limit_bytes=VMEM_LIMIT),
        name="peer",
    )(x2d, ml2d, rw2d, woml, worw, gffn, gfin, wqt, keys, u, u, vt, vt)


def _prep_weights(norm_mix_g, w_in, ml_conv_w, ml_conv_b, ml_b_i, ml_b_f, ml_norm_g,
                  rw_mu, rw_w0, rw_w2, rw_a0, rw_a2, rw_g2, rw_k_k, rw_k_a, rw_r_k, rw_gn_g, rw_gn_b,
                  w_out, norm_ffn_g, peer_w_q, peer_sub_keys, peer_u, peer_v, norm_final_g):
    assert w_in.shape[0] == 1, "one layer"
    w = w_in[0]
    wg = jnp.pad(w[:, ML_QKVO:ML_QKVO + 2 * ML_HEADS], ((0, 0), (0, LANE - 2 * ML_HEADS)))
    wg_hi = wg.astype(BF16)
    wg_lo = (wg - wg_hi.astype(F32)).astype(BF16)
    zeros = jnp.zeros((DECAY_LORA, RW_WIDTH), F32)
    wwa = jnp.concatenate([jnp.concatenate([rw_w2[0], zeros], axis=1),
                           jnp.concatenate([zeros, rw_a2[0]], axis=1)], axis=0)
    gate_bias = jnp.pad(jnp.concatenate([ml_b_i[0], ml_b_f[0]]), (0, LANE - 2 * ML_HEADS))[None, :]
    row = lambda a: a.reshape(1, -1)
    return dict(
        g_mix=row(norm_mix_g[0]),
        wml=w[:, 0:ML_QKVO].astype(BF16),
        wg=jnp.stack([wg_hi, wg_lo]),
        wrw=w[:, ML_QKVO + 2 * ML_HEADS:].astype(BF16),
        conv_w=ml_conv_w[0], conv_b=row(ml_conv_b[0]), gate_bias=gate_bias, ml_norm_g=row(ml_norm_g[0]),
        mu=row(rw_mu[0]), w0=row(rw_w0[0]), wwa=wwa.astype(BF16), a0=row(rw_a0[0]),
        g2=rw_g2[0].astype(BF16), k_k=row(rw_k_k[0]), k_a=row(rw_k_a[0]), r_k=row(rw_r_k[0]),
        gn_g=row(rw_gn_g[0]), gn_b=row(rw_gn_b[0]),
        wo_ml=w_out[0, 0:ML_WIDTH].astype(BF16), wo_rw=w_out[0, ML_WIDTH:].astype(BF16),
        g_ffn=row(norm_ffn_g[0]), g_fin=row(norm_final_g),
        wqt=peer_w_q[0].T.astype(BF16),
        keys=peer_sub_keys[0].reshape(2 * PEER_HEADS, N_KEYS, PEER_HALF).astype(BF16),
        u=peer_u[0].astype(BF16),
        vt=peer_v[0].reshape(-1, PEER_EXPERT_CHUNK // 2, D_MODEL).transpose(0, 2, 1).astype(BF16),
    )


def _rwkv_tiling(t):
    tr = min(t, RW_CHUNK)
    chunk = max(tr, RW_MIN_CHUNK)
    return tr, chunk, RW_STEP_ROWS // chunk // 2 if chunk < RW_CHUNK else RW_STEP_ROWS // chunk


def _mlstm_tiling(t, b):
    tr = min(t, ML_CHUNK)
    chunk = max(tr, ML_MIN_CHUNK)
    return tr, min(b, ML_STEP_ROWS // chunk), chunk


def _trunk(x, states, wp, tr_rw, rw_chunk, rw_batch, tm_in, tm_peer, ec):
    b, t, d = x.shape
    c0, n0, m0, conv0, s0, shift0 = (s[0] for s in states)
    x2d = x.reshape(b * t, d)
    ml, gates, rw = _proj_in(x2d, wp["g_mix"], wp["wml"], wp["wg"], wp["wrw"], tm_in)
    ml3 = ml.reshape(b, t, ML_QKVO)
    rw3 = rw.reshape(b, t, RW_COLS)
    ml_out, c1, n1, m1, conv1 = _mlstm(
        ml3, gates.reshape(b, t, LANE), c0, n0, m0.reshape(b, 1, ML_HEADS), conv0,
        wp["conv_w"], wp["conv_b"], wp["gate_bias"], wp["ml_norm_g"], *_mlstm_tiling(t, b))
    rw_out, s1 = _rwkv(
        rw3, shift0[:, None, :], s0.reshape(b, RW_WIDTH, RW_HD), wp["mu"], wp["w0"], wp["wwa"], wp["a0"], wp["g2"],
        wp["k_k"], wp["k_a"], wp["r_k"], wp["gn_g"], wp["gn_b"], tr_rw, min(b, rw_batch), rw_chunk)
    y = _peer(x2d, ml_out.reshape(b * t, ML_WIDTH), rw_out.reshape(b * t, RW_WIDTH),
              wp["wo_ml"], wp["wo_rw"], wp["g_ffn"], wp["g_fin"], wp["wqt"], wp["keys"],
              wp["u"], wp["vt"], tm_peer, ec)
    new_states = (c1[None], n1[None], m1.reshape(b, ML_HEADS)[None], conv1[None],
                  s1.reshape(b, RW_HEADS, RW_HD, RW_HD)[None], rw3[:, -1][None])
    return y.reshape(b, t, d), new_states


def kernel(x_prompt, x_sample, state_mlstm_C, state_mlstm_n, state_mlstm_m, state_mlstm_conv, state_rwkv_S, state_rwkv_shift, norm_mix_g, w_in, ml_conv_w, ml_conv_b, ml_b_i, ml_b_f, ml_norm_g, rw_mu, rw_w0, rw_w2, rw_a0, rw_a2, rw_g2, rw_k_k, rw_k_a, rw_r_k, rw_gn_g, rw_gn_b, w_out, norm_ffn_g, peer_w_q, peer_sub_keys, peer_u, peer_v, norm_final_g):
    wp = _prep_weights(norm_mix_g, w_in, ml_conv_w, ml_conv_b, ml_b_i, ml_b_f, ml_norm_g,
                       rw_mu, rw_w0, rw_w2, rw_a0, rw_a2, rw_g2, rw_k_k, rw_k_a, rw_r_k, rw_gn_g, rw_gn_b,
                       w_out, norm_ffn_g, peer_w_q, peer_sub_keys, peer_u, peer_v, norm_final_g)
    bp = x_prompt.shape[0]
    z = lambda *s: jnp.zeros((1, bp) + s, F32)
    prompt_states = (z(ML_HEADS, ML_HD, ML_HD), z(ML_HEADS, ML_HD), jnp.full((1, bp, ML_HEADS), M_INIT, F32),
                     z(CONV_W - 1, 2 * ML_WIDTH), z(RW_HEADS, RW_HD, RW_HD), z(RW_COLS))
    sample_states = (state_mlstm_C, state_mlstm_n, state_mlstm_m, state_mlstm_conv,
                     state_rwkv_S, state_rwkv_shift)
    tp, ts = x_prompt.shape[1], x_sample.shape[1]
    np_, ns = bp * tp, x_sample.shape[0] * ts
    y_p, st_p = _trunk(x_prompt, prompt_states, wp, *_rwkv_tiling(tp),
                       min(np_, PROJ_TILE), min(np_, PEER_TILE), PEER_EXPERT_CHUNK)
    y_s, st_s = _trunk(x_sample, sample_states, wp, *_rwkv_tiling(ts),
                       min(ns, PROJ_TILE), min(ns, PEER_TILE), PEER_EXPERT_CHUNK)
    return (y_p, y_s) + tuple(st_p) + tuple(st_s)
```

```python
import functools

import jax
import jax.numpy as jnp
from jax import lax
from jax.experimental import pallas as pl
from jax.experimental.pallas import tpu as pltpu

F32 = jnp.float32
BF16 = jnp.bfloat16

D_MODEL = 1024
ML_HEADS = 4
ML_HD = 128
ML_WIDTH = ML_HEADS * ML_HD
CONV_W = 4
RW_HEADS = 8
RW_HD = 64
RW_WIDTH = RW_HEADS * RW_HD
DECAY_LORA = 64
AAA_LORA = 64
GATE_LORA = 128
RW_COLS = 3 * RW_WIDTH + DECAY_LORA + AAA_LORA + GATE_LORA
ML_QKVO = 4 * ML_WIDTH
PEER_HEADS = 8
N_KEYS = 128
PEER_TOPK = 16
PEER_HALF = 128
NORM_EPS = 1e-6
ML_NORM_EPS = 1e-6
GN_EPS = RW_HD * 1e-5
M_INIT = -1e30
NEG_BIG = -1e30

LANE = 128
ML_CHUNK = 128
ML_MIN_CHUNK = 16
ML_STEP_ROWS = 128
RW_CHUNK = 64
RW_GROUP = 4
RW_MIN_CHUNK = 16
RW_STEP_ROWS = 256
PROJ_TILE = 512
PEER_TILE = 512
PEER_EXPERT_CHUNK = 1024
VMEM_LIMIT = 56 * 1024 * 1024


def _dot(a, b):
    return jnp.dot(a, b, preferred_element_type=F32)


def _dot_nt(a, b):
    return lax.dot_general(a, b, (((1,), (1,)), ((), ())), preferred_element_type=F32)


def _split_bf16(x, n):
    parts = []
    r = x
    for _ in range(n):
        p = r.astype(BF16)
        parts.append(p)
        r = r - p.astype(F32)
    return parts


def _dot_exact_lhs(mask_bf16, x, n):
    return sum(_dot(mask_bf16, p) for p in _split_bf16(x, n))


def _dot_exact_rhs(x, mask_bf16, n):
    return sum(_dot(p, mask_bf16) for p in _split_bf16(x, n))


def _sigmoid(x):
    return 1.0 / (1.0 + jnp.exp(-x))


def _softplus(x):
    return jnp.maximum(x, 0.0) + jnp.log1p(jnp.exp(-jnp.abs(x)))


def _iota(shape, dim):
    return lax.broadcasted_iota(jnp.int32, shape, dim)


def _pad_rows(x, rows, value=0.0):
    if x.shape[0] == rows:
        return x
    return jnp.concatenate([x, jnp.full((rows - x.shape[0], x.shape[1]), value, x.dtype)], axis=0)


def _proj_in_kernel(x_ref, g_ref, wml_ref, wg_ref, wrw_ref, ml_ref, gate_ref, rw_ref):
    x = x_ref[...]
    xn = x * lax.rsqrt(jnp.mean(x * x, axis=-1, keepdims=True) + NORM_EPS) * g_ref[...]
    xb = xn.astype(BF16)
    ml_ref[...] = _dot(xb, wml_ref[...])
    rw_ref[...] = _dot(xb, wrw_ref[...])
    xlo = (xn - xb.astype(F32)).astype(BF16)
    gate_ref[...] = _dot(xb, wg_ref[0]) + _dot(xlo, wg_ref[0]) + _dot(xb, wg_ref[1])


def _proj_in(x2d, g, wml, wg, wrw, tm):
    n = x2d.shape[0]
    const2 = lambda i: (0, 0)
    return pl.pallas_call(
        _proj_in_kernel,
        grid=(n // tm,),
        in_specs=[
            pl.BlockSpec((tm, D_MODEL), lambda i: (i, 0)),
            pl.BlockSpec((1, D_MODEL), const2),
            pl.BlockSpec((D_MODEL, ML_QKVO), const2),
            pl.BlockSpec((2, D_MODEL, LANE), lambda i: (0, 0, 0)),
            pl.BlockSpec((D_MODEL, RW_COLS), const2),
        ],
        out_specs=[
            pl.BlockSpec((tm, ML_QKVO), lambda i: (i, 0)),
            pl.BlockSpec((tm, LANE), lambda i: (i, 0)),
            pl.BlockSpec((tm, RW_COLS), lambda i: (i, 0)),
        ],
        out_shape=[
            jax.ShapeDtypeStruct((n, ML_QKVO), F32),
            jax.ShapeDtypeStruct((n, LANE), F32),
            jax.ShapeDtypeStruct((n, RW_COLS), F32),
        ],
        compiler_params=pltpu.CompilerParams(
            dimension_semantics=("arbitrary",), vmem_limit_bytes=VMEM_LIMIT),
        name="proj_in",
    )(x2d, g, wml, wg, wrw)


def _mlstm_kernel(ml_ref, gate_ref, c0_ref, n0_ref, m0_ref, conv0_ref, cw_ref, cb_ref, gb_ref, ng_ref,
                  out_ref, c1_ref, n1_ref, m1_ref, conv1_ref,
                  caug_ref, m_ref, ext_ref, *, tr, nb, L):
    c = pl.program_id(1)
    nc = pl.num_programs(1)

    @pl.when(c == 0)
    def _init():
        m_ref[...] = jnp.zeros(m_ref.shape, F32)
        for bi in range(nb):
            for h in range(ML_HEADS):
                caug_ref[bi * ML_HEADS + h, :, 0:ML_HD] = c0_ref[bi, h]
                nrow = n0_ref[bi, h:h + 1, :]
                caug_ref[bi * ML_HEADS + h, :, ML_HD:2 * ML_HD] = jnp.broadcast_to(nrow, (ML_HD, ML_HD)).T
            m_ref[bi:bi + 1, 0:ML_HEADS] = m0_ref[bi]
            ext_ref[bi, 5:8, :] = conv0_ref[bi]

    row = _iota((L, L), 0)
    col = _iota((L, L), 1)
    causal = row >= col
    tri = jnp.where(causal, 1.0, 0.0).astype(BF16)
    ones = jnp.ones((L, ML_HD), BF16)

    seqs = []
    for bi in range(nb):
        ext_ref[bi, 8:8 + tr, :] = ml_ref[bi, :, 0:2 * ML_WIDTH]
        acc = cb_ref[...] + ext_ref[bi, 5:5 + tr, :] * cw_ref[0:1, :]
        for j in range(1, CONV_W):
            acc = acc + ext_ref[bi, 5 + j:5 + j + tr, :] * cw_ref[j:j + 1, :]
        qk = acc * _sigmoid(acc)
        tail = ext_ref[bi, tr + 5:tr + 8, :]
        ext_ref[bi, 5:8, :] = tail

        @pl.when(c == nc - 1)
        def _conv_out():
            conv1_ref[bi] = tail

        g = gate_ref[bi] + gb_ref[...]
        i_all = _pad_rows(g, L, NEG_BIG)
        b_col = _dot_exact_lhs(tri, _pad_rows(-_softplus(-g), L, 0.0), 3)
        seqs.append(dict(
            q=_pad_rows(qk[:, 0:ML_WIDTH], L),
            k=_pad_rows(qk[:, ML_WIDTH:2 * ML_WIDTH] * (ML_HD ** -0.5), L),
            v=_pad_rows(ml_ref[bi, :, 2 * ML_WIDTH:3 * ML_WIDTH], L),
            b_col=b_col, b_t=b_col.T, i_t=i_all.T))

    chains = [(bi, h) for bi in range(nb) for h in range(ML_HEADS)]
    ch = []
    for bi, h in chains:
        q = seqs[bi]
        sl = slice(h * ML_HD, (h + 1) * ML_HD)
        bc = q["b_col"][:, ML_HEADS + h:ML_HEADS + h + 1]
        br = q["b_t"][ML_HEADS + h:ML_HEADS + h + 1, :]
        ir = q["i_t"][h:h + 1, :]
        m_prev = m_ref[bi:bi + 1, h:h + 1]
        logd = jnp.where(causal, bc - br + ir, -jnp.inf)
        linter = bc + m_prev
        m_t = jnp.maximum(linter, jnp.max(logd, axis=1, keepdims=True))
        m_new = m_t[L - 1:L, :]
        b_last = bc[L - 1:L, :]
        ch.append(dict(
            sl=sl, m_t=m_t, d=jnp.exp(logd - m_t), s_inter=jnp.exp(linter - m_t), m_new=m_new,
            s_state=jnp.exp(b_last + m_prev - m_new), w_row=jnp.exp(b_last - br + ir - m_new),
            qh=q["q"][:, sl].astype(BF16), k_t=q["k"][:, sl].T,
            vaug=jnp.concatenate([q["v"][:, sl].astype(BF16), ones], axis=1),
            caug=caug_ref[bi * ML_HEADS + h]))
    for c_ in ch:
        c_["s"] = (_dot(c_["qh"], c_["k_t"].astype(BF16)) * c_["d"]).astype(BF16)
    for c_ in ch:
        c_["num"] = c_["s_inter"] * _dot(c_["qh"], c_["caug"].astype(BF16)) + _dot(c_["s"], c_["vaug"])
    for (bi, h), c_ in zip(chains, ch):
        num = c_["num"]
        den = num[:, ML_HD:2 * ML_HD]
        hh = num[:, 0:ML_HD] / jnp.maximum(jnp.abs(den), jnp.exp(-c_["m_t"]))
        mu = jnp.mean(hh, axis=-1, keepdims=True)
        dv = hh - mu
        var = jnp.mean(dv * dv, axis=-1, keepdims=True)
        y = dv * lax.rsqrt(var + ML_NORM_EPS) * ng_ref[:, c_["sl"]]
        o = ml_ref[bi, :, 3 * ML_WIDTH + h * ML_HD:3 * ML_WIDTH + (h + 1) * ML_HD]
        out_ref[bi, :, c_["sl"]] = y[0:tr] * _sigmoid(o)
    for (bi, h), c_ in zip(chains, ch):
        upd = _dot((c_["k_t"] * c_["w_row"]).astype(BF16), c_["vaug"])
        caug_ref[bi * ML_HEADS + h] = c_["s_state"] * c_["caug"] + upd
        m_ref[bi:bi + 1, h:h + 1] = c_["m_new"]

    @pl.when(c == nc - 1)
    def _state_out():
        for bi in range(nb):
            for h in range(ML_HEADS):
                caug = caug_ref[bi * ML_HEADS + h]
                c1_ref[bi, h] = caug[:, 0:ML_HD]
                n1_ref[bi, h:h + 1, :] = caug[:, ML_HD:2 * ML_HD].T[0:1, :]
            m1_ref[bi] = m_ref[bi:bi + 1, 0:ML_HEADS]


def _mlstm(ml, gates, c0, n0, m0, conv0, cw, cb, gb, ng, tr, nb, chunk):
    b, t, _ = ml.shape
    nc = t // tr
    assert b % nb == 0 and t % tr == 0 and nb <= 8 and tr <= chunk
    bmap = lambda i, j: (i, 0, 0)
    const2 = lambda i, j: (0, 0)
    return pl.pallas_call(
        functools.partial(_mlstm_kernel, tr=tr, nb=nb, L=chunk),
        grid=(b // nb, nc),
        in_specs=[
            pl.BlockSpec((nb, tr, ML_QKVO), lambda i, j: (i, j, 0)),
            pl.BlockSpec((nb, tr, LANE), lambda i, j: (i, j, 0)),
            pl.BlockSpec((nb, ML_HEADS, ML_HD, ML_HD), lambda i, j: (i, 0, 0, 0)),
            pl.BlockSpec((nb, ML_HEADS, ML_HD), bmap),
            pl.BlockSpec((nb, 1, ML_HEADS), bmap),
            pl.BlockSpec((nb, CONV_W - 1, 2 * ML_WIDTH), bmap),
            pl.BlockSpec((CONV_W, 2 * ML_WIDTH), const2),
            pl.BlockSpec((1, 2 * ML_WIDTH), const2),
            pl.BlockSpec((1, LANE), const2),
            pl.BlockSpec((1, ML_WIDTH), const2),
        ],
        out_specs=[
            pl.BlockSpec((nb, tr, ML_WIDTH), lambda i, j: (i, j, 0)),
            pl.BlockSpec((nb, ML_HEADS, ML_HD, ML_HD), lambda i, j: (i, 0, 0, 0)),
            pl.BlockSpec((nb, ML_HEADS, ML_HD), bmap),
            pl.BlockSpec((nb, 1, ML_HEADS), bmap),
            pl.BlockSpec((nb, CONV_W - 1, 2 * ML_WIDTH), bmap),
        ],
        out_shape=[
            jax.ShapeDtypeStruct((b, t, ML_WIDTH), F32),
            jax.ShapeDtypeStruct((b, ML_HEADS, ML_HD, ML_HD), F32),
            jax.ShapeDtypeStruct((b, ML_HEADS, ML_HD), F32),
            jax.ShapeDtypeStruct((b, 1, ML_HEADS), F32),
            jax.ShapeDtypeStruct((b, CONV_W - 1, 2 * ML_WIDTH), F32),
        ],
        scratch_shapes=[
            pltpu.VMEM((nb * ML_HEADS, ML_HD, 2 * ML_HD), F32),
            pltpu.VMEM((8, LANE), F32),
            pltpu.VMEM((nb, tr + 8, 2 * ML_WIDTH), F32),
        ],
        compiler_params=pltpu.CompilerParams(
            dimension_semantics=("arbitrary", "arbitrary"), vmem_limit_bytes=VMEM_LIMIT),
        name="mlstm",
    )(ml, gates, c0, n0, m0, conv0, cw, cb, gb, ng)


def _rwkv_kernel(c_ref, sh_ref, s0_ref, mu_ref, w0_ref, wwa_ref, a0_ref, g2_ref, kk_ref, ka_ref, rk_ref,
                 gng_ref, gnb_ref, out_ref, s1_ref, s_ref, last_ref, *, tr, nb, L):
    W = RW_WIDTH
    GW = RW_GROUP * RW_HD
    GL = RW_GROUP * L
    NGRP = RW_HEADS // RW_GROUP
    ci = pl.program_id(1)
    nc = pl.num_programs(1)

    r2 = _iota((GL, GW), 0)
    c2 = _iota((GL, GW), 1)
    bd = (r2 // L) == (c2 // RW_HD)
    rg = _iota((GL, GL), 0)
    cg = _iota((GL, GL), 1)
    s_lower = rg > cg
    i_lower = rg >= cg

    @pl.when(ci == 0)
    def _init():
        sbd = (_iota((GW, GW), 0) // RW_HD) == (_iota((GW, GW), 1) // RW_HD)
        for bi in range(nb):
            last_ref[bi, 0:1, :] = sh_ref[bi]
            for gi in range(NGRP):
                x = s0_ref[bi, gi * GW:(gi + 1) * GW, :]
                s_ref[bi * NGRP + gi] = jnp.where(sbd, jnp.concatenate([x] * RW_GROUP, axis=1), 0.0)

    c = c_ref[...].reshape(nb * tr, RW_COLS)
    prev = pltpu.roll(c, 1, axis=0)
    rowid = _iota(c.shape, 0)
    for bi in range(nb):
        prev = jnp.where(rowid == bi * tr, last_ref[bi, 0:1, :], prev)
        last_ref[bi, 0:1, :] = c[(bi + 1) * tr - 1:(bi + 1) * tr, :]
    xs = c + (prev - c) * mu_ref[...]
    r = xs[:, 0:W]
    k = xs[:, W:2 * W]
    v = xs[:, 2 * W:3 * W]
    slab = xs[:, 3 * W:3 * W + DECAY_LORA + AAA_LORA]
    gd = xs[:, 3 * W + DECAY_LORA + AAA_LORA:]
    lane = _iota(slab.shape, 1)
    t_in = jnp.where(lane < DECAY_LORA, jnp.tanh(slab), slab)
    la = _dot(t_in.astype(BF16), wwa_ref[...])
    w_log = -_softplus(-(w0_ref[...] + la[:, 0:W])) - 0.5
    lw_all = -jnp.exp(w_log)
    a_all = _sigmoid(a0_ref[...] + la[:, W:2 * W])
    g = _dot(_sigmoid(gd).astype(BF16), g2_ref[...])

    rs = _iota((W, W), 0)
    cs = _iota((W, W), 1)
    seg = jnp.where((rs // RW_HD) == (cs // RW_HD), 1.0, 0.0).astype(BF16)

    kk = k * kk_ref[...]
    kn_all = k * (1.0 + (a_all - 1.0) * ka_ref[...])
    ss = _dot_exact_rhs(kk * kk, seg, 2)
    kap_all = kk / jnp.maximum(jnp.sqrt(ss), 1e-12)
    bonus = _dot_exact_rhs(r * kn_all * rk_ref[...], seg, 2) * v

    rl = _iota((L, L), 0)
    cl_ = _iota((L, L), 1)
    tri = jnp.where(rl >= cl_, 1.0, 0.0).astype(BF16)

    def stack(x):
        return jnp.where(bd, jnp.concatenate([x] * RW_GROUP, axis=0), 0.0).astype(BF16)

    seqs = []
    for bi in range(nb):
        rows = slice(bi * tr, (bi + 1) * tr)
        lw = _pad_rows(lw_all[rows], L)
        kap = _pad_rows(kap_all[rows], L)
        kn_p = _pad_rows(kn_all[rows], L)
        a_p = _pad_rows(a_all[rows], L)
        v_p = _pad_rows(v[rows], L)
        r_p = _pad_rows(r[rows], L)
        cum = _dot_exact_lhs(tri, lw, 3)
        cum_last = cum[L - 1:L, :]
        e_neg = jnp.exp(-cum)
        e_rem = jnp.exp(cum_last - cum)
        seqs.append(dict(
            abar=-kap * jnp.exp(cum - lw), btil=kap * a_p * e_neg, ktil=kn_p * e_neg, rbar=r_p * jnp.exp(cum),
            bhat=kap * a_p * e_rem, khat=kn_p * e_rem, v=v_p, g_last=jnp.exp(cum_last)))

    chains = [(bi, gi) for bi in range(nb) for gi in range(NGRP)]
    ch = []
    for bi, gi in chains:
        q = seqs[bi]
        ls = slice(gi * GW, (gi + 1) * GW)
        a_s, b_s, k_s, r_s, v_s = (stack(q[n][:, ls]) for n in ("abar", "btil", "ktil", "rbar", "v"))
        ch.append(dict(
            ar=jnp.concatenate([a_s, r_s], axis=0),
            bk=jnp.concatenate([b_s, k_s], axis=0),
            bkh=jnp.concatenate([stack(q["bhat"][:, ls]), stack(q["khat"][:, ls])], axis=0),
            v_s=v_s, g_last=q["g_last"][:, ls], s=s_ref[bi * NGRP + gi]))
    for c_ in ch:
        p = _dot_nt(c_["ar"], c_["bk"])
        c_["npow"] = jnp.where(s_lower, p[0:GL, 0:GL], 0.0)
        c_["a_ak"] = jnp.where(s_lower, p[0:GL, GL:2 * GL], 0.0).astype(BF16)
        c_["r_b"] = jnp.where(i_lower, p[GL:2 * GL, 0:GL], 0.0).astype(BF16)
        c_["r_k"] = jnp.where(i_lower, p[GL:2 * GL, GL:2 * GL], 0.0).astype(BF16)
    for c_ in ch:
        c_["q0"] = _dot_nt(c_["ar"], c_["s"].astype(BF16))
    for c_ in ch:
        c_["u"] = c_["q0"][0:GL] + _dot(c_["a_ak"], c_["v_s"])
    lvls = L.bit_length() - 1
    for lvl in range(lvls):
        for c_ in ch:
            nb16 = c_["npow"].astype(BF16)
            c_["u"] = c_["u"] + _dot(nb16, c_["u"].astype(BF16))
            if lvl < lvls - 1:
                c_["npow"] = _dot(nb16, nb16)
    for c_ in ch:
        y = (c_["q0"][GL:2 * GL] + _dot(c_["r_b"], c_["u"].astype(BF16)) + _dot(c_["r_k"], c_["v_s"]))
        yg = y[0:L]
        for j in range(1, RW_GROUP):
            yg = yg + y[j * L:(j + 1) * L]
        c_["yg"] = yg
    for (bi, gi), c_ in zip(chains, ch):
        uv_t = jnp.concatenate([c_["u"], c_["v_s"].astype(F32)], axis=0).T.astype(BF16)
        s_ref[bi * NGRP + gi] = c_["s"] * c_["g_last"] + _dot(uv_t, c_["bkh"])
    y_rows = [jnp.concatenate([ch[bi * NGRP + gi]["yg"] for gi in range(NGRP)], axis=1)[0:tr]
              for bi in range(nb)]

    y_all = jnp.concatenate(y_rows, axis=0) if nb > 1 else y_rows[0]
    inv = 1.0 / RW_HD
    mu_ = _dot_exact_rhs(y_all, seg, 2) * inv
    dy = y_all - mu_
    var = _dot_exact_rhs(dy * dy, seg, 2) * inv
    yn = dy * lax.rsqrt(var + GN_EPS) * gng_ref[...] + gnb_ref[...]
    out_ref[...] = ((yn + bonus) * g).reshape(nb, tr, W)

    @pl.when(ci == nc - 1)
    def _state_out():
        for bi in range(nb):
            for gi in range(NGRP):
                s = s_ref[bi * NGRP + gi]
                f = s[:, 0:RW_HD]
                for j in range(1, RW_GROUP):
                    f = f + s[:, j * RW_HD:(j + 1) * RW_HD]
                s1_ref[bi, gi * GW:(gi + 1) * GW, :] = f


def _rwkv(cols, shift0, s0, mu, w0, wwa, a0, g2, kk, ka, rk, gng, gnb, tr, nb, chunk):
    b, t, _ = cols.shape
    nc = t // tr
    assert b % nb == 0 and t % tr == 0 and tr <= chunk
    const2 = lambda i, j: (0, 0)
    vec = pl.BlockSpec((1, RW_WIDTH), const2)
    return pl.pallas_call(
        functools.partial(_rwkv_kernel, tr=tr, nb=nb, L=chunk),
        grid=(b // nb, nc),
        in_specs=[
            pl.BlockSpec((nb, tr, RW_COLS), lambda i, j: (i, j, 0)),
            pl.BlockSpec((nb, 1, RW_COLS), lambda i, j: (i, 0, 0)),
            pl.BlockSpec((nb, RW_WIDTH, RW_HD), lambda i, j: (i, 0, 0)),
            pl.BlockSpec((1, RW_COLS), const2),
            vec,
            pl.BlockSpec((DECAY_LORA + AAA_LORA, 2 * RW_WIDTH), const2),
            vec,
            pl.BlockSpec((GATE_LORA, RW_WIDTH), const2),
            vec, vec, vec, vec, vec,
        ],
        out_specs=[
            pl.BlockSpec((nb, tr, RW_WIDTH), lambda i, j: (i, j, 0)),
            pl.BlockSpec((nb, RW_WIDTH, RW_HD), lambda i, j: (i, 0, 0)),
        ],
        out_shape=[
            jax.ShapeDtypeStruct((b, t, RW_WIDTH), F32),
            jax.ShapeDtypeStruct((b, RW_WIDTH, RW_HD), F32),
        ],
        scratch_shapes=[pltpu.VMEM((nb * (RW_HEADS // RW_GROUP), RW_GROUP * RW_HD, RW_GROUP * RW_HD), F32),
                        pltpu.VMEM((nb, 8, RW_COLS), F32)],
        compiler_params=pltpu.CompilerParams(
            dimension_semantics=("arbitrary", "arbitrary"), vmem_limit_bytes=VMEM_LIMIT),
        name="rwkv",
    )(cols, shift0, s0, mu, w0, wwa, a0, g2, kk, ka, rk, gng, gnb)


def _extract_top(works, idx, n):
    ranks = [jnp.full(w.shape, 99.0, F32) for w in works]
    vals = [[] for _ in works]
    for j in range(n):
        ms = [jnp.max(w, axis=0, keepdims=True) for w in works]
        if idx is None:
            sels = [w == m for w, m in zip(works, ms)]
        else:
            firsts = [jnp.min(jnp.where(w == m, idx, 1e9), axis=0, keepdims=True) for w, m in zip(works, ms)]
            sels = [idx == f for f in firsts]
        ranks = [jnp.where(sel, float(j), r) for sel, r in zip(sels, ranks)]
        works = [jnp.where(sel, -jnp.inf, w) for sel, w in zip(sels, works)]
        for v, m in zip(vals, ms):
            v.append(m)
    return vals, ranks


def _miscount(rank, n):
    taken = jnp.sum(jnp.where(rank < float(n), 1.0, 0.0), axis=0, keepdims=True)
    return jnp.abs(taken - float(n))


def _rows_bf16(row, rows):
    packed = jnp.broadcast_to(row, (16, row.shape[1])).astype(BF16)
    return jnp.concatenate([packed] * (rows // 16), axis=0)


def _peer_kernel(x_ref, ml_ref, rw_ref, woml_ref, worw_ref, gffn_ref, gfin_ref, wqt_ref, keys_ref,
                 ua_ref, ub_ref, vta_ref, vtb_ref, y_ref,
                 ht_ref, qt_ref, rank_ref, e_ref, vals_ref, r2b_ref, e2b_ref, cnt_ref, gate_ref,
                 p_ref, acc_ref, *, tm, ec):
    e = pl.program_id(1)
    ne = pl.num_programs(1)
    ng = tm // LANE
    K = PEER_TOPK
    nslab = ec // N_KEYS
    act_ref = qt_ref

    @pl.when(e == 0)
    def _select():
        x1 = (x_ref[...] + _dot(ml_ref[...].astype(BF16), woml_ref[...])
              + _dot(rw_ref[...].astype(BF16), worw_ref[...]))
        y_ref[...] = x1
        h = x1 * lax.rsqrt(jnp.mean(x1 * x1, axis=-1, keepdims=True) + NORM_EPS) * gffn_ref[...]
        ht = h.T.astype(BF16)
        ht_ref[...] = ht
        qt_ref[...] = _dot(wqt_ref[...], ht)
        acc_ref[...] = jnp.zeros(acc_ref.shape, F32)

        def score_body(hp, carry):
            q = qt_ref[pl.ds(pl.multiple_of(hp * PEER_HALF, PEER_HALF), PEER_HALF), :]
            e_ref[hp] = _dot(keys_ref[hp], q.astype(BF16))
            return carry

        lax.fori_loop(0, 2 * PEER_HEADS, score_body, 0)

        key_idx = _iota((N_KEYS, LANE), 0).astype(F32)
        groups = [slice(gi * LANE, (gi + 1) * LANE) for gi in range(ng)]

        def topk_groups(hp, exact_ties):
            scores = [e_ref[hp, :, lanes] for lanes in groups]
            vals, ranks = _extract_top(scores, key_idx if exact_ties else None, K)
            bad = None
            for lanes, v, rank in zip(groups, vals, ranks):
                rank_ref[hp, :, lanes] = rank
                vals_ref[hp, :, lanes] = jnp.concatenate(v, axis=0)
                miss = _miscount(rank, K)
                bad = miss if bad is None else jnp.maximum(bad, miss)
            return bad

        def topk_body(hp, carry):
            bad = topk_groups(hp, False)

            @pl.when(jnp.max(bad) > 0.0)
            def _redo():
                topk_groups(hp, True)

            return carry

        lax.fori_loop(0, 2 * PEER_HEADS, topk_body, 0)

        sub8 = _iota((8, LANE), 0)

        def cand_groups(h, exact_ties):
            works, v1s = [], []
            slab_a = [a_i for a_i in range(K) for _ in range(0, K // (a_i + 1), 8)]
            for lanes in groups:
                v1 = vals_ref[2 * h, :, lanes]
                v2 = vals_ref[2 * h + 1, :, lanes]
                slabs = []
                for a_i in range(K):
                    nb = K // (a_i + 1)
                    for b0 in range(0, nb, 8):
                        rows = v1[a_i:a_i + 1, :] + v2[b0:b0 + 8, :]
                        slabs.append(jnp.where(sub8 + b0 < nb, rows, -jnp.inf))
                works.append(jnp.concatenate(slabs, axis=0))
                v1s.append(v1)
            idx = None
            if exact_ties:
                idx = jnp.concatenate([(sub8 + (b0 + a_i * K)).astype(F32) for a_i in range(K)
                                       for b0 in range(0, K // (a_i + 1), 8)], axis=0)
            vals, ranks = _extract_top(works, idx, K)
            bad = None
            for lanes, v1, v, rank in zip(groups, v1s, vals, ranks):
                z = jnp.ones_like(v[0])
                for j in range(1, K):
                    z = z + jnp.exp(v[j] - v[0])
                picked = jnp.where(rank < float(K), 1.0, 0.0)
                r1 = rank_ref[2 * h, :, lanes]
                cnt = jnp.zeros((N_KEYS, LANE), F32)
                for a_i in range(K):
                    ca = None
                    for si, sa in enumerate(slab_a):
                        if sa == a_i:
                            part = jnp.sum(picked[si * 8:(si + 1) * 8], axis=0, keepdims=True)
                            ca = part if ca is None else ca + part
                    cnt = jnp.where(r1 == float(a_i), ca, cnt)
                cnt_ref[h, :, lanes] = cnt
                e1 = jnp.exp(e_ref[2 * h, :, lanes] - v1[0:1, :])
                gate_ref[h, :, lanes] = jnp.where(r1 < float(K), e1 / z, 0.0)
                miss = _miscount(rank, K)
                bad = miss if bad is None else jnp.maximum(bad, miss)
            return bad

        def cand_body(h, carry):
            bad = cand_groups(h, False)

            @pl.when(jnp.max(bad) > 0.0)
            def _redo():
                cand_groups(h, True)

            r2b_ref[h] = rank_ref[2 * h + 1].astype(BF16)
            e2b_ref[h] = jnp.exp(e_ref[2 * h + 1] - vals_ref[2 * h + 1, 0:1, :]).astype(BF16)
            return carry

        lax.fori_loop(0, PEER_HEADS, cand_body, 0)

    key1 = pl.ds(pl.multiple_of(e * nslab, nslab), nslab)
    hs = nslab // 2
    hrows = hs * N_KEYS
    for half in range(2):
        act_ref[half * hrows:(half + 1) * hrows, :] = _dot((ua_ref, ub_ref)[half][...], ht_ref[...])
    def gate_half(half):
        for gi in range(ng):
            lanes = slice(gi * LANE, (gi + 1) * LANE)
            w = [jnp.zeros((N_KEYS, LANE), BF16) for _ in range(hs)]
            for h in range(PEER_HEADS):
                r2 = r2b_ref[h, :, lanes]
                e2 = e2b_ref[h, :, lanes]
                cw = cnt_ref[h, key1, lanes]
                gw = gate_ref[h, key1, lanes]
                for jj in range(hs):
                    j = half * hs + jj
                    hit = jnp.clip(_rows_bf16(cw[j:j + 1, :], N_KEYS) - r2, 0.0, 1.0)
                    w[jj] = w[jj] + (e2 * hit) * _rows_bf16(gw[j:j + 1, :], N_KEYS)
            for jj in range(hs):
                rows = slice((half * hs + jj) * N_KEYS, (half * hs + jj + 1) * N_KEYS)
                act = act_ref[rows, lanes]
                gelu = 0.5 * act * (1.0 + lax.erf(act * 0.7071067811865476))
                p_ref[rows, lanes] = w[jj] * gelu.astype(BF16)

    for half in range(2):
        gate_half(half)
        acc_ref[...] += _dot((vta_ref, vtb_ref)[half][0], p_ref[half * hrows:(half + 1) * hrows, :])

    @pl.when(e == ne - 1)
    def _finish():
        x2 = y_ref[...] + acc_ref[...].T
        y_ref[...] = x2 * lax.rsqrt(jnp.mean(x2 * x2, axis=-1, keepdims=True) + NORM_EPS) * gfin_ref[...]


def _peer(x2d, ml2d, rw2d, woml, worw, gffn, gfin, wqt, keys, u, vt, tm, ec):
    n = x2d.shape[0]
    n_exp = u.shape[0]
    assert ec == 8 * N_KEYS and tm % LANE == 0 and n % tm == 0 and n_exp % ec == 0
    tok = lambda i, e: (i, 0)
    const2 = lambda i, e: (0, 0)
    qrows = 2 * PEER_HEADS * PEER_HALF
    eh = ec // 2
    once = pl.Buffered(1)
    return pl.pallas_call(
        functools.partial(_peer_kernel, tm=tm, ec=ec),
        grid=(n // tm, n_exp // ec),
        in_specs=[
            pl.BlockSpec((tm, D_MODEL), tok),
            pl.BlockSpec((tm, ML_WIDTH), tok),
            pl.BlockSpec((tm, RW_WIDTH), tok),
            pl.BlockSpec((ML_WIDTH, D_MODEL), const2, pipeline_mode=once),
            pl.BlockSpec((RW_WIDTH, D_MODEL), const2, pipeline_mode=once),
            pl.BlockSpec((1, D_MODEL), const2),
            pl.BlockSpec((1, D_MODEL), const2),
            pl.BlockSpec((qrows, D_MODEL), const2, pipeline_mode=once),
            pl.BlockSpec((2 * PEER_HEADS, N_KEYS, PEER_HALF), lambda i, e: (0, 0, 0), pipeline_mode=once),
            pl.BlockSpec((eh, D_MODEL), lambda i, e: (2 * e, 0)),
            pl.BlockSpec((eh, D_MODEL), lambda i, e: (2 * e + 1, 0)),
            pl.BlockSpec((1, D_MODEL, eh), lambda i, e: (2 * e, 0, 0)),
            pl.BlockSpec((1, D_MODEL, eh), lambda i, e: (2 * e + 1, 0, 0)),
        ],
        out_specs=pl.BlockSpec((tm, D_MODEL), tok),
        out_shape=jax.ShapeDtypeStruct((n, D_MODEL), F32),
        scratch_shapes=[
            pltpu.VMEM((D_MODEL, tm), BF16),
            pltpu.VMEM((qrows, tm), F32),
            pltpu.VMEM((2 * PEER_HEADS, N_KEYS, tm), F32),
            pltpu.VMEM((2 * PEER_HEADS, N_KEYS, tm), F32),
            pltpu.VMEM((2 * PEER_HEADS, PEER_TOPK, tm), F32),
            pltpu.VMEM((PEER_HEADS, N_KEYS, tm), BF16),
            pltpu.VMEM((PEER_HEADS, N_KEYS, tm), BF16),
            pltpu.VMEM((PEER_HEADS, N_KEYS, tm), F32),
            pltpu.VMEM((PEER_HEADS, N_KEYS, tm), F32),
            pltpu.VMEM((ec, tm), BF16),
            pltpu.VMEM((D_MODEL, tm), F32),
        ],
        compiler_params=pltpu.CompilerParams(
            dimension_semantics=("arbitrary", "arbitrary"), vmem_limit_bytes=VMEM_LIMIT),
        name="peer",
    )(x2d, ml2d, rw2d, woml, worw, gffn, gfin, wqt, keys, u, u, vt, vt)


def _prep_weights(norm_mix_g, w_in, ml_conv_w, ml_conv_b, ml_b_i, ml_b_f, ml_norm_g,
                  rw_mu, rw_w0, rw_w2, rw_a0, rw_a2, rw_g2, rw_k_k, rw_k_a, rw_r_k, rw_gn_g, rw_gn_b,
                  w_out, norm_ffn_g, peer_w_q, peer_sub_keys, peer_u, peer_v, norm_final_g):
    assert w_in.shape[0] == 1, "one layer"
    w = w_in[0]
    wg = jnp.pad(w[:, ML_QKVO:ML_QKVO + 2 * ML_HEADS], ((0, 0), (0, LANE - 2 * ML_HEADS)))
    wg_hi = wg.astype(BF16)
    wg_lo = (wg - wg_hi.astype(F32)).astype(BF16)
    zeros = jnp.zeros((DECAY_LORA, RW_WIDTH), F32)
    wwa = jnp.concatenate([jnp.concatenate([rw_w2[0], zeros], axis=1),
                           jnp.concatenate([zeros, rw_a2[0]], axis=1)], axis=0)
    gate_bias = jnp.pad(jnp.concatenate([ml_b_i[0], ml_b_f[0]]), (0, LANE - 2 * ML_HEADS))[None, :]
    row = lambda a: a.reshape(1, -1)
    return dict(
        g_mix=row(norm_mix_g[0]),
        wml=w[:, 0:ML_QKVO].astype(BF16),
        wg=jnp.stack([wg_hi, wg_lo]),
        wrw=w[:, ML_QKVO + 2 * ML_HEADS:].astype(BF16),
        conv_w=ml_conv_w[0], conv_b=row(ml_conv_b[0]), gate_bias=gate_bias, ml_norm_g=row(ml_norm_g[0]),
        mu=row(rw_mu[0]), w0=row(rw_w0[0]), wwa=wwa.astype(BF16), a0=row(rw_a0[0]),
        g2=rw_g2[0].astype(BF16), k_k=row(rw_k_k[0]), k_a=row(rw_k_a[0]), r_k=row(rw_r_k[0]),
        gn_g=row(rw_gn_g[0]), gn_b=row(rw_gn_b[0]),
        wo_ml=w_out[0, 0:ML_WIDTH].astype(BF16), wo_rw=w_out[0, ML_WIDTH:].astype(BF16),
        g_ffn=row(norm_ffn_g[0]), g_fin=row(norm_final_g),
        wqt=peer_w_q[0].T.astype(BF16),
        keys=peer_sub_keys[0].reshape(2 * PEER_HEADS, N_KEYS, PEER_HALF).astype(BF16),
        u=peer_u[0].astype(BF16),
        vt=peer_v[0].reshape(-1, PEER_EXPERT_CHUNK // 2, D_MODEL).transpose(0, 2, 1).astype(BF16),
    )


def _rwkv_tiling(t):
    tr = min(t, RW_CHUNK)
    chunk = max(tr, RW_MIN_CHUNK)
    return tr, chunk, RW_STEP_ROWS // chunk // 2 if chunk < RW_CHUNK else RW_STEP_ROWS // chunk


def _mlstm_tiling(t, b):
    tr = min(t, ML_CHUNK)
    chunk = max(tr, ML_MIN_CHUNK)
    return tr, min(b, ML_STEP_ROWS // chunk), chunk


def _trunk(x, states, wp, tr_rw, rw_chunk, rw_batch, tm_in, tm_peer, ec):
    b, t, d = x.shape
    c0, n0, m0, conv0, s0, shift0 = (s[0] for s in states)
    x2d = x.reshape(b * t, d)
    ml, gates, rw = _proj_in(x2d, wp["g_mix"], wp["wml"], wp["wg"], wp["wrw"], tm_in)
    ml3 = ml.reshape(b, t, ML_QKVO)
    rw3 = rw.reshape(b, t, RW_COLS)
    ml_out, c1, n1, m1, conv1 = _mlstm(
        ml3, gates.reshape(b, t, LANE), c0, n0, m0.reshape(b, 1, ML_HEADS), conv0,
        wp["conv_w"], wp["conv_b"], wp["gate_bias"], wp["ml_norm_g"], *_mlstm_tiling(t, b))
    rw_out, s1 = _rwkv(
        rw3, shift0[:, None, :], s0.reshape(b, RW_WIDTH, RW_HD), wp["mu"], wp["w0"], wp["wwa"], wp["a0"], wp["g2"],
        wp["k_k"], wp["k_a"], wp["r_k"], wp["gn_g"], wp["gn_b"], tr_rw, min(b, rw_batch), rw_chunk)
    y = _peer(x2d, ml_out.reshape(b * t, ML_WIDTH), rw_out.reshape(b * t, RW_WIDTH),
              wp["wo_ml"], wp["wo_rw"], wp["g_ffn"], wp["g_fin"], wp["wqt"], wp["keys"],
              wp["u"], wp["vt"], tm_peer, ec)
    new_states = (c1[None], n1[None], m1.reshape(b, ML_HEADS)[None], conv1[None],
                  s1.reshape(b, RW_HEADS, RW_HD, RW_HD)[None], rw3[:, -1][None])
    return y.reshape(b, t, d), new_states


def kernel(x_prompt, x_sample, state_mlstm_C, state_mlstm_n, state_mlstm_m, state_mlstm_conv, state_rwkv_S, state_rwkv_shift, norm_mix_g, w_in, ml_conv_w, ml_conv_b, ml_b_i, ml_b_f, ml_norm_g, rw_mu, rw_w0, rw_w2, rw_a0, rw_a2, rw_g2, rw_k_k, rw_k_a, rw_r_k, rw_gn_g, rw_gn_b, w_out, norm_ffn_g, peer_w_q, peer_sub_keys, peer_u, peer_v, norm_final_g):
    wp = _prep_weights(norm_mix_g, w_in, ml_conv_w, ml_conv_b, ml_b_i, ml_b_f, ml_norm_g,
                       rw_mu, rw_w0, rw_w2, rw_a0, rw_a2, rw_g2, rw_k_k, rw_k_a, rw_r_k, rw_gn_g, rw_gn_b,
                       w_out, norm_ffn_g, peer_w_q, peer_sub_keys, peer_u, peer_v, norm_final_g)
    bp = x_prompt.shape[0]
    z = lambda *s: jnp.zeros((1, bp) + s, F32)
    prompt_states = (z(ML_HEADS, ML_HD, ML_HD), z(ML_HEADS, ML_HD), jnp.full((1, bp, ML_HEADS), M_INIT, F32),
                     z(CONV_W - 1, 2 * ML_WIDTH), z(RW_HEADS, RW_HD, RW_HD), z(RW_COLS))
    sample_states = (state_mlstm_C, state_mlstm_n, state_mlstm_m, state_mlstm_conv,
                     state_rwkv_S, state_rwkv_shift)
    tp, ts = x_prompt.shape[1], x_sample.shape[1]
    np_, ns = bp * tp, x_sample.shape[0] * ts
    y_p, st_p = _trunk(x_prompt, prompt_states, wp, *_rwkv_tiling(tp),
                       min(np_, PROJ_TILE), min(np_, PEER_TILE), PEER_EXPERT_CHUNK)
    y_s, st_s = _trunk(x_sample, sample_states, wp, *_rwkv_tiling(ts),
                       min(ns, PROJ_TILE), min(ns, PEER_TILE), PEER_EXPERT_CHUNK)
    return (y_p, y_s) + tuple(st_p) + tuple(st_s)
```

```python
import functools

import jax
import jax.numpy as jnp
from jax import lax
from jax.experimental import pallas as pl
from jax.experimental.pallas import tpu as pltpu

F32 = jnp.float32
BF16 = jnp.bfloat16

D_MODEL = 1024
ML_HEADS = 4
ML_HD = 128
ML_WIDTH = ML_HEADS * ML_HD
CONV_W = 4
RW_HEADS = 8
RW_HD = 64
RW_WIDTH = RW_HEADS * RW_HD
DECAY_LORA = 64
AAA_LORA = 64
GATE_LORA = 128
RW_COLS = 3 * RW_WIDTH + DECAY_LORA + AAA_LORA + GATE_LORA
ML_QKVO = 4 * ML_WIDTH
PEER_HEADS = 8
N_KEYS = 128
PEER_TOPK = 16
PEER_HALF = 128
NORM_EPS = 1e-6
ML_NORM_EPS = 1e-6
GN_EPS = RW_HD * 1e-5
M_INIT = -1e30
NEG_BIG = -1e30

LANE = 128
ML_CHUNK = 128
ML_MIN_CHUNK = 16
ML_STEP_ROWS = 128
RW_CHUNK = 64
RW_GROUP = 4
RW_MIN_CHUNK = 16
RW_STEP_ROWS = 256
PROJ_TILE = 512
PEER_TILE = 512
PEER_EXPERT_CHUNK = 1024
VMEM_LIMIT = 56 * 1024 * 1024


def _dot(a, b):
    return jnp.dot(a, b, preferred_element_type=F32)


def _dot_nt(a, b):
    return lax.dot_general(a, b, (((1,), (1,)), ((), ())), preferred_element_type=F32)


def _split_bf16(x, n):
    parts = []
    r = x
    for _ in range(n):
        p = r.astype(BF16)
        parts.append(p)
        r = r - p.astype(F32)
    return parts


def _dot_exact_lhs(mask_bf16, x, n):
    return sum(_dot(mask_bf16, p) for p in _split_bf16(x, n))


def _dot_exact_rhs(x, mask_bf16, n):
    return sum(_dot(p, mask_bf16) for p in _split_bf16(x, n))


def _sigmoid(x):
    return 1.0 / (1.0 + jnp.exp(-x))


def _softplus(x):
    return jnp.maximum(x, 0.0) + jnp.log1p(jnp.exp(-jnp.abs(x)))


def _iota(shape, dim):
    return lax.broadcasted_iota(jnp.int32, shape, dim)


def _pad_rows(x, rows, value=0.0):
    if x.shape[0] == rows:
        return x
    return jnp.concatenate([x, jnp.full((rows - x.shape[0], x.shape[1]), value, x.dtype)], axis=0)


def _proj_in_kernel(x_ref, g_ref, wml_ref, wg_ref, wrw_ref, ml_ref, gate_ref, rw_ref):
    x = x_ref[...]
    xn = x * lax.rsqrt(jnp.mean(x * x, axis=-1, keepdims=True) + NORM_EPS) * g_ref[...]
    xb = xn.astype(BF16)
    ml_ref[...] = _dot(xb, wml_ref[...])
    rw_ref[...] = _dot(xb, wrw_ref[...])
    xlo = (xn - xb.astype(F32)).astype(BF16)
    gate_ref[...] = _dot(xb, wg_ref[0]) + _dot(xlo, wg_ref[0]) + _dot(xb, wg_ref[1])


def _proj_in(x2d, g, wml, wg, wrw, tm):
    n = x2d.shape[0]
    const2 = lambda i: (0, 0)
    return pl.pallas_call(
        _proj_in_kernel,
        grid=(n // tm,),
        in_specs=[
            pl.BlockSpec((tm, D_MODEL), lambda i: (i, 0)),
            pl.BlockSpec((1, D_MODEL), const2),
            pl.BlockSpec((D_MODEL, ML_QKVO), const2),
            pl.BlockSpec((2, D_MODEL, LANE), lambda i: (0, 0, 0)),
            pl.BlockSpec((D_MODEL, RW_COLS), const2),
        ],
        out_specs=[
            pl.BlockSpec((tm, ML_QKVO), lambda i: (i, 0)),
            pl.BlockSpec((tm, LANE), lambda i: (i, 0)),
            pl.BlockSpec((tm, RW_COLS), lambda i: (i, 0)),
        ],
        out_shape=[
            jax.ShapeDtypeStruct((n, ML_QKVO), F32),
            jax.ShapeDtypeStruct((n, LANE), F32),
            jax.ShapeDtypeStruct((n, RW_COLS), F32),
        ],
        compiler_params=pltpu.CompilerParams(
            dimension_semantics=("arbitrary",), vmem_limit_bytes=VMEM_LIMIT),
        name="proj_in",
    )(x2d, g, wml, wg, wrw)


def _mlstm_kernel(ml_ref, gate_ref, c0_ref, n0_ref, m0_ref, conv0_ref, cw_ref, cb_ref, gb_ref, ng_ref,
                  out_ref, c1_ref, n1_ref, m1_ref, conv1_ref,
                  caug_ref, m_ref, ext_ref, *, tr, nb, L):
    c = pl.program_id(1)
    nc = pl.num_programs(1)

    @pl.when(c == 0)
    def _init():
        m_ref[...] = jnp.zeros(m_ref.shape, F32)
        for bi in range(nb):
            for h in range(ML_HEADS):
                caug_ref[bi * ML_HEADS + h, :, 0:ML_HD] = c0_ref[bi, h]
                nrow = n0_ref[bi, h:h + 1, :]
                caug_ref[bi * ML_HEADS + h, :, ML_HD:2 * ML_HD] = jnp.broadcast_to(nrow, (ML_HD, ML_HD)).T
            m_ref[bi:bi + 1, 0:ML_HEADS] = m0_ref[bi]
            ext_ref[bi, 5:8, :] = conv0_ref[bi]

    row = _iota((L, L), 0)
    col = _iota((L, L), 1)
    causal = row >= col
    tri = jnp.where(causal, 1.0, 0.0).astype(BF16)
    ones = jnp.ones((L, ML_HD), BF16)

    seqs = []
    for bi in range(nb):
        ext_ref[bi, 8:8 + tr, :] = ml_ref[bi, :, 0:2 * ML_WIDTH]
        acc = cb_ref[...] + ext_ref[bi, 5:5 + tr, :] * cw_ref[0:1, :]
        for j in range(1, CONV_W):
            acc = acc + ext_ref[bi, 5 + j:5 + j + tr, :] * cw_ref[j:j + 1, :]
        qk = acc * _sigmoid(acc)
        tail = ext_ref[bi, tr + 5:tr + 8, :]
        ext_ref[bi, 5:8, :] = tail

        @pl.when(c == nc - 1)
        def _conv_out():
            conv1_ref[bi] = tail

        g = gate_ref[bi] + gb_ref[...]
        i_all = _pad_rows(g, L, NEG_BIG)
        b_col = _dot_exact_lhs(tri, _pad_rows(-_softplus(-g), L, 0.0), 3)
        seqs.append(dict(
            q=_pad_rows(qk[:, 0:ML_WIDTH], L),
            k=_pad_rows(qk[:, ML_WIDTH:2 * ML_WIDTH] * (ML_HD ** -0.5), L),
            v=_pad_rows(ml_ref[bi, :, 2 * ML_WIDTH:3 * ML_WIDTH], L),
            b_col=b_col, b_t=b_col.T, i_t=i_all.T))

    chains = [(bi, h) for bi in range(nb) for h in range(ML_HEADS)]
    ch = []
    for bi, h in chains:
        q = seqs[bi]
        sl = slice(h * ML_HD, (h + 1) * ML_HD)
        bc = q["b_col"][:, ML_HEADS + h:ML_HEADS + h + 1]
        br = q["b_t"][ML_HEADS + h:ML_HEADS + h + 1, :]
        ir = q["i_t"][h:h + 1, :]
        m_prev = m_ref[bi:bi + 1, h:h + 1]
        logd = jnp.where(causal, bc - br + ir, -jnp.inf)
        linter = bc + m_prev
        m_t = jnp.maximum(linter, jnp.max(logd, axis=1, keepdims=True))
        m_new = m_t[L - 1:L, :]
        b_last = bc[L - 1:L, :]
        ch.append(dict(
            sl=sl, m_t=m_t, d=jnp.exp(logd - m_t), s_inter=jnp.exp(linter - m_t), m_new=m_new,
            s_state=jnp.exp(b_last + m_prev - m_new), w_row=jnp.exp(b_last - br + ir - m_new),
            qh=q["q"][:, sl].astype(BF16), k_t=q["k"][:, sl].T,
            vaug=jnp.concatenate([q["v"][:, sl].astype(BF16), ones], axis=1),
            caug=caug_ref[bi * ML_HEADS + h]))
    for c_ in ch:
        c_["s"] = (_dot(c_["qh"], c_["k_t"].astype(BF16)) * c_["d"]).astype(BF16)
    for c_ in ch:
        c_["num"] = c_["s_inter"] * _dot(c_["qh"], c_["caug"].astype(BF16)) + _dot(c_["s"], c_["vaug"])
    for (bi, h), c_ in zip(chains, ch):
        num = c_["num"]
        den = num[:, ML_HD:2 * ML_HD]
        hh = num[:, 0:ML_HD] / jnp.maximum(jnp.abs(den), jnp.exp(-c_["m_t"]))
        mu = jnp.mean(hh, axis=-1, keepdims=True)
        dv = hh - mu
        var = jnp.mean(dv * dv, axis=-1, keepdims=True)
        y = dv * lax.rsqrt(var + ML_NORM_EPS) * ng_ref[:, c_["sl"]]
        o = ml_ref[bi, :, 3 * ML_WIDTH + h * ML_HD:3 * ML_WIDTH + (h + 1) * ML_HD]
        out_ref[bi, :, c_["sl"]] = y[0:tr] * _sigmoid(o)
    for (bi, h), c_ in zip(chains, ch):
        upd = _dot((c_["k_t"] * c_["w_row"]).astype(BF16), c_["vaug"])
        caug_ref[bi * ML_HEADS + h] = c_["s_state"] * c_["caug"] + upd
        m_ref[bi:bi + 1, h:h + 1] = c_["m_new"]

    @pl.when(c == nc - 1)
    def _state_out():
        for bi in range(nb):
            for h in range(ML_HEADS):
                caug = caug_ref[bi * ML_HEADS + h]
                c1_ref[bi, h] = caug[:, 0:ML_HD]
                n1_ref[bi, h:h + 1, :] = caug[:, ML_HD:2 * ML_HD].T[0:1, :]
            m1_ref[bi] = m_ref[bi:bi + 1, 0:ML_HEADS]


def _mlstm(ml, gates, c0, n0, m0, conv0, cw, cb, gb, ng, tr, nb, chunk):
    b, t, _ = ml.shape
    nc = t // tr
    assert b % nb == 0 and t % tr == 0 and nb <= 8 and tr <= chunk
    bmap = lambda i, j: (i, 0, 0)
    const2 = lambda i, j: (0, 0)
    return pl.pallas_call(
        functools.partial(_mlstm_kernel, tr=tr, nb=nb, L=chunk),
        grid=(b // nb, nc),
        in_specs=[
            pl.BlockSpec((nb, tr, ML_QKVO), lambda i, j: (i, j, 0)),
            pl.BlockSpec((nb, tr, LANE), lambda i, j: (i, j, 0)),
            pl.BlockSpec((nb, ML_HEADS, ML_HD, ML_HD), lambda i, j: (i, 0, 0, 0)),
            pl.BlockSpec((nb, ML_HEADS, ML_HD), bmap),
            pl.BlockSpec((nb, 1, ML_HEADS), bmap),
            pl.BlockSpec((nb, CONV_W - 1, 2 * ML_WIDTH), bmap),
            pl.BlockSpec((CONV_W, 2 * ML_WIDTH), const2),
            pl.BlockSpec((1, 2 * ML_WIDTH), const2),
            pl.BlockSpec((1, LANE), const2),
            pl.BlockSpec((1, ML_WIDTH), const2),
        ],
        out_specs=[
            pl.BlockSpec((nb, tr, ML_WIDTH), lambda i, j: (i, j, 0)),
            pl.BlockSpec((nb, ML_HEADS, ML_HD, ML_HD), lambda i, j: (i, 0, 0, 0)),
            pl.BlockSpec((nb, ML_HEADS, ML_HD), bmap),
            pl.BlockSpec((nb, 1, ML_HEADS), bmap),
            pl.BlockSpec((nb, CONV_W - 1, 2 * ML_WIDTH), bmap),
        ],
        out_shape=[
            jax.ShapeDtypeStruct((b, t, ML_WIDTH), F32),
            jax.ShapeDtypeStruct((b, ML_HEADS, ML_HD, ML_HD), F32),
            jax.ShapeDtypeStruct((b, ML_HEADS, ML_HD), F32),
            jax.ShapeDtypeStruct((b, 1, ML_HEADS), F32),
            jax.ShapeDtypeStruct((b, CONV_W - 1, 2 * ML_WIDTH), F32),
        ],
        scratch_shapes=[
            pltpu.VMEM((nb * ML_HEADS, ML_HD, 2 * ML_HD), F32),
            pltpu.VMEM((8, LANE), F32),
            pltpu.VMEM((nb, tr + 8, 2 * ML_WIDTH), F32),
        ],
        compiler_params=pltpu.CompilerParams(
            dimension_semantics=("arbitrary", "arbitrary"), vmem_limit_bytes=VMEM_LIMIT),
        name="mlstm",
    )(ml, gates, c0, n0, m0, conv0, cw, cb, gb, ng)


def _rwkv_kernel(c_ref, sh_ref, s0_ref, mu_ref, w0_ref, wwa_ref, a0_ref, g2_ref, kk_ref, ka_ref, rk_ref,
                 gng_ref, gnb_ref, out_ref, s1_ref, s_ref, last_ref, *, tr, nb, L):
    W = RW_WIDTH
    GW = RW_GROUP * RW_HD
    GL = RW_GROUP * L
    NGRP = RW_HEADS // RW_GROUP
    ci = pl.program_id(1)
    nc = pl.num_programs(1)

    r2 = _iota((GL, GW), 0)
    c2 = _iota((GL, GW), 1)
    bd = (r2 // L) == (c2 // RW_HD)
    rg = _iota((GL, GL), 0)
    cg = _iota((GL, GL), 1)
    s_lower = rg > cg
    i_lower = rg >= cg

    @pl.when(ci == 0)
    def _init():
        sbd = (_iota((GW, GW), 0) // RW_HD) == (_iota((GW, GW), 1) // RW_HD)
        for bi in range(nb):
            last_ref[bi, 0:1, :] = sh_ref[bi]
            for gi in range(NGRP):
                x = s0_ref[bi, gi * GW:(gi + 1) * GW, :]
                s_ref[bi * NGRP + gi] = jnp.where(sbd, jnp.concatenate([x] * RW_GROUP, axis=1), 0.0)

    c = c_ref[...].reshape(nb * tr, RW_COLS)
    prev = pltpu.roll(c, 1, axis=0)
    rowid = _iota(c.shape, 0)
    for bi in range(nb):
        prev = jnp.where(rowid == bi * tr, last_ref[bi, 0:1, :], prev)
        last_ref[bi, 0:1, :] = c[(bi + 1) * tr - 1:(bi + 1) * tr, :]
    xs = c + (prev - c) * mu_ref[...]
    r = xs[:, 0:W]
    k = xs[:, W:2 * W]
    v = xs[:, 2 * W:3 * W]
    slab = xs[:, 3 * W:3 * W + DECAY_LORA + AAA_LORA]
    gd = xs[:, 3 * W + DECAY_LORA + AAA_LORA:]
    lane = _iota(slab.shape, 1)
    t_in = jnp.where(lane < DECAY_LORA, jnp.tanh(slab), slab)
    la = _dot(t_in.astype(BF16), wwa_ref[...])
    w_log = -_softplus(-(w0_ref[...] + la[:, 0:W])) - 0.5
    lw_all = -jnp.exp(w_log)
    a_all = _sigmoid(a0_ref[...] + la[:, W:2 * W])
    g = _dot(_sigmoid(gd).astype(BF16), g2_ref[...])

    rs = _iota((W, W), 0)
    cs = _iota((W, W), 1)
    seg = jnp.where((rs // RW_HD) == (cs // RW_HD), 1.0, 0.0).astype(BF16)

    kk = k * kk_ref[...]
    kn_all = k * (1.0 + (a_all - 1.0) * ka_ref[...])
    ss = _dot_exact_rhs(kk * kk, seg, 2)
    kap_all = kk / jnp.maximum(jnp.sqrt(ss), 1e-12)
    bonus = _dot_exact_rhs(r * kn_all * rk_ref[...], seg, 2) * v

    rl = _iota((L, L), 0)
    cl_ = _iota((L, L), 1)
    tri = jnp.where(rl >= cl_, 1.0, 0.0).astype(BF16)

    def stack(x):
        return jnp.where(bd, jnp.concatenate([x] * RW_GROUP, axis=0), 0.0).astype(BF16)

    seqs = []
    for bi in range(nb):
        rows = slice(bi * tr, (bi + 1) * tr)
        lw = _pad_rows(lw_all[rows], L)
        kap = _pad_rows(kap_all[rows], L)
        kn_p = _pad_rows(kn_all[rows], L)
        a_p = _pad_rows(a_all[rows], L)
        v_p = _pad_rows(v[rows], L)
        r_p = _pad_rows(r[rows], L)
        cum = _dot_exact_lhs(tri, lw, 3)
        cum_last = cum[L - 1:L, :]
        e_neg = jnp.exp(-cum)
        e_rem = jnp.exp(cum_last - cum)
        seqs.append(dict(
            abar=-kap * jnp.exp(cum - lw), btil=kap * a_p * e_neg, ktil=kn_p * e_neg, rbar=r_p * jnp.exp(cum),
            bhat=kap * a_p * e_rem, khat=kn_p * e_rem, v=v_p, g_last=jnp.exp(cum_last)))

    chains = [(bi, gi) for bi in range(nb) for gi in range(NGRP)]
    ch = []
    for bi, gi in chains:
        q = seqs[bi]
        ls = slice(gi * GW, (gi + 1) * GW)
        a_s, b_s, k_s, r_s, v_s = (stack(q[n][:, ls]) for n in ("abar", "btil", "ktil", "rbar", "v"))
        ch.append(dict(
            ar=jnp.concatenate([a_s, r_s], axis=0),
            bk=jnp.concatenate([b_s, k_s], axis=0),
            bkh=jnp.concatenate([stack(q["bhat"][:, ls]), stack(q["khat"][:, ls])], axis=0),
            v_s=v_s, g_last=q["g_last"][:, ls], s=s_ref[bi * NGRP + gi]))
    for c_ in ch:
        p = _dot_nt(c_["ar"], c_["bk"])
        c_["npow"] = jnp.where(s_lower, p[0:GL, 0:GL], 0.0)
        c_["a_ak"] = jnp.where(s_lower, p[0:GL, GL:2 * GL], 0.0).astype(BF16)
        c_["r_b"] = jnp.where(i_lower, p[GL:2 * GL, 0:GL], 0.0).astype(BF16)
        c_["r_k"] = jnp.where(i_lower, p[GL:2 * GL, GL:2 * GL], 0.0).astype(BF16)
    for c_ in ch:
        c_["q0"] = _dot_nt(c_["ar"], c_["s"].astype(BF16))
    for c_ in ch:
        c_["u"] = c_["q0"][0:GL] + _dot(c_["a_ak"], c_["v_s"])
    lvls = L.bit_length() - 1
    for lvl in range(lvls):
        for c_ in ch:
            nb16 = c_["npow"].astype(BF16)
            c_["u"] = c_["u"] + _dot(nb16, c_["u"].astype(BF16))
            if lvl < lvls - 1:
                c_["npow"] = _dot(nb16, nb16)
    for c_ in ch:
        y = (c_["q0"][GL:2 * GL] + _dot(c_["r_b"], c_["u"].astype(BF16)) + _dot(c_["r_k"], c_["v_s"]))
        yg = y[0:L]
        for j in range(1, RW_GROUP):
            yg = yg + y[j * L:(j + 1) * L]
        c_["yg"] = yg
    for (bi, gi), c_ in zip(chains, ch):
        uv_t = jnp.concatenate([c_["u"], c_["v_s"].astype(F32)], axis=0).T.astype(BF16)
        s_ref[bi * NGRP + gi] = c_["s"] * c_["g_last"] + _dot(uv_t, c_["bkh"])
    y_rows = [jnp.concatenate([ch[bi * NGRP + gi]["yg"] for gi in range(NGRP)], axis=1)[0:tr]
              for bi in range(nb)]

    y_all = jnp.concatenate(y_rows, axis=0) if nb > 1 else y_rows[0]
    inv = 1.0 / RW_HD
    mu_ = _dot_exact_rhs(y_all, seg, 2) * inv
    dy = y_all - mu_
    var = _dot_exact_rhs(dy * dy, seg, 2) * inv
    yn = dy * lax.rsqrt(var + GN_EPS) * gng_ref[...] + gnb_ref[...]
    out_ref[...] = ((yn + bonus) * g).reshape(nb, tr, W)

    @pl.when(ci == nc - 1)
    def _state_out():
        for bi in range(nb):
            for gi in range(NGRP):
                s = s_ref[bi * NGRP + gi]
                f = s[:, 0:RW_HD]
                for j in range(1, RW_GROUP):
                    f = f + s[:, j * RW_HD:(j + 1) * RW_HD]
                s1_ref[bi, gi * GW:(gi + 1) * GW, :] = f


def _rwkv(cols, shift0, s0, mu, w0, wwa, a0, g2, kk, ka, rk, gng, gnb, tr, nb, chunk):
    b, t, _ = cols.shape
    nc = t // tr
    assert b % nb == 0 and t % tr == 0 and tr <= chunk
    const2 = lambda i, j: (0, 0)
    vec = pl.BlockSpec((1, RW_WIDTH), const2)
    return pl.pallas_call(
        functools.partial(_rwkv_kernel, tr=tr, nb=nb, L=chunk),
        grid=(b // nb, nc),
        in_specs=[
            pl.BlockSpec((nb, tr, RW_COLS), lambda i, j: (i, j, 0)),
            pl.BlockSpec((nb, 1, RW_COLS), lambda i, j: (i, 0, 0)),
            pl.BlockSpec((nb, RW_WIDTH, RW_HD), lambda i, j: (i, 0, 0)),
            pl.BlockSpec((1, RW_COLS), const2),
            vec,
            pl.BlockSpec((DECAY_LORA + AAA_LORA, 2 * RW_WIDTH), const2),
            vec,
            pl.BlockSpec((GATE_LORA, RW_WIDTH), const2),
            vec, vec, vec, vec, vec,
        ],
        out_specs=[
            pl.BlockSpec((nb, tr, RW_WIDTH), lambda i, j: (i, j, 0)),
            pl.BlockSpec((nb, RW_WIDTH, RW_HD), lambda i, j: (i, 0, 0)),
        ],
        out_shape=[
            jax.ShapeDtypeStruct((b, t, RW_WIDTH), F32),
            jax.ShapeDtypeStruct((b, RW_WIDTH, RW_HD), F32),
        ],
        scratch_shapes=[pltpu.VMEM((nb * (RW_HEADS // RW_GROUP), RW_GROUP * RW_HD, RW_GROUP * RW_HD), F32),
                        pltpu.VMEM((nb, 8, RW_COLS), F32)],
        compiler_params=pltpu.CompilerParams(
            dimension_semantics=("arbitrary", "arbitrary"), vmem_limit_bytes=VMEM_LIMIT),
        name="rwkv",
    )(cols, shift0, s0, mu, w0, wwa, a0, g2, kk, ka, rk, gng, gnb)


def _extract_top(works, idx, n):
    ranks = [jnp.full(w.shape, 99.0, F32) for w in works]
    vals = [[] for _ in works]
    for j in range(n):
        ms = [jnp.max(w, axis=0, keepdims=True) for w in works]
        if idx is None:
            sels = [w == m for w, m in zip(works, ms)]
        else:
            firsts = [jnp.min(jnp.where(w == m, idx, 1e9), axis=0, keepdims=True) for w, m in zip(works, ms)]
            sels = [idx == f for f in firsts]
        ranks = [jnp.where(sel, float(j), r) for sel, r in zip(sels, ranks)]
        works = [jnp.where(sel, -jnp.inf, w) for sel, w in zip(sels, works)]
        for v, m in zip(vals, ms):
            v.append(m)
    return vals, ranks


def _miscount(rank, n):
    taken = jnp.sum(jnp.where(rank < float(n), 1.0, 0.0), axis=0, keepdims=True)
    return jnp.abs(taken - float(n))


def _rows_bf16(row, rows):
    packed = jnp.broadcast_to(row, (16, row.shape[1])).astype(BF16)
    return jnp.concatenate([packed] * (rows // 16), axis=0)


def _peer_kernel(x_ref, ml_ref, rw_ref, woml_ref, worw_ref, gffn_ref, gfin_ref, wqt_ref, keys_ref,
                 ua_ref, ub_ref, vta_ref, vtb_ref, y_ref,
                 ht_ref, qt_ref, rank_ref, e_ref, vals_ref, r2b_ref, e2b_ref, cnt_ref, gate_ref,
                 p_ref, acc_ref, *, tm, ec):
    e = pl.program_id(1)
    ne = pl.num_programs(1)
    ng = tm // LANE
    K = PEER_TOPK
    nslab = ec // N_KEYS
    act_ref = qt_ref

    @pl.when(e == 0)
    def _select():
        x1 = (x_ref[...] + _dot(ml_ref[...].astype(BF16), woml_ref[...])
              + _dot(rw_ref[...].astype(BF16), worw_ref[...]))
        y_ref[...] = x1
        h = x1 * lax.rsqrt(jnp.mean(x1 * x1, axis=-1, keepdims=True) + NORM_EPS) * gffn_ref[...]
        ht = h.T.astype(BF16)
        ht_ref[...] = ht
        qt_ref[...] = _dot(wqt_ref[...], ht)
        acc_ref[...] = jnp.zeros(acc_ref.shape, F32)

        def score_body(hp, carry):
            q = qt_ref[pl.ds(pl.multiple_of(hp * PEER_HALF, PEER_HALF), PEER_HALF), :]
            e_ref[hp] = _dot(keys_ref[hp], q.astype(BF16))
            return carry

        lax.fori_loop(0, 2 * PEER_HEADS, score_body, 0)

        key_idx = _iota((N_KEYS, LANE), 0).astype(F32)
        groups = [slice(gi * LANE, (gi + 1) * LANE) for gi in range(ng)]

        def topk_groups(hp, exact_ties):
            scores = [e_ref[hp, :, lanes] for lanes in groups]
            vals, ranks = _extract_top(scores, key_idx if exact_ties else None, K)
            bad = None
            for lanes, v, rank in zip(groups, vals, ranks):
                rank_ref[hp, :, lanes] = rank
                vals_ref[hp, :, lanes] = jnp.concatenate(v, axis=0)
                miss = _miscount(rank, K)
                bad = miss if bad is None else jnp.maximum(bad, miss)
            return bad

        def topk_body(hp, carry):
            bad = topk_groups(hp, False)

            @pl.when(jnp.max(bad) > 0.0)
            def _redo():
                topk_groups(hp, True)

            return carry

        lax.fori_loop(0, 2 * PEER_HEADS, topk_body, 0)

        sub8 = _iota((8, LANE), 0)

        def cand_groups(h, exact_ties):
            works, v1s = [], []
            slab_a = [a_i for a_i in range(K) for _ in range(0, K // (a_i + 1), 8)]
            for lanes in groups:
                v1 = vals_ref[2 * h, :, lanes]
                v2 = vals_ref[2 * h + 1, :, lanes]
                slabs = []
                for a_i in range(K):
                    nb = K // (a_i + 1)
                    for b0 in range(0, nb, 8):
                        rows = v1[a_i:a_i + 1, :] + v2[b0:b0 + 8, :]
                        slabs.append(jnp.where(sub8 + b0 < nb, rows, -jnp.inf))
                works.append(jnp.concatenate(slabs, axis=0))
                v1s.append(v1)
            idx = None
            if exact_ties:
                idx = jnp.concatenate([(sub8 + (b0 + a_i * K)).astype(F32) for a_i in range(K)
                                       for b0 in range(0, K // (a_i + 1), 8)], axis=0)
            vals, ranks = _extract_top(works, idx, K)
            bad = None
            for lanes, v1, v, rank in zip(groups, v1s, vals, ranks):
                z = jnp.ones_like(v[0])
                for j in range(1, K):
                    z = z + jnp.exp(v[j] - v[0])
                picked = jnp.where(rank < float(K), 1.0, 0.0)
                r1 = rank_ref[2 * h, :, lanes]
                cnt = jnp.zeros((N_KEYS, LANE), F32)
                for a_i in range(K):
                    ca = None
                    for si, sa in enumerate(slab_a):
                        if sa == a_i:
                            part = jnp.sum(picked[si * 8:(si + 1) * 8], axis=0, keepdims=True)
                            ca = part if ca is None else ca + part
                    cnt = jnp.where(r1 == float(a_i), ca, cnt)
                cnt_ref[h, :, lanes] = cnt
                e1 = jnp.exp(e_ref[2 * h, :, lanes] - v1[0:1, :])
                gate_ref[h, :, lanes] = jnp.where(r1 < float(K), 0.5 * e1 / z, 0.0)
                miss = _miscount(rank, K)
                bad = miss if bad is None else jnp.maximum(bad, miss)
            return bad

        def cand_body(h, carry):
            bad = cand_groups(h, False)

            @pl.when(jnp.max(bad) > 0.0)
            def _redo():
                cand_groups(h, True)

            r2b_ref[h] = rank_ref[2 * h + 1].astype(BF16)
            e2b_ref[h] = jnp.exp(e_ref[2 * h + 1] - vals_ref[2 * h + 1, 0:1, :]).astype(BF16)
            return carry

        lax.fori_loop(0, PEER_HEADS, cand_body, 0)

    key1 = pl.ds(pl.multiple_of(e * nslab, nslab), nslab)
    hs = nslab // 2
    hrows = hs * N_KEYS
    for half in range(2):
        act_ref[half * hrows:(half + 1) * hrows, :] = _dot((ua_ref, ub_ref)[half][...], ht_ref[...])
    def gate_half(half):
        for gi in range(ng):
            lanes = slice(gi * LANE, (gi + 1) * LANE)
            w = [jnp.zeros((N_KEYS, LANE), BF16) for _ in range(hs)]
            for h in range(PEER_HEADS):
                r2 = r2b_ref[h, :, lanes]
                e2 = e2b_ref[h, :, lanes]
                cw = cnt_ref[h, key1, lanes]
                gw = gate_ref[h, key1, lanes]
                for jj in range(hs):
                    j = half * hs + jj
                    hit = r2 < _rows_bf16(cw[j:j + 1, :], N_KEYS)
                    w[jj] = w[jj] + jnp.where(hit, e2, jnp.zeros_like(e2)) * _rows_bf16(gw[j:j + 1, :], N_KEYS)
            for jj in range(hs):
                rows = slice((half * hs + jj) * N_KEYS, (half * hs + jj + 1) * N_KEYS)
                act = act_ref[rows, lanes]
                ab = act.astype(BF16)
                p_ref[rows, lanes] = (w[jj] * ab) * (1.0 + lax.erf(ab * 0.7071067811865476))

    for half in range(2):
        gate_half(half)
        acc_ref[...] += _dot((vta_ref, vtb_ref)[half][0], p_ref[half * hrows:(half + 1) * hrows, :])

    @pl.when(e == ne - 1)
    def _finish():
        x2 = y_ref[...] + acc_ref[...].T
        y_ref[...] = x2 * lax.rsqrt(jnp.mean(x2 * x2, axis=-1, keepdims=True) + NORM_EPS) * gfin_ref[...]


def _peer(x2d, ml2d, rw2d, woml, worw, gffn, gfin, wqt, keys, u, vt, tm, ec):
    n = x2d.shape[0]
    n_exp = u.shape[0]
    assert ec == 8 * N_KEYS and tm % LANE == 0 and n % tm == 0 and n_exp % ec == 0
    tok = lambda i, e: (i, 0)
    const2 = lambda i, e: (0, 0)
    qrows = 2 * PEER_HEADS * PEER_HALF
    eh = ec // 2
    once = pl.Buffered(1)
    return pl.pallas_call(
        functools.partial(_peer_kernel, tm=tm, ec=ec),
        grid=(n // tm, n_exp // ec),
        in_specs=[
            pl.BlockSpec((tm, D_MODEL), tok),
            pl.BlockSpec((tm, ML_WIDTH), tok),
            pl.BlockSpec((tm, RW_WIDTH), tok),
            pl.BlockSpec((ML_WIDTH, D_MODEL), const2, pipeline_mode=once),
            pl.BlockSpec((RW_WIDTH, D_MODEL), const2, pipeline_mode=once),
            pl.BlockSpec((1, D_MODEL), const2),
            pl.BlockSpec((1, D_MODEL), const2),
            pl.BlockSpec((qrows, D_MODEL), const2, pipeline_mode=once),
            pl.BlockSpec((2 * PEER_HEADS, N_KEYS, PEER_HALF), lambda i, e: (0, 0, 0), pipeline_mode=once),
            pl.BlockSpec((eh, D_MODEL), lambda i, e: (2 * e, 0)),
            pl.BlockSpec((eh, D_MODEL), lambda i, e: (2 * e + 1, 0)),
            pl.BlockSpec((1, D_MODEL, eh), lambda i, e: (2 * e, 0, 0)),
            pl.BlockSpec((1, D_MODEL, eh), lambda i, e: (2 * e + 1, 0, 0)),
        ],
        out_specs=pl.BlockSpec((tm, D_MODEL), tok),
        out_shape=jax.ShapeDtypeStruct((n, D_MODEL), F32),
        scratch_shapes=[
            pltpu.VMEM((D_MODEL, tm), BF16),
            pltpu.VMEM((qrows, tm), F32),
            pltpu.VMEM((2 * PEER_HEADS, N_KEYS, tm), F32),
            pltpu.VMEM((2 * PEER_HEADS, N_KEYS, tm), F32),
            pltpu.VMEM((2 * PEER_HEADS, PEER_TOPK, tm), F32),
            pltpu.VMEM((PEER_HEADS, N_KEYS, tm), BF16),
            pltpu.VMEM((PEER_HEADS, N_KEYS, tm), BF16),
            pltpu.VMEM((PEER_HEADS, N_KEYS, tm), F32),
            pltpu.VMEM((PEER_HEADS, N_KEYS, tm), F32),
            pltpu.VMEM((ec, tm), BF16),
            pltpu.VMEM((D_MODEL, tm), F32),
        ],
        compiler_params=pltpu.CompilerParams(
            dimension_semantics=("arbitrary", "arbitrary"), vmem_limit_bytes=VMEM_LIMIT),
        name="peer",
    )(x2d, ml2d, rw2d, woml, worw, gffn, gfin, wqt, keys, u, u, vt, vt)


def _prep_weights(norm_mix_g, w_in, ml_conv_w, ml_conv_b, ml_b_i, ml_b_f, ml_norm_g,
                  rw_mu, rw_w0, rw_w2, rw_a0, rw_a2, rw_g2, rw_k_k, rw_k_a, rw_r_k, rw_gn_g, rw_gn_b,
                  w_out, norm_ffn_g, peer_w_q, peer_sub_keys, peer_u, peer_v, norm_final_g):
    assert w_in.shape[0] == 1, "one layer"
    w = w_in[0]
    wg = jnp.pad(w[:, ML_QKVO:ML_QKVO + 2 * ML_HEADS], ((0, 0), (0, LANE - 2 * ML_HEADS)))
    wg_hi = wg.astype(BF16)
    wg_lo = (wg - wg_hi.astype(F32)).astype(BF16)
    zeros = jnp.zeros((DECAY_LORA, RW_WIDTH), F32)
    wwa = jnp.concatenate([jnp.concatenate([rw_w2[0], zeros], axis=1),
                           jnp.concatenate([zeros, rw_a2[0]], axis=1)], axis=0)
    gate_bias = jnp.pad(jnp.concatenate([ml_b_i[0], ml_b_f[0]]), (0, LANE - 2 * ML_HEADS))[None, :]
    row = lambda a: a.reshape(1, -1)
    return dict(
        g_mix=row(norm_mix_g[0]),
        wml=w[:, 0:ML_QKVO].astype(BF16),
        wg=jnp.stack([wg_hi, wg_lo]),
        wrw=w[:, ML_QKVO + 2 * ML_HEADS:].astype(BF16),
        conv_w=ml_conv_w[0], conv_b=row(ml_conv_b[0]), gate_bias=gate_bias, ml_norm_g=row(ml_norm_g[0]),
        mu=row(rw_mu[0]), w0=row(rw_w0[0]), wwa=wwa.astype(BF16), a0=row(rw_a0[0]),
        g2=rw_g2[0].astype(BF16), k_k=row(rw_k_k[0]), k_a=row(rw_k_a[0]), r_k=row(rw_r_k[0]),
        gn_g=row(rw_gn_g[0]), gn_b=row(rw_gn_b[0]),
        wo_ml=w_out[0, 0:ML_WIDTH].astype(BF16), wo_rw=w_out[0, ML_WIDTH:].astype(BF16),
        g_ffn=row(norm_ffn_g[0]), g_fin=row(norm_final_g),
        wqt=peer_w_q[0].T.astype(BF16),
        keys=peer_sub_keys[0].reshape(2 * PEER_HEADS, N_KEYS, PEER_HALF).astype(BF16),
        u=peer_u[0].astype(BF16),
        vt=peer_v[0].astype(BF16).reshape(-1, PEER_EXPERT_CHUNK // 2, D_MODEL).transpose(0, 2, 1),
    )


def _rwkv_tiling(t):
    tr = min(t, RW_CHUNK)
    chunk = max(tr, RW_MIN_CHUNK)
    return tr, chunk, RW_STEP_ROWS // chunk // 2 if chunk < RW_CHUNK else RW_STEP_ROWS // chunk


def _mlstm_tiling(t, b):
    tr = min(t, ML_CHUNK)
    chunk = max(tr, ML_MIN_CHUNK)
    return tr, min(b, ML_STEP_ROWS // chunk), chunk


def _trunk(x, states, wp, tr_rw, rw_chunk, rw_batch, tm_in, tm_peer, ec):
    b, t, d = x.shape
    c0, n0, m0, conv0, s0, shift0 = (s[0] for s in states)
    x2d = x.reshape(b * t, d)
    ml, gates, rw = _proj_in(x2d, wp["g_mix"], wp["wml"], wp["wg"], wp["wrw"], tm_in)
    ml3 = ml.reshape(b, t, ML_QKVO)
    rw3 = rw.reshape(b, t, RW_COLS)
    ml_out, c1, n1, m1, conv1 = _mlstm(
        ml3, gates.reshape(b, t, LANE), c0, n0, m0.reshape(b, 1, ML_HEADS), conv0,
        wp["conv_w"], wp["conv_b"], wp["gate_bias"], wp["ml_norm_g"], *_mlstm_tiling(t, b))
    rw_out, s1 = _rwkv(
        rw3, shift0[:, None, :], s0.reshape(b, RW_WIDTH, RW_HD), wp["mu"], wp["w0"], wp["wwa"], wp["a0"], wp["g2"],
        wp["k_k"], wp["k_a"], wp["r_k"], wp["gn_g"], wp["gn_b"], tr_rw, min(b, rw_batch), rw_chunk)
    y = _peer(x2d, ml_out.reshape(b * t, ML_WIDTH), rw_out.reshape(b * t, RW_WIDTH),
              wp["wo_ml"], wp["wo_rw"], wp["g_ffn"], wp["g_fin"], wp["wqt"], wp["keys"],
              wp["u"], wp["vt"], tm_peer, ec)
    new_states = (c1[None], n1[None], m1.reshape(b, ML_HEADS)[None], conv1[None],
                  s1.reshape(b, RW_HEADS, RW_HD, RW_HD)[None], rw3[:, -1][None])
    return y.reshape(b, t, d), new_states


def kernel(x_prompt, x_sample, state_mlstm_C, state_mlstm_n, state_mlstm_m, state_mlstm_conv, state_rwkv_S, state_rwkv_shift, norm_mix_g, w_in, ml_conv_w, ml_conv_b, ml_b_i, ml_b_f, ml_norm_g, rw_mu, rw_w0, rw_w2, rw_a0, rw_a2, rw_g2, rw_k_k, rw_k_a, rw_r_k, rw_gn_g, rw_gn_b, w_out, norm_ffn_g, peer_w_q, peer_sub_keys, peer_u, peer_v, norm_final_g):
    wp = _prep_weights(norm_mix_g, w_in, ml_conv_w, ml_conv_b, ml_b_i, ml_b_f, ml_norm_g,
                       rw_mu, rw_w0, rw_w2, rw_a0, rw_a2, rw_g2, rw_k_k, rw_k_a, rw_r_k, rw_gn_g, rw_gn_b,
                       w_out, norm_ffn_g, peer_w_q, peer_sub_keys, peer_u, peer_v, norm_final_g)
    bp = x_prompt.shape[0]
    z = lambda *s: jnp.zeros((1, bp) + s, F32)
    prompt_states = (z(ML_HEADS, ML_HD, ML_HD), z(ML_HEADS, ML_HD), jnp.full((1, bp, ML_HEADS), M_INIT, F32),
                     z(CONV_W - 1, 2 * ML_WIDTH), z(RW_HEADS, RW_HD, RW_HD), z(RW_COLS))
    sample_states = (state_mlstm_C, state_mlstm_n, state_mlstm_m, state_mlstm_conv,
                     state_rwkv_S, state_rwkv_shift)
    tp, ts = x_prompt.shape[1], x_sample.shape[1]
    np_, ns = bp * tp, x_sample.shape[0] * ts
    y_p, st_p = _trunk(x_prompt, prompt_states, wp, *_rwkv_tiling(tp),
                       min(np_, PROJ_TILE), min(np_, PEER_TILE), PEER_EXPERT_CHUNK)
    y_s, st_s = _trunk(x_sample, sample_states, wp, *_rwkv_tiling(ts),
                       min(ns, PROJ_TILE), min(ns, PEER_TILE), PEER_EXPERT_CHUNK)
    return (y_p, y_s) + tuple(st_p) + tuple(st_s)
```

```python
import functools

import jax
import jax.numpy as jnp
from jax import lax
from jax.experimental import pallas as pl
from jax.experimental.pallas import tpu as pltpu

F32 = jnp.float32
BF16 = jnp.bfloat16

D_MODEL = 1024
ML_HEADS = 4
ML_HD = 128
ML_WIDTH = ML_HEADS * ML_HD
CONV_W = 4
RW_HEADS = 8
RW_HD = 64
RW_WIDTH = RW_HEADS * RW_HD
DECAY_LORA = 64
AAA_LORA = 64
GATE_LORA = 128
RW_COLS = 3 * RW_WIDTH + DECAY_LORA + AAA_LORA + GATE_LORA
ML_QKVO = 4 * ML_WIDTH
PEER_HEADS = 8
N_KEYS = 128
PEER_TOPK = 16
PEER_HALF = 128
NORM_EPS = 1e-6
ML_NORM_EPS = 1e-6
GN_EPS = RW_HD * 1e-5
M_INIT = -1e30
NEG_BIG = -1e30

LANE = 128
ML_CHUNK = 128
ML_MIN_CHUNK = 16
ML_STEP_ROWS = 128
RW_CHUNK = 64
RW_GROUP = 4
RW_MIN_CHUNK = 16
RW_STEP_ROWS = 256
PROJ_TILE = 512
PEER_TILE = 512
PEER_EXPERT_CHUNK = 1024
VMEM_LIMIT = 56 * 1024 * 1024


def _dot(a, b):
    return jnp.dot(a, b, preferred_element_type=F32)


def _dot_nt(a, b):
    return lax.dot_general(a, b, (((1,), (1,)), ((), ())), preferred_element_type=F32)


def _split_bf16(x, n):
    parts = []
    r = x
    for _ in range(n):
        p = r.astype(BF16)
        parts.append(p)
        r = r - p.astype(F32)
    return parts


def _dot_exact_lhs(mask_bf16, x, n):
    return sum(_dot(mask_bf16, p) for p in _split_bf16(x, n))


def _dot_exact_rhs(x, mask_bf16, n):
    return sum(_dot(p, mask_bf16) for p in _split_bf16(x, n))


def _sigmoid(x):
    return 1.0 / (1.0 + jnp.exp(-x))


def _softplus(x):
    return jnp.maximum(x, 0.0) + jnp.log1p(jnp.exp(-jnp.abs(x)))


def _iota(shape, dim):
    return lax.broadcasted_iota(jnp.int32, shape, dim)


def _pad_rows(x, rows, value=0.0):
    if x.shape[0] == rows:
        return x
    return jnp.concatenate([x, jnp.full((rows - x.shape[0], x.shape[1]), value, x.dtype)], axis=0)


def _proj_in_kernel(x_ref, g_ref, wml_ref, wg_ref, wrw_ref, ml_ref, gate_ref, rw_ref):
    x = x_ref[...]
    xn = x * lax.rsqrt(jnp.mean(x * x, axis=-1, keepdims=True) + NORM_EPS) * g_ref[...]
    xb = xn.astype(BF16)
    ml_ref[...] = _dot(xb, wml_ref[...])
    rw_ref[...] = _dot(xb, wrw_ref[...])
    xlo = (xn - xb.astype(F32)).astype(BF16)
    gate_ref[...] = _dot(xb, wg_ref[0]) + _dot(xlo, wg_ref[0]) + _dot(xb, wg_ref[1])


def _proj_in(x2d, g, wml, wg, wrw, tm):
    n = x2d.shape[0]
    const2 = lambda i: (0, 0)
    return pl.pallas_call(
        _proj_in_kernel,
        grid=(n // tm,),
        in_specs=[
            pl.BlockSpec((tm, D_MODEL), lambda i: (i, 0)),
            pl.BlockSpec((1, D_MODEL), const2),
            pl.BlockSpec((D_MODEL, ML_QKVO), const2),
            pl.BlockSpec((2, D_MODEL, LANE), lambda i: (0, 0, 0)),
            pl.BlockSpec((D_MODEL, RW_COLS), const2),
        ],
        out_specs=[
            pl.BlockSpec((tm, ML_QKVO), lambda i: (i, 0)),
            pl.BlockSpec((tm, LANE), lambda i: (i, 0)),
            pl.BlockSpec((tm, RW_COLS), lambda i: (i, 0)),
        ],
        out_shape=[
            jax.ShapeDtypeStruct((n, ML_QKVO), F32),
            jax.ShapeDtypeStruct((n, LANE), F32),
            jax.ShapeDtypeStruct((n, RW_COLS), F32),
        ],
        compiler_params=pltpu.CompilerParams(
            dimension_semantics=("arbitrary",), vmem_limit_bytes=VMEM_LIMIT),
        name="proj_in",
    )(x2d, g, wml, wg, wrw)


def _mlstm_kernel(ml_ref, gate_ref, c0_ref, n0_ref, m0_ref, conv0_ref, cw_ref, cb_ref, gb_ref, ng_ref,
                  out_ref, c1_ref, n1_ref, m1_ref, conv1_ref,
                  caug_ref, m_ref, ext_ref, *, tr, nb, L):
    c = pl.program_id(1)
    nc = pl.num_programs(1)

    @pl.when(c == 0)
    def _init():
        m_ref[...] = jnp.zeros(m_ref.shape, F32)
        for bi in range(nb):
            for h in range(ML_HEADS):
                caug_ref[bi * ML_HEADS + h, :, 0:ML_HD] = c0_ref[bi, h]
                nrow = n0_ref[bi, h:h + 1, :]
                caug_ref[bi * ML_HEADS + h, :, ML_HD:2 * ML_HD] = jnp.broadcast_to(nrow, (ML_HD, ML_HD)).T
            m_ref[bi:bi + 1, 0:ML_HEADS] = m0_ref[bi]
            ext_ref[bi, 5:8, :] = conv0_ref[bi]

    row = _iota((L, L), 0)
    col = _iota((L, L), 1)
    causal = row >= col
    tri = jnp.where(causal, 1.0, 0.0).astype(BF16)
    ones = jnp.ones((L, ML_HD), BF16)

    seqs = []
    for bi in range(nb):
        ext_ref[bi, 8:8 + tr, :] = ml_ref[bi, :, 0:2 * ML_WIDTH]
        acc = cb_ref[...] + ext_ref[bi, 5:5 + tr, :] * cw_ref[0:1, :]
        for j in range(1, CONV_W):
            acc = acc + ext_ref[bi, 5 + j:5 + j + tr, :] * cw_ref[j:j + 1, :]
        qk = acc * _sigmoid(acc)
        tail = ext_ref[bi, tr + 5:tr + 8, :]
        ext_ref[bi, 5:8, :] = tail

        @pl.when(c == nc - 1)
        def _conv_out():
            conv1_ref[bi] = tail

        g = gate_ref[bi] + gb_ref[...]
        i_all = _pad_rows(g, L, NEG_BIG)
        b_col = _dot_exact_lhs(tri, _pad_rows(-_softplus(-g), L, 0.0), 3)
        seqs.append(dict(
            q=_pad_rows(qk[:, 0:ML_WIDTH], L),
            k=_pad_rows(qk[:, ML_WIDTH:2 * ML_WIDTH] * (ML_HD ** -0.5), L),
            v=_pad_rows(ml_ref[bi, :, 2 * ML_WIDTH:3 * ML_WIDTH], L),
            b_col=b_col, b_t=b_col.T, i_t=i_all.T))

    chains = [(bi, h) for bi in range(nb) for h in range(ML_HEADS)]
    ch = []
    for bi, h in chains:
        q = seqs[bi]
        sl = slice(h * ML_HD, (h + 1) * ML_HD)
        bc = q["b_col"][:, ML_HEADS + h:ML_HEADS + h + 1]
        br = q["b_t"][ML_HEADS + h:ML_HEADS + h + 1, :]
        ir = q["i_t"][h:h + 1, :]
        m_prev = m_ref[bi:bi + 1, h:h + 1]
        logd = jnp.where(causal, bc - br + ir, -jnp.inf)
        linter = bc + m_prev
        m_t = jnp.maximum(linter, jnp.max(logd, axis=1, keepdims=True))
        m_new = m_t[L - 1:L, :]
        b_last = bc[L - 1:L, :]
        ch.append(dict(
            sl=sl, m_t=m_t, d=jnp.exp(logd - m_t), s_inter=jnp.exp(linter - m_t), m_new=m_new,
            s_state=jnp.exp(b_last + m_prev - m_new), w_row=jnp.exp(b_last - br + ir - m_new),
            qh=q["q"][:, sl].astype(BF16), k_t=q["k"][:, sl].T,
            vaug=jnp.concatenate([q["v"][:, sl].astype(BF16), ones], axis=1),
            caug=caug_ref[bi * ML_HEADS + h]))
    for c_ in ch:
        c_["s"] = (_dot(c_["qh"], c_["k_t"].astype(BF16)) * c_["d"]).astype(BF16)
    for c_ in ch:
        c_["num"] = c_["s_inter"] * _dot(c_["qh"], c_["caug"].astype(BF16)) + _dot(c_["s"], c_["vaug"])
    for (bi, h), c_ in zip(chains, ch):
        num = c_["num"]
        den = num[:, ML_HD:2 * ML_HD]
        hh = num[:, 0:ML_HD] / jnp.maximum(jnp.abs(den), jnp.exp(-c_["m_t"]))
        mu = jnp.mean(hh, axis=-1, keepdims=True)
        dv = hh - mu
        var = jnp.mean(dv * dv, axis=-1, keepdims=True)
        y = dv * lax.rsqrt(var + ML_NORM_EPS) * ng_ref[:, c_["sl"]]
        o = ml_ref[bi, :, 3 * ML_WIDTH + h * ML_HD:3 * ML_WIDTH + (h + 1) * ML_HD]
        out_ref[bi, :, c_["sl"]] = y[0:tr] * _sigmoid(o)
    for (bi, h), c_ in zip(chains, ch):
        upd = _dot((c_["k_t"] * c_["w_row"]).astype(BF16), c_["vaug"])
        caug_ref[bi * ML_HEADS + h] = c_["s_state"] * c_["caug"] + upd
        m_ref[bi:bi + 1, h:h + 1] = c_["m_new"]

    @pl.when(c == nc - 1)
    def _state_out():
        for bi in range(nb):
            for h in range(ML_HEADS):
                caug = caug_ref[bi * ML_HEADS + h]
                c1_ref[bi, h] = caug[:, 0:ML_HD]
                n1_ref[bi, h:h + 1, :] = caug[:, ML_HD:2 * ML_HD].T[0:1, :]
            m1_ref[bi] = m_ref[bi:bi + 1, 0:ML_HEADS]


def _mlstm(ml, gates, c0, n0, m0, conv0, cw, cb, gb, ng, tr, nb, chunk):
    b, t, _ = ml.shape
    nc = t // tr
    assert b % nb == 0 and t % tr == 0 and nb <= 8 and tr <= chunk
    bmap = lambda i, j: (i, 0, 0)
    const2 = lambda i, j: (0, 0)
    return pl.pallas_call(
        functools.partial(_mlstm_kernel, tr=tr, nb=nb, L=chunk),
        grid=(b // nb, nc),
        in_specs=[
            pl.BlockSpec((nb, tr, ML_QKVO), lambda i, j: (i, j, 0)),
            pl.BlockSpec((nb, tr, LANE), lambda i, j: (i, j, 0)),
            pl.BlockSpec((nb, ML_HEADS, ML_HD, ML_HD), lambda i, j: (i, 0, 0, 0)),
            pl.BlockSpec((nb, ML_HEADS, ML_HD), bmap),
            pl.BlockSpec((nb, 1, ML_HEADS), bmap),
            pl.BlockSpec((nb, CONV_W - 1, 2 * ML_WIDTH), bmap),
            pl.BlockSpec((CONV_W, 2 * ML_WIDTH), const2),
            pl.BlockSpec((1, 2 * ML_WIDTH), const2),
            pl.BlockSpec((1, LANE), const2),
            pl.BlockSpec((1, ML_WIDTH), const2),
        ],
        out_specs=[
            pl.BlockSpec((nb, tr, ML_WIDTH), lambda i, j: (i, j, 0)),
            pl.BlockSpec((nb, ML_HEADS, ML_HD, ML_HD), lambda i, j: (i, 0, 0, 0)),
            pl.BlockSpec((nb, ML_HEADS, ML_HD), bmap),
            pl.BlockSpec((nb, 1, ML_HEADS), bmap),
            pl.BlockSpec((nb, CONV_W - 1, 2 * ML_WIDTH), bmap),
        ],
        out_shape=[
            jax.ShapeDtypeStruct((b, t, ML_WIDTH), F32),
            jax.ShapeDtypeStruct((b, ML_HEADS, ML_HD, ML_HD), F32),
            jax.ShapeDtypeStruct((b, ML_HEADS, ML_HD), F32),
            jax.ShapeDtypeStruct((b, 1, ML_HEADS), F32),
            jax.ShapeDtypeStruct((b, CONV_W - 1, 2 * ML_WIDTH), F32),
        ],
        scratch_shapes=[
            pltpu.VMEM((nb * ML_HEADS, ML_HD, 2 * ML_HD), F32),
            pltpu.VMEM((8, LANE), F32),
            pltpu.VMEM((nb, tr + 8, 2 * ML_WIDTH), F32),
        ],
        compiler_params=pltpu.CompilerParams(
            dimension_semantics=("arbitrary", "arbitrary"), vmem_limit_bytes=VMEM_LIMIT),
        name="mlstm",
    )(ml, gates, c0, n0, m0, conv0, cw, cb, gb, ng)


def _rwkv_kernel(c_ref, sh_ref, s0_ref, mu_ref, w0_ref, wwa_ref, a0_ref, g2_ref, kk_ref, ka_ref, rk_ref,
                 gng_ref, gnb_ref, out_ref, s1_ref, s_ref, last_ref, *, tr, nb, L):
    W = RW_WIDTH
    GW = RW_GROUP * RW_HD
    GL = RW_GROUP * L
    NGRP = RW_HEADS // RW_GROUP
    ci = pl.program_id(1)
    nc = pl.num_programs(1)

    r2 = _iota((GL, GW), 0)
    c2 = _iota((GL, GW), 1)
    bd = (r2 // L) == (c2 // RW_HD)
    rg = _iota((GL, GL), 0)
    cg = _iota((GL, GL), 1)
    s_lower = rg > cg
    i_lower = rg >= cg

    @pl.when(ci == 0)
    def _init():
        sbd = (_iota((GW, GW), 0) // RW_HD) == (_iota((GW, GW), 1) // RW_HD)
        for bi in range(nb):
            last_ref[bi, 0:1, :] = sh_ref[bi]
            for gi in range(NGRP):
                x = s0_ref[bi, gi * GW:(gi + 1) * GW, :]
                s_ref[bi * NGRP + gi] = jnp.where(sbd, jnp.concatenate([x] * RW_GROUP, axis=1), 0.0)

    c = c_ref[...].reshape(nb * tr, RW_COLS)
    prev = pltpu.roll(c, 1, axis=0)
    rowid = _iota(c.shape, 0)
    for bi in range(nb):
        prev = jnp.where(rowid == bi * tr, last_ref[bi, 0:1, :], prev)
        last_ref[bi, 0:1, :] = c[(bi + 1) * tr - 1:(bi + 1) * tr, :]
    xs = c + (prev - c) * mu_ref[...]
    r = xs[:, 0:W]
    k = xs[:, W:2 * W]
    v = xs[:, 2 * W:3 * W]
    slab = xs[:, 3 * W:3 * W + DECAY_LORA + AAA_LORA]
    gd = xs[:, 3 * W + DECAY_LORA + AAA_LORA:]
    lane = _iota(slab.shape, 1)
    t_in = jnp.where(lane < DECAY_LORA, jnp.tanh(slab), slab)
    la = _dot(t_in.astype(BF16), wwa_ref[...])
    w_log = -_softplus(-(w0_ref[...] + la[:, 0:W])) - 0.5
    lw_all = -jnp.exp(w_log)
    a_all = _sigmoid(a0_ref[...] + la[:, W:2 * W])
    g = _dot(_sigmoid(gd).astype(BF16), g2_ref[...])

    rs = _iota((W, W), 0)
    cs = _iota((W, W), 1)
    seg = jnp.where((rs // RW_HD) == (cs // RW_HD), 1.0, 0.0).astype(BF16)

    kk = k * kk_ref[...]
    kn_all = k * (1.0 + (a_all - 1.0) * ka_ref[...])
    ss = _dot_exact_rhs(kk * kk, seg, 2)
    kap_all = kk / jnp.maximum(jnp.sqrt(ss), 1e-12)
    bonus = _dot_exact_rhs(r * kn_all * rk_ref[...], seg, 2) * v

    rl = _iota((L, L), 0)
    cl_ = _iota((L, L), 1)
    tri = jnp.where(rl >= cl_, 1.0, 0.0).astype(BF16)

    def stack(x):
        return jnp.where(bd, jnp.concatenate([x] * RW_GROUP, axis=0), 0.0).astype(BF16)

    seqs = []
    for bi in range(nb):
        rows = slice(bi * tr, (bi + 1) * tr)
        lw = _pad_rows(lw_all[rows], L)
        kap = _pad_rows(kap_all[rows], L)
        kn_p = _pad_rows(kn_all[rows], L)
        a_p = _pad_rows(a_all[rows], L)
        v_p = _pad_rows(v[rows], L)
        r_p = _pad_rows(r[rows], L)
        cum = _dot_exact_lhs(tri, lw, 3)
        cum_last = cum[L - 1:L, :]
        e_neg = jnp.exp(-cum)
        e_rem = jnp.exp(cum_last - cum)
        seqs.append(dict(
            abar=-kap * jnp.exp(cum - lw), btil=kap * a_p * e_neg, ktil=kn_p * e_neg, rbar=r_p * jnp.exp(cum),
            bhat=kap * a_p * e_rem, khat=kn_p * e_rem, v=v_p, g_last=jnp.exp(cum_last)))

    chains = [(bi, gi) for bi in range(nb) for gi in range(NGRP)]
    ch = []
    for bi, gi in chains:
        q = seqs[bi]
        ls = slice(gi * GW, (gi + 1) * GW)
        a_s, b_s, k_s, r_s, v_s = (stack(q[n][:, ls]) for n in ("abar", "btil", "ktil", "rbar", "v"))
        ch.append(dict(
            ar=jnp.concatenate([a_s, r_s], axis=0),
            bk=jnp.concatenate([b_s, k_s], axis=0),
            bkh=jnp.concatenate([stack(q["bhat"][:, ls]), stack(q["khat"][:, ls])], axis=0),
            v_s=v_s, g_last=q["g_last"][:, ls], s=s_ref[bi * NGRP + gi]))
    for c_ in ch:
        p = _dot_nt(c_["ar"], c_["bk"])
        c_["npow"] = jnp.where(s_lower, p[0:GL, 0:GL], 0.0)
        c_["a_ak"] = jnp.where(s_lower, p[0:GL, GL:2 * GL], 0.0).astype(BF16)
        c_["r_b"] = jnp.where(i_lower, p[GL:2 * GL, 0:GL], 0.0).astype(BF16)
        c_["r_k"] = jnp.where(i_lower, p[GL:2 * GL, GL:2 * GL], 0.0).astype(BF16)
    for c_ in ch:
        c_["q0"] = _dot_nt(c_["ar"], c_["s"].astype(BF16))
    for c_ in ch:
        c_["u"] = c_["q0"][0:GL] + _dot(c_["a_ak"], c_["v_s"])
    lvls = L.bit_length() - 1
    for lvl in range(lvls):
        for c_ in ch:
            nb16 = c_["npow"].astype(BF16)
            c_["u"] = c_["u"] + _dot(nb16, c_["u"].astype(BF16))
            if lvl < lvls - 1:
                c_["npow"] = _dot(nb16, nb16)
    for c_ in ch:
        y = (c_["q0"][GL:2 * GL] + _dot(c_["r_b"], c_["u"].astype(BF16)) + _dot(c_["r_k"], c_["v_s"]))
        yg = y[0:L]
        for j in range(1, RW_GROUP):
            yg = yg + y[j * L:(j + 1) * L]
        c_["yg"] = yg
    for (bi, gi), c_ in zip(chains, ch):
        uv_t = jnp.concatenate([c_["u"], c_["v_s"].astype(F32)], axis=0).T.astype(BF16)
        s_ref[bi * NGRP + gi] = c_["s"] * c_["g_last"] + _dot(uv_t, c_["bkh"])
    y_rows = [jnp.concatenate([ch[bi * NGRP + gi]["yg"] for gi in range(NGRP)], axis=1)[0:tr]
              for bi in range(nb)]

    y_all = jnp.concatenate(y_rows, axis=0) if nb > 1 else y_rows[0]
    inv = 1.0 / RW_HD
    mu_ = _dot_exact_rhs(y_all, seg, 2) * inv
    dy = y_all - mu_
    var = _dot_exact_rhs(dy * dy, seg, 2) * inv
    yn = dy * lax.rsqrt(var + GN_EPS) * gng_ref[...] + gnb_ref[...]
    out_ref[...] = ((yn + bonus) * g).reshape(nb, tr, W)

    @pl.when(ci == nc - 1)
    def _state_out():
        for bi in range(nb):
            for gi in range(NGRP):
                s = s_ref[bi * NGRP + gi]
                f = s[:, 0:RW_HD]
                for j in range(1, RW_GROUP):
                    f = f + s[:, j * RW_HD:(j + 1) * RW_HD]
                s1_ref[bi, gi * GW:(gi + 1) * GW, :] = f


def _rwkv(cols, shift0, s0, mu, w0, wwa, a0, g2, kk, ka, rk, gng, gnb, tr, nb, chunk):
    b, t, _ = cols.shape
    nc = t // tr
    assert b % nb == 0 and t % tr == 0 and tr <= chunk
    const2 = lambda i, j: (0, 0)
    vec = pl.BlockSpec((1, RW_WIDTH), const2)
    return pl.pallas_call(
        functools.partial(_rwkv_kernel, tr=tr, nb=nb, L=chunk),
        grid=(b // nb, nc),
        in_specs=[
            pl.BlockSpec((nb, tr, RW_COLS), lambda i, j: (i, j, 0)),
            pl.BlockSpec((nb, 1, RW_COLS), lambda i, j: (i, 0, 0)),
            pl.BlockSpec((nb, RW_WIDTH, RW_HD), lambda i, j: (i, 0, 0)),
            pl.BlockSpec((1, RW_COLS), const2),
            vec,
            pl.BlockSpec((DECAY_LORA + AAA_LORA, 2 * RW_WIDTH), const2),
            vec,
            pl.BlockSpec((GATE_LORA, RW_WIDTH), const2),
            vec, vec, vec, vec, vec,
        ],
        out_specs=[
            pl.BlockSpec((nb, tr, RW_WIDTH), lambda i, j: (i, j, 0)),
            pl.BlockSpec((nb, RW_WIDTH, RW_HD), lambda i, j: (i, 0, 0)),
        ],
        out_shape=[
            jax.ShapeDtypeStruct((b, t, RW_WIDTH), F32),
            jax.ShapeDtypeStruct((b, RW_WIDTH, RW_HD), F32),
        ],
        scratch_shapes=[pltpu.VMEM((nb * (RW_HEADS // RW_GROUP), RW_GROUP * RW_HD, RW_GROUP * RW_HD), F32),
                        pltpu.VMEM((nb, 8, RW_COLS), F32)],
        compiler_params=pltpu.CompilerParams(
            dimension_semantics=("arbitrary", "arbitrary"), vmem_limit_bytes=VMEM_LIMIT),
        name="rwkv",
    )(cols, shift0, s0, mu, w0, wwa, a0, g2, kk, ka, rk, gng, gnb)


def _extract_top(works, idx, n, ranked=True):
    ranks = [jnp.full(w.shape, 99.0, F32) for w in works]
    vals = [[] for _ in works]
    for j in range(n):
        ms = [jnp.max(w, axis=0, keepdims=True) for w in works]
        if idx is None:
            sels = [w == m for w, m in zip(works, ms)]
        else:
            firsts = [jnp.min(jnp.where(w == m, idx, 1e9), axis=0, keepdims=True) for w, m in zip(works, ms)]
            sels = [idx == f for f in firsts]
        if ranked:
            ranks = [jnp.where(sel, float(j), r) for sel, r in zip(sels, ranks)]
        works = [jnp.where(sel, -jnp.inf, w) for sel, w in zip(sels, works)]
        for v, m in zip(vals, ms):
            v.append(m)
    return vals, (ranks if ranked else works)


def _miscount(rank, n):
    taken = jnp.sum(jnp.where(rank < float(n), 1.0, 0.0), axis=0, keepdims=True)
    return jnp.abs(taken - float(n))


def _rows_bf16(row, rows):
    packed = jnp.broadcast_to(row, (16, row.shape[1])).astype(BF16)
    return jnp.concatenate([packed] * (rows // 16), axis=0)


def _peer_kernel(x_ref, ml_ref, rw_ref, woml_ref, worw_ref, gffn_ref, gfin_ref, wqt_ref, keys_ref,
                 ua_ref, ub_ref, vta_ref, vtb_ref, y_ref,
                 ht_ref, qt_ref, rank_ref, e_ref, vals_ref, r2b_ref, e2b_ref, cnt_ref, gate_ref,
                 p_ref, acc_ref, *, tm, ec):
    e = pl.program_id(1)
    ne = pl.num_programs(1)
    ng = tm // LANE
    K = PEER_TOPK
    nslab = ec // N_KEYS
    act_ref = qt_ref

    @pl.when(e == 0)
    def _select():
        x1 = (x_ref[...] + _dot(ml_ref[...].astype(BF16), woml_ref[...])
              + _dot(rw_ref[...].astype(BF16), worw_ref[...]))
        y_ref[...] = x1
        h = x1 * lax.rsqrt(jnp.mean(x1 * x1, axis=-1, keepdims=True) + NORM_EPS) * gffn_ref[...]
        ht = h.T.astype(BF16)
        ht_ref[...] = ht
        qt_ref[...] = _dot(wqt_ref[...], ht)
        acc_ref[...] = jnp.zeros(acc_ref.shape, F32)

        def score_body(hp, carry):
            q = qt_ref[pl.ds(pl.multiple_of(hp * PEER_HALF, PEER_HALF), PEER_HALF), :]
            e_ref[hp] = _dot(keys_ref[hp], q.astype(BF16))
            return carry

        lax.fori_loop(0, 2 * PEER_HEADS, score_body, 0)

        key_idx = _iota((N_KEYS, LANE), 0).astype(F32)
        groups = [slice(gi * LANE, (gi + 1) * LANE) for gi in range(ng)]

        def topk_groups(hp, exact_ties):
            scores = [e_ref[hp, :, lanes] for lanes in groups]
            vals, ranks = _extract_top(scores, key_idx if exact_ties else None, K)
            bad = None
            for lanes, v, rank in zip(groups, vals, ranks):
                rank_ref[hp, :, lanes] = rank
                vals_ref[hp, :, lanes] = jnp.concatenate(v, axis=0)
                miss = _miscount(rank, K)
                bad = miss if bad is None else jnp.maximum(bad, miss)
            return bad

        def topk_body(hp, carry):
            bad = topk_groups(hp, False)

            @pl.when(jnp.max(bad) > 0.0)
            def _redo():
                topk_groups(hp, True)

            return carry

        lax.fori_loop(0, 2 * PEER_HEADS, topk_body, 0)

        sub8 = _iota((8, LANE), 0)

        def cand_groups(h, exact_ties):
            works, v1s = [], []
            slab_a = [a_i for a_i in range(K) for _ in range(0, K // (a_i + 1), 8)]
            for lanes in groups:
                v1 = vals_ref[2 * h, :, lanes]
                v2 = vals_ref[2 * h + 1, :, lanes]
                slabs = []
                for a_i in range(K):
                    nb = K // (a_i + 1)
                    for b0 in range(0, nb, 8):
                        rows = v1[a_i:a_i + 1, :] + v2[b0:b0 + 8, :]
                        slabs.append(jnp.where(sub8 + b0 < nb, rows, -jnp.inf))
                works.append(jnp.concatenate(slabs, axis=0))
                v1s.append(v1)
            idx = None
            if exact_ties:
                idx = jnp.concatenate([(sub8 + (b0 + a_i * K)).astype(F32) for a_i in range(K)
                                       for b0 in range(0, K // (a_i + 1), 8)], axis=0)
            vals, left = _extract_top(works, idx, K, ranked=False)
            valid = jnp.concatenate([jnp.where(sub8 + b0 < K // (a_i + 1), 1.0, 0.0) for a_i in range(K)
                                     for b0 in range(0, K // (a_i + 1), 8)], axis=0)
            bad = None
            for lanes, v1, v, rest in zip(groups, v1s, vals, left):
                z = jnp.ones_like(v[0])
                for j in range(1, K):
                    z = z + jnp.exp(v[j] - v[0])
                picked = jnp.where(rest == -jnp.inf, valid, 0.0)
                r1 = rank_ref[2 * h, :, lanes]
                cnt = jnp.zeros((N_KEYS, LANE), F32)
                for a_i in range(K):
                    ca = None
                    for si, sa in enumerate(slab_a):
                        if sa == a_i:
                            part = jnp.sum(picked[si * 8:(si + 1) * 8], axis=0, keepdims=True)
                            ca = part if ca is None else ca + part
                    cnt = jnp.where(r1 == float(a_i), ca, cnt)
                cnt_ref[h, :, lanes] = cnt
                e1 = jnp.exp(e_ref[2 * h, :, lanes] - v1[0:1, :])
                gate_ref[h, :, lanes] = jnp.where(r1 < float(K), 0.5 * e1 / z, 0.0)
                miss = jnp.abs(jnp.sum(picked, axis=0, keepdims=True) - float(K))
                bad = miss if bad is None else jnp.maximum(bad, miss)
            return bad

        def cand_body(h, carry):
            bad = cand_groups(h, False)

            @pl.when(jnp.max(bad) > 0.0)
            def _redo():
                cand_groups(h, True)

            r2b_ref[h] = rank_ref[2 * h + 1].astype(BF16)
            e2b_ref[h] = jnp.exp(e_ref[2 * h + 1] - vals_ref[2 * h + 1, 0:1, :]).astype(BF16)
            return carry

        lax.fori_loop(0, PEER_HEADS, cand_body, 0)

    key1 = pl.ds(pl.multiple_of(e * nslab, nslab), nslab)
    hs = nslab // 2
    hrows = hs * N_KEYS
    for half in range(2):
        act_ref[half * hrows:(half + 1) * hrows, :] = _dot((ua_ref, ub_ref)[half][...], ht_ref[...])
    def gate_half(half):
        for gi in range(ng):
            lanes = slice(gi * LANE, (gi + 1) * LANE)
            w = [jnp.zeros((N_KEYS, LANE), BF16) for _ in range(hs)]
            for h in range(PEER_HEADS):
                r2 = r2b_ref[h, :, lanes]
                e2 = e2b_ref[h, :, lanes]
                cw = cnt_ref[h, key1, lanes]
                gw = gate_ref[h, key1, lanes]
                for jj in range(hs):
                    j = half * hs + jj
                    hit = r2 < _rows_bf16(cw[j:j + 1, :], N_KEYS)
                    w[jj] = w[jj] + jnp.where(hit, e2, jnp.zeros_like(e2)) * _rows_bf16(gw[j:j + 1, :], N_KEYS)
            for jj in range(hs):
                rows = slice((half * hs + jj) * N_KEYS, (half * hs + jj + 1) * N_KEYS)
                act = act_ref[rows, lanes]
                ab = act.astype(BF16)
                p_ref[rows, lanes] = (w[jj] * ab) * (1.0 + lax.erf(ab * 0.7071067811865476))

    for half in range(2):
        gate_half(half)
        acc_ref[...] += _dot((vta_ref, vtb_ref)[half][0], p_ref[half * hrows:(half + 1) * hrows, :])

    @pl.when(e == ne - 1)
    def _finish():
        x2 = y_ref[...] + acc_ref[...].T
        y_ref[...] = x2 * lax.rsqrt(jnp.mean(x2 * x2, axis=-1, keepdims=True) + NORM_EPS) * gfin_ref[...]


def _peer(x2d, ml2d, rw2d, woml, worw, gffn, gfin, wqt, keys, u, vt, tm, ec):
    n = x2d.shape[0]
    n_exp = u.shape[0]
    assert ec == 8 * N_KEYS and tm % LANE == 0 and n % tm == 0 and n_exp % ec == 0
    tok = lambda i, e: (i, 0)
    const2 = lambda i, e: (0, 0)
    qrows = 2 * PEER_HEADS * PEER_HALF
    eh = ec // 2
    once = pl.Buffered(1)
    return pl.pallas_call(
        functools.partial(_peer_kernel, tm=tm, ec=ec),
        grid=(n // tm, n_exp // ec),
        in_specs=[
            pl.BlockSpec((tm, D_MODEL), tok),
            pl.BlockSpec((tm, ML_WIDTH), tok),
            pl.BlockSpec((tm, RW_WIDTH), tok),
            pl.BlockSpec((ML_WIDTH, D_MODEL), const2, pipeline_mode=once),
            pl.BlockSpec((RW_WIDTH, D_MODEL), const2, pipeline_mode=once),
            pl.BlockSpec((1, D_MODEL), const2),
            pl.BlockSpec((1, D_MODEL), const2),
            pl.BlockSpec((qrows, D_MODEL), const2, pipeline_mode=once),
            pl.BlockSpec((2 * PEER_HEADS, N_KEYS, PEER_HALF), lambda i, e: (0, 0, 0), pipeline_mode=once),
            pl.BlockSpec((eh, D_MODEL), lambda i, e: (2 * e, 0)),
            pl.BlockSpec((eh, D_MODEL), lambda i, e: (2 * e + 1, 0)),
            pl.BlockSpec((1, D_MODEL, eh), lambda i, e: (2 * e, 0, 0)),
            pl.BlockSpec((1, D_MODEL, eh), lambda i, e: (2 * e + 1, 0, 0)),
        ],
        out_specs=pl.BlockSpec((tm, D_MODEL), tok),
        out_shape=jax.ShapeDtypeStruct((n, D_MODEL), F32),
        scratch_shapes=[
            pltpu.VMEM((D_MODEL, tm), BF16),
            pltpu.VMEM((qrows, tm), F32),
            pltpu.VMEM((2 * PEER_HEADS, N_KEYS, tm), F32),
            pltpu.VMEM((2 * PEER_HEADS, N_KEYS, tm), F32),
            pltpu.VMEM((2 * PEER_HEADS, PEER_TOPK, tm), F32),
            pltpu.VMEM((PEER_HEADS, N_KEYS, tm), BF16),
            pltpu.VMEM((PEER_HEADS, N_KEYS, tm), BF16),
            pltpu.VMEM((PEER_HEADS, N_KEYS, tm), F32),
            pltpu.VMEM((PEER_HEADS, N_KEYS, tm), F32),
            pltpu.VMEM((ec, tm), BF16),
            pltpu.VMEM((D_MODEL, tm), F32),
        ],
        compiler_params=pltpu.CompilerParams(
            dimension_semantics=("arbitrary", "arbitrary"), vmem_limit_bytes=VMEM_LIMIT),
        name="peer",
    )(x2d, ml2d, rw2d, woml, worw, gffn, gfin, wqt, keys, u, u, vt, vt)


def _prep_weights(norm_mix_g, w_in, ml_conv_w, ml_conv_b, ml_b_i, ml_b_f, ml_norm_g,
                  rw_mu, rw_w0, rw_w2, rw_a0, rw_a2, rw_g2, rw_k_k, rw_k_a, rw_r_k, rw_gn_g, rw_gn_b,
                  w_out, norm_ffn_g, peer_w_q, peer_sub_keys, peer_u, peer_v, norm_final_g):
    assert w_in.shape[0] == 1, "one layer"
    w = w_in[0]
    wg = jnp.pad(w[:, ML_QKVO:ML_QKVO + 2 * ML_HEADS], ((0, 0), (0, LANE - 2 * ML_HEADS)))
    wg_hi = wg.astype(BF16)
    wg_lo = (wg - wg_hi.astype(F32)).astype(BF16)
    zeros = jnp.zeros((DECAY_LORA, RW_WIDTH), F32)
    wwa = jnp.concatenate([jnp.concatenate([rw_w2[0], zeros], axis=1),
                           jnp.concatenate([zeros, rw_a2[0]], axis=1)], axis=0)
    gate_bias = jnp.pad(jnp.concatenate([ml_b_i[0], ml_b_f[0]]), (0, LANE - 2 * ML_HEADS))[None, :]
    row = lambda a: a.reshape(1, -1)
    return dict(
        g_mix=row(norm_mix_g[0]),
        wml=w[:, 0:ML_QKVO].astype(BF16),
        wg=jnp.stack([wg_hi, wg_lo]),
        wrw=w[:, ML_QKVO + 2 * ML_HEADS:].astype(BF16),
        conv_w=ml_conv_w[0], conv_b=row(ml_conv_b[0]), gate_bias=gate_bias, ml_norm_g=row(ml_norm_g[0]),
        mu=row(rw_mu[0]), w0=row(rw_w0[0]), wwa=wwa.astype(BF16), a0=row(rw_a0[0]),
        g2=rw_g2[0].astype(BF16), k_k=row(rw_k_k[0]), k_a=row(rw_k_a[0]), r_k=row(rw_r_k[0]),
        gn_g=row(rw_gn_g[0]), gn_b=row(rw_gn_b[0]),
        wo_ml=w_out[0, 0:ML_WIDTH].astype(BF16), wo_rw=w_out[0, ML_WIDTH:].astype(BF16),
        g_ffn=row(norm_ffn_g[0]), g_fin=row(norm_final_g),
        wqt=peer_w_q[0].T.astype(BF16),
        keys=peer_sub_keys[0].reshape(2 * PEER_HEADS, N_KEYS, PEER_HALF).astype(BF16),
        u=peer_u[0].astype(BF16),
        vt=peer_v[0].astype(BF16).reshape(-1, PEER_EXPERT_CHUNK // 2, D_MODEL).transpose(0, 2, 1),
    )


def _rwkv_tiling(t):
    tr = min(t, RW_CHUNK)
    chunk = max(tr, RW_MIN_CHUNK)
    return tr, chunk, RW_STEP_ROWS // chunk // 2 if chunk < RW_CHUNK else RW_STEP_ROWS // chunk


def _mlstm_tiling(t, b):
    tr = min(t, ML_CHUNK)
    chunk = max(tr, ML_MIN_CHUNK)
    return tr, min(b, ML_STEP_ROWS // chunk), chunk


def _trunk(x, states, wp, tr_rw, rw_chunk, rw_batch, tm_in, tm_peer, ec):
    b, t, d = x.shape
    c0, n0, m0, conv0, s0, shift0 = (s[0] for s in states)
    x2d = x.reshape(b * t, d)
    ml, gates, rw = _proj_in(x2d, wp["g_mix"], wp["wml"], wp["wg"], wp["wrw"], tm_in)
    ml3 = ml.reshape(b, t, ML_QKVO)
    rw3 = rw.reshape(b, t, RW_COLS)
    ml_out, c1, n1, m1, conv1 = _mlstm(
        ml3, gates.reshape(b, t, LANE), c0, n0, m0.reshape(b, 1, ML_HEADS), conv0,
        wp["conv_w"], wp["conv_b"], wp["gate_bias"], wp["ml_norm_g"], *_mlstm_tiling(t, b))
    rw_out, s1 = _rwkv(
        rw3, shift0[:, None, :], s0.reshape(b, RW_WIDTH, RW_HD), wp["mu"], wp["w0"], wp["wwa"], wp["a0"], wp["g2"],
        wp["k_k"], wp["k_a"], wp["r_k"], wp["gn_g"], wp["gn_b"], tr_rw, min(b, rw_batch), rw_chunk)
    y = _peer(x2d, ml_out.reshape(b * t, ML_WIDTH), rw_out.reshape(b * t, RW_WIDTH),
              wp["wo_ml"], wp["wo_rw"], wp["g_ffn"], wp["g_fin"], wp["wqt"], wp["keys"],
              wp["u"], wp["vt"], tm_peer, ec)
    new_states = (c1[None], n1[None], m1.reshape(b, ML_HEADS)[None], conv1[None],
                  s1.reshape(b, RW_HEADS, RW_HD, RW_HD)[None], rw3[:, -1][None])
    return y.reshape(b, t, d), new_states


def kernel(x_prompt, x_sample, state_mlstm_C, state_mlstm_n, state_mlstm_m, state_mlstm_conv, state_rwkv_S, state_rwkv_shift, norm_mix_g, w_in, ml_conv_w, ml_conv_b, ml_b_i, ml_b_f, ml_norm_g, rw_mu, rw_w0, rw_w2, rw_a0, rw_a2, rw_g2, rw_k_k, rw_k_a, rw_r_k, rw_gn_g, rw_gn_b, w_out, norm_ffn_g, peer_w_q, peer_sub_keys, peer_u, peer_v, norm_final_g):
    wp = _prep_weights(norm_mix_g, w_in, ml_conv_w, ml_conv_b, ml_b_i, ml_b_f, ml_norm_g,
                       rw_mu, rw_w0, rw_w2, rw_a0, rw_a2, rw_g2, rw_k_k, rw_k_a, rw_r_k, rw_gn_g, rw_gn_b,
                       w_out, norm_ffn_g, peer_w_q, peer_sub_keys, peer_u, peer_v, norm_final_g)
    bp = x_prompt.shape[0]
    z = lambda *s: jnp.zeros((1, bp) + s, F32)
    prompt_states = (z(ML_HEADS, ML_HD, ML_HD), z(ML_HEADS, ML_HD), jnp.full((1, bp, ML_HEADS), M_INIT, F32),
                     z(CONV_W - 1, 2 * ML_WIDTH), z(RW_HEADS, RW_HD, RW_HD), z(RW_COLS))
    sample_states = (state_mlstm_C, state_mlstm_n, state_mlstm_m, state_mlstm_conv,
                     state_rwkv_S, state_rwkv_shift)
    tp, ts = x_prompt.shape[1], x_sample.shape[1]
    np_, ns = bp * tp, x_sample.shape[0] * ts
    y_p, st_p = _trunk(x_prompt, prompt_states, wp, *_rwkv_tiling(tp),
                       min(np_, PROJ_TILE), min(np_, PEER_TILE), PEER_EXPERT_CHUNK)
    y_s, st_s = _trunk(x_sample, sample_states, wp, *_rwkv_tiling(ts),
                       min(ns, PROJ_TILE), min(ns, PEER_TILE), PEER_EXPERT_CHUNK)
    return (y_p, y_s) + tuple(st_p) + tuple(st_s)
```

```python
import functools

import jax
import jax.numpy as jnp
from jax import lax
from jax.experimental import pallas as pl
from jax.experimental.pallas import tpu as pltpu

F32 = jnp.float32
BF16 = jnp.bfloat16

D_MODEL = 1024
ML_HEADS = 4
ML_HD = 128
ML_WIDTH = ML_HEADS * ML_HD
CONV_W = 4
RW_HEADS = 8
RW_HD = 64
RW_WIDTH = RW_HEADS * RW_HD
DECAY_LORA = 64
AAA_LORA = 64
GATE_LORA = 128
RW_COLS = 3 * RW_WIDTH + DECAY_LORA + AAA_LORA + GATE_LORA
ML_QKVO = 4 * ML_WIDTH
PEER_HEADS = 8
N_KEYS = 128
PEER_TOPK = 16
PEER_HALF = 128
NORM_EPS = 1e-6
ML_NORM_EPS = 1e-6
GN_EPS = RW_HD * 1e-5
M_INIT = -1e30
NEG_BIG = -1e30
NOT_TAKEN = 99.0
PAST_ALL_IDX = 1e9

LANE = 128
SUBLANE = 8
ML_CHUNK = 128
ML_MIN_CHUNK = 16
ML_STEP_ROWS = 128
RW_CHUNK = 64
RW_GROUP = 4
RW_MIN_CHUNK = 16
RW_STEP_ROWS = 256
PROJ_TILE = 512
PEER_TILE = 512
PEER_EXPERT_CHUNK = 1024
VMEM_LIMIT = 56 * 1024 * 1024


def _dot(a, b):
    return jnp.dot(a, b, preferred_element_type=F32)


def _dot_nt(a, b):
    return lax.dot_general(a, b, (((1,), (1,)), ((), ())), preferred_element_type=F32)


def _split_bf16(x, n):
    parts = []
    r = x
    for _ in range(n):
        p = r.astype(BF16)
        parts.append(p)
        r = r - p.astype(F32)
    return parts


def _dot_exact_lhs(mask_bf16, x, n):
    return sum(_dot(mask_bf16, p) for p in _split_bf16(x, n))


def _dot_exact_rhs(x, mask_bf16, n):
    return sum(_dot(p, mask_bf16) for p in _split_bf16(x, n))


def _sigmoid(x):
    return 1.0 / (1.0 + jnp.exp(-x))


def _softplus(x):
    return jnp.maximum(x, 0.0) + jnp.log1p(jnp.exp(-jnp.abs(x)))


def _iota(shape, dim):
    return lax.broadcasted_iota(jnp.int32, shape, dim)


def _pad_rows(x, rows, value=0.0):
    if x.shape[0] == rows:
        return x
    return jnp.concatenate([x, jnp.full((rows - x.shape[0], x.shape[1]), value, x.dtype)], axis=0)


def _proj_in_kernel(x_ref, g_ref, wml_ref, wg_ref, wrw_ref, ml_ref, gate_ref, rw_ref):
    x = x_ref[...]
    xn = x * lax.rsqrt(jnp.mean(x * x, axis=-1, keepdims=True) + NORM_EPS) * g_ref[...]
    xb = xn.astype(BF16)
    ml_ref[...] = _dot(xb, wml_ref[...])
    rw_ref[...] = _dot(xb, wrw_ref[...])
    xlo = (xn - xb.astype(F32)).astype(BF16)
    gate_ref[...] = _dot(xb, wg_ref[0]) + _dot(xlo, wg_ref[0]) + _dot(xb, wg_ref[1])


def _proj_in(x2d, g, wml, wg, wrw, tm):
    n = x2d.shape[0]
    const2 = lambda i: (0, 0)
    return pl.pallas_call(
        _proj_in_kernel,
        grid=(n // tm,),
        in_specs=[
            pl.BlockSpec((tm, D_MODEL), lambda i: (i, 0)),
            pl.BlockSpec((1, D_MODEL), const2),
            pl.BlockSpec((D_MODEL, ML_QKVO), const2),
            pl.BlockSpec((2, D_MODEL, LANE), lambda i: (0, 0, 0)),
            pl.BlockSpec((D_MODEL, RW_COLS), const2),
        ],
        out_specs=[
            pl.BlockSpec((tm, ML_QKVO), lambda i: (i, 0)),
            pl.BlockSpec((tm, LANE), lambda i: (i, 0)),
            pl.BlockSpec((tm, RW_COLS), lambda i: (i, 0)),
        ],
        out_shape=[
            jax.ShapeDtypeStruct((n, ML_QKVO), F32),
            jax.ShapeDtypeStruct((n, LANE), F32),
            jax.ShapeDtypeStruct((n, RW_COLS), F32),
        ],
        compiler_params=pltpu.CompilerParams(
            dimension_semantics=("arbitrary",), vmem_limit_bytes=VMEM_LIMIT),
        name="proj_in",
    )(x2d, g, wml, wg, wrw)


def _mlstm_kernel(ml_ref, gate_ref, c0_ref, n0_ref, m0_ref, conv0_ref, cw_ref, cb_ref, gb_ref, ng_ref,
                  out_ref, c1_ref, n1_ref, m1_ref, conv1_ref,
                  caug_ref, m_ref, ext_ref, *, tr, nb, L):
    c = pl.program_id(1)
    nc = pl.num_programs(1)
    hist = CONV_W - 1

    @pl.when(c == 0)
    def _init():
        m_ref[...] = jnp.zeros(m_ref.shape, F32)
        for bi in range(nb):
            for h in range(ML_HEADS):
                caug_ref[bi * ML_HEADS + h, :, 0:ML_HD] = c0_ref[bi, h]
                nrow = n0_ref[bi, h:h + 1, :]
                caug_ref[bi * ML_HEADS + h, :, ML_HD:2 * ML_HD] = jnp.broadcast_to(nrow, (ML_HD, ML_HD)).T
            m_ref[bi:bi + 1, 0:ML_HEADS] = m0_ref[bi]
            ext_ref[bi, SUBLANE - hist:SUBLANE, :] = conv0_ref[bi]

    row = _iota((L, L), 0)
    col = _iota((L, L), 1)
    causal = row >= col
    tri = jnp.where(causal, 1.0, 0.0).astype(BF16)
    ones = jnp.ones((L, ML_HD), BF16)

    seqs = []
    for bi in range(nb):
        ext_ref[bi, SUBLANE:SUBLANE + tr, :] = ml_ref[bi, :, 0:2 * ML_WIDTH]
        first = SUBLANE - hist
        acc = cb_ref[...] + ext_ref[bi, first:first + tr, :] * cw_ref[0:1, :]
        for j in range(1, CONV_W):
            acc = acc + ext_ref[bi, first + j:first + j + tr, :] * cw_ref[j:j + 1, :]
        qk = acc * _sigmoid(acc)
        tail = ext_ref[bi, tr + first:tr + SUBLANE, :]
        ext_ref[bi, first:SUBLANE, :] = tail

        @pl.when(c == nc - 1)
        def _conv_out():
            conv1_ref[bi] = tail

        g = gate_ref[bi] + gb_ref[...]
        i_all = _pad_rows(g, L, NEG_BIG)
        b_col = _dot_exact_lhs(tri, _pad_rows(-_softplus(-g), L, 0.0), 3)
        seqs.append(dict(
            q=_pad_rows(qk[:, 0:ML_WIDTH], L),
            k=_pad_rows(qk[:, ML_WIDTH:2 * ML_WIDTH] * (ML_HD ** -0.5), L),
            v=_pad_rows(ml_ref[bi, :, 2 * ML_WIDTH:3 * ML_WIDTH], L),
            b_col=b_col, b_t=b_col.T, i_t=i_all.T))

    chains = [(bi, h) for bi in range(nb) for h in range(ML_HEADS)]
    ch = []
    for bi, h in chains:
        q = seqs[bi]
        sl = slice(h * ML_HD, (h + 1) * ML_HD)
        bc = q["b_col"][:, ML_HEADS + h:ML_HEADS + h + 1]
        br = q["b_t"][ML_HEADS + h:ML_HEADS + h + 1, :]
        ir = q["i_t"][h:h + 1, :]
        m_prev = m_ref[bi:bi + 1, h:h + 1]
        logd = jnp.where(causal, bc - br + ir, -jnp.inf)
        linter = bc + m_prev
        m_t = jnp.maximum(linter, jnp.max(logd, axis=1, keepdims=True))
        m_new = m_t[L - 1:L, :]
        b_last = bc[L - 1:L, :]
        ch.append(dict(
            sl=sl, m_t=m_t, d=jnp.exp(logd - m_t), s_inter=jnp.exp(linter - m_t), m_new=m_new,
            s_state=jnp.exp(b_last + m_prev - m_new), w_row=jnp.exp(b_last - br + ir - m_new),
            qh=q["q"][:, sl].astype(BF16), k_t=q["k"][:, sl].T,
            vaug=jnp.concatenate([q["v"][:, sl].astype(BF16), ones], axis=1),
            caug=caug_ref[bi * ML_HEADS + h]))
    for c_ in ch:
        c_["s"] = (_dot(c_["qh"], c_["k_t"].astype(BF16)) * c_["d"]).astype(BF16)
    for c_ in ch:
        c_["num"] = c_["s_inter"] * _dot(c_["qh"], c_["caug"].astype(BF16)) + _dot(c_["s"], c_["vaug"])
    for (bi, h), c_ in zip(chains, ch):
        num = c_["num"]
        den = num[:, ML_HD:2 * ML_HD]
        hh = num[:, 0:ML_HD] / jnp.maximum(jnp.abs(den), jnp.exp(-c_["m_t"]))
        mu = jnp.mean(hh, axis=-1, keepdims=True)
        dv = hh - mu
        var = jnp.mean(dv * dv, axis=-1, keepdims=True)
        y = dv * lax.rsqrt(var + ML_NORM_EPS) * ng_ref[:, c_["sl"]]
        o = ml_ref[bi, :, 3 * ML_WIDTH + h * ML_HD:3 * ML_WIDTH + (h + 1) * ML_HD]
        out_ref[bi, :, c_["sl"]] = y[0:tr] * _sigmoid(o)
    for (bi, h), c_ in zip(chains, ch):
        upd = _dot((c_["k_t"] * c_["w_row"]).astype(BF16), c_["vaug"])
        caug_ref[bi * ML_HEADS + h] = c_["s_state"] * c_["caug"] + upd
        m_ref[bi:bi + 1, h:h + 1] = c_["m_new"]

    @pl.when(c == nc - 1)
    def _state_out():
        for bi in range(nb):
            for h in range(ML_HEADS):
                caug = caug_ref[bi * ML_HEADS + h]
                c1_ref[bi, h] = caug[:, 0:ML_HD]
                n1_ref[bi, h:h + 1, :] = caug[:, ML_HD:2 * ML_HD].T[0:1, :]
            m1_ref[bi] = m_ref[bi:bi + 1, 0:ML_HEADS]


def _mlstm(ml, gates, c0, n0, m0, conv0, cw, cb, gb, ng, tr, nb, chunk):
    b, t, _ = ml.shape
    nc = t // tr
    assert b % nb == 0 and t % tr == 0 and nb <= SUBLANE and tr <= chunk
    bmap = lambda i, j: (i, 0, 0)
    const2 = lambda i, j: (0, 0)
    return pl.pallas_call(
        functools.partial(_mlstm_kernel, tr=tr, nb=nb, L=chunk),
        grid=(b // nb, nc),
        in_specs=[
            pl.BlockSpec((nb, tr, ML_QKVO), lambda i, j: (i, j, 0)),
            pl.BlockSpec((nb, tr, LANE), lambda i, j: (i, j, 0)),
            pl.BlockSpec((nb, ML_HEADS, ML_HD, ML_HD), lambda i, j: (i, 0, 0, 0)),
            pl.BlockSpec((nb, ML_HEADS, ML_HD), bmap),
            pl.BlockSpec((nb, 1, ML_HEADS), bmap),
            pl.BlockSpec((nb, CONV_W - 1, 2 * ML_WIDTH), bmap),
            pl.BlockSpec((CONV_W, 2 * ML_WIDTH), const2),
            pl.BlockSpec((1, 2 * ML_WIDTH), const2),
            pl.BlockSpec((1, LANE), const2),
            pl.BlockSpec((1, ML_WIDTH), const2),
        ],
        out_specs=[
            pl.BlockSpec((nb, tr, ML_WIDTH), lambda i, j: (i, j, 0)),
            pl.BlockSpec((nb, ML_HEADS, ML_HD, ML_HD), lambda i, j: (i, 0, 0, 0)),
            pl.BlockSpec((nb, ML_HEADS, ML_HD), bmap),
            pl.BlockSpec((nb, 1, ML_HEADS), bmap),
            pl.BlockSpec((nb, CONV_W - 1, 2 * ML_WIDTH), bmap),
        ],
        out_shape=[
            jax.ShapeDtypeStruct((b, t, ML_WIDTH), F32),
            jax.ShapeDtypeStruct((b, ML_HEADS, ML_HD, ML_HD), F32),
            jax.ShapeDtypeStruct((b, ML_HEADS, ML_HD), F32),
            jax.ShapeDtypeStruct((b, 1, ML_HEADS), F32),
            jax.ShapeDtypeStruct((b, CONV_W - 1, 2 * ML_WIDTH), F32),
        ],
        scratch_shapes=[
            pltpu.VMEM((nb * ML_HEADS, ML_HD, 2 * ML_HD), F32),
            pltpu.VMEM((SUBLANE, LANE), F32),
            pltpu.VMEM((nb, tr + SUBLANE, 2 * ML_WIDTH), F32),
        ],
        compiler_params=pltpu.CompilerParams(
            dimension_semantics=("arbitrary", "arbitrary"), vmem_limit_bytes=VMEM_LIMIT),
        name="mlstm",
    )(ml, gates, c0, n0, m0, conv0, cw, cb, gb, ng)


def _rwkv_kernel(c_ref, sh_ref, s0_ref, mu_ref, w0_ref, wwa_ref, a0_ref, g2_ref, kk_ref, ka_ref, rk_ref,
                 gng_ref, gnb_ref, out_ref, s1_ref, s_ref, last_ref, *, tr, nb, L):
    W = RW_WIDTH
    GW = RW_GROUP * RW_HD
    GL = RW_GROUP * L
    NGRP = RW_HEADS // RW_GROUP
    ci = pl.program_id(1)
    nc = pl.num_programs(1)

    r2 = _iota((GL, GW), 0)
    c2 = _iota((GL, GW), 1)
    bd = (r2 // L) == (c2 // RW_HD)
    rg = _iota((GL, GL), 0)
    cg = _iota((GL, GL), 1)
    s_lower = rg > cg
    i_lower = rg >= cg

    @pl.when(ci == 0)
    def _init():
        sbd = (_iota((GW, GW), 0) // RW_HD) == (_iota((GW, GW), 1) // RW_HD)
        for bi in range(nb):
            last_ref[bi, 0:1, :] = sh_ref[bi]
            for gi in range(NGRP):
                x = s0_ref[bi, gi * GW:(gi + 1) * GW, :]
                s_ref[bi * NGRP + gi] = jnp.where(sbd, jnp.concatenate([x] * RW_GROUP, axis=1), 0.0)

    c = c_ref[...].reshape(nb * tr, RW_COLS)
    prev = pltpu.roll(c, 1, axis=0)
    rowid = _iota(c.shape, 0)
    for bi in range(nb):
        prev = jnp.where(rowid == bi * tr, last_ref[bi, 0:1, :], prev)
        last_ref[bi, 0:1, :] = c[(bi + 1) * tr - 1:(bi + 1) * tr, :]
    xs = c + (prev - c) * mu_ref[...]
    r = xs[:, 0:W]
    k = xs[:, W:2 * W]
    v = xs[:, 2 * W:3 * W]
    slab = xs[:, 3 * W:3 * W + DECAY_LORA + AAA_LORA]
    gd = xs[:, 3 * W + DECAY_LORA + AAA_LORA:]
    lane = _iota(slab.shape, 1)
    t_in = jnp.where(lane < DECAY_LORA, jnp.tanh(slab), slab)
    la = _dot(t_in.astype(BF16), wwa_ref[...])
    w_log = -_softplus(-(w0_ref[...] + la[:, 0:W])) - 0.5
    lw_all = -jnp.exp(w_log)
    a_all = _sigmoid(a0_ref[...] + la[:, W:2 * W])
    g = _dot(_sigmoid(gd).astype(BF16), g2_ref[...])

    rs = _iota((W, W), 0)
    cs = _iota((W, W), 1)
    seg = jnp.where((rs // RW_HD) == (cs // RW_HD), 1.0, 0.0).astype(BF16)

    kk = k * kk_ref[...]
    kn_all = k * (1.0 + (a_all - 1.0) * ka_ref[...])
    ss = _dot_exact_rhs(kk * kk, seg, 2)
    kap_all = kk / jnp.maximum(jnp.sqrt(ss), 1e-12)
    bonus = _dot_exact_rhs(r * kn_all * rk_ref[...], seg, 2) * v

    rl = _iota((L, L), 0)
    cl_ = _iota((L, L), 1)
    tri = jnp.where(rl >= cl_, 1.0, 0.0).astype(BF16)

    def stack(x):
        return jnp.where(bd, jnp.concatenate([x] * RW_GROUP, axis=0), 0.0).astype(BF16)

    seqs = []
    for bi in range(nb):
        rows = slice(bi * tr, (bi + 1) * tr)
        lw = _pad_rows(lw_all[rows], L)
        kap = _pad_rows(kap_all[rows], L)
        kn_p = _pad_rows(kn_all[rows], L)
        a_p = _pad_rows(a_all[rows], L)
        v_p = _pad_rows(v[rows], L)
        r_p = _pad_rows(r[rows], L)
        cum = _dot_exact_lhs(tri, lw, 3)
        cum_last = cum[L - 1:L, :]
        e_neg = jnp.exp(-cum)
        e_rem = jnp.exp(cum_last - cum)
        seqs.append(dict(
            abar=-kap * jnp.exp(cum - lw), btil=kap * a_p * e_neg, ktil=kn_p * e_neg, rbar=r_p * jnp.exp(cum),
            bhat=kap * a_p * e_rem, khat=kn_p * e_rem, v=v_p, g_last=jnp.exp(cum_last)))

    chains = [(bi, gi) for bi in range(nb) for gi in range(NGRP)]
    ch = []
    for bi, gi in chains:
        q = seqs[bi]
        ls = slice(gi * GW, (gi + 1) * GW)
        a_s, b_s, k_s, r_s, v_s = (stack(q[n][:, ls]) for n in ("abar", "btil", "ktil", "rbar", "v"))
        ch.append(dict(
            ar=jnp.concatenate([a_s, r_s], axis=0),
            bk=jnp.concatenate([b_s, k_s], axis=0),
            bkh=jnp.concatenate([stack(q["bhat"][:, ls]), stack(q["khat"][:, ls])], axis=0),
            v_s=v_s, g_last=q["g_last"][:, ls], s=s_ref[bi * NGRP + gi]))
    for c_ in ch:
        p = _dot_nt(c_["ar"], c_["bk"])
        c_["npow"] = jnp.where(s_lower, p[0:GL, 0:GL], 0.0)
        c_["a_ak"] = jnp.where(s_lower, p[0:GL, GL:2 * GL], 0.0).astype(BF16)
        c_["r_b"] = jnp.where(i_lower, p[GL:2 * GL, 0:GL], 0.0).astype(BF16)
        c_["r_k"] = jnp.where(i_lower, p[GL:2 * GL, GL:2 * GL], 0.0).astype(BF16)
    for c_ in ch:
        c_["q0"] = _dot_nt(c_["ar"], c_["s"].astype(BF16))
    for c_ in ch:
        c_["u"] = c_["q0"][0:GL] + _dot(c_["a_ak"], c_["v_s"])
    lvls = L.bit_length() - 1
    for lvl in range(lvls):
        for c_ in ch:
            nb16 = c_["npow"].astype(BF16)
            c_["u"] = c_["u"] + _dot(nb16, c_["u"].astype(BF16))
            if lvl < lvls - 1:
                c_["npow"] = _dot(nb16, nb16)
    for c_ in ch:
        y = (c_["q0"][GL:2 * GL] + _dot(c_["r_b"], c_["u"].astype(BF16)) + _dot(c_["r_k"], c_["v_s"]))
        yg = y[0:L]
        for j in range(1, RW_GROUP):
            yg = yg + y[j * L:(j + 1) * L]
        c_["yg"] = yg
    for (bi, gi), c_ in zip(chains, ch):
        uv_t = jnp.concatenate([c_["u"], c_["v_s"].astype(F32)], axis=0).T.astype(BF16)
        s_ref[bi * NGRP + gi] = c_["s"] * c_["g_last"] + _dot(uv_t, c_["bkh"])
    y_rows = [jnp.concatenate([ch[bi * NGRP + gi]["yg"] for gi in range(NGRP)], axis=1)[0:tr]
              for bi in range(nb)]

    y_all = jnp.concatenate(y_rows, axis=0) if nb > 1 else y_rows[0]
    inv = 1.0 / RW_HD
    mu_ = _dot_exact_rhs(y_all, seg, 2) * inv
    dy = y_all - mu_
    var = _dot_exact_rhs(dy * dy, seg, 2) * inv
    yn = dy * lax.rsqrt(var + GN_EPS) * gng_ref[...] + gnb_ref[...]
    out_ref[...] = ((yn + bonus) * g).reshape(nb, tr, W)

    @pl.when(ci == nc - 1)
    def _state_out():
        for bi in range(nb):
            for gi in range(NGRP):
                s = s_ref[bi * NGRP + gi]
                f = s[:, 0:RW_HD]
                for j in range(1, RW_GROUP):
                    f = f + s[:, j * RW_HD:(j + 1) * RW_HD]
                s1_ref[bi, gi * GW:(gi + 1) * GW, :] = f


def _rwkv(cols, shift0, s0, mu, w0, wwa, a0, g2, kk, ka, rk, gng, gnb, tr, nb, chunk):
    b, t, _ = cols.shape
    nc = t // tr
    assert b % nb == 0 and t % tr == 0 and tr <= chunk
    const2 = lambda i, j: (0, 0)
    vec = pl.BlockSpec((1, RW_WIDTH), const2)
    return pl.pallas_call(
        functools.partial(_rwkv_kernel, tr=tr, nb=nb, L=chunk),
        grid=(b // nb, nc),
        in_specs=[
            pl.BlockSpec((nb, tr, RW_COLS), lambda i, j: (i, j, 0)),
            pl.BlockSpec((nb, 1, RW_COLS), lambda i, j: (i, 0, 0)),
            pl.BlockSpec((nb, RW_WIDTH, RW_HD), lambda i, j: (i, 0, 0)),
            pl.BlockSpec((1, RW_COLS), const2),
            vec,
            pl.BlockSpec((DECAY_LORA + AAA_LORA, 2 * RW_WIDTH), const2),
            vec,
            pl.BlockSpec((GATE_LORA, RW_WIDTH), const2),
            vec, vec, vec, vec, vec,
        ],
        out_specs=[
            pl.BlockSpec((nb, tr, RW_WIDTH), lambda i, j: (i, j, 0)),
            pl.BlockSpec((nb, RW_WIDTH, RW_HD), lambda i, j: (i, 0, 0)),
        ],
        out_shape=[
            jax.ShapeDtypeStruct((b, t, RW_WIDTH), F32),
            jax.ShapeDtypeStruct((b, RW_WIDTH, RW_HD), F32),
        ],
        scratch_shapes=[pltpu.VMEM((nb * (RW_HEADS // RW_GROUP), RW_GROUP * RW_HD, RW_GROUP * RW_HD), F32),
                        pltpu.VMEM((nb, SUBLANE, RW_COLS), F32)],
        compiler_params=pltpu.CompilerParams(
            dimension_semantics=("arbitrary", "arbitrary"), vmem_limit_bytes=VMEM_LIMIT),
        name="rwkv",
    )(cols, shift0, s0, mu, w0, wwa, a0, g2, kk, ka, rk, gng, gnb)


def _extract_top(works, idx, n, ranked=True):
    ranks = [jnp.full(w.shape, NOT_TAKEN, F32) for w in works]
    vals = [[] for _ in works]
    for j in range(n):
        ms = [jnp.max(w, axis=0, keepdims=True) for w in works]
        if idx is None:
            sels = [w == m for w, m in zip(works, ms)]
        else:
            firsts = [jnp.min(jnp.where(w == m, idx, PAST_ALL_IDX), axis=0, keepdims=True) for w, m in zip(works, ms)]
            sels = [idx == f for f in firsts]
        if ranked:
            ranks = [jnp.where(sel, float(j), r) for sel, r in zip(sels, ranks)]
        works = [jnp.where(sel, -jnp.inf, w) for sel, w in zip(sels, works)]
        for v, m in zip(vals, ms):
            v.append(m)
    return vals, (ranks if ranked else works)


def _miscount(rank, n):
    taken = jnp.sum(jnp.where(rank < float(n), 1.0, 0.0), axis=0, keepdims=True)
    return jnp.abs(taken - float(n))


def _rows_bf16(row, rows):
    packed = jnp.broadcast_to(row, (16, row.shape[1])).astype(BF16)
    return jnp.concatenate([packed] * (rows // 16), axis=0)


def _peer_kernel(x_ref, ml_ref, rw_ref, woml_ref, worw_ref, gffn_ref, gfin_ref, wqt_ref, keys_ref,
                 ua_ref, ub_ref, vta_ref, vtb_ref, y_ref,
                 ht_ref, qt_ref, rank_ref, e_ref, vals_ref, r2b_ref, e2b_ref, cnt_ref, gate_ref,
                 p_ref, acc_ref, *, tm, ec):
    e = pl.program_id(1)
    ne = pl.num_programs(1)
    ng = tm // LANE
    K = PEER_TOPK
    nslab = ec // N_KEYS
    act_ref = qt_ref

    @pl.when(e == 0)
    def _select():
        x1 = (x_ref[...] + _dot(ml_ref[...].astype(BF16), woml_ref[...])
              + _dot(rw_ref[...].astype(BF16), worw_ref[...]))
        y_ref[...] = x1
        h = x1 * lax.rsqrt(jnp.mean(x1 * x1, axis=-1, keepdims=True) + NORM_EPS) * gffn_ref[...]
        ht = h.T.astype(BF16)
        ht_ref[...] = ht
        qt_ref[...] = _dot(wqt_ref[...], ht)
        acc_ref[...] = jnp.zeros(acc_ref.shape, F32)

        def score_body(hp, carry):
            q = qt_ref[pl.ds(pl.multiple_of(hp * PEER_HALF, PEER_HALF), PEER_HALF), :]
            e_ref[hp] = _dot(keys_ref[hp], q.astype(BF16))
            return carry

        lax.fori_loop(0, 2 * PEER_HEADS, score_body, 0)

        key_idx = _iota((N_KEYS, LANE), 0).astype(F32)
        groups = [slice(gi * LANE, (gi + 1) * LANE) for gi in range(ng)]

        def topk_groups(hp, exact_ties):
            scores = [e_ref[hp, :, lanes] for lanes in groups]
            vals, ranks = _extract_top(scores, key_idx if exact_ties else None, K)
            bad = None
            for lanes, v, rank in zip(groups, vals, ranks):
                rank_ref[hp, :, lanes] = rank
                vals_ref[hp, :, lanes] = jnp.concatenate(v, axis=0)
                miss = _miscount(rank, K)
                bad = miss if bad is None else jnp.maximum(bad, miss)
            return bad

        def topk_body(hp, carry):
            bad = topk_groups(hp, False)

            @pl.when(jnp.max(bad) > 0.0)
            def _redo():
                topk_groups(hp, True)

            return carry

        lax.fori_loop(0, 2 * PEER_HEADS, topk_body, 0)

        sub8 = _iota((SUBLANE, LANE), 0)

        def cand_groups(h, exact_ties):
            works, v1s = [], []
            slab_a = [a_i for a_i in range(K) for _ in range(0, K // (a_i + 1), SUBLANE)]
            for lanes in groups:
                v1 = vals_ref[2 * h, :, lanes]
                v2 = vals_ref[2 * h + 1, :, lanes]
                slabs = []
                for a_i in range(K):
                    nb = K // (a_i + 1)
                    for b0 in range(0, nb, SUBLANE):
                        rows = v1[a_i:a_i + 1, :] + v2[b0:b0 + SUBLANE, :]
                        slabs.append(jnp.where(sub8 + b0 < nb, rows, -jnp.inf))
                works.append(jnp.concatenate(slabs, axis=0))
                v1s.append(v1)
            idx = None
            if exact_ties:
                idx = jnp.concatenate([(sub8 + (b0 + a_i * K)).astype(F32) for a_i in range(K)
                                       for b0 in range(0, K // (a_i + 1), SUBLANE)], axis=0)
            vals, left = _extract_top(works, idx, K, ranked=False)
            valid = jnp.concatenate([jnp.where(sub8 + b0 < K // (a_i + 1), 1.0, 0.0) for a_i in range(K)
                                     for b0 in range(0, K // (a_i + 1), SUBLANE)], axis=0)
            bad = None
            for lanes, v1, v, rest in zip(groups, v1s, vals, left):
                z = jnp.ones_like(v[0])
                for j in range(1, K):
                    z = z + jnp.exp(v[j] - v[0])
                picked = jnp.where(rest == -jnp.inf, valid, 0.0)
                r1 = rank_ref[2 * h, :, lanes]
                cnt = jnp.zeros((N_KEYS, LANE), F32)
                for a_i in range(K):
                    ca = None
                    for si, sa in enumerate(slab_a):
                        if sa == a_i:
                            part = jnp.sum(picked[si * SUBLANE:(si + 1) * SUBLANE], axis=0, keepdims=True)
                            ca = part if ca is None else ca + part
                    cnt = jnp.where(r1 == float(a_i), ca, cnt)
                cnt_ref[h, :, lanes] = cnt
                e1 = jnp.exp(e_ref[2 * h, :, lanes] - v1[0:1, :])
                gate_ref[h, :, lanes] = jnp.where(r1 < float(K), 0.5 * e1 / z, 0.0)
                miss = jnp.abs(jnp.sum(picked, axis=0, keepdims=True) - float(K))
                bad = miss if bad is None else jnp.maximum(bad, miss)
            return bad

        def cand_body(h, carry):
            bad = cand_groups(h, False)

            @pl.when(jnp.max(bad) > 0.0)
            def _redo():
                cand_groups(h, True)

            r2b_ref[h] = rank_ref[2 * h + 1].astype(BF16)
            e2b_ref[h] = jnp.exp(e_ref[2 * h + 1] - vals_ref[2 * h + 1, 0:1, :]).astype(BF16)
            return carry

        lax.fori_loop(0, PEER_HEADS, cand_body, 0)

    key1 = pl.ds(pl.multiple_of(e * nslab, nslab), nslab)
    hs = nslab // 2
    hrows = hs * N_KEYS
    for half in range(2):
        act_ref[half * hrows:(half + 1) * hrows, :] = _dot((ua_ref, ub_ref)[half][...], ht_ref[...])
    def gate_half(half):
        for gi in range(ng):
            lanes = slice(gi * LANE, (gi + 1) * LANE)
            w = [jnp.zeros((N_KEYS, LANE), BF16) for _ in range(hs)]
            for h in range(PEER_HEADS):
                r2 = r2b_ref[h, :, lanes]
                e2 = e2b_ref[h, :, lanes]
                cw = cnt_ref[h, key1, lanes]
                gw = gate_ref[h, key1, lanes]
                for jj in range(hs):
                    j = half * hs + jj
                    hit = r2 < _rows_bf16(cw[j:j + 1, :], N_KEYS)
                    w[jj] = w[jj] + jnp.where(hit, e2, jnp.zeros_like(e2)) * _rows_bf16(gw[j:j + 1, :], N_KEYS)
            for jj in range(hs):
                rows = slice((half * hs + jj) * N_KEYS, (half * hs + jj + 1) * N_KEYS)
                act = act_ref[rows, lanes]
                ab = act.astype(BF16)
                p_ref[rows, lanes] = (w[jj] * ab) * (1.0 + lax.erf(ab * 0.7071067811865476))

    for half in range(2):
        gate_half(half)
        acc_ref[...] += _dot((vta_ref, vtb_ref)[half][0], p_ref[half * hrows:(half + 1) * hrows, :])

    @pl.when(e == ne - 1)
    def _finish():
        x2 = y_ref[...] + acc_ref[...].T
        y_ref[...] = x2 * lax.rsqrt(jnp.mean(x2 * x2, axis=-1, keepdims=True) + NORM_EPS) * gfin_ref[...]


def _peer(x2d, ml2d, rw2d, woml, worw, gffn, gfin, wqt, keys, u, vt, tm, ec):
    n = x2d.shape[0]
    n_exp = u.shape[0]
    assert ec == 8 * N_KEYS and tm % LANE == 0 and n % tm == 0 and n_exp % ec == 0
    tok = lambda i, e: (i, 0)
    const2 = lambda i, e: (0, 0)
    qrows = 2 * PEER_HEADS * PEER_HALF
    eh = ec // 2
    once = pl.Buffered(1)
    return pl.pallas_call(
        functools.partial(_peer_kernel, tm=tm, ec=ec),
        grid=(n // tm, n_exp // ec),
        in_specs=[
            pl.BlockSpec((tm, D_MODEL), tok),
            pl.BlockSpec((tm, ML_WIDTH), tok),
            pl.BlockSpec((tm, RW_WIDTH), tok),
            pl.BlockSpec((ML_WIDTH, D_MODEL), const2, pipeline_mode=once),
            pl.BlockSpec((RW_WIDTH, D_MODEL), const2, pipeline_mode=once),
            pl.BlockSpec((1, D_MODEL), const2),
            pl.BlockSpec((1, D_MODEL), const2),
            pl.BlockSpec((qrows, D_MODEL), const2, pipeline_mode=once),
            pl.BlockSpec((2 * PEER_HEADS, N_KEYS, PEER_HALF), lambda i, e: (0, 0, 0), pipeline_mode=once),
            pl.BlockSpec((eh, D_MODEL), lambda i, e: (2 * e, 0)),
            pl.BlockSpec((eh, D_MODEL), lambda i, e: (2 * e + 1, 0)),
            pl.BlockSpec((1, D_MODEL, eh), lambda i, e: (2 * e, 0, 0)),
            pl.BlockSpec((1, D_MODEL, eh), lambda i, e: (2 * e + 1, 0, 0)),
        ],
        out_specs=pl.BlockSpec((tm, D_MODEL), tok),
        out_shape=jax.ShapeDtypeStruct((n, D_MODEL), F32),
        scratch_shapes=[
            pltpu.VMEM((D_MODEL, tm), BF16),
            pltpu.VMEM((qrows, tm), F32),
            pltpu.VMEM((2 * PEER_HEADS, N_KEYS, tm), F32),
            pltpu.VMEM((2 * PEER_HEADS, N_KEYS, tm), F32),
            pltpu.VMEM((2 * PEER_HEADS, PEER_TOPK, tm), F32),
            pltpu.VMEM((PEER_HEADS, N_KEYS, tm), BF16),
            pltpu.VMEM((PEER_HEADS, N_KEYS, tm), BF16),
            pltpu.VMEM((PEER_HEADS, N_KEYS, tm), F32),
            pltpu.VMEM((PEER_HEADS, N_KEYS, tm), F32),
            pltpu.VMEM((ec, tm), BF16),
            pltpu.VMEM((D_MODEL, tm), F32),
        ],
        compiler_params=pltpu.CompilerParams(
            dimension_semantics=("arbitrary", "arbitrary"), vmem_limit_bytes=VMEM_LIMIT),
        name="peer",
    )(x2d, ml2d, rw2d, woml, worw, gffn, gfin, wqt, keys, u, u, vt, vt)


def _prep_weights(norm_mix_g, w_in, ml_conv_w, ml_conv_b, ml_b_i, ml_b_f, ml_norm_g,
                  rw_mu, rw_w0, rw_w2, rw_a0, rw_a2, rw_g2, rw_k_k, rw_k_a, rw_r_k, rw_gn_g, rw_gn_b,
                  w_out, norm_ffn_g, peer_w_q, peer_sub_keys, peer_u, peer_v, norm_final_g):
    assert w_in.shape[0] == 1, "one layer"
    w = w_in[0]
    wg = jnp.pad(w[:, ML_QKVO:ML_QKVO + 2 * ML_HEADS], ((0, 0), (0, LANE - 2 * ML_HEADS)))
    wg_hi = wg.astype(BF16)
    wg_lo = (wg - wg_hi.astype(F32)).astype(BF16)
    zeros = jnp.zeros((DECAY_LORA, RW_WIDTH), F32)
    wwa = jnp.concatenate([jnp.concatenate([rw_w2[0], zeros], axis=1),
                           jnp.concatenate([zeros, rw_a2[0]], axis=1)], axis=0)
    gate_bias = jnp.pad(jnp.concatenate([ml_b_i[0], ml_b_f[0]]), (0, LANE - 2 * ML_HEADS))[None, :]
    row = lambda a: a.reshape(1, -1)
    return dict(
        g_mix=row(norm_mix_g[0]),
        wml=w[:, 0:ML_QKVO].astype(BF16),
        wg=jnp.stack([wg_hi, wg_lo]),
        wrw=w[:, ML_QKVO + 2 * ML_HEADS:].astype(BF16),
        conv_w=ml_conv_w[0], conv_b=row(ml_conv_b[0]), gate_bias=gate_bias, ml_norm_g=row(ml_norm_g[0]),
        mu=row(rw_mu[0]), w0=row(rw_w0[0]), wwa=wwa.astype(BF16), a0=row(rw_a0[0]),
        g2=rw_g2[0].astype(BF16), k_k=row(rw_k_k[0]), k_a=row(rw_k_a[0]), r_k=row(rw_r_k[0]),
        gn_g=row(rw_gn_g[0]), gn_b=row(rw_gn_b[0]),
        wo_ml=w_out[0, 0:ML_WIDTH].astype(BF16), wo_rw=w_out[0, ML_WIDTH:].astype(BF16),
        g_ffn=row(norm_ffn_g[0]), g_fin=row(norm_final_g),
        wqt=peer_w_q[0].T.astype(BF16),
        keys=peer_sub_keys[0].reshape(2 * PEER_HEADS, N_KEYS, PEER_HALF).astype(BF16),
        u=peer_u[0].astype(BF16),
        vt=peer_v[0].astype(BF16).reshape(-1, PEER_EXPERT_CHUNK // 2, D_MODEL).transpose(0, 2, 1),
    )


def _rwkv_tiling(t):
    tr = min(t, RW_CHUNK)
    chunk = max(tr, RW_MIN_CHUNK)
    return tr, chunk, RW_STEP_ROWS // chunk // 2 if chunk < RW_CHUNK else RW_STEP_ROWS // chunk


def _mlstm_tiling(t, b):
    tr = min(t, ML_CHUNK)
    chunk = max(tr, ML_MIN_CHUNK)
    return tr, min(b, ML_STEP_ROWS // chunk), chunk


def _trunk(x, states, wp, tr_rw, rw_chunk, rw_batch, tm_in, tm_peer, ec):
    b, t, d = x.shape
    c0, n0, m0, conv0, s0, shift0 = (s[0] for s in states)
    x2d = x.reshape(b * t, d)
    ml, gates, rw = _proj_in(x2d, wp["g_mix"], wp["wml"], wp["wg"], wp["wrw"], tm_in)
    ml3 = ml.reshape(b, t, ML_QKVO)
    rw3 = rw.reshape(b, t, RW_COLS)
    ml_out, c1, n1, m1, conv1 = _mlstm(
        ml3, gates.reshape(b, t, LANE), c0, n0, m0.reshape(b, 1, ML_HEADS), conv0,
        wp["conv_w"], wp["conv_b"], wp["gate_bias"], wp["ml_norm_g"], *_mlstm_tiling(t, b))
    rw_out, s1 = _rwkv(
        rw3, shift0[:, None, :], s0.reshape(b, RW_WIDTH, RW_HD), wp["mu"], wp["w0"], wp["wwa"], wp["a0"], wp["g2"],
        wp["k_k"], wp["k_a"], wp["r_k"], wp["gn_g"], wp["gn_b"], tr_rw, min(b, rw_batch), rw_chunk)
    y = _peer(x2d, ml_out.reshape(b * t, ML_WIDTH), rw_out.reshape(b * t, RW_WIDTH),
              wp["wo_ml"], wp["wo_rw"], wp["g_ffn"], wp["g_fin"], wp["wqt"], wp["keys"],
              wp["u"], wp["vt"], tm_peer, ec)
    new_states = (c1[None], n1[None], m1.reshape(b, ML_HEADS)[None], conv1[None],
                  s1.reshape(b, RW_HEADS, RW_HD, RW_HD)[None], rw3[:, -1][None])
    return y.reshape(b, t, d), new_states


def kernel(x_prompt, x_sample, state_mlstm_C, state_mlstm_n, state_mlstm_m, state_mlstm_conv, state_rwkv_S, state_rwkv_shift, norm_mix_g, w_in, ml_conv_w, ml_conv_b, ml_b_i, ml_b_f, ml_norm_g, rw_mu, rw_w0, rw_w2, rw_a0, rw_a2, rw_g2, rw_k_k, rw_k_a, rw_r_k, rw_gn_g, rw_gn_b, w_out, norm_ffn_g, peer_w_q, peer_sub_keys, peer_u, peer_v, norm_final_g):
    wp = _prep_weights(norm_mix_g, w_in, ml_conv_w, ml_conv_b, ml_b_i, ml_b_f, ml_norm_g,
                       rw_mu, rw_w0, rw_w2, rw_a0, rw_a2, rw_g2, rw_k_k, rw_k_a, rw_r_k, rw_gn_g, rw_gn_b,
                       w_out, norm_ffn_g, peer_w_q, peer_sub_keys, peer_u, peer_v, norm_final_g)
    bp = x_prompt.shape[0]
    z = lambda *s: jnp.zeros((1, bp) + s, F32)
    prompt_states = (z(ML_HEADS, ML_HD, ML_HD), z(ML_HEADS, ML_HD), jnp.full((1, bp, ML_HEADS), M_INIT, F32),
                     z(CONV_W - 1, 2 * ML_WIDTH), z(RW_HEADS, RW_HD, RW_HD), z(RW_COLS))
    sample_states = (state_mlstm_C, state_mlstm_n, state_mlstm_m, state_mlstm_conv,
                     state_rwkv_S, state_rwkv_shift)
    tp, ts = x_prompt.shape[1], x_sample.shape[1]
    np_, ns = bp * tp, x_sample.shape[0] * ts
    y_p, st_p = _trunk(x_prompt, prompt_states, wp, *_rwkv_tiling(tp),
                       min(np_, PROJ_TILE), min(np_, PEER_TILE), PEER_EXPERT_CHUNK)
    y_s, st_s = _trunk(x_sample, sample_states, wp, *_rwkv_tiling(ts),
                       min(ns, PROJ_TILE), min(ns, PEER_TILE), PEER_EXPERT_CHUNK)
    return (y_p, y_s) + tuple(st_p) + tuple(st_s)
```

```python
import functools

import jax
import jax.numpy as jnp
from jax import lax
from jax.experimental import pallas as pl
from jax.experimental.pallas import tpu as pltpu

F32 = jnp.float32
BF16 = jnp.bfloat16

D_MODEL = 1024
ML_HEADS = 4
ML_HD = 128
ML_WIDTH = ML_HEADS * ML_HD
CONV_W = 4
RW_HEADS = 8
RW_HD = 64
RW_WIDTH = RW_HEADS * RW_HD
DECAY_LORA = 64
AAA_LORA = 64
GATE_LORA = 128
RW_COLS = 3 * RW_WIDTH + DECAY_LORA + AAA_LORA + GATE_LORA
ML_QKVO = 4 * ML_WIDTH
PEER_HEADS = 8
N_KEYS = 128
PEER_TOPK = 16
PEER_HALF = 128
NORM_EPS = 1e-6
ML_NORM_EPS = 1e-6
GN_EPS = RW_HD * 1e-5
M_INIT = -1e30
NEG_BIG = -1e30
NOT_TAKEN = 99.0
PAST_ALL_IDX = 1e9

LANE = 128
SUBLANE = 8
ML_CHUNK = 128
ML_MIN_CHUNK = 16
ML_STEP_ROWS = 128
RW_CHUNK = 64
RW_GROUP = 4
RW_MIN_CHUNK = 16
RW_STEP_ROWS = 256
PROJ_TILE = 512
PEER_TILE = 512
PEER_EXPERT_CHUNK = 1024
VMEM_LIMIT = 56 * 1024 * 1024


def _dot(a, b):
    return jnp.dot(a, b, preferred_element_type=F32)


def _dot_nt(a, b):
    return lax.dot_general(a, b, (((1,), (1,)), ((), ())), preferred_element_type=F32)


def _split_bf16(x, n):
    parts = []
    r = x
    for _ in range(n):
        p = r.astype(BF16)
        parts.append(p)
        r = r - p.astype(F32)
    return parts


def _dot_exact_lhs(mask_bf16, x, n):
    return sum(_dot(mask_bf16, p) for p in _split_bf16(x, n))


def _dot_exact_rhs(x, mask_bf16, n):
    return sum(_dot(p, mask_bf16) for p in _split_bf16(x, n))


def _sigmoid(x):
    return 1.0 / (1.0 + jnp.exp(-x))


def _softplus(x):
    return jnp.maximum(x, 0.0) + jnp.log1p(jnp.exp(-jnp.abs(x)))


def _iota(shape, dim):
    return lax.broadcasted_iota(jnp.int32, shape, dim)


def _pad_rows(x, rows, value=0.0):
    if x.shape[0] == rows:
        return x
    return jnp.concatenate([x, jnp.full((rows - x.shape[0], x.shape[1]), value, x.dtype)], axis=0)


def _proj_in_kernel(x_ref, g_ref, wml_ref, wg_ref, wrw_ref, ml_ref, gate_ref, rw_ref):
    x = x_ref[...]
    xn = x * lax.rsqrt(jnp.mean(x * x, axis=-1, keepdims=True) + NORM_EPS) * g_ref[...]
    xb = xn.astype(BF16)
    ml_ref[...] = _dot(xb, wml_ref[...])
    rw_ref[...] = _dot(xb, wrw_ref[...])
    xlo = (xn - xb.astype(F32)).astype(BF16)
    gate_ref[...] = _dot(xb, wg_ref[0]) + _dot(xlo, wg_ref[0]) + _dot(xb, wg_ref[1])


def _proj_in(x2d, g, wml, wg, wrw, tm):
    n = x2d.shape[0]
    const2 = lambda i: (0, 0)
    return pl.pallas_call(
        _proj_in_kernel,
        grid=(n // tm,),
        in_specs=[
            pl.BlockSpec((tm, D_MODEL), lambda i: (i, 0)),
            pl.BlockSpec((1, D_MODEL), const2),
            pl.BlockSpec((D_MODEL, ML_QKVO), const2),
            pl.BlockSpec((2, D_MODEL, LANE), lambda i: (0, 0, 0)),
            pl.BlockSpec((D_MODEL, RW_COLS), const2),
        ],
        out_specs=[
            pl.BlockSpec((tm, ML_QKVO), lambda i: (i, 0)),
            pl.BlockSpec((tm, LANE), lambda i: (i, 0)),
            pl.BlockSpec((tm, RW_COLS), lambda i: (i, 0)),
        ],
        out_shape=[
            jax.ShapeDtypeStruct((n, ML_QKVO), F32),
            jax.ShapeDtypeStruct((n, LANE), F32),
            jax.ShapeDtypeStruct((n, RW_COLS), F32),
        ],
        compiler_params=pltpu.CompilerParams(
            dimension_semantics=("arbitrary",), vmem_limit_bytes=VMEM_LIMIT),
        name="proj_in",
    )(x2d, g, wml, wg, wrw)


def _mlstm_kernel(ml_ref, gate_ref, c0_ref, n0_ref, m0_ref, conv0_ref, cw_ref, cb_ref, gb_ref, ng_ref,
                  out_ref, c1_ref, n1_ref, m1_ref, conv1_ref,
                  caug_ref, m_ref, ext_ref, *, tr, nb, L):
    c = pl.program_id(1)
    nc = pl.num_programs(1)
    hist = CONV_W - 1

    @pl.when(c == 0)
    def _init():
        m_ref[...] = jnp.zeros(m_ref.shape, F32)
        for bi in range(nb):
            for h in range(ML_HEADS):
                caug_ref[bi * ML_HEADS + h, :, 0:ML_HD] = c0_ref[bi, h]
                nrow = n0_ref[bi, h:h + 1, :]
                caug_ref[bi * ML_HEADS + h, :, ML_HD:2 * ML_HD] = jnp.broadcast_to(nrow, (ML_HD, ML_HD)).T
            m_ref[bi:bi + 1, 0:ML_HEADS] = m0_ref[bi]
            ext_ref[bi, SUBLANE - hist:SUBLANE, :] = conv0_ref[bi]

    row = _iota((L, L), 0)
    col = _iota((L, L), 1)
    causal = row >= col
    tri = jnp.where(causal, 1.0, 0.0).astype(BF16)
    ones = jnp.ones((L, ML_HD), BF16)

    seqs = []
    for bi in range(nb):
        ext_ref[bi, SUBLANE:SUBLANE + tr, :] = ml_ref[bi, :, 0:2 * ML_WIDTH]
        first = SUBLANE - hist
        acc = cb_ref[...] + ext_ref[bi, first:first + tr, :] * cw_ref[0:1, :]
        for j in range(1, CONV_W):
            acc = acc + ext_ref[bi, first + j:first + j + tr, :] * cw_ref[j:j + 1, :]
        qk = acc * _sigmoid(acc)
        tail = ext_ref[bi, tr + first:tr + SUBLANE, :]
        ext_ref[bi, first:SUBLANE, :] = tail

        @pl.when(c == nc - 1)
        def _conv_out():
            conv1_ref[bi] = tail

        g = gate_ref[bi] + gb_ref[...]
        i_all = _pad_rows(g, L, NEG_BIG)
        b_col = _dot_exact_lhs(tri, _pad_rows(-_softplus(-g), L, 0.0), 3)
        seqs.append(dict(
            q=_pad_rows(qk[:, 0:ML_WIDTH], L),
            k=_pad_rows(qk[:, ML_WIDTH:2 * ML_WIDTH] * (ML_HD ** -0.5), L),
            v=_pad_rows(ml_ref[bi, :, 2 * ML_WIDTH:3 * ML_WIDTH], L),
            b_col=b_col, b_t=b_col.T, i_t=i_all.T))

    chains = [(bi, h) for bi in range(nb) for h in range(ML_HEADS)]
    ch = []
    for bi, h in chains:
        q = seqs[bi]
        sl = slice(h * ML_HD, (h + 1) * ML_HD)
        bc = q["b_col"][:, ML_HEADS + h:ML_HEADS + h + 1]
        br = q["b_t"][ML_HEADS + h:ML_HEADS + h + 1, :]
        ir = q["i_t"][h:h + 1, :]
        m_prev = m_ref[bi:bi + 1, h:h + 1]
        logd = jnp.where(causal, bc - br + ir, -jnp.inf)
        linter = bc + m_prev
        m_t = jnp.maximum(linter, jnp.max(logd, axis=1, keepdims=True))
        m_new = m_t[L - 1:L, :]
        b_last = bc[L - 1:L, :]
        ch.append(dict(
            sl=sl, m_t=m_t, d=jnp.exp(logd - m_t), s_inter=jnp.exp(linter - m_t), m_new=m_new,
            s_state=jnp.exp(b_last + m_prev - m_new), w_row=jnp.exp(b_last - br + ir - m_new),
            qh=q["q"][:, sl].astype(BF16), k_t=q["k"][:, sl].T,
            vaug=jnp.concatenate([q["v"][:, sl].astype(BF16), ones], axis=1),
            caug=caug_ref[bi * ML_HEADS + h]))
    for c_ in ch:
        c_["s"] = (_dot(c_["qh"], c_["k_t"].astype(BF16)) * c_["d"]).astype(BF16)
    for c_ in ch:
        c_["num"] = c_["s_inter"] * _dot(c_["qh"], c_["caug"].astype(BF16)) + _dot(c_["s"], c_["vaug"])
    for (bi, h), c_ in zip(chains, ch):
        num = c_["num"]
        den = num[:, ML_HD:2 * ML_HD]
        hh = num[:, 0:ML_HD] / jnp.maximum(jnp.abs(den), jnp.exp(-c_["m_t"]))
        mu = jnp.mean(hh, axis=-1, keepdims=True)
        dv = hh - mu
        var = jnp.mean(dv * dv, axis=-1, keepdims=True)
        y = dv * lax.rsqrt(var + ML_NORM_EPS) * ng_ref[:, c_["sl"]]
        o = ml_ref[bi, :, 3 * ML_WIDTH + h * ML_HD:3 * ML_WIDTH + (h + 1) * ML_HD]
        out_ref[bi, :, c_["sl"]] = y[0:tr] * _sigmoid(o)
    for (bi, h), c_ in zip(chains, ch):
        upd = _dot((c_["k_t"] * c_["w_row"]).astype(BF16), c_["vaug"])
        caug_ref[bi * ML_HEADS + h] = c_["s_state"] * c_["caug"] + upd
        m_ref[bi:bi + 1, h:h + 1] = c_["m_new"]

    @pl.when(c == nc - 1)
    def _state_out():
        for bi in range(nb):
            for h in range(ML_HEADS):
                caug = caug_ref[bi * ML_HEADS + h]
                c1_ref[bi, h] = caug[:, 0:ML_HD]
                n1_ref[bi, h:h + 1, :] = caug[:, ML_HD:2 * ML_HD].T[0:1, :]
            m1_ref[bi] = m_ref[bi:bi + 1, 0:ML_HEADS]


def _mlstm(ml, gates, c0, n0, m0, conv0, cw, cb, gb, ng, tr, nb, chunk):
    b, t, _ = ml.shape
    nc = t // tr
    assert b % nb == 0 and t % tr == 0 and nb <= SUBLANE and tr <= chunk
    bmap = lambda i, j: (i, 0, 0)
    const2 = lambda i, j: (0, 0)
    return pl.pallas_call(
        functools.partial(_mlstm_kernel, tr=tr, nb=nb, L=chunk),
        grid=(b // nb, nc),
        in_specs=[
            pl.BlockSpec((nb, tr, ML_QKVO), lambda i, j: (i, j, 0)),
            pl.BlockSpec((nb, tr, LANE), lambda i, j: (i, j, 0)),
            pl.BlockSpec((nb, ML_HEADS, ML_HD, ML_HD), lambda i, j: (i, 0, 0, 0)),
            pl.BlockSpec((nb, ML_HEADS, ML_HD), bmap),
            pl.BlockSpec((nb, 1, ML_HEADS), bmap),
            pl.BlockSpec((nb, CONV_W - 1, 2 * ML_WIDTH), bmap),
            pl.BlockSpec((CONV_W, 2 * ML_WIDTH), const2),
            pl.BlockSpec((1, 2 * ML_WIDTH), const2),
            pl.BlockSpec((1, LANE), const2),
            pl.BlockSpec((1, ML_WIDTH), const2),
        ],
        out_specs=[
            pl.BlockSpec((nb, tr, ML_WIDTH), lambda i, j: (i, j, 0)),
            pl.BlockSpec((nb, ML_HEADS, ML_HD, ML_HD), lambda i, j: (i, 0, 0, 0)),
            pl.BlockSpec((nb, ML_HEADS, ML_HD), bmap),
            pl.BlockSpec((nb, 1, ML_HEADS), bmap),
            pl.BlockSpec((nb, CONV_W - 1, 2 * ML_WIDTH), bmap),
        ],
        out_shape=[
            jax.ShapeDtypeStruct((b, t, ML_WIDTH), F32),
            jax.ShapeDtypeStruct((b, ML_HEADS, ML_HD, ML_HD), F32),
            jax.ShapeDtypeStruct((b, ML_HEADS, ML_HD), F32),
            jax.ShapeDtypeStruct((b, 1, ML_HEADS), F32),
            jax.ShapeDtypeStruct((b, CONV_W - 1, 2 * ML_WIDTH), F32),
        ],
        scratch_shapes=[
            pltpu.VMEM((nb * ML_HEADS, ML_HD, 2 * ML_HD), F32),
            pltpu.VMEM((SUBLANE, LANE), F32),
            pltpu.VMEM((nb, tr + SUBLANE, 2 * ML_WIDTH), F32),
        ],
        compiler_params=pltpu.CompilerParams(
            dimension_semantics=("arbitrary", "arbitrary"), vmem_limit_bytes=VMEM_LIMIT),
        name="mlstm",
    )(ml, gates, c0, n0, m0, conv0, cw, cb, gb, ng)


def _rwkv_kernel(c_ref, sh_ref, s0_ref, mu_ref, w0_ref, wwa_ref, a0_ref, g2_ref, kk_ref, ka_ref, rk_ref,
                 gng_ref, gnb_ref, out_ref, s1_ref, s_ref, last_ref, *, tr, nb, L):
    W = RW_WIDTH
    GW = RW_GROUP * RW_HD
    GL = RW_GROUP * L
    NGRP = RW_HEADS // RW_GROUP
    ci = pl.program_id(1)
    nc = pl.num_programs(1)

    r2 = _iota((GL, GW), 0)
    c2 = _iota((GL, GW), 1)
    bd = (r2 // L) == (c2 // RW_HD)
    rg = _iota((GL, GL), 0)
    cg = _iota((GL, GL), 1)
    s_lower = rg > cg
    i_lower = rg >= cg

    @pl.when(ci == 0)
    def _init():
        sbd = (_iota((GW, GW), 0) // RW_HD) == (_iota((GW, GW), 1) // RW_HD)
        for bi in range(nb):
            last_ref[bi, 0:1, :] = sh_ref[bi]
            for gi in range(NGRP):
                x = s0_ref[bi, gi * GW:(gi + 1) * GW, :]
                s_ref[bi * NGRP + gi] = jnp.where(sbd, jnp.concatenate([x] * RW_GROUP, axis=1), 0.0)

    c = c_ref[...].reshape(nb * tr, RW_COLS)
    prev = pltpu.roll(c, 1, axis=0)
    rowid = _iota(c.shape, 0)
    for bi in range(nb):
        prev = jnp.where(rowid == bi * tr, last_ref[bi, 0:1, :], prev)
        last_ref[bi, 0:1, :] = c[(bi + 1) * tr - 1:(bi + 1) * tr, :]
    xs = c + (prev - c) * mu_ref[...]
    r = xs[:, 0:W]
    k = xs[:, W:2 * W]
    v = xs[:, 2 * W:3 * W]
    slab = xs[:, 3 * W:3 * W + DECAY_LORA + AAA_LORA]
    gd = xs[:, 3 * W + DECAY_LORA + AAA_LORA:]
    lane = _iota(slab.shape, 1)
    t_in = jnp.where(lane < DECAY_LORA, jnp.tanh(slab), slab)
    la = _dot(t_in.astype(BF16), wwa_ref[...])
    w_log = -_softplus(-(w0_ref[...] + la[:, 0:W])) - 0.5
    lw_all = -jnp.exp(w_log)
    a_all = _sigmoid(a0_ref[...] + la[:, W:2 * W])
    g = _dot(_sigmoid(gd).astype(BF16), g2_ref[...])

    rs = _iota((W, W), 0)
    cs = _iota((W, W), 1)
    seg = jnp.where((rs // RW_HD) == (cs // RW_HD), 1.0, 0.0).astype(BF16)

    kk = k * kk_ref[...]
    kn_all = k * (1.0 + (a_all - 1.0) * ka_ref[...])
    ss = _dot_exact_rhs(kk * kk, seg, 2)
    kap_all = kk / jnp.maximum(jnp.sqrt(ss), 1e-12)
    bonus = _dot_exact_rhs(r * kn_all * rk_ref[...], seg, 2) * v

    rl = _iota((L, L), 0)
    cl_ = _iota((L, L), 1)
    tri = jnp.where(rl >= cl_, 1.0, 0.0).astype(BF16)

    def stack(x):
        return jnp.where(bd, jnp.concatenate([x] * RW_GROUP, axis=0), 0.0).astype(BF16)

    seqs = []
    for bi in range(nb):
        rows = slice(bi * tr, (bi + 1) * tr)
        lw = _pad_rows(lw_all[rows], L)
        kap = _pad_rows(kap_all[rows], L)
        kn_p = _pad_rows(kn_all[rows], L)
        a_p = _pad_rows(a_all[rows], L)
        v_p = _pad_rows(v[rows], L)
        r_p = _pad_rows(r[rows], L)
        cum = _dot_exact_lhs(tri, lw, 3)
        cum_last = cum[L - 1:L, :]
        e_neg = jnp.exp(-cum)
        e_rem = jnp.exp(cum_last - cum)
        seqs.append(dict(
            abar=-kap * jnp.exp(cum - lw), btil=kap * a_p * e_neg, ktil=kn_p * e_neg, rbar=r_p * jnp.exp(cum),
            bhat=kap * a_p * e_rem, khat=kn_p * e_rem, v=v_p, g_last=jnp.exp(cum_last)))

    chains = [(bi, gi) for bi in range(nb) for gi in range(NGRP)]
    ch = []
    for bi, gi in chains:
        q = seqs[bi]
        ls = slice(gi * GW, (gi + 1) * GW)
        a_s, b_s, k_s, r_s, v_s = (stack(q[n][:, ls]) for n in ("abar", "btil", "ktil", "rbar", "v"))
        ch.append(dict(
            ar=jnp.concatenate([a_s, r_s], axis=0),
            bk=jnp.concatenate([b_s, k_s], axis=0),
            bkh=jnp.concatenate([stack(q["bhat"][:, ls]), stack(q["khat"][:, ls])], axis=0),
            v_s=v_s, g_last=q["g_last"][:, ls], s=s_ref[bi * NGRP + gi]))
    for c_ in ch:
        p = _dot_nt(c_["ar"], c_["bk"])
        c_["npow"] = jnp.where(s_lower, p[0:GL, 0:GL], 0.0)
        c_["a_ak"] = jnp.where(s_lower, p[0:GL, GL:2 * GL], 0.0).astype(BF16)
        c_["r_b"] = jnp.where(i_lower, p[GL:2 * GL, 0:GL], 0.0).astype(BF16)
        c_["r_k"] = jnp.where(i_lower, p[GL:2 * GL, GL:2 * GL], 0.0).astype(BF16)
    for c_ in ch:
        c_["q0"] = _dot_nt(c_["ar"], c_["s"].astype(BF16))
    for c_ in ch:
        c_["u"] = c_["q0"][0:GL] + _dot(c_["a_ak"], c_["v_s"])
    lvls = L.bit_length() - 1
    for lvl in range(lvls):
        for c_ in ch:
            nb16 = c_["npow"].astype(BF16)
            c_["u"] = c_["u"] + _dot(nb16, c_["u"].astype(BF16))
            if lvl < lvls - 1:
                c_["npow"] = _dot(nb16, nb16)
    for c_ in ch:
        y = (c_["q0"][GL:2 * GL] + _dot(c_["r_b"], c_["u"].astype(BF16)) + _dot(c_["r_k"], c_["v_s"]))
        yg = y[0:L]
        for j in range(1, RW_GROUP):
            yg = yg + y[j * L:(j + 1) * L]
        c_["yg"] = yg
    for (bi, gi), c_ in zip(chains, ch):
        uv_t = jnp.concatenate([c_["u"], c_["v_s"].astype(F32)], axis=0).T.astype(BF16)
        s_ref[bi * NGRP + gi] = c_["s"] * c_["g_last"] + _dot(uv_t, c_["bkh"])
    y_rows = [jnp.concatenate([ch[bi * NGRP + gi]["yg"] for gi in range(NGRP)], axis=1)[0:tr]
              for bi in range(nb)]

    y_all = jnp.concatenate(y_rows, axis=0) if nb > 1 else y_rows[0]
    inv = 1.0 / RW_HD
    mu_ = _dot_exact_rhs(y_all, seg, 2) * inv
    dy = y_all - mu_
    var = _dot_exact_rhs(dy * dy, seg, 2) * inv
    yn = dy * lax.rsqrt(var + GN_EPS) * gng_ref[...] + gnb_ref[...]
    out_ref[...] = ((yn + bonus) * g).reshape(nb, tr, W)

    @pl.when(ci == nc - 1)
    def _state_out():
        for bi in range(nb):
            for gi in range(NGRP):
                s = s_ref[bi * NGRP + gi]
                f = s[:, 0:RW_HD]
                for j in range(1, RW_GROUP):
                    f = f + s[:, j * RW_HD:(j + 1) * RW_HD]
                s1_ref[bi, gi * GW:(gi + 1) * GW, :] = f


def _rwkv(cols, shift0, s0, mu, w0, wwa, a0, g2, kk, ka, rk, gng, gnb, tr, nb, chunk):
    b, t, _ = cols.shape
    nc = t // tr
    assert b % nb == 0 and t % tr == 0 and tr <= chunk
    const2 = lambda i, j: (0, 0)
    vec = pl.BlockSpec((1, RW_WIDTH), const2)
    return pl.pallas_call(
        functools.partial(_rwkv_kernel, tr=tr, nb=nb, L=chunk),
        grid=(b // nb, nc),
        in_specs=[
            pl.BlockSpec((nb, tr, RW_COLS), lambda i, j: (i, j, 0)),
            pl.BlockSpec((nb, 1, RW_COLS), lambda i, j: (i, 0, 0)),
            pl.BlockSpec((nb, RW_WIDTH, RW_HD), lambda i, j: (i, 0, 0)),
            pl.BlockSpec((1, RW_COLS), const2),
            vec,
            pl.BlockSpec((DECAY_LORA + AAA_LORA, 2 * RW_WIDTH), const2),
            vec,
            pl.BlockSpec((GATE_LORA, RW_WIDTH), const2),
            vec, vec, vec, vec, vec,
        ],
        out_specs=[
            pl.BlockSpec((nb, tr, RW_WIDTH), lambda i, j: (i, j, 0)),
            pl.BlockSpec((nb, RW_WIDTH, RW_HD), lambda i, j: (i, 0, 0)),
        ],
        out_shape=[
            jax.ShapeDtypeStruct((b, t, RW_WIDTH), F32),
            jax.ShapeDtypeStruct((b, RW_WIDTH, RW_HD), F32),
        ],
        scratch_shapes=[pltpu.VMEM((nb * (RW_HEADS // RW_GROUP), RW_GROUP * RW_HD, RW_GROUP * RW_HD), F32),
                        pltpu.VMEM((nb, SUBLANE, RW_COLS), F32)],
        compiler_params=pltpu.CompilerParams(
            dimension_semantics=("arbitrary", "arbitrary"), vmem_limit_bytes=VMEM_LIMIT),
        name="rwkv",
    )(cols, shift0, s0, mu, w0, wwa, a0, g2, kk, ka, rk, gng, gnb)


def _extract_top(works, idx, n, ranked=True):
    ranks = [jnp.full(w.shape, NOT_TAKEN, F32) for w in works]
    vals = [[] for _ in works]
    for j in range(n):
        ms = [jnp.max(w, axis=0, keepdims=True) for w in works]
        if idx is None:
            sels = [w == m for w, m in zip(works, ms)]
        else:
            firsts = [jnp.min(jnp.where(w == m, idx, PAST_ALL_IDX), axis=0, keepdims=True) for w, m in zip(works, ms)]
            sels = [idx == f for f in firsts]
        if ranked:
            ranks = [jnp.where(sel, float(j), r) for sel, r in zip(sels, ranks)]
        works = [jnp.where(sel, -jnp.inf, w) for sel, w in zip(sels, works)]
        for v, m in zip(vals, ms):
            v.append(m)
    return vals, (ranks if ranked else works)


def _miscount(rank, n):
    taken = jnp.sum(jnp.where(rank < float(n), 1.0, 0.0), axis=0, keepdims=True)
    return jnp.abs(taken - float(n))


def _rows_bf16(row, rows):
    packed = jnp.broadcast_to(row, (16, row.shape[1])).astype(BF16)
    return jnp.concatenate([packed] * (rows // 16), axis=0)


def _peer_kernel(x_ref, ml_ref, rw_ref, woml_ref, worw_ref, gffn_ref, gfin_ref, wqt_ref, keys_ref,
                 ua_ref, ub_ref, vta_ref, vtb_ref, y_ref,
                 ht_ref, qt_ref, rank_ref, e_ref, vals_ref, r2b_ref, e2b_ref, cnt_ref, gate_ref,
                 p_ref, acc_ref, *, tm, ec):
    e = pl.program_id(1)
    ne = pl.num_programs(1)
    ng = tm // LANE
    K = PEER_TOPK
    nslab = ec // N_KEYS
    act_ref = qt_ref

    @pl.when(e == 0)
    def _select():
        x1 = (x_ref[...] + _dot(ml_ref[...].astype(BF16), woml_ref[...])
              + _dot(rw_ref[...].astype(BF16), worw_ref[...]))
        y_ref[...] = x1
        h = x1 * lax.rsqrt(jnp.mean(x1 * x1, axis=-1, keepdims=True) + NORM_EPS) * gffn_ref[...]
        ht = h.T.astype(BF16)
        ht_ref[...] = ht
        qt_ref[...] = _dot(wqt_ref[...], ht)
        acc_ref[...] = jnp.zeros(acc_ref.shape, F32)

        def score_body(hp, carry):
            q = qt_ref[pl.ds(pl.multiple_of(hp * PEER_HALF, PEER_HALF), PEER_HALF), :]
            e_ref[hp] = _dot(keys_ref[hp], q.astype(BF16))
            return carry

        lax.fori_loop(0, 2 * PEER_HEADS, score_body, 0)

        key_idx = _iota((N_KEYS, LANE), 0).astype(F32)
        groups = [slice(gi * LANE, (gi + 1) * LANE) for gi in range(ng)]

        def topk_groups(hp, exact_ties):
            scores = [e_ref[hp, :, lanes] for lanes in groups]
            vals, ranks = _extract_top(scores, key_idx if exact_ties else None, K)
            bad = None
            for lanes, v, rank in zip(groups, vals, ranks):
                rank_ref[hp, :, lanes] = rank
                vals_ref[hp, :, lanes] = jnp.concatenate(v, axis=0)
                miss = _miscount(rank, K)
                bad = miss if bad is None else jnp.maximum(bad, miss)
            return bad

        def topk_body(hp, carry):
            bad = topk_groups(hp, False)

            @pl.when(jnp.max(bad) > 0.0)
            def _redo():
                topk_groups(hp, True)

            return carry

        lax.fori_loop(0, 2 * PEER_HEADS, topk_body, 0)

        sub8 = _iota((SUBLANE, LANE), 0)

        def cand_groups(h, exact_ties):
            works, v1s = [], []
            slab_a = [a_i for a_i in range(K) for _ in range(0, K // (a_i + 1), SUBLANE)]
            for lanes in groups:
                v1 = vals_ref[2 * h, :, lanes]
                v2 = vals_ref[2 * h + 1, :, lanes]
                slabs = []
                for a_i in range(K):
                    nb = K // (a_i + 1)
                    for b0 in range(0, nb, SUBLANE):
                        rows = v1[a_i:a_i + 1, :] + v2[b0:b0 + SUBLANE, :]
                        slabs.append(jnp.where(sub8 + b0 < nb, rows, -jnp.inf))
                works.append(jnp.concatenate(slabs, axis=0))
                v1s.append(v1)
            idx = None
            if exact_ties:
                idx = jnp.concatenate([(sub8 + (b0 + a_i * K)).astype(F32) for a_i in range(K)
                                       for b0 in range(0, K // (a_i + 1), SUBLANE)], axis=0)
            vals, left = _extract_top(works, idx, K, ranked=False)
            valid = jnp.concatenate([jnp.where(sub8 + b0 < K // (a_i + 1), 1.0, 0.0) for a_i in range(K)
                                     for b0 in range(0, K // (a_i + 1), SUBLANE)], axis=0)
            bad = None
            for lanes, v1, v, rest in zip(groups, v1s, vals, left):
                z = jnp.ones_like(v[0])
                for j in range(1, K):
                    z = z + jnp.exp(v[j] - v[0])
                picked = jnp.where(rest == -jnp.inf, valid, 0.0)
                r1 = rank_ref[2 * h, :, lanes]
                cnt = jnp.zeros((N_KEYS, LANE), F32)
                for a_i in range(K):
                    ca = None
                    for si, sa in enumerate(slab_a):
                        if sa == a_i:
                            part = jnp.sum(picked[si * SUBLANE:(si + 1) * SUBLANE], axis=0, keepdims=True)
                            ca = part if ca is None else ca + part
                    cnt = jnp.where(r1 == float(a_i), ca, cnt)
                cnt_ref[h, :, lanes] = cnt
                e1 = jnp.exp(e_ref[2 * h, :, lanes] - v1[0:1, :])
                gate_ref[h, :, lanes] = jnp.where(r1 < float(K), 0.5 * e1 / z, 0.0)
                miss = jnp.abs(jnp.sum(picked, axis=0, keepdims=True) - float(K))
                bad = miss if bad is None else jnp.maximum(bad, miss)
            return bad

        def cand_body(h, carry):
            bad = cand_groups(h, False)

            @pl.when(jnp.max(bad) > 0.0)
            def _redo():
                cand_groups(h, True)

            r2b_ref[h] = rank_ref[2 * h + 1].astype(BF16)
            e2b_ref[h] = jnp.exp(e_ref[2 * h + 1] - vals_ref[2 * h + 1, 0:1, :]).astype(BF16)
            return carry

        lax.fori_loop(0, PEER_HEADS, cand_body, 0)

    key1 = pl.ds(pl.multiple_of(e * nslab, nslab), nslab)
    hs = nslab // 2
    hrows = hs * N_KEYS
    for half in range(2):
        act_ref[half * hrows:(half + 1) * hrows, :] = _dot((ua_ref, ub_ref)[half][...], ht_ref[...])
    def gate_half(half):
        for gi in range(ng):
            lanes = slice(gi * LANE, (gi + 1) * LANE)
            for jj in range(hs):
                j = half * hs + jj
                w = None
                for h in range(PEER_HEADS):
                    r2 = r2b_ref[h, :, lanes]
                    e2 = e2b_ref[h, :, lanes]
                    cw = cnt_ref[h, key1, lanes]
                    gw = gate_ref[h, key1, lanes]
                    hit = r2 < _rows_bf16(cw[j:j + 1, :], N_KEYS)
                    term = jnp.where(hit, e2, jnp.zeros_like(e2)) * _rows_bf16(gw[j:j + 1, :], N_KEYS)
                    w = term if w is None else w + term
                rows = slice(j * N_KEYS, (j + 1) * N_KEYS)
                ab = act_ref[rows, lanes].astype(BF16)
                p_ref[rows, lanes] = (w * ab) * (1.0 + lax.erf(ab * 0.7071067811865476))

    for half in range(2):
        gate_half(half)
        acc_ref[...] += _dot((vta_ref, vtb_ref)[half][0], p_ref[half * hrows:(half + 1) * hrows, :])

    @pl.when(e == ne - 1)
    def _finish():
        x2 = y_ref[...] + acc_ref[...].T
        y_ref[...] = x2 * lax.rsqrt(jnp.mean(x2 * x2, axis=-1, keepdims=True) + NORM_EPS) * gfin_ref[...]


def _peer(x2d, ml2d, rw2d, woml, worw, gffn, gfin, wqt, keys, u, vt, tm, ec):
    n = x2d.shape[0]
    n_exp = u.shape[0]
    assert ec == 8 * N_KEYS and tm % LANE == 0 and n % tm == 0 and n_exp % ec == 0
    tok = lambda i, e: (i, 0)
    const2 = lambda i, e: (0, 0)
    qrows = 2 * PEER_HEADS * PEER_HALF
    eh = ec // 2
    once = pl.Buffered(1)
    return pl.pallas_call(
        functools.partial(_peer_kernel, tm=tm, ec=ec),
        grid=(n // tm, n_exp // ec),
        in_specs=[
            pl.BlockSpec((tm, D_MODEL), tok),
            pl.BlockSpec((tm, ML_WIDTH), tok),
            pl.BlockSpec((tm, RW_WIDTH), tok),
            pl.BlockSpec((ML_WIDTH, D_MODEL), const2, pipeline_mode=once),
            pl.BlockSpec((RW_WIDTH, D_MODEL), const2, pipeline_mode=once),
            pl.BlockSpec((1, D_MODEL), const2),
            pl.BlockSpec((1, D_MODEL), const2),
            pl.BlockSpec((qrows, D_MODEL), const2, pipeline_mode=once),
            pl.BlockSpec((2 * PEER_HEADS, N_KEYS, PEER_HALF), lambda i, e: (0, 0, 0), pipeline_mode=once),
            pl.BlockSpec((eh, D_MODEL), lambda i, e: (2 * e, 0)),
            pl.BlockSpec((eh, D_MODEL), lambda i, e: (2 * e + 1, 0)),
            pl.BlockSpec((1, D_MODEL, eh), lambda i, e: (2 * e, 0, 0)),
            pl.BlockSpec((1, D_MODEL, eh), lambda i, e: (2 * e + 1, 0, 0)),
        ],
        out_specs=pl.BlockSpec((tm, D_MODEL), tok),
        out_shape=jax.ShapeDtypeStruct((n, D_MODEL), F32),
        scratch_shapes=[
            pltpu.VMEM((D_MODEL, tm), BF16),
            pltpu.VMEM((qrows, tm), F32),
            pltpu.VMEM((2 * PEER_HEADS, N_KEYS, tm), F32),
            pltpu.VMEM((2 * PEER_HEADS, N_KEYS, tm), F32),
            pltpu.VMEM((2 * PEER_HEADS, PEER_TOPK, tm), F32),
            pltpu.VMEM((PEER_HEADS, N_KEYS, tm), BF16),
            pltpu.VMEM((PEER_HEADS, N_KEYS, tm), BF16),
            pltpu.VMEM((PEER_HEADS, N_KEYS, tm), F32),
            pltpu.VMEM((PEER_HEADS, N_KEYS, tm), F32),
            pltpu.VMEM((ec, tm), BF16),
            pltpu.VMEM((D_MODEL, tm), F32),
        ],
        compiler_params=pltpu.CompilerParams(
            dimension_semantics=("arbitrary", "arbitrary"), vmem_limit_bytes=VMEM_LIMIT),
        name="peer",
    )(x2d, ml2d, rw2d, woml, worw, gffn, gfin, wqt, keys, u, u, vt, vt)


def _prep_weights(norm_mix_g, w_in, ml_conv_w, ml_conv_b, ml_b_i, ml_b_f, ml_norm_g,
                  rw_mu, rw_w0, rw_w2, rw_a0, rw_a2, rw_g2, rw_k_k, rw_k_a, rw_r_k, rw_gn_g, rw_gn_b,
                  w_out, norm_ffn_g, peer_w_q, peer_sub_keys, peer_u, peer_v, norm_final_g):
    assert w_in.shape[0] == 1, "one layer"
    w = w_in[0]
    wg = jnp.pad(w[:, ML_QKVO:ML_QKVO + 2 * ML_HEADS], ((0, 0), (0, LANE - 2 * ML_HEADS)))
    wg_hi = wg.astype(BF16)
    wg_lo = (wg - wg_hi.astype(F32)).astype(BF16)
    zeros = jnp.zeros((DECAY_LORA, RW_WIDTH), F32)
    wwa = jnp.concatenate([jnp.concatenate([rw_w2[0], zeros], axis=1),
                           jnp.concatenate([zeros, rw_a2[0]], axis=1)], axis=0)
    gate_bias = jnp.pad(jnp.concatenate([ml_b_i[0], ml_b_f[0]]), (0, LANE - 2 * ML_HEADS))[None, :]
    row = lambda a: a.reshape(1, -1)
    return dict(
        g_mix=row(norm_mix_g[0]),
        wml=w[:, 0:ML_QKVO].astype(BF16),
        wg=jnp.stack([wg_hi, wg_lo]),
        wrw=w[:, ML_QKVO + 2 * ML_HEADS:].astype(BF16),
        conv_w=ml_conv_w[0], conv_b=row(ml_conv_b[0]), gate_bias=gate_bias, ml_norm_g=row(ml_norm_g[0]),
        mu=row(rw_mu[0]), w0=row(rw_w0[0]), wwa=wwa.astype(BF16), a0=row(rw_a0[0]),
        g2=rw_g2[0].astype(BF16), k_k=row(rw_k_k[0]), k_a=row(rw_k_a[0]), r_k=row(rw_r_k[0]),
        gn_g=row(rw_gn_g[0]), gn_b=row(rw_gn_b[0]),
        wo_ml=w_out[0, 0:ML_WIDTH].astype(BF16), wo_rw=w_out[0, ML_WIDTH:].astype(BF16),
        g_ffn=row(norm_ffn_g[0]), g_fin=row(norm_final_g),
        wqt=peer_w_q[0].T.astype(BF16),
        keys=peer_sub_keys[0].reshape(2 * PEER_HEADS, N_KEYS, PEER_HALF).astype(BF16),
        u=peer_u[0].astype(BF16),
        vt=peer_v[0].astype(BF16).reshape(-1, PEER_EXPERT_CHUNK // 2, D_MODEL).transpose(0, 2, 1),
    )


def _rwkv_tiling(t):
    tr = min(t, RW_CHUNK)
    chunk = max(tr, RW_MIN_CHUNK)
    return tr, chunk, RW_STEP_ROWS // chunk // 2 if chunk < RW_CHUNK else RW_STEP_ROWS // chunk


def _mlstm_tiling(t, b):
    tr = min(t, ML_CHUNK)
    chunk = max(tr, ML_MIN_CHUNK)
    return tr, min(b, ML_STEP_ROWS // chunk), chunk


def _trunk(x, states, wp, tr_rw, rw_chunk, rw_batch, tm_in, tm_peer, ec):
    b, t, d = x.shape
    c0, n0, m0, conv0, s0, shift0 = (s[0] for s in states)
    x2d = x.reshape(b * t, d)
    ml, gates, rw = _proj_in(x2d, wp["g_mix"], wp["wml"], wp["wg"], wp["wrw"], tm_in)
    ml3 = ml.reshape(b, t, ML_QKVO)
    rw3 = rw.reshape(b, t, RW_COLS)
    ml_out, c1, n1, m1, conv1 = _mlstm(
        ml3, gates.reshape(b, t, LANE), c0, n0, m0.reshape(b, 1, ML_HEADS), conv0,
        wp["conv_w"], wp["conv_b"], wp["gate_bias"], wp["ml_norm_g"], *_mlstm_tiling(t, b))
    rw_out, s1 = _rwkv(
        rw3, shift0[:, None, :], s0.reshape(b, RW_WIDTH, RW_HD), wp["mu"], wp["w0"], wp["wwa"], wp["a0"], wp["g2"],
        wp["k_k"], wp["k_a"], wp["r_k"], wp["gn_g"], wp["gn_b"], tr_rw, min(b, rw_batch), rw_chunk)
    y = _peer(x2d, ml_out.reshape(b * t, ML_WIDTH), rw_out.reshape(b * t, RW_WIDTH),
              wp["wo_ml"], wp["wo_rw"], wp["g_ffn"], wp["g_fin"], wp["wqt"], wp["keys"],
              wp["u"], wp["vt"], tm_peer, ec)
    new_states = (c1[None], n1[None], m1.reshape(b, ML_HEADS)[None], conv1[None],
                  s1.reshape(b, RW_HEADS, RW_HD, RW_HD)[None], rw3[:, -1][None])
    return y.reshape(b, t, d), new_states


def kernel(x_prompt, x_sample, state_mlstm_C, state_mlstm_n, state_mlstm_m, state_mlstm_conv, state_rwkv_S, state_rwkv_shift, norm_mix_g, w_in, ml_conv_w, ml_conv_b, ml_b_i, ml_b_f, ml_norm_g, rw_mu, rw_w0, rw_w2, rw_a0, rw_a2, rw_g2, rw_k_k, rw_k_a, rw_r_k, rw_gn_g, rw_gn_b, w_out, norm_ffn_g, peer_w_q, peer_sub_keys, peer_u, peer_v, norm_final_g):
    wp = _prep_weights(norm_mix_g, w_in, ml_conv_w, ml_conv_b, ml_b_i, ml_b_f, ml_norm_g,
                       rw_mu, rw_w0, rw_w2, rw_a0, rw_a2, rw_g2, rw_k_k, rw_k_a, rw_r_k, rw_gn_g, rw_gn_b,
                       w_out, norm_ffn_g, peer_w_q, peer_sub_keys, peer_u, peer_v, norm_final_g)
    bp = x_prompt.shape[0]
    z = lambda *s: jnp.zeros((1, bp) + s, F32)
    prompt_states = (z(ML_HEADS, ML_HD, ML_HD), z(ML_HEADS, ML_HD), jnp.full((1, bp, ML_HEADS), M_INIT, F32),
                     z(CONV_W - 1, 2 * ML_WIDTH), z(RW_HEADS, RW_HD, RW_HD), z(RW_COLS))
    sample_states = (state_mlstm_C, state_mlstm_n, state_mlstm_m, state_mlstm_conv,
                     state_rwkv_S, state_rwkv_shift)
    tp, ts = x_prompt.shape[1], x_sample.shape[1]
    np_, ns = bp * tp, x_sample.shape[0] * ts
    y_p, st_p = _trunk(x_prompt, prompt_states, wp, *_rwkv_tiling(tp),
                       min(np_, PROJ_TILE), min(np_, PEER_TILE), PEER_EXPERT_CHUNK)
    y_s, st_s = _trunk(x_sample, sample_states, wp, *_rwkv_tiling(ts),
                       min(ns, PROJ_TILE), min(ns, PEER_TILE), PEER_EXPERT_CHUNK)
    return (y_p, y_s) + tuple(st_p) + tuple(st_s)
```

```python
import functools

import jax
import jax.numpy as jnp
from jax import lax
from jax.experimental import pallas as pl
from jax.experimental.pallas import tpu as pltpu

F32 = jnp.float32
BF16 = jnp.bfloat16

D_MODEL = 1024
ML_HEADS = 4
ML_HD = 128
ML_WIDTH = ML_HEADS * ML_HD
CONV_W = 4
RW_HEADS = 8
RW_HD = 64
RW_WIDTH = RW_HEADS * RW_HD
DECAY_LORA = 64
AAA_LORA = 64
GATE_LORA = 128
RW_COLS = 3 * RW_WIDTH + DECAY_LORA + AAA_LORA + GATE_LORA
ML_QKVO = 4 * ML_WIDTH
PEER_HEADS = 8
N_KEYS = 128
PEER_TOPK = 16
PEER_HALF = 128
NORM_EPS = 1e-6
ML_NORM_EPS = 1e-6
GN_EPS = RW_HD * 1e-5
M_INIT = -1e30
NEG_BIG = -1e30
NOT_TAKEN = 99.0
PAST_ALL_IDX = 1e9

LANE = 128
SUBLANE = 8
ML_CHUNK = 128
ML_MIN_CHUNK = 16
ML_STEP_ROWS = 128
RW_CHUNK = 64
RW_GROUP = 4
RW_MIN_CHUNK = 16
RW_STEP_ROWS = 256
PROJ_TILE = 512
PEER_TILE = 512
PEER_EXPERT_CHUNK = 1024
VMEM_LIMIT = 56 * 1024 * 1024


def _dot(a, b):
    return jnp.dot(a, b, preferred_element_type=F32)


def _dot_nt(a, b):
    return lax.dot_general(a, b, (((1,), (1,)), ((), ())), preferred_element_type=F32)


def _split_bf16(x, n):
    parts = []
    r = x
    for _ in range(n):
        p = r.astype(BF16)
        parts.append(p)
        r = r - p.astype(F32)
    return parts


def _dot_exact_lhs(mask_bf16, x, n):
    return sum(_dot(mask_bf16, p) for p in _split_bf16(x, n))


def _dot_exact_rhs(x, mask_bf16, n):
    return sum(_dot(p, mask_bf16) for p in _split_bf16(x, n))


def _sigmoid(x):
    return 1.0 / (1.0 + jnp.exp(-x))


def _softplus(x):
    return jnp.maximum(x, 0.0) + jnp.log1p(jnp.exp(-jnp.abs(x)))


def _iota(shape, dim):
    return lax.broadcasted_iota(jnp.int32, shape, dim)


def _pad_rows(x, rows, value=0.0):
    if x.shape[0] == rows:
        return x
    return jnp.concatenate([x, jnp.full((rows - x.shape[0], x.shape[1]), value, x.dtype)], axis=0)


def _proj_in_kernel(x_ref, g_ref, wml_ref, wg_ref, wrw_ref, ml_ref, gate_ref, rw_ref):
    x = x_ref[...]
    xn = x * lax.rsqrt(jnp.mean(x * x, axis=-1, keepdims=True) + NORM_EPS) * g_ref[...]
    xb = xn.astype(BF16)
    ml_ref[...] = _dot(xb, wml_ref[...])
    rw_ref[...] = _dot(xb, wrw_ref[...])
    xlo = (xn - xb.astype(F32)).astype(BF16)
    gate_ref[...] = _dot(xb, wg_ref[0]) + _dot(xlo, wg_ref[0]) + _dot(xb, wg_ref[1])


def _proj_in(x2d, g, wml, wg, wrw, tm):
    n = x2d.shape[0]
    const2 = lambda i: (0, 0)
    return pl.pallas_call(
        _proj_in_kernel,
        grid=(n // tm,),
        in_specs=[
            pl.BlockSpec((tm, D_MODEL), lambda i: (i, 0)),
            pl.BlockSpec((1, D_MODEL), const2),
            pl.BlockSpec((D_MODEL, ML_QKVO), const2),
            pl.BlockSpec((2, D_MODEL, LANE), lambda i: (0, 0, 0)),
            pl.BlockSpec((D_MODEL, RW_COLS), const2),
        ],
        out_specs=[
            pl.BlockSpec((tm, ML_QKVO), lambda i: (i, 0)),
            pl.BlockSpec((tm, LANE), lambda i: (i, 0)),
            pl.BlockSpec((tm, RW_COLS), lambda i: (i, 0)),
        ],
        out_shape=[
            jax.ShapeDtypeStruct((n, ML_QKVO), F32),
            jax.ShapeDtypeStruct((n, LANE), F32),
            jax.ShapeDtypeStruct((n, RW_COLS), F32),
        ],
        compiler_params=pltpu.CompilerParams(
            dimension_semantics=("arbitrary",), vmem_limit_bytes=VMEM_LIMIT),
        name="proj_in",
    )(x2d, g, wml, wg, wrw)


def _mlstm_kernel(ml_ref, gate_ref, c0_ref, n0_ref, m0_ref, conv0_ref, cw_ref, cb_ref, gb_ref, ng_ref,
                  out_ref, c1_ref, n1_ref, m1_ref, conv1_ref,
                  caug_ref, m_ref, ext_ref, *, tr, nb, L):
    c = pl.program_id(1)
    nc = pl.num_programs(1)
    hist = CONV_W - 1

    @pl.when(c == 0)
    def _init():
        m_ref[...] = jnp.zeros(m_ref.shape, F32)
        for bi in range(nb):
            for h in range(ML_HEADS):
                caug_ref[bi * ML_HEADS + h, :, 0:ML_HD] = c0_ref[bi, h]
                nrow = n0_ref[bi, h:h + 1, :]
                caug_ref[bi * ML_HEADS + h, :, ML_HD:2 * ML_HD] = jnp.broadcast_to(nrow, (ML_HD, ML_HD)).T
            m_ref[bi:bi + 1, 0:ML_HEADS] = m0_ref[bi]
            ext_ref[bi, SUBLANE - hist:SUBLANE, :] = conv0_ref[bi]

    row = _iota((L, L), 0)
    col = _iota((L, L), 1)
    causal = row >= col
    tri = jnp.where(causal, 1.0, 0.0).astype(BF16)
    ones = jnp.ones((L, ML_HD), BF16)

    seqs = []
    for bi in range(nb):
        ext_ref[bi, SUBLANE:SUBLANE + tr, :] = ml_ref[bi, :, 0:2 * ML_WIDTH]
        first = SUBLANE - hist
        acc = cb_ref[...] + ext_ref[bi, first:first + tr, :] * cw_ref[0:1, :]
        for j in range(1, CONV_W):
            acc = acc + ext_ref[bi, first + j:first + j + tr, :] * cw_ref[j:j + 1, :]
        qk = acc * _sigmoid(acc)
        tail = ext_ref[bi, tr + first:tr + SUBLANE, :]
        ext_ref[bi, first:SUBLANE, :] = tail

        @pl.when(c == nc - 1)
        def _conv_out():
            conv1_ref[bi] = tail

        g = gate_ref[bi] + gb_ref[...]
        i_all = _pad_rows(g, L, NEG_BIG)
        b_col = _dot_exact_lhs(tri, _pad_rows(-_softplus(-g), L, 0.0), 3)
        seqs.append(dict(
            q=_pad_rows(qk[:, 0:ML_WIDTH], L),
            k=_pad_rows(qk[:, ML_WIDTH:2 * ML_WIDTH] * (ML_HD ** -0.5), L),
            v=_pad_rows(ml_ref[bi, :, 2 * ML_WIDTH:3 * ML_WIDTH], L),
            b_col=b_col, b_t=b_col.T, i_t=i_all.T))

    chains = [(bi, h) for bi in range(nb) for h in range(ML_HEADS)]
    ch = []
    for bi, h in chains:
        q = seqs[bi]
        sl = slice(h * ML_HD, (h + 1) * ML_HD)
        bc = q["b_col"][:, ML_HEADS + h:ML_HEADS + h + 1]
        br = q["b_t"][ML_HEADS + h:ML_HEADS + h + 1, :]
        ir = q["i_t"][h:h + 1, :]
        m_prev = m_ref[bi:bi + 1, h:h + 1]
        logd = jnp.where(causal, bc - br + ir, -jnp.inf)
        linter = bc + m_prev
        m_t = jnp.maximum(linter, jnp.max(logd, axis=1, keepdims=True))
        m_new = m_t[L - 1:L, :]
        b_last = bc[L - 1:L, :]
        ch.append(dict(
            sl=sl, m_t=m_t, d=jnp.exp(logd - m_t), s_inter=jnp.exp(linter - m_t), m_new=m_new,
            s_state=jnp.exp(b_last + m_prev - m_new), w_row=jnp.exp(b_last - br + ir - m_new),
            qh=q["q"][:, sl].astype(BF16), k_t=q["k"][:, sl].T,
            vaug=jnp.concatenate([q["v"][:, sl].astype(BF16), ones], axis=1),
            caug=caug_ref[bi * ML_HEADS + h]))
    for c_ in ch:
        c_["s"] = (_dot(c_["qh"], c_["k_t"].astype(BF16)) * c_["d"]).astype(BF16)
    for c_ in ch:
        c_["num"] = c_["s_inter"] * _dot(c_["qh"], c_["caug"].astype(BF16)) + _dot(c_["s"], c_["vaug"])
    for (bi, h), c_ in zip(chains, ch):
        num = c_["num"]
        den = num[:, ML_HD:2 * ML_HD]
        hh = num[:, 0:ML_HD] / jnp.maximum(jnp.abs(den), jnp.exp(-c_["m_t"]))
        mu = jnp.mean(hh, axis=-1, keepdims=True)
        dv = hh - mu
        var = jnp.mean(dv * dv, axis=-1, keepdims=True)
        y = dv * lax.rsqrt(var + ML_NORM_EPS) * ng_ref[:, c_["sl"]]
        o = ml_ref[bi, :, 3 * ML_WIDTH + h * ML_HD:3 * ML_WIDTH + (h + 1) * ML_HD]
        out_ref[bi, :, c_["sl"]] = y[0:tr] * _sigmoid(o)
    for (bi, h), c_ in zip(chains, ch):
        upd = _dot((c_["k_t"] * c_["w_row"]).astype(BF16), c_["vaug"])
        caug_ref[bi * ML_HEADS + h] = c_["s_state"] * c_["caug"] + upd
        m_ref[bi:bi + 1, h:h + 1] = c_["m_new"]

    @pl.when(c == nc - 1)
    def _state_out():
        for bi in range(nb):
            for h in range(ML_HEADS):
                caug = caug_ref[bi * ML_HEADS + h]
                c1_ref[bi, h] = caug[:, 0:ML_HD]
                n1_ref[bi, h:h + 1, :] = caug[:, ML_HD:2 * ML_HD].T[0:1, :]
            m1_ref[bi] = m_ref[bi:bi + 1, 0:ML_HEADS]


def _mlstm(ml, gates, c0, n0, m0, conv0, cw, cb, gb, ng, tr, nb, chunk):
    b, t, _ = ml.shape
    nc = t // tr
    assert b % nb == 0 and t % tr == 0 and nb <= SUBLANE and tr <= chunk
    bmap = lambda i, j: (i, 0, 0)
    const2 = lambda i, j: (0, 0)
    return pl.pallas_call(
        functools.partial(_mlstm_kernel, tr=tr, nb=nb, L=chunk),
        grid=(b // nb, nc),
        in_specs=[
            pl.BlockSpec((nb, tr, ML_QKVO), lambda i, j: (i, j, 0)),
            pl.BlockSpec((nb, tr, LANE), lambda i, j: (i, j, 0)),
            pl.BlockSpec((nb, ML_HEADS, ML_HD, ML_HD), lambda i, j: (i, 0, 0, 0)),
            pl.BlockSpec((nb, ML_HEADS, ML_HD), bmap),
            pl.BlockSpec((nb, 1, ML_HEADS), bmap),
            pl.BlockSpec((nb, CONV_W - 1, 2 * ML_WIDTH), bmap),
            pl.BlockSpec((CONV_W, 2 * ML_WIDTH), const2),
            pl.BlockSpec((1, 2 * ML_WIDTH), const2),
            pl.BlockSpec((1, LANE), const2),
            pl.BlockSpec((1, ML_WIDTH), const2),
        ],
        out_specs=[
            pl.BlockSpec((nb, tr, ML_WIDTH), lambda i, j: (i, j, 0)),
            pl.BlockSpec((nb, ML_HEADS, ML_HD, ML_HD), lambda i, j: (i, 0, 0, 0)),
            pl.BlockSpec((nb, ML_HEADS, ML_HD), bmap),
            pl.BlockSpec((nb, 1, ML_HEADS), bmap),
            pl.BlockSpec((nb, CONV_W - 1, 2 * ML_WIDTH), bmap),
        ],
        out_shape=[
            jax.ShapeDtypeStruct((b, t, ML_WIDTH), F32),
            jax.ShapeDtypeStruct((b, ML_HEADS, ML_HD, ML_HD), F32),
            jax.ShapeDtypeStruct((b, ML_HEADS, ML_HD), F32),
            jax.ShapeDtypeStruct((b, 1, ML_HEADS), F32),
            jax.ShapeDtypeStruct((b, CONV_W - 1, 2 * ML_WIDTH), F32),
        ],
        scratch_shapes=[
            pltpu.VMEM((nb * ML_HEADS, ML_HD, 2 * ML_HD), F32),
            pltpu.VMEM((SUBLANE, LANE), F32),
            pltpu.VMEM((nb, tr + SUBLANE, 2 * ML_WIDTH), F32),
        ],
        compiler_params=pltpu.CompilerParams(
            dimension_semantics=("arbitrary", "arbitrary"), vmem_limit_bytes=VMEM_LIMIT),
        name="mlstm",
    )(ml, gates, c0, n0, m0, conv0, cw, cb, gb, ng)


def _rwkv_kernel(c_ref, sh_ref, s0_ref, mu_ref, w0_ref, wwa_ref, a0_ref, g2_ref, kk_ref, ka_ref, rk_ref,
                 gng_ref, gnb_ref, out_ref, s1_ref, s_ref, last_ref, *, tr, nb, L):
    W = RW_WIDTH
    GW = RW_GROUP * RW_HD
    GL = RW_GROUP * L
    NGRP = RW_HEADS // RW_GROUP
    ci = pl.program_id(1)
    nc = pl.num_programs(1)

    r2 = _iota((GL, GW), 0)
    c2 = _iota((GL, GW), 1)
    bd = (r2 // L) == (c2 // RW_HD)
    rg = _iota((GL, GL), 0)
    cg = _iota((GL, GL), 1)
    s_lower = rg > cg
    i_lower = rg >= cg

    @pl.when(ci == 0)
    def _init():
        sbd = (_iota((GW, GW), 0) // RW_HD) == (_iota((GW, GW), 1) // RW_HD)
        for bi in range(nb):
            last_ref[bi, 0:1, :] = sh_ref[bi]
            for gi in range(NGRP):
                x = s0_ref[bi, gi * GW:(gi + 1) * GW, :]
                s_ref[bi * NGRP + gi] = jnp.where(sbd, jnp.concatenate([x] * RW_GROUP, axis=1), 0.0)

    c = c_ref[...].reshape(nb * tr, RW_COLS)
    prev = pltpu.roll(c, 1, axis=0)
    rowid = _iota(c.shape, 0)
    for bi in range(nb):
        prev = jnp.where(rowid == bi * tr, last_ref[bi, 0:1, :], prev)
        last_ref[bi, 0:1, :] = c[(bi + 1) * tr - 1:(bi + 1) * tr, :]
    xs = c + (prev - c) * mu_ref[...]
    r = xs[:, 0:W]
    k = xs[:, W:2 * W]
    v = xs[:, 2 * W:3 * W]
    slab = xs[:, 3 * W:3 * W + DECAY_LORA + AAA_LORA]
    gd = xs[:, 3 * W + DECAY_LORA + AAA_LORA:]
    lane = _iota(slab.shape, 1)
    t_in = jnp.where(lane < DECAY_LORA, jnp.tanh(slab), slab)
    la = _dot(t_in.astype(BF16), wwa_ref[...])
    w_log = -_softplus(-(w0_ref[...] + la[:, 0:W])) - 0.5
    lw_all = -jnp.exp(w_log)
    a_all = _sigmoid(a0_ref[...] + la[:, W:2 * W])
    g = _dot(_sigmoid(gd).astype(BF16), g2_ref[...])

    rs = _iota((W, W), 0)
    cs = _iota((W, W), 1)
    seg = jnp.where((rs // RW_HD) == (cs // RW_HD), 1.0, 0.0).astype(BF16)

    kk = k * kk_ref[...]
    kn_all = k * (1.0 + (a_all - 1.0) * ka_ref[...])
    ss = _dot_exact_rhs(kk * kk, seg, 2)
    kap_all = kk / jnp.maximum(jnp.sqrt(ss), 1e-12)
    bonus = _dot_exact_rhs(r * kn_all * rk_ref[...], seg, 2) * v

    rl = _iota((L, L), 0)
    cl_ = _iota((L, L), 1)
    tri = jnp.where(rl >= cl_, 1.0, 0.0).astype(BF16)

    def stack(x):
        return jnp.where(bd, jnp.concatenate([x] * RW_GROUP, axis=0), 0.0).astype(BF16)

    seqs = []
    for bi in range(nb):
        rows = slice(bi * tr, (bi + 1) * tr)
        lw = _pad_rows(lw_all[rows], L)
        kap = _pad_rows(kap_all[rows], L)
        kn_p = _pad_rows(kn_all[rows], L)
        a_p = _pad_rows(a_all[rows], L)
        v_p = _pad_rows(v[rows], L)
        r_p = _pad_rows(r[rows], L)
        cum = _dot_exact_lhs(tri, lw, 3)
        cum_last = cum[L - 1:L, :]
        e_neg = jnp.exp(-cum)
        e_rem = jnp.exp(cum_last - cum)
        seqs.append(dict(
            abar=-kap * jnp.exp(cum - lw), btil=kap * a_p * e_neg, ktil=kn_p * e_neg, rbar=r_p * jnp.exp(cum),
            bhat=kap * a_p * e_rem, khat=kn_p * e_rem, v=v_p, g_last=jnp.exp(cum_last)))

    chains = [(bi, gi) for bi in range(nb) for gi in range(NGRP)]
    ch = []
    for bi, gi in chains:
        q = seqs[bi]
        ls = slice(gi * GW, (gi + 1) * GW)
        a_s, b_s, k_s, r_s, v_s = (stack(q[n][:, ls]) for n in ("abar", "btil", "ktil", "rbar", "v"))
        ch.append(dict(
            ar=jnp.concatenate([a_s, r_s], axis=0),
            bk=jnp.concatenate([b_s, k_s], axis=0),
            bkh=jnp.concatenate([stack(q["bhat"][:, ls]), stack(q["khat"][:, ls])], axis=0),
            v_s=v_s, g_last=q["g_last"][:, ls], s=s_ref[bi * NGRP + gi]))
    for c_ in ch:
        p = _dot_nt(c_["ar"], c_["bk"])
        c_["npow"] = jnp.where(s_lower, p[0:GL, 0:GL], 0.0)
        c_["a_ak"] = jnp.where(s_lower, p[0:GL, GL:2 * GL], 0.0).astype(BF16)
        c_["r_b"] = jnp.where(i_lower, p[GL:2 * GL, 0:GL], 0.0).astype(BF16)
        c_["r_k"] = jnp.where(i_lower, p[GL:2 * GL, GL:2 * GL], 0.0).astype(BF16)
    for c_ in ch:
        c_["q0"] = _dot_nt(c_["ar"], c_["s"].astype(BF16))
    for c_ in ch:
        c_["u"] = c_["q0"][0:GL] + _dot(c_["a_ak"], c_["v_s"])
    lvls = L.bit_length() - 1
    for lvl in range(lvls):
        for c_ in ch:
            nb16 = c_["npow"].astype(BF16)
            c_["u"] = c_["u"] + _dot(nb16, c_["u"].astype(BF16))
            if lvl < lvls - 1:
                c_["npow"] = _dot(nb16, nb16)
    for c_ in ch:
        y = (c_["q0"][GL:2 * GL] + _dot(c_["r_b"], c_["u"].astype(BF16)) + _dot(c_["r_k"], c_["v_s"]))
        yg = y[0:L]
        for j in range(1, RW_GROUP):
            yg = yg + y[j * L:(j + 1) * L]
        c_["yg"] = yg
    for (bi, gi), c_ in zip(chains, ch):
        uv_t = jnp.concatenate([c_["u"], c_["v_s"].astype(F32)], axis=0).T.astype(BF16)
        s_ref[bi * NGRP + gi] = c_["s"] * c_["g_last"] + _dot(uv_t, c_["bkh"])
    y_rows = [jnp.concatenate([ch[bi * NGRP + gi]["yg"] for gi in range(NGRP)], axis=1)[0:tr]
              for bi in range(nb)]

    y_all = jnp.concatenate(y_rows, axis=0) if nb > 1 else y_rows[0]
    inv = 1.0 / RW_HD
    mu_ = _dot_exact_rhs(y_all, seg, 2) * inv
    dy = y_all - mu_
    var = _dot_exact_rhs(dy * dy, seg, 2) * inv
    yn = dy * lax.rsqrt(var + GN_EPS) * gng_ref[...] + gnb_ref[...]
    out_ref[...] = ((yn + bonus) * g).reshape(nb, tr, W)

    @pl.when(ci == nc - 1)
    def _state_out():
        for bi in range(nb):
            for gi in range(NGRP):
                s = s_ref[bi * NGRP + gi]
                f = s[:, 0:RW_HD]
                for j in range(1, RW_GROUP):
                    f = f + s[:, j * RW_HD:(j + 1) * RW_HD]
                s1_ref[bi, gi * GW:(gi + 1) * GW, :] = f


def _rwkv(cols, shift0, s0, mu, w0, wwa, a0, g2, kk, ka, rk, gng, gnb, tr, nb, chunk):
    b, t, _ = cols.shape
    nc = t // tr
    assert b % nb == 0 and t % tr == 0 and tr <= chunk
    const2 = lambda i, j: (0, 0)
    vec = pl.BlockSpec((1, RW_WIDTH), const2)
    return pl.pallas_call(
        functools.partial(_rwkv_kernel, tr=tr, nb=nb, L=chunk),
        grid=(b // nb, nc),
        in_specs=[
            pl.BlockSpec((nb, tr, RW_COLS), lambda i, j: (i, j, 0)),
            pl.BlockSpec((nb, 1, RW_COLS), lambda i, j: (i, 0, 0)),
            pl.BlockSpec((nb, RW_WIDTH, RW_HD), lambda i, j: (i, 0, 0)),
            pl.BlockSpec((1, RW_COLS), const2),
            vec,
            pl.BlockSpec((DECAY_LORA + AAA_LORA, 2 * RW_WIDTH), const2),
            vec,
            pl.BlockSpec((GATE_LORA, RW_WIDTH), const2),
            vec, vec, vec, vec, vec,
        ],
        out_specs=[
            pl.BlockSpec((nb, tr, RW_WIDTH), lambda i, j: (i, j, 0)),
            pl.BlockSpec((nb, RW_WIDTH, RW_HD), lambda i, j: (i, 0, 0)),
        ],
        out_shape=[
            jax.ShapeDtypeStruct((b, t, RW_WIDTH), F32),
            jax.ShapeDtypeStruct((b, RW_WIDTH, RW_HD), F32),
        ],
        scratch_shapes=[pltpu.VMEM((nb * (RW_HEADS // RW_GROUP), RW_GROUP * RW_HD, RW_GROUP * RW_HD), F32),
                        pltpu.VMEM((nb, SUBLANE, RW_COLS), F32)],
        compiler_params=pltpu.CompilerParams(
            dimension_semantics=("arbitrary", "arbitrary"), vmem_limit_bytes=VMEM_LIMIT),
        name="rwkv",
    )(cols, shift0, s0, mu, w0, wwa, a0, g2, kk, ka, rk, gng, gnb)


def _extract_top(works, idx, n, ranked=True):
    ranks = [jnp.full(w.shape, NOT_TAKEN, F32) for w in works]
    vals = [[] for _ in works]
    for j in range(n):
        ms = [jnp.max(w, axis=0, keepdims=True) for w in works]
        if idx is None:
            sels = [w == m for w, m in zip(works, ms)]
        else:
            firsts = [jnp.min(jnp.where(w == m, idx, PAST_ALL_IDX), axis=0, keepdims=True) for w, m in zip(works, ms)]
            sels = [idx == f for f in firsts]
        if ranked:
            ranks = [jnp.where(sel, float(j), r) for sel, r in zip(sels, ranks)]
        works = [jnp.where(sel, -jnp.inf, w) for sel, w in zip(sels, works)]
        for v, m in zip(vals, ms):
            v.append(m)
    return vals, (ranks if ranked else works)


def _miscount(rank, n):
    taken = jnp.sum(jnp.where(rank < float(n), 1.0, 0.0), axis=0, keepdims=True)
    return jnp.abs(taken - float(n))


def _rows_bf16(row, rows):
    packed = jnp.broadcast_to(row, (16, row.shape[1])).astype(BF16)
    return jnp.concatenate([packed] * (rows // 16), axis=0)


def _peer_kernel(x_ref, ml_ref, rw_ref, woml_ref, worw_ref, gffn_ref, gfin_ref, wqt_ref, keys_ref,
                 ua_ref, ub_ref, vta_ref, vtb_ref, y_ref,
                 ht_ref, qt_ref, rank_ref, e_ref, vals_ref, bad_ref, r2b_ref, e2b_ref, cnt_ref, gate_ref,
                 p_ref, acc_ref, *, tm, ec):
    e = pl.program_id(1)
    ne = pl.num_programs(1)
    ng = tm // LANE
    K = PEER_TOPK
    nslab = ec // N_KEYS
    act_ref = qt_ref

    @pl.when(e == 0)
    def _select():
        x1 = (x_ref[...] + _dot(ml_ref[...].astype(BF16), woml_ref[...])
              + _dot(rw_ref[...].astype(BF16), worw_ref[...]))
        y_ref[...] = x1
        h = x1 * lax.rsqrt(jnp.mean(x1 * x1, axis=-1, keepdims=True) + NORM_EPS) * gffn_ref[...]
        ht = h.T.astype(BF16)
        ht_ref[...] = ht
        qt_ref[...] = _dot(wqt_ref[...], ht)
        acc_ref[...] = jnp.zeros(acc_ref.shape, F32)

        def score_body(hp, carry):
            q = qt_ref[pl.ds(pl.multiple_of(hp * PEER_HALF, PEER_HALF), PEER_HALF), :]
            e_ref[hp] = _dot(keys_ref[hp], q.astype(BF16))
            return carry

        lax.fori_loop(0, 2 * PEER_HEADS, score_body, 0)

        key_idx = _iota((N_KEYS, LANE), 0).astype(F32)
        groups = [slice(gi * LANE, (gi + 1) * LANE) for gi in range(ng)]

        def topk_groups(hp, exact_ties):
            scores = [e_ref[hp, :, lanes] for lanes in groups]
            vals, ranks = _extract_top(scores, key_idx if exact_ties else None, K)
            bad = None
            for lanes, v, rank in zip(groups, vals, ranks):
                rank_ref[hp, :, lanes] = rank
                vals_ref[hp, :, lanes] = jnp.concatenate(v, axis=0)
                miss = _miscount(rank, K)
                bad = miss if bad is None else jnp.maximum(bad, miss)
            return bad

        def topk_body(hp, carry):
            bad_ref[hp] = jnp.broadcast_to(topk_groups(hp, False), (SUBLANE, LANE))
            return carry

        lax.fori_loop(0, 2 * PEER_HEADS, topk_body, 0)

        @pl.when(jnp.max(bad_ref[0:2 * PEER_HEADS]) > 0.0)
        def _redo_topk():
            def redo_body(hp, carry):
                @pl.when(jnp.max(bad_ref[hp]) > 0.0)
                def _():
                    topk_groups(hp, True)

                return carry

            lax.fori_loop(0, 2 * PEER_HEADS, redo_body, 0)

        sub8 = _iota((SUBLANE, LANE), 0)

        def cand_groups(h, exact_ties):
            works, v1s = [], []
            slab_a = [a_i for a_i in range(K) for _ in range(0, K // (a_i + 1), SUBLANE)]
            for lanes in groups:
                v1 = vals_ref[2 * h, :, lanes]
                v2 = vals_ref[2 * h + 1, :, lanes]
                slabs = []
                for a_i in range(K):
                    nb = K // (a_i + 1)
                    for b0 in range(0, nb, SUBLANE):
                        rows = v1[a_i:a_i + 1, :] + v2[b0:b0 + SUBLANE, :]
                        slabs.append(jnp.where(sub8 + b0 < nb, rows, -jnp.inf))
                works.append(jnp.concatenate(slabs, axis=0))
                v1s.append(v1)
            idx = None
            if exact_ties:
                idx = jnp.concatenate([(sub8 + (b0 + a_i * K)).astype(F32) for a_i in range(K)
                                       for b0 in range(0, K // (a_i + 1), SUBLANE)], axis=0)
            vals, left = _extract_top(works, idx, K, ranked=False)
            valid = jnp.concatenate([jnp.where(sub8 + b0 < K // (a_i + 1), 1.0, 0.0) for a_i in range(K)
                                     for b0 in range(0, K // (a_i + 1), SUBLANE)], axis=0)
            bad = None
            for lanes, v1, v, rest in zip(groups, v1s, vals, left):
                z = jnp.ones_like(v[0])
                for j in range(1, K):
                    z = z + jnp.exp(v[j] - v[0])
                picked = jnp.where(rest == -jnp.inf, valid, 0.0)
                r1 = rank_ref[2 * h, :, lanes]
                cnt = jnp.zeros((N_KEYS, LANE), F32)
                for a_i in range(K):
                    ca = None
                    for si, sa in enumerate(slab_a):
                        if sa == a_i:
                            part = jnp.sum(picked[si * SUBLANE:(si + 1) * SUBLANE], axis=0, keepdims=True)
                            ca = part if ca is None else ca + part
                    cnt = jnp.where(r1 == float(a_i), ca, cnt)
                cnt_ref[h, :, lanes] = cnt
                e1 = jnp.exp(e_ref[2 * h, :, lanes] - v1[0:1, :])
                gate_ref[h, :, lanes] = jnp.where(r1 < float(K), 0.5 * e1 / z, 0.0)
                miss = jnp.abs(jnp.sum(picked, axis=0, keepdims=True) - float(K))
                bad = miss if bad is None else jnp.maximum(bad, miss)
            return bad

        def cand_body(h, carry):
            bad_ref[2 * PEER_HEADS + h] = jnp.broadcast_to(cand_groups(h, False), (SUBLANE, LANE))
            r2b_ref[h] = rank_ref[2 * h + 1].astype(BF16)
            e2b_ref[h] = jnp.exp(e_ref[2 * h + 1] - vals_ref[2 * h + 1, 0:1, :]).astype(BF16)
            return carry

        lax.fori_loop(0, PEER_HEADS, cand_body, 0)

        @pl.when(jnp.max(bad_ref[2 * PEER_HEADS:3 * PEER_HEADS]) > 0.0)
        def _redo_cand():
            def redo_body(h, carry):
                @pl.when(jnp.max(bad_ref[2 * PEER_HEADS + h]) > 0.0)
                def _():
                    cand_groups(h, True)

                return carry

            lax.fori_loop(0, PEER_HEADS, redo_body, 0)

    key1 = pl.ds(pl.multiple_of(e * nslab, nslab), nslab)
    hs = nslab // 2
    hrows = hs * N_KEYS
    for half in range(2):
        act_ref[half * hrows:(half + 1) * hrows, :] = _dot((ua_ref, ub_ref)[half][...], ht_ref[...])
    def gate_half(half):
        for gi in range(ng):
            lanes = slice(gi * LANE, (gi + 1) * LANE)
            w = [None] * hs
            for h in range(PEER_HEADS):
                r2 = r2b_ref[h, :, lanes]
                e2 = e2b_ref[h, :, lanes]
                cw = cnt_ref[h, key1, lanes]
                gw = gate_ref[h, key1, lanes]
                for jj in range(hs):
                    j = half * hs + jj
                    hit = r2 < _rows_bf16(cw[j:j + 1, :], N_KEYS)
                    term = jnp.where(hit, e2, jnp.zeros_like(e2)) * _rows_bf16(gw[j:j + 1, :], N_KEYS)
                    w[jj] = term if h == 0 else w[jj] + term
            for jj in range(hs):
                rows = slice((half * hs + jj) * N_KEYS, (half * hs + jj + 1) * N_KEYS)
                act = act_ref[rows, lanes]
                ab = act.astype(BF16)
                p_ref[rows, lanes] = (w[jj] * ab) * (1.0 + lax.erf(ab * 0.7071067811865476))

    for half in range(2):
        gate_half(half)
        acc_ref[...] += _dot((vta_ref, vtb_ref)[half][0], p_ref[half * hrows:(half + 1) * hrows, :])

    @pl.when(e == ne - 1)
    def _finish():
        x2 = y_ref[...] + acc_ref[...].T
        y_ref[...] = x2 * lax.rsqrt(jnp.mean(x2 * x2, axis=-1, keepdims=True) + NORM_EPS) * gfin_ref[...]


def _peer(x2d, ml2d, rw2d, woml, worw, gffn, gfin, wqt, keys, u, vt, tm, ec):
    n = x2d.shape[0]
    n_exp = u.shape[0]
    assert ec == 8 * N_KEYS and tm % LANE == 0 and n % tm == 0 and n_exp % ec == 0
    tok = lambda i, e: (i, 0)
    const2 = lambda i, e: (0, 0)
    qrows = 2 * PEER_HEADS * PEER_HALF
    eh = ec // 2
    once = pl.Buffered(1)
    return pl.pallas_call(
        functools.partial(_peer_kernel, tm=tm, ec=ec),
        grid=(n // tm, n_exp // ec),
        in_specs=[
            pl.BlockSpec((tm, D_MODEL), tok),
            pl.BlockSpec((tm, ML_WIDTH), tok),
            pl.BlockSpec((tm, RW_WIDTH), tok),
            pl.BlockSpec((ML_WIDTH, D_MODEL), const2, pipeline_mode=once),
            pl.BlockSpec((RW_WIDTH, D_MODEL), const2, pipeline_mode=once),
            pl.BlockSpec((1, D_MODEL), const2),
            pl.BlockSpec((1, D_MODEL), const2),
            pl.BlockSpec((qrows, D_MODEL), const2, pipeline_mode=once),
            pl.BlockSpec((2 * PEER_HEADS, N_KEYS, PEER_HALF), lambda i, e: (0, 0, 0), pipeline_mode=once),
            pl.BlockSpec((eh, D_MODEL), lambda i, e: (2 * e, 0)),
            pl.BlockSpec((eh, D_MODEL), lambda i, e: (2 * e + 1, 0)),
            pl.BlockSpec((1, D_MODEL, eh), lambda i, e: (2 * e, 0, 0)),
            pl.BlockSpec((1, D_MODEL, eh), lambda i, e: (2 * e + 1, 0, 0)),
        ],
        out_specs=pl.BlockSpec((tm, D_MODEL), tok),
        out_shape=jax.ShapeDtypeStruct((n, D_MODEL), F32),
        scratch_shapes=[
            pltpu.VMEM((D_MODEL, tm), BF16),
            pltpu.VMEM((qrows, tm), F32),
            pltpu.VMEM((2 * PEER_HEADS, N_KEYS, tm), F32),
            pltpu.VMEM((2 * PEER_HEADS, N_KEYS, tm), F32),
            pltpu.VMEM((2 * PEER_HEADS, PEER_TOPK, tm), F32),
            pltpu.VMEM((3 * PEER_HEADS, SUBLANE, LANE), F32),
            pltpu.VMEM((PEER_HEADS, N_KEYS, tm), BF16),
            pltpu.VMEM((PEER_HEADS, N_KEYS, tm), BF16),
            pltpu.VMEM((PEER_HEADS, N_KEYS, tm), F32),
            pltpu.VMEM((PEER_HEADS, N_KEYS, tm), F32),
            pltpu.VMEM((ec, tm), BF16),
            pltpu.VMEM((D_MODEL, tm), F32),
        ],
        compiler_params=pltpu.CompilerParams(
            dimension_semantics=("arbitrary", "arbitrary"), vmem_limit_bytes=VMEM_LIMIT),
        name="peer",
    )(x2d, ml2d, rw2d, woml, worw, gffn, gfin, wqt, keys, u, u, vt, vt)


def _prep_weights(norm_mix_g, w_in, ml_conv_w, ml_conv_b, ml_b_i, ml_b_f, ml_norm_g,
                  rw_mu, rw_w0, rw_w2, rw_a0, rw_a2, rw_g2, rw_k_k, rw_k_a, rw_r_k, rw_gn_g, rw_gn_b,
                  w_out, norm_ffn_g, peer_w_q, peer_sub_keys, peer_u, peer_v, norm_final_g):
    assert w_in.shape[0] == 1, "one layer"
    w = w_in[0]
    wg = jnp.pad(w[:, ML_QKVO:ML_QKVO + 2 * ML_HEADS], ((0, 0), (0, LANE - 2 * ML_HEADS)))
    wg_hi = wg.astype(BF16)
    wg_lo = (wg - wg_hi.astype(F32)).astype(BF16)
    zeros = jnp.zeros((DECAY_LORA, RW_WIDTH), F32)
    wwa = jnp.concatenate([jnp.concatenate([rw_w2[0], zeros], axis=1),
                           jnp.concatenate([zeros, rw_a2[0]], axis=1)], axis=0)
    gate_bias = jnp.pad(jnp.concatenate([ml_b_i[0], ml_b_f[0]]), (0, LANE - 2 * ML_HEADS))[None, :]
    row = lambda a: a.reshape(1, -1)
    return dict(
        g_mix=row(norm_mix_g[0]),
        wml=w[:, 0:ML_QKVO].astype(BF16),
        wg=jnp.stack([wg_hi, wg_lo]),
        wrw=w[:, ML_QKVO + 2 * ML_HEADS:].astype(BF16),
        conv_w=ml_conv_w[0], conv_b=row(ml_conv_b[0]), gate_bias=gate_bias, ml_norm_g=row(ml_norm_g[0]),
        mu=row(rw_mu[0]), w0=row(rw_w0[0]), wwa=wwa.astype(BF16), a0=row(rw_a0[0]),
        g2=rw_g2[0].astype(BF16), k_k=row(rw_k_k[0]), k_a=row(rw_k_a[0]), r_k=row(rw_r_k[0]),
        gn_g=row(rw_gn_g[0]), gn_b=row(rw_gn_b[0]),
        wo_ml=w_out[0, 0:ML_WIDTH].astype(BF16), wo_rw=w_out[0, ML_WIDTH:].astype(BF16),
        g_ffn=row(norm_ffn_g[0]), g_fin=row(norm_final_g),
        wqt=peer_w_q[0].T.astype(BF16),
        keys=peer_sub_keys[0].reshape(2 * PEER_HEADS, N_KEYS, PEER_HALF).astype(BF16),
        u=peer_u[0].astype(BF16),
        vt=peer_v[0].astype(BF16).reshape(-1, PEER_EXPERT_CHUNK // 2, D_MODEL).transpose(0, 2, 1),
    )


def _rwkv_tiling(t):
    tr = min(t, RW_CHUNK)
    chunk = max(tr, RW_MIN_CHUNK)
    return tr, chunk, RW_STEP_ROWS // chunk // 2 if chunk < RW_CHUNK else RW_STEP_ROWS // chunk


def _mlstm_tiling(t, b):
    tr = min(t, ML_CHUNK)
    chunk = max(tr, ML_MIN_CHUNK)
    return tr, min(b, ML_STEP_ROWS // chunk), chunk


def _trunk(x, states, wp, tr_rw, rw_chunk, rw_batch, tm_in, tm_peer, ec):
    b, t, d = x.shape
    c0, n0, m0, conv0, s0, shift0 = (s[0] for s in states)
    x2d = x.reshape(b * t, d)
    ml, gates, rw = _proj_in(x2d, wp["g_mix"], wp["wml"], wp["wg"], wp["wrw"], tm_in)
    ml3 = ml.reshape(b, t, ML_QKVO)
    rw3 = rw.reshape(b, t, RW_COLS)
    ml_out, c1, n1, m1, conv1 = _mlstm(
        ml3, gates.reshape(b, t, LANE), c0, n0, m0.reshape(b, 1, ML_HEADS), conv0,
        wp["conv_w"], wp["conv_b"], wp["gate_bias"], wp["ml_norm_g"], *_mlstm_tiling(t, b))
    rw_out, s1 = _rwkv(
        rw3, shift0[:, None, :], s0.reshape(b, RW_WIDTH, RW_HD), wp["mu"], wp["w0"], wp["wwa"], wp["a0"], wp["g2"],
        wp["k_k"], wp["k_a"], wp["r_k"], wp["gn_g"], wp["gn_b"], tr_rw, min(b, rw_batch), rw_chunk)
    y = _peer(x2d, ml_out.reshape(b * t, ML_WIDTH), rw_out.reshape(b * t, RW_WIDTH),
              wp["wo_ml"], wp["wo_rw"], wp["g_ffn"], wp["g_fin"], wp["wqt"], wp["keys"],
              wp["u"], wp["vt"], tm_peer, ec)
    new_states = (c1[None], n1[None], m1.reshape(b, ML_HEADS)[None], conv1[None],
                  s1.reshape(b, RW_HEADS, RW_HD, RW_HD)[None], rw3[:, -1][None])
    return y.reshape(b, t, d), new_states


def kernel(x_prompt, x_sample, state_mlstm_C, state_mlstm_n, state_mlstm_m, state_mlstm_conv, state_rwkv_S, state_rwkv_shift, norm_mix_g, w_in, ml_conv_w, ml_conv_b, ml_b_i, ml_b_f, ml_norm_g, rw_mu, rw_w0, rw_w2, rw_a0, rw_a2, rw_g2, rw_k_k, rw_k_a, rw_r_k, rw_gn_g, rw_gn_b, w_out, norm_ffn_g, peer_w_q, peer_sub_keys, peer_u, peer_v, norm_final_g):
    wp = _prep_weights(norm_mix_g, w_in, ml_conv_w, ml_conv_b, ml_b_i, ml_b_f, ml_norm_g,
                       rw_mu, rw_w0, rw_w2, rw_a0, rw_a2, rw_g2, rw_k_k, rw_k_a, rw_r_k, rw_gn_g, rw_gn_b,
                       w_out, norm_ffn_g, peer_w_q, peer_sub_keys, peer_u, peer_v, norm_final_g)
    bp = x_prompt.shape[0]
    z = lambda *s: jnp.zeros((1, bp) + s, F32)
    prompt_states = (z(ML_HEADS, ML_HD, ML_HD), z(ML_HEADS, ML_HD), jnp.full((1, bp, ML_HEADS), M_INIT, F32),
                     z(CONV_W - 1, 2 * ML_WIDTH), z(RW_HEADS, RW_HD, RW_HD), z(RW_COLS))
    sample_states = (state_mlstm_C, state_mlstm_n, state_mlstm_m, state_mlstm_conv,
                     state_rwkv_S, state_rwkv_shift)
    tp, ts = x_prompt.shape[1], x_sample.shape[1]
    np_, ns = bp * tp, x_sample.shape[0] * ts
    y_p, st_p = _trunk(x_prompt, prompt_states, wp, *_rwkv_tiling(tp),
                       min(np_, PROJ_TILE), min(np_, PEER_TILE), PEER_EXPERT_CHUNK)
    y_s, st_s = _trunk(x_sample, sample_states, wp, *_rwkv_tiling(ts),
                       min(ns, PROJ_TILE), min(ns, PEER_TILE), PEER_EXPERT_CHUNK)
    return (y_p, y_s) + tuple(st_p) + tuple(st_s)
```
